```python
import jax, jax.numpy as jnp
from jax import lax
import numpy as np

D_MODEL = 2048
BATCH = 4
SEQ = 4096
DEPTH = 1
DEC_BATCH = 32
DEC_SEQ = 8
PAST_LEN = 16384
PAGE_SIZE = 128

D_RNN = 1536
N_RNN_BLOCKS = 16
RNN_BLOCK = D_RNN // N_RNN_BLOCKS
CONV_W = 4
LRU_C = 8.0
LRU_A_MIN = 0.9
LRU_A_MAX = 0.999
HEAD_DIM = 128
HEADS_PER_GROUP = 4
DILATIONS = (1, 4, 16)
KEYS_PER_GROUP = 128
WINDOWS = tuple(KEYS_PER_GROUP * d for d in DILATIONS)
N_GROUPS = len(DILATIONS)
D_ATT = N_GROUPS * HEADS_PER_GROUP * HEAD_DIM
D_ATT_OUT = HEADS_PER_GROUP * HEAD_DIM
ATT_SCALE = HEAD_DIM ** -0.5
N_MEM = 256
MEM_HEADS = 4
MEM_HEAD_DIM = 256
D_MEM = MEM_HEADS * MEM_HEAD_DIM
MEM_SCALE = MEM_HEAD_DIM ** -0.5
N_BRANCH = 3
D_IN = 2 * D_RNN + 3 * D_ATT + D_MEM + N_BRANCH * D_MODEL
_SPLITS = (D_RNN, 2 * D_RNN, 2 * D_RNN + D_ATT, 2 * D_RNN + 2 * D_ATT,
           2 * D_RNN + 3 * D_ATT, 2 * D_RNN + 3 * D_ATT + D_MEM)
N_EXPERT_GROUPS = 4
EXPERTS_PER_GROUP = 4
N_EXPERTS = N_EXPERT_GROUPS * EXPERTS_PER_GROUP
TOP_K = 2
D_EXPERT = 512
DN_ALPHA = (2 * DEPTH) ** 0.25
DN_BETA = (8 * DEPTH) ** -0.25
LN_EPS = 1e-5
NEG_INF = -1e30

kernel_name = 'hawk_dilated_mem_hmoe_step'


def _layer_norm(x, g, b):
    xf = x.astype(jnp.float32)
    mu = jnp.mean(xf, axis=-1, keepdims=True)
    var = jnp.mean(jnp.square(xf - mu), axis=-1, keepdims=True)
    out = (xf - mu) * lax.rsqrt(var + LN_EPS) * g.astype(jnp.float32) + b.astype(jnp.float32)
    return out.astype(x.dtype)


def _causal_conv(xr, conv_buf, conv_w, conv_b):
    T = xr.shape[1]
    xp = jnp.concatenate([conv_buf, xr], axis=1)
    y = conv_b + sum(xp[:, k:k + T] * conv_w[k] for k in range(CONV_W))
    return y, xp[:, -(CONV_W - 1):]


def _rglru(xc, h0, w_a, b_a, w_x, b_x, lru_lambda):
    B, T, C = xc.shape
    xb = xc.reshape(B, T, N_RNN_BLOCKS, RNN_BLOCK)
    r = jax.nn.sigmoid(jnp.einsum('btnc,ncd->btnd', xb, w_a).reshape(B, T, C) + b_a)
    i = jax.nn.sigmoid(jnp.einsum('btnc,ncd->btnd', xb, w_x).reshape(B, T, C) + b_x)
    log_a = -LRU_C * r.astype(jnp.float32) * jax.nn.softplus(-lru_lambda.astype(jnp.float32))
    a = jnp.exp(log_a)
    u = jnp.sqrt(-jnp.expm1(2.0 * log_a)) * (i * xc).astype(jnp.float32)
    u = u.at[:, 0].add(a[:, 0] * h0.astype(jnp.float32))

    def combine(left, right):
        a1, b1 = left
        a2, b2 = right
        return a1 * a2, a2 * b1 + b2

    _, h = lax.associative_scan(combine, (a, u), axis=1)
    return h.astype(xc.dtype), h[:, -1].astype(xc.dtype)


def _dilated_prompt(q, kv, dil):
    B, S, H, Dh = q.shape
    blk = KEYS_PER_GROUP
    L = S // dil
    nb = -(-L // blk)
    Lp = nb * blk

    def to_blocks(t):
        t = t.reshape((B, L, dil) + t.shape[2:])
        t = jnp.moveaxis(t, 2, 1)
        t = jnp.pad(t, [(0, 0), (0, 0), (0, Lp - L)] + [(0, 0)] * (t.ndim - 3))
        return t.reshape((B, dil, nb, blk) + t.shape[3:])

    def from_blocks(t):
        t = t.reshape((B, dil, Lp) + t.shape[4:])[:, :, :L]
        t = jnp.moveaxis(t, 1, 2)
        return t.reshape((B, S) + t.shape[3:])

    qb = to_blocks(q)
    kvb = to_blocks(kv)
    prev = jnp.pad(kvb, [(0, 0), (0, 0), (1, 0)] + [(0, 0)] * 4)[:, :, :-1]
    kvw = jnp.concatenate([prev, kvb], axis=3)
    s = jnp.einsum('brnqhc,brnkhc->brnhqk', qb, kvw[:, :, :, :, 0]).astype(jnp.float32) * ATT_SCALE
    dist = blk + jnp.arange(blk)[:, None] - jnp.arange(2 * blk)[None, :]
    band = (dist >= 0) & (dist <= KEYS_PER_GROUP)
    has_prev = (jnp.arange(nb)[:, None] > 0) | (jnp.arange(2 * blk)[None, :] >= blk)
    mask = band[None] & has_prev[:, None, :]
    s = jnp.where(mask[:, None], s, NEG_INF)
    lse = jax.nn.logsumexp(s, axis=-1)
    p = jnp.exp(s - lse[..., None]).astype(q.dtype)
    o = jnp.einsum('brnhqk,brnkhc->brnqhc', p, kvw[:, :, :, :, 1])
    return from_blocks(o), from_blocks(jnp.swapaxes(lse, -1, -2))


def _dilated_sample(q, kv, kv_buf, dil):
    T = q.shape[1]
    Wb = kv_buf.shape[1]
    ext = jnp.concatenate([kv_buf, kv], axis=1)
    e = Wb + jnp.arange(T)[:, None] - dil * jnp.arange(KEYS_PER_GROUP + 1)[None, :]
    valid = e >= 0
    g = jnp.take(ext, jnp.clip(e, 0), axis=1)
    s = jnp.einsum('bthc,btjhc->bthj', q, g[:, :, :, 0]).astype(jnp.float32) * ATT_SCALE
    s = jnp.where(valid[None, :, None, :], s, NEG_INF)
    lse = jax.nn.logsumexp(s, axis=-1)
    p = jnp.exp(s - lse[..., None]).astype(q.dtype)
    o = jnp.einsum('bthj,btjhc->bthc', p, g[:, :, :, 1])
    return o, lse


def _mem_attention(qm, mem_kv):
    s = jnp.einsum('bthc,bmhc->bhtm', qm, mem_kv[:, :, 0]).astype(jnp.float32) * MEM_SCALE
    p = jax.nn.softmax(s, axis=-1).astype(qm.dtype)
    return jnp.einsum('bhtm,bmhc->bthc', p, mem_kv[:, :, 1])


def _hier_moe(x, w_rg, b_rg, w_re, b_re, w_gate, w_up, w_down):
    B, T, D = x.shape
    xt = x.reshape(B * T, D)
    g_prob = jax.nn.softmax((xt @ w_rg).astype(jnp.float32) + b_rg.astype(jnp.float32), axis=-1)
    g_p, g_idx = lax.top_k(g_prob, 1)
    e_logits = ((xt @ w_re).astype(jnp.float32) + b_re.astype(jnp.float32)).reshape(-1, N_EXPERT_GROUPS, EXPERTS_PER_GROUP)
    e_in = jnp.take_along_axis(e_logits, g_idx[:, :, None], axis=1)[:, 0]
    e_val, e_idx = lax.top_k(e_in, TOP_K)
    e_w = jax.nn.softmax(e_val, axis=-1) * g_p
    expert_id = g_idx * EXPERTS_PER_GROUP + e_idx
    dense_w = jnp.sum(jax.nn.one_hot(expert_id, N_EXPERTS, dtype=jnp.float32) * e_w[..., None], axis=1).astype(x.dtype)
    y = jnp.zeros_like(xt)
    for e in range(N_EXPERTS):
        h = jax.nn.gelu(xt @ w_gate[e]) * (xt @ w_up[e])
        y = y + dense_w[:, e:e + 1] * (h @ w_down[e])
    return y.reshape(B, T, D)


def setup_inputs(seed: int = 0) -> dict:
    key = jax.random.key(seed)
    ks = jax.random.split(key, 40)
    f32 = jnp.float32

    def nrm(k, shape, scale):
        return scale * jax.random.normal(k, shape, f32)

    wb = [min(w, PAST_LEN) for w in WINDOWS]
    kv_tail = (2, HEADS_PER_GROUP, HEAD_DIM)
    u = jax.random.uniform(ks[17], (D_RNN,), f32, LRU_A_MIN, LRU_A_MAX)
    a = u ** (1.0 / LRU_C)
    lru_lambda = jnp.log(a) - jnp.log1p(-a)
    return {
        'x_prompt': nrm(ks[0], (BATCH, SEQ, D_MODEL), 1.0),
        'x_sample': nrm(ks[1], (DEC_BATCH, DEC_SEQ, D_MODEL), 1.0),
        'cache_kv_w128': nrm(ks[2], (DEC_BATCH, wb[0]) + kv_tail, 1.0),
        'cache_kv_w512': nrm(ks[3], (DEC_BATCH, wb[1]) + kv_tail, 1.0),
        'cache_kv_w2048': nrm(ks[4], (DEC_BATCH, wb[2]) + kv_tail, 1.0),
        'cache_mem_kv': nrm(ks[5], (DEC_BATCH, N_MEM, 2, MEM_HEADS, MEM_HEAD_DIM), 1.0),
        'state_h': nrm(ks[6], (DEC_BATCH, D_RNN), 0.5),
        'state_conv': nrm(ks[7], (DEC_BATCH, CONV_W - 1, D_RNN), 1.0),
        'mem_prompt': nrm(ks[8], (BATCH, N_MEM, D_MODEL), 1.0),
        'w_in': nrm(ks[9], (D_MODEL, D_IN), D_MODEL ** -0.5),
        'b_gates': nrm(ks[10], (N_BRANCH * D_MODEL,), 0.02),
        'conv_w': nrm(ks[11], (CONV_W, D_RNN), CONV_W ** -0.5),
        'conv_b': nrm(ks[12], (D_RNN,), 0.02),
        'w_a': nrm(ks[13], (N_RNN_BLOCKS, RNN_BLOCK, RNN_BLOCK), RNN_BLOCK ** -0.5),
        'b_a': nrm(ks[14], (D_RNN,), 0.02),
        'w_x': nrm(ks[15], (N_RNN_BLOCKS, RNN_BLOCK, RNN_BLOCK), RNN_BLOCK ** -0.5),
        'b_x': nrm(ks[16], (D_RNN,), 0.02),
        'lru_lambda': lru_lambda,
        'w_br_lru': nrm(ks[18], (D_RNN, D_MODEL), DN_BETA * D_RNN ** -0.5),
        'w_br_att': nrm(ks[19], (D_ATT_OUT, D_MODEL), DN_BETA * D_ATT_OUT ** -0.5),
        'w_br_mem': nrm(ks[20], (D_MEM, D_MODEL), DN_BETA * D_MEM ** -0.5),
        'w_o': nrm(ks[21], (D_MODEL, D_MODEL), DN_BETA * D_MODEL ** -0.5),
        'w_mem_kv': nrm(ks[22], (D_MODEL, 2 * D_MEM), D_MODEL ** -0.5),
        'ln1_g': 1.0 + nrm(ks[23], (D_MODEL,), 0.02),
        'ln1_b': nrm(ks[24], (D_MODEL,), 0.02),
        'w_rg': nrm(ks[25], (D_MODEL, N_EXPERT_GROUPS), D_MODEL ** -0.5),
        'b_rg': nrm(ks[26], (N_EXPERT_GROUPS,), 0.01),
        'w_re': nrm(ks[27], (D_MODEL, N_EXPERTS), D_MODEL ** -0.5),
        'b_re': nrm(ks[28], (N_EXPERTS,), 0.01),
        'w_gate': nrm(ks[29], (N_EXPERTS, D_MODEL, D_EXPERT), D_MODEL ** -0.5),
        'w_up': nrm(ks[30], (N_EXPERTS, D_MODEL, D_EXPERT), D_MODEL ** -0.5),
        'w_down': nrm(ks[31], (N_EXPERTS, D_EXPERT, D_MODEL), DN_BETA * D_EXPERT ** -0.5),
        'ln2_g': 1.0 + nrm(ks[32], (D_MODEL,), 0.02),
        'ln2_b': nrm(ks[33], (D_MODEL,), 0.02),
    }


def reference(x_prompt, x_sample, cache_kv_w128, cache_kv_w512, cache_kv_w2048, cache_mem_kv,
              state_h, state_conv, mem_prompt, w_in, b_gates, conv_w, conv_b, w_a, b_a, w_x, b_x,
              lru_lambda, w_br_lru, w_br_att, w_br_mem, w_o, w_mem_kv, ln1_g, ln1_b,
              w_rg, b_rg, w_re, b_re, w_gate, w_up, w_down, ln2_g, ln2_b):

    def token_mixing(x, conv_buf, h0, mem_kv, kv_bufs):
        B, T, _ = x.shape
        z = x @ w_in
        xr, xg, q, k, v, qm, gl = jnp.split(z, _SPLITS, axis=-1)
        xc, new_conv = _causal_conv(xr, conv_buf, conv_w, conv_b)
        hs, h_last = _rglru(xc, h0, w_a, b_a, w_x, b_x, lru_lambda)
        u_lru = (hs * jax.nn.gelu(xg)) @ w_br_lru
        q = q.reshape(B, T, N_GROUPS, HEADS_PER_GROUP, HEAD_DIM)
        kv = jnp.stack([k, v], axis=2).reshape(B, T, 2, N_GROUPS, HEADS_PER_GROUP, HEAD_DIM)
        outs, lses = [], []
        for g, dil in enumerate(DILATIONS):
            if kv_bufs is None:
                o, l = _dilated_prompt(q[:, :, g], kv[:, :, :, g], dil)
            else:
                o, l = _dilated_sample(q[:, :, g], kv[:, :, :, g], kv_bufs[g], dil)
            outs.append(o)
            lses.append(l)
        wts = jax.nn.softmax(jnp.stack(lses), axis=0).astype(x.dtype)
        att = jnp.sum(wts[..., None] * jnp.stack(outs), axis=0)
        u_att = att.reshape(B, T, D_ATT_OUT) @ w_br_att
        mem_out = _mem_attention(qm.reshape(B, T, MEM_HEADS, MEM_HEAD_DIM), mem_kv)
        u_mem = mem_out.reshape(B, T, D_MEM) @ w_br_mem
        gates = jax.nn.sigmoid(gl + b_gates).reshape(B, T, N_BRANCH, D_MODEL)
        merged = gates[:, :, 0] * u_lru + gates[:, :, 1] * u_att + gates[:, :, 2] * u_mem
        return merged @ w_o, new_conv, h_last, kv

    def layer(x, conv_buf, h0, mem_kv, kv_bufs):
        mix, new_conv, h_last, kv = token_mixing(x, conv_buf, h0, mem_kv, kv_bufs)
        x1 = _layer_norm(DN_ALPHA * x + mix, ln1_g, ln1_b)
        y = _layer_norm(DN_ALPHA * x1 + _hier_moe(x1, w_rg, b_rg, w_re, b_re, w_gate, w_up, w_down), ln2_g, ln2_b)
        return y, new_conv, h_last, kv

    Bp, S, _ = x_prompt.shape
    mem_kv_prompt = (mem_prompt @ w_mem_kv).reshape(Bp, N_MEM, 2, MEM_HEADS, MEM_HEAD_DIM)
    y_prompt, conv_prompt, h_prompt, kv_p = layer(
        x_prompt, jnp.zeros((Bp, CONV_W - 1, D_RNN), x_prompt.dtype),
        jnp.zeros((Bp, D_RNN), x_prompt.dtype), mem_kv_prompt, None)
    kv_w128_prompt = kv_p[:, S - min(WINDOWS[0], S):, :, 0]
    kv_w512_prompt = kv_p[:, S - min(WINDOWS[1], S):, :, 1]
    kv_w2048_prompt = kv_p[:, S - min(WINDOWS[2], S):, :, 2]

    y_sample, conv_sample, h_sample, kv_s = layer(
        x_sample, state_conv, state_h, cache_mem_kv, (cache_kv_w128, cache_kv_w512, cache_kv_w2048))
    kv_w128_sample = kv_s[:, :, :, 0]
    kv_w512_sample = kv_s[:, :, :, 1]
    kv_w2048_sample = kv_s[:, :, :, 2]

    return (y_prompt, y_sample, kv_w128_prompt, kv_w512_prompt, kv_w2048_prompt, mem_kv_prompt,
            h_prompt, conv_prompt, kv_w128_sample, kv_w512_sample, kv_w2048_sample, h_sample, conv_sample)
```

```python
import functools

import jax
import jax.numpy as jnp
from jax import lax
from jax.experimental import pallas as pl
from jax.experimental.pallas import tpu as pltpu

F32 = jnp.float32
BF16 = jnp.bfloat16

D_MODEL = 2048
D_RNN = 1536
N_RNN_BLOCKS = 16
RNN_BLOCK = D_RNN // N_RNN_BLOCKS
RNN_CHUNK = 384
N_RNN_CHUNKS = D_RNN // RNN_CHUNK
CONV_W = 4
LRU_C = 8.0
HEAD_DIM = 128
HEADS = 4
DILATIONS = (1, 4, 16)
KEYS = 128
N_GROUPS = 3
D_ATT_OUT = HEADS * HEAD_DIM
ATT_SCALE = HEAD_DIM ** -0.5
N_MEM = 256
MEM_HEADS = 4
MEM_HEAD_DIM = 256
D_MEM = MEM_HEADS * MEM_HEAD_DIM
MEM_SCALE = MEM_HEAD_DIM ** -0.5
N_EXPERT_GROUPS = 4
EXPERTS_PER_GROUP = 4
N_EXPERTS = 16
D_EXPERT = 512
DN_ALPHA = 2.0 ** 0.25
LN_EPS = 1e-5
NEG_INF = -1e30

Z_MAIN = 2 * D_RNN + 3 * (N_GROUPS * D_ATT_OUT) + D_MEM
COL_Q = 2 * D_RNN
COL_K = COL_Q + N_GROUPS * D_ATT_OUT
COL_V = COL_K + N_GROUPS * D_ATT_OUT
COL_QM = COL_V + N_GROUPS * D_ATT_OUT
N_GATE_COLS = 3 * D_MODEL
ROUTER_ROWS = 32

VMEM_LIMIT = 56 * 1024 * 1024


def _cparams(sem):
    return pltpu.CompilerParams(dimension_semantics=sem, vmem_limit_bytes=VMEM_LIMIT)


def _gelu(x):
    return 0.5 * x * (1.0 + jnp.tanh(0.7978845608028654 * (x + 0.044715 * (x * x * x))))


def _layer_norm(x, g, b):
    mu = jnp.mean(x, axis=-1, keepdims=True)
    xc = x - mu
    var = jnp.mean(xc * xc, axis=-1, keepdims=True)
    return xc * lax.rsqrt(var + LN_EPS) * g + b


def _dot_nt(a, b):
    return lax.dot_general(a, b, (((1,), (1,)), ((), ())), preferred_element_type=F32)


def _mm_body(x_ref, w_ref, o_ref):
    acc = jnp.dot(x_ref[...].astype(BF16), w_ref[...].astype(BF16), preferred_element_type=F32)
    o_ref[...] = acc.astype(o_ref.dtype)


def _mm_gate_body(x_ref, w_ref, b_ref, o_ref):
    acc = jnp.dot(x_ref[...].astype(BF16), w_ref[...].astype(BF16), preferred_element_type=F32)
    o_ref[...] = jax.nn.sigmoid(acc + b_ref[...]).astype(o_ref.dtype)


def _matmul(x, w, col_off, n_cols, out_dtype, tm, tn, name, bias=None):
    m, k = x.shape
    cb = col_off // tn
    in_specs = [pl.BlockSpec((tm, k), lambda i, j: (i, 0)),
                pl.BlockSpec((k, tn), lambda i, j: (0, j + cb))]
    args = [x, w]
    body = _mm_body
    if bias is not None:
        in_specs.append(pl.BlockSpec((1, tn), lambda i, j: (0, j)))
        args.append(bias)
        body = _mm_gate_body
    return pl.pallas_call(
        body,
        out_shape=jax.ShapeDtypeStruct((m, n_cols), out_dtype),
        grid=(m // tm, n_cols // tn),
        in_specs=in_specs,
        out_specs=pl.BlockSpec((tm, tn), lambda i, j: (i, j)),
        compiler_params=_cparams(("parallel", "arbitrary")),
        name=name,
    )(*args)


def _rglru_body(xr_ref, xg_ref, cbuf_ref, h0_ref, cw_ref, cb_ref, wax_ref, ba_ref, bx_ref, lam_ref,
                out_ref, hl_ref, ext_s, a_s, u_s, h_s, *, tt):
    t = pl.program_id(1)

    @pl.when(t == 0)
    def _():
        ext_s[0:8, :] = jnp.zeros((8, D_RNN), F32)
        ext_s[5:8, :] = cbuf_ref[0]
        h_s[...] = jnp.broadcast_to(h0_ref[0], (8, D_RNN))

    @pl.when(t > 0)
    def _():
        ext_s[0:8, :] = ext_s[tt:tt + 8, :]

    ext_s[8:8 + tt, :] = xr_ref[0]
    cw = cw_ref[...]
    xc = (cb_ref[...] + cw[3:4, :] * ext_s[8:8 + tt, :] + cw[2:3, :] * ext_s[7:7 + tt, :]
          + cw[1:2, :] * ext_s[6:6 + tt, :] + cw[0:1, :] * ext_s[5:5 + tt, :])
    xcb = xc.astype(BF16)
    r_parts, i_parts = [], []
    for c in range(N_RNN_CHUNKS):
        g = jnp.dot(xcb[:, c * RNN_CHUNK:(c + 1) * RNN_CHUNK], wax_ref[c], preferred_element_type=F32)
        r_parts.append(g[:, :RNN_CHUNK])
        i_parts.append(g[:, RNN_CHUNK:])
    r = jax.nn.sigmoid(jnp.concatenate(r_parts, axis=1) + ba_ref[...])
    gi = jax.nn.sigmoid(jnp.concatenate(i_parts, axis=1) + bx_ref[...])
    nl = -lam_ref[...]
    softplus = jnp.maximum(nl, 0.0) + jnp.log1p(jnp.exp(-jnp.abs(nl)))
    log_a = (-LRU_C) * r * softplus
    th = jnp.tanh(log_a)
    a_s[...] = jnp.exp(log_a)
    u_s[...] = jnp.sqrt(-2.0 * th / (1.0 - th)) * (gi * xc)

    rows = lax.broadcasted_iota(jnp.int32, (8, D_RNN), 0)

    def blk(i, h):
        r0 = pl.multiple_of(i * 8, 8)
        ab = a_s[pl.ds(r0, 8), :]
        ub = u_s[pl.ds(r0, 8), :]
        for s in (1, 2, 4):
            keep = rows >= s
            ub = ab * jnp.where(keep, pltpu.roll(ub, s, 0), 0.0) + ub
            ab = ab * jnp.where(keep, pltpu.roll(ab, s, 0), 1.0)
        hb = ab * h + ub
        u_s[pl.ds(r0, 8), :] = hb
        return jnp.broadcast_to(hb[7:8, :], (8, D_RNN))

    h_fin = lax.fori_loop(0, tt // 8, blk, h_s[...])
    h_s[...] = h_fin
    hl_ref[0] = h_fin[0:1, :]
    out_ref[0] = (u_s[...] * _gelu(xg_ref[0])).astype(out_ref.dtype)


def _rglru(z3, conv_buf, h0, conv_w, conv_b, wax, b_a, b_x, lam, tt):
    b, t, _ = z3.shape
    vec = lambda: pl.BlockSpec((1, D_RNN), lambda i, j: (0, 0))
    return pl.pallas_call(
        functools.partial(_rglru_body, tt=tt),
        out_shape=(jax.ShapeDtypeStruct((b, t, D_RNN), BF16), jax.ShapeDtypeStruct((b, 1, D_RNN), F32)),
        grid=(b, t // tt),
        in_specs=[pl.BlockSpec((1, tt, D_RNN), lambda i, j: (i, j, 0)),
                  pl.BlockSpec((1, tt, D_RNN), lambda i, j: (i, j, 1)),
                  pl.BlockSpec((1, CONV_W - 1, D_RNN), lambda i, j: (i, 0, 0)),
                  pl.BlockSpec((1, 1, D_RNN), lambda i, j: (i, 0, 0)),
                  pl.BlockSpec((CONV_W, D_RNN), lambda i, j: (0, 0)),
                  vec(),
                  pl.BlockSpec((N_RNN_CHUNKS, RNN_CHUNK, 2 * RNN_CHUNK), lambda i, j: (0, 0, 0)),
                  vec(), vec(), vec()],
        out_specs=(pl.BlockSpec((1, tt, D_RNN), lambda i, j: (i, j, 0)),
                   pl.BlockSpec((1, 1, D_RNN), lambda i, j: (i, 0, 0))),
        scratch_shapes=[pltpu.VMEM((tt + 8, D_RNN), F32), pltpu.VMEM((tt, D_RNN), F32),
                        pltpu.VMEM((tt, D_RNN), F32), pltpu.VMEM((8, D_RNN), F32)],
        compiler_params=_cparams(("parallel", "arbitrary")),
        name="rglru",
    )(z3, z3, conv_buf, h0, conv_w, conv_b, wax, b_a, b_x, lam)


def _attn_block(q, ko, vo, kp, vp, has_prev):
    row = lax.broadcasted_iota(jnp.int32, (KEYS, KEYS), 0)
    col = lax.broadcasted_iota(jnp.int32, (KEYS, KEYS), 1)
    s_o = jnp.where(col <= row, _dot_nt(q, ko) * ATT_SCALE, NEG_INF)
    s_p = jnp.where(jnp.logical_and(col >= row, has_prev), _dot_nt(q, kp) * ATT_SCALE, NEG_INF)
    m = jnp.maximum(jnp.max(s_o, axis=-1, keepdims=True), jnp.max(s_p, axis=-1, keepdims=True))
    p_o = jnp.exp(s_o - m)
    p_p = jnp.exp(s_p - m)
    l = jnp.sum(p_o, axis=-1, keepdims=True) + jnp.sum(p_p, axis=-1, keepdims=True)
    o = (jnp.dot(p_o.astype(BF16), vo, preferred_element_type=F32)
         + jnp.dot(p_p.astype(BF16), vp, preferred_element_type=F32)) / l
    return o, m + jnp.log(l)


def _attn_prompt_body(q_ref, k_ref, v_ref, kh_ref, vh_ref, o_ref, l_ref, *, lt):
    n = pl.program_id(2)
    for h in range(HEADS):
        cs = slice(h * HEAD_DIM, (h + 1) * HEAD_DIM)

        def put(r0, o, lse):
            o_ref[0, pl.ds(r0, KEYS), cs] = o.astype(o_ref.dtype)
            l_ref[0, pl.ds(r0, KEYS), cs] = jnp.broadcast_to(lse, (KEYS, HEAD_DIM))

        o, lse = _attn_block(q_ref[0, 0:KEYS, cs].astype(BF16),
                             k_ref[0, 0:KEYS, cs].astype(BF16), v_ref[0, 0:KEYS, cs].astype(BF16),
                             kh_ref[0, :, cs].astype(BF16), vh_ref[0, :, cs].astype(BF16), n > 0)
        put(0, o, lse)

        def qb_body(qb, carry):
            r0 = pl.multiple_of(qb * KEYS, KEYS)
            rp = pl.multiple_of(r0 - KEYS, KEYS)
            o, lse = _attn_block(q_ref[0, pl.ds(r0, KEYS), cs].astype(BF16),
                                 k_ref[0, pl.ds(r0, KEYS), cs].astype(BF16),
                                 v_ref[0, pl.ds(r0, KEYS), cs].astype(BF16),
                                 k_ref[0, pl.ds(rp, KEYS), cs].astype(BF16),
                                 v_ref[0, pl.ds(rp, KEYS), cs].astype(BF16), True)
            put(r0, o, lse)
            return carry

        lax.fori_loop(1, lt // KEYS, qb_body, 0)


def _attn_prompt(zmain, b, s, g, d, lt):
    l = s // d
    zv = zmain.reshape(b, l, d * Z_MAIN)
    wz = Z_MAIN // D_ATT_OUT
    cq, ck, cv = COL_Q // D_ATT_OUT + g, COL_K // D_ATT_OUT + g, COL_V // D_ATT_OUT + g
    hb = lt // KEYS
    cur = lambda c: pl.BlockSpec((1, lt, D_ATT_OUT), lambda i, r, n: (i, n, r * wz + c))
    halo = lambda c: pl.BlockSpec((1, KEYS, D_ATT_OUT),
                                  lambda i, r, n: (i, jnp.maximum(n * hb - 1, 0), r * wz + c))
    out = pl.BlockSpec((1, lt, D_ATT_OUT), lambda i, r, n: (i, n, r))
    o, lse = pl.pallas_call(
        functools.partial(_attn_prompt_body, lt=lt),
        out_shape=(jax.ShapeDtypeStruct((b, l, d * D_ATT_OUT), BF16),
                   jax.ShapeDtypeStruct((b, l, d * D_ATT_OUT), F32)),
        grid=(b, d, l // lt),
        in_specs=[cur(cq), cur(ck), cur(cv), halo(ck), halo(cv)],
        out_specs=(out, out),
        compiler_params=_cparams(("parallel", "parallel", "arbitrary")),
        name=f"attn_prompt_d{d}",
    )(zv, zv, zv, zv, zv)
    return o.reshape(b * s, D_ATT_OUT), lse.reshape(b * s, D_ATT_OUT)


def _attn_sample_body(q_ref, kn_ref, vn_ref, c_ref, o_ref, l_ref, *, d, t_new, wb):
    nr = HEADS * t_new
    tsh = t_new.bit_length() - 1
    q = q_ref[0]
    qt = jnp.concatenate([q] * HEADS, axis=0)
    rq = lax.broadcasted_iota(jnp.int32, (nr, D_ATT_OUT), 0)
    cq = lax.broadcasted_iota(jnp.int32, (nr, D_ATT_OUT), 1)
    head_lo = (rq >> tsh) * HEAD_DIM
    qbd = jnp.where(jnp.logical_and(cq >= head_lo, cq < head_lo + HEAD_DIM), qt, 0.0).astype(BF16)
    kc = c_ref[0, :, 0:D_ATT_OUT].astype(BF16)
    vc = c_ref[0, :, D_ATT_OUT:2 * D_ATT_OUT].astype(BF16)
    kn = kn_ref[0].astype(BF16)
    vn = vn_ref[0].astype(BF16)
    tq_c = lax.broadcasted_iota(jnp.int32, (nr, wb), 0) & (t_new - 1)
    dist_c = wb + tq_c - lax.broadcasted_iota(jnp.int32, (nr, wb), 1)
    ok_c = jnp.logical_and((dist_c & (d - 1)) == 0, dist_c <= KEYS * d)
    tq_n = lax.broadcasted_iota(jnp.int32, (nr, t_new), 0) & (t_new - 1)
    dist_n = tq_n - lax.broadcasted_iota(jnp.int32, (nr, t_new), 1)
    ok_n = jnp.logical_and(jnp.logical_and(dist_n >= 0, (dist_n & (d - 1)) == 0), dist_n <= KEYS * d)
    s_c = jnp.where(ok_c, _dot_nt(qbd, kc) * ATT_SCALE, NEG_INF)
    s_n = jnp.where(ok_n, _dot_nt(qbd, kn) * ATT_SCALE, NEG_INF)
    m = jnp.maximum(jnp.max(s_c, axis=-1, keepdims=True), jnp.max(s_n, axis=-1, keepdims=True))
    p_c = jnp.exp(s_c - m)
    p_n = jnp.exp(s_n - m)
    l = jnp.sum(p_c, axis=-1, keepdims=True) + jnp.sum(p_n, axis=-1, keepdims=True)
    o = (jnp.dot(p_c.astype(BF16), vc, preferred_element_type=F32)
         + jnp.dot(p_n.astype(BF16), vn, preferred_element_type=F32)) / l
    lse = m + jnp.log(l)
    o_ref[0] = jnp.concatenate(
        [o[h * t_new:(h + 1) * t_new, h * HEAD_DIM:(h + 1) * HEAD_DIM] for h in range(HEADS)], axis=1)
    l_ref[0] = jnp.concatenate(
        [jnp.broadcast_to(lse[h * t_new:(h + 1) * t_new], (t_new, HEAD_DIM)) for h in range(HEADS)], axis=1)


def _attn_sample(z3, cache, g, d):
    b, t_new, _ = z3.shape
    wb = cache.shape[1]
    cache2 = cache.reshape(b, wb, 2 * D_ATT_OUT)
    cq, ck, cv = COL_Q // D_ATT_OUT + g, COL_K // D_ATT_OUT + g, COL_V // D_ATT_OUT + g
    new = lambda c: pl.BlockSpec((1, t_new, D_ATT_OUT), lambda i: (i, 0, c))
    out = pl.BlockSpec((1, t_new, D_ATT_OUT), lambda i: (i, 0, 0))
    o, lse = pl.pallas_call(
        functools.partial(_attn_sample_body, d=d, t_new=t_new, wb=wb),
        out_shape=(jax.ShapeDtypeStruct((b, t_new, D_ATT_OUT), F32),
                   jax.ShapeDtypeStruct((b, t_new, D_ATT_OUT), F32)),
        grid=(b,),
        in_specs=[new(cq), new(ck), new(cv),
                  pl.BlockSpec((1, wb, 2 * D_ATT_OUT), lambda i: (i, 0, 0))],
        out_specs=(out, out),
        compiler_params=_cparams(("parallel",)),
        name=f"attn_sample_d{d}",
    )(z3, z3, z3, cache2)
    return o.reshape(b * t_new, D_ATT_OUT), lse.reshape(b * t_new, D_ATT_OUT)


def _mem_attn_body(q_ref, k_ref, v_ref, o_ref):
    s = _dot_nt(q_ref[0].astype(BF16), k_ref[0].astype(BF16)) * MEM_SCALE
    m = jnp.max(s, axis=-1, keepdims=True)
    p = jnp.exp(s - m)
    l = jnp.sum(p, axis=-1, keepdims=True)
    o = jnp.dot(p.astype(BF16), v_ref[0].astype(BF16), preferred_element_type=F32) / l
    o_ref[0] = o.astype(o_ref.dtype)


def _mem_attn(z3, mem_kv, tm, out_dtype):
    b, t, _ = z3.shape
    cq = COL_QM // MEM_HEAD_DIM
    return pl.pallas_call(
        _mem_attn_body,
        out_shape=jax.ShapeDtypeStruct((b, t, D_MEM), out_dtype),
        grid=(b, t // tm, MEM_HEADS),
        in_specs=[pl.BlockSpec((1, tm, MEM_HEAD_DIM), lambda i, j, h: (i, j, cq + h)),
                  pl.BlockSpec((1, N_MEM, MEM_HEAD_DIM), lambda i, j, h: (i, 0, h)),
                  pl.BlockSpec((1, N_MEM, MEM_HEAD_DIM), lambda i, j, h: (i, 0, MEM_HEADS + h))],
        out_specs=pl.BlockSpec((1, tm, MEM_HEAD_DIM), lambda i, j, h: (i, j, h)),
        compiler_params=_cparams(("parallel", "parallel", "arbitrary")),
        name="mem_attn",
    )(z3, mem_kv, mem_kv)


def _branch_body(al_ref, o0_ref, o1_ref, o2_ref, l0_ref, l1_ref, l2_ref, mem_ref, gt_ref,
                 wl_ref, wa_ref, wm_ref, out_ref):
    l0, l1, l2 = l0_ref[...], l1_ref[...], l2_ref[...]
    m = jnp.maximum(jnp.maximum(l0, l1), l2)
    e0, e1, e2 = jnp.exp(l0 - m), jnp.exp(l1 - m), jnp.exp(l2 - m)
    att = (e0 * o0_ref[...].astype(F32) + e1 * o1_ref[...].astype(F32) + e2 * o2_ref[...].astype(F32)) / (e0 + e1 + e2)
    acc = gt_ref[:, 0:D_MODEL].astype(F32) * jnp.dot(al_ref[...].astype(BF16), wl_ref[...], preferred_element_type=F32)
    acc = acc + gt_ref[:, D_MODEL:2 * D_MODEL].astype(F32) * jnp.dot(att.astype(BF16), wa_ref[...], preferred_element_type=F32)
    acc = acc + gt_ref[:, 2 * D_MODEL:3 * D_MODEL].astype(F32) * jnp.dot(mem_ref[...].astype(BF16), wm_ref[...], preferred_element_type=F32)
    out_ref[...] = acc.astype(out_ref.dtype)


def _branch_merge(a_lru, os_, ls_, mem, gates, wl, wa, wm, tm):
    n = a_lru.shape[0]
    row = lambda w: pl.BlockSpec((tm, w), lambda i: (i, 0))
    full = lambda a: pl.BlockSpec(a.shape, lambda i: (0, 0))
    return pl.pallas_call(
        _branch_body,
        out_shape=jax.ShapeDtypeStruct((n, D_MODEL), BF16),
        grid=(n // tm,),
        in_specs=[row(D_RNN)] + [row(D_ATT_OUT)] * 6 + [row(D_MEM), row(N_GATE_COLS), full(wl), full(wa), full(wm)],
        out_specs=row(D_MODEL),
        compiler_params=_cparams(("parallel",)),
        name="branch_merge",
    )(a_lru, *os_, *ls_, mem, gates, wl, wa, wm)


def _split_bf16(x):
    hi = x.astype(BF16)
    return hi, (x - hi.astype(F32)).astype(BF16)


def _route_rows(lg):
    g = [lg[i:i + 1, :] for i in range(N_EXPERT_GROUPS)]
    gmax = jnp.maximum(jnp.maximum(g[0], g[1]), jnp.maximum(g[2], g[3]))
    gidx = jnp.where(g[0] == gmax, 0.0, jnp.where(g[1] == gmax, 1.0, jnp.where(g[2] == gmax, 2.0, 3.0)))
    g_p = 1.0 / (jnp.exp(g[0] - gmax) + jnp.exp(g[1] - gmax) + jnp.exp(g[2] - gmax) + jnp.exp(g[3] - gmax))
    e = []
    for k in range(EXPERTS_PER_GROUP):
        rows = [lg[N_EXPERT_GROUPS + gg * EXPERTS_PER_GROUP + k:N_EXPERT_GROUPS + gg * EXPERTS_PER_GROUP + k + 1, :]
                for gg in range(N_EXPERT_GROUPS)]
        e.append(jnp.where(gidx == 0.0, rows[0], jnp.where(gidx == 1.0, rows[1], jnp.where(gidx == 2.0, rows[2], rows[3]))))

    def first_argmax(v):
        mx = jnp.maximum(jnp.maximum(v[0], v[1]), jnp.maximum(v[2], v[3]))
        ix = jnp.where(v[0] == mx, 0.0, jnp.where(v[1] == mx, 1.0, jnp.where(v[2] == mx, 2.0, 3.0)))
        return mx, ix

    v1, i1 = first_argmax(e)
    v2, i2 = first_argmax([jnp.where(i1 == float(k), -jnp.inf, e[k]) for k in range(EXPERTS_PER_GROUP)])
    ex = jnp.exp(v2 - v1)
    w1 = g_p / (1.0 + ex)
    w2 = g_p * ex / (1.0 + ex)
    base = gidx * float(EXPERTS_PER_GROUP)
    zero = jnp.zeros_like(w1)
    return jnp.concatenate([base + i1, base + i2, w1, w2, zero, zero, zero, zero], axis=0)


def _proj_ln_body(mg_ref, x_ref, wo_ref, g_ref, b_ref, wr_ref, br_ref, x1_ref, x1b_ref, meta_ref):
    mix = jnp.dot(mg_ref[...], wo_ref[...], preferred_element_type=F32)
    x1 = _layer_norm(DN_ALPHA * x_ref[...] + mix, g_ref[...], b_ref[...])
    x1_ref[...] = x1
    x1b_ref[...] = x1.astype(BF16)
    xh, xl = _split_bf16(x1)
    wh, wl = _split_bf16(wr_ref[...])
    lg = _dot_nt(wh, xh) + (_dot_nt(wh, xl) + _dot_nt(wl, xh)) + br_ref[...]
    meta_ref[...] = _route_rows(lg)


def _proj_ln(merged, x, wo, g, b, wr, br, tm):
    n = merged.shape[0]
    row = lambda w: pl.BlockSpec((tm, w), lambda i: (i, 0))
    full = lambda a: pl.BlockSpec(a.shape, lambda i: (0, 0))
    return pl.pallas_call(
        _proj_ln_body,
        out_shape=(jax.ShapeDtypeStruct((n, D_MODEL), F32), jax.ShapeDtypeStruct((n, D_MODEL), BF16),
                   jax.ShapeDtypeStruct((8, n), F32)),
        grid=(n // tm,),
        in_specs=[row(D_MODEL), row(D_MODEL), full(wo), full(g), full(b), full(wr), full(br)],
        out_specs=(row(D_MODEL), row(D_MODEL), pl.BlockSpec((8, tm), lambda i: (0, i))),
        compiler_params=_cparams(("parallel",)),
        name="proj_ln_router",
    )(merged, x, wo, g, b, wr, br)


def _moe_body(x1b_ref, x1_ref, meta_ref, wg_ref, wu_ref, wd_ref, g_ref, b_ref, o_ref, acc_s):
    e = pl.program_id(1)

    @pl.when(e == 0)
    def _():
        acc_s[...] = jnp.zeros_like(acc_s)

    xb = x1b_ref[...]
    hid = _gelu(jnp.dot(xb, wg_ref[0], preferred_element_type=F32)) * jnp.dot(xb, wu_ref[0], preferred_element_type=F32)
    y = jnp.dot(hid.astype(BF16), wd_ref[0], preferred_element_type=F32)
    ef = e.astype(F32)
    meta = meta_ref[...]
    dw = jnp.where(meta[:, 0:1] == ef, meta[:, 2:3], 0.0) + jnp.where(meta[:, 1:2] == ef, meta[:, 3:4], 0.0)
    acc_s[...] += dw * y

    @pl.when(e == N_EXPERTS - 1)
    def _():
        o_ref[...] = _layer_norm(DN_ALPHA * x1_ref[...] + acc_s[...], g_ref[...], b_ref[...])


def _moe(x1b, x1, meta_n, wg, wu, wd, g, b, tm):
    n = x1.shape[0]
    row = lambda w: pl.BlockSpec((tm, w), lambda i, e: (i, 0))
    vec = lambda: pl.BlockSpec((1, D_MODEL), lambda i, e: (0, 0))
    return pl.pallas_call(
        _moe_body,
        out_shape=jax.ShapeDtypeStruct((n, D_MODEL), F32),
        grid=(n // tm, N_EXPERTS),
        in_specs=[row(D_MODEL), row(D_MODEL), row(8),
                  pl.BlockSpec((1, D_MODEL, D_EXPERT), lambda i, e: (e, 0, 0)),
                  pl.BlockSpec((1, D_MODEL, D_EXPERT), lambda i, e: (e, 0, 0)),
                  pl.BlockSpec((1, D_EXPERT, D_MODEL), lambda i, e: (e, 0, 0)),
                  vec(), vec()],
        out_specs=row(D_MODEL),
        scratch_shapes=[pltpu.VMEM((tm, D_MODEL), F32)],
        compiler_params=_cparams(("parallel", "arbitrary")),
        name="moe_ln",
    )(x1b, x1, meta_n, wg, wu, wd, g, b)


def _tile(m, cap):
    t = min(m, cap)
    assert m % t == 0, (m, t)
    return t


def _layer(x, conv_buf, h0, mem_kv, kv_bufs, p):
    b, t, _ = x.shape
    n = b * t
    x2 = x.reshape(n, D_MODEL)
    xb = x2.astype(BF16)
    tm_a = _tile(n, 1024)
    zmain = _matmul(xb, p["w_in"], 0, Z_MAIN, F32, tm_a, 512, "in_proj")
    gates = _matmul(xb, p["w_in"], Z_MAIN, N_GATE_COLS, BF16, tm_a, 512, "gate_proj", bias=p["b_gates"])
    z3 = zmain.reshape(b, t, Z_MAIN)

    a_lru, h_last = _rglru(z3, conv_buf, h0.reshape(b, 1, D_RNN), p["conv_w"], p["conv_b"], p["wax"],
                           p["b_a"], p["b_x"], p["lam"], _tile(t, 256))

    os_, ls_ = [], []
    for g, d in enumerate(DILATIONS):
        if kv_bufs is None:
            o, lse = _attn_prompt(zmain, b, t, g, d, _tile(t // d, 1024))
        else:
            o, lse = _attn_sample(z3, kv_bufs[g], g, d)
        os_.append(o)
        ls_.append(lse)

    mem = _mem_attn(z3, mem_kv, _tile(t, 512), BF16 if t % 16 == 0 else F32).reshape(n, D_MEM)

    tm = _tile(n, 256)
    merged = _branch_merge(a_lru.reshape(n, D_RNN), os_, ls_, mem, gates, p["w_br_lru"], p["w_br_att"], p["w_br_mem"], tm)
    x1, x1b, meta_t = _proj_ln(merged, x2, p["w_o"], p["ln1_g"], p["ln1_b"], p["w_router"], p["b_router"], tm)
    y = _moe(x1b, x1, meta_t.T, p["w_gate"], p["w_up"], p["w_down"], p["ln2_g"], p["ln2_b"], _tile(n, 512))
    return y.reshape(b, t, D_MODEL), z3, h_last.reshape(b, D_RNN)


def _kv_rows(z3, g, lo):
    b, t, _ = z3.shape
    k = z3[:, lo:, COL_K + g * D_ATT_OUT:COL_K + (g + 1) * D_ATT_OUT].reshape(b, t - lo, HEADS, HEAD_DIM)
    v = z3[:, lo:, COL_V + g * D_ATT_OUT:COL_V + (g + 1) * D_ATT_OUT].reshape(b, t - lo, HEADS, HEAD_DIM)
    return jnp.stack([k, v], axis=2)


def _block_diag_gates(w_a, w_x):
    per = RNN_CHUNK // RNN_BLOCK
    chunks = []
    for c in range(N_RNN_CHUNKS):
        halves = []
        for w in (w_a, w_x):
            m = jnp.zeros((RNN_CHUNK, RNN_CHUNK), F32)
            for i in range(per):
                m = lax.dynamic_update_slice(m, w[c * per + i], (i * RNN_BLOCK, i * RNN_BLOCK))
            halves.append(m)
        chunks.append(jnp.concatenate(halves, axis=1))
    return jnp.stack(chunks).astype(BF16)


def kernel(x_prompt, x_sample, cache_kv_w128, cache_kv_w512, cache_kv_w2048, cache_mem_kv, state_h, state_conv, mem_prompt, w_in, b_gates, conv_w, conv_b, w_a, b_a, w_x, b_x, lru_lambda, w_br_lru, w_br_att, w_br_mem, w_o, w_mem_kv, ln1_g, ln1_b, w_rg, b_rg, w_re, b_re, w_gate, w_up, w_down, ln2_g, ln2_b):
    row = lambda v: v.reshape(1, -1).astype(F32)
    w_router = jnp.zeros((ROUTER_ROWS, D_MODEL), F32)
    w_router = w_router.at[:N_EXPERT_GROUPS].set(w_rg.T).at[N_EXPERT_GROUPS:N_EXPERT_GROUPS + N_EXPERTS].set(w_re.T)
    b_router = jnp.zeros((ROUTER_ROWS, 1), F32)
    b_router = b_router.at[:N_EXPERT_GROUPS, 0].set(b_rg).at[N_EXPERT_GROUPS:N_EXPERT_GROUPS + N_EXPERTS, 0].set(b_re)
    p = dict(
        w_in=w_in, b_gates=row(b_gates), conv_w=conv_w, conv_b=row(conv_b), wax=_block_diag_gates(w_a, w_x),
        b_a=row(b_a), b_x=row(b_x), lam=row(lru_lambda),
        w_br_lru=w_br_lru.astype(BF16), w_br_att=w_br_att.astype(BF16), w_br_mem=w_br_mem.astype(BF16),
        w_o=w_o.astype(BF16), ln1_g=row(ln1_g), ln1_b=row(ln1_b), w_router=w_router, b_router=b_router,
        w_gate=w_gate.astype(BF16), w_up=w_up.astype(BF16), w_down=w_down.astype(BF16),
        ln2_g=row(ln2_g), ln2_b=row(ln2_b))

    bp, s, _ = x_prompt.shape
    bs, ts, _ = x_sample.shape
    mem_rows = mem_prompt.reshape(bp * N_MEM, D_MODEL)
    mem_kv_p = _matmul(mem_rows, w_mem_kv, 0, 2 * D_MEM, F32, _tile(bp * N_MEM, 512), 512, "mem_kv_proj")
    mem_kv_p = mem_kv_p.reshape(bp, N_MEM, 2 * D_MEM)

    y_p, z_p, h_p = _layer(x_prompt, jnp.zeros((bp, CONV_W - 1, D_RNN), F32), jnp.zeros((bp, D_RNN), F32),
                           mem_kv_p, None, p)
    y_s, z_s, h_s = _layer(x_sample, state_conv, state_h, cache_mem_kv.reshape(bs, N_MEM, 2 * D_MEM),
                           (cache_kv_w128, cache_kv_w512, cache_kv_w2048), p)

    conv_p = jnp.concatenate([jnp.zeros((bp, CONV_W - 1, D_RNN), F32), z_p[:, :, :D_RNN]], axis=1)[:, -(CONV_W - 1):]
    conv_s = jnp.concatenate([state_conv, z_s[:, :, :D_RNN]], axis=1)[:, -(CONV_W - 1):]
    kv_p = [_kv_rows(z_p, g, s - min(KEYS * d, s)) for g, d in enumerate(DILATIONS)]
    kv_s = [_kv_rows(z_s, g, 0) for g in range(N_GROUPS)]
    return (y_p, y_s, kv_p[0], kv_p[1], kv_p[2],
            mem_kv_p.reshape(bp, N_MEM, 2, MEM_HEADS, MEM_HEAD_DIM), h_p, conv_p,
            kv_s[0], kv_s[1], kv_s[2], h_s, conv_s)
```

```python
import functools

import jax
import jax.numpy as jnp
from jax import lax
from jax.experimental import pallas as pl
from jax.experimental.pallas import tpu as pltpu

F32 = jnp.float32
BF16 = jnp.bfloat16

D_MODEL = 2048
D_RNN = 1536
N_RNN_BLOCKS = 16
RNN_BLOCK = D_RNN // N_RNN_BLOCKS
RNN_CHUNK = 384
N_RNN_CHUNKS = D_RNN // RNN_CHUNK
CONV_W = 4
LRU_C = 8.0
HEAD_DIM = 128
HEADS = 4
DILATIONS = (1, 4, 16)
KEYS = 128
N_GROUPS = 3
D_ATT_OUT = HEADS * HEAD_DIM
N_QKV_HEADS = 3 * N_GROUPS * HEADS
ATT_SCALE = HEAD_DIM ** -0.5
ATT_TILE = KEYS * max(DILATIONS)
N_MEM = 256
MEM_HEADS = 4
MEM_HEAD_DIM = 256
D_MEM = MEM_HEADS * MEM_HEAD_DIM
MEM_SCALE = MEM_HEAD_DIM ** -0.5
N_EXPERT_GROUPS = 4
EXPERTS_PER_GROUP = 4
N_EXPERTS = 16
D_EXPERT = 512
MOE_TILE = 256
DN_ALPHA = 2.0 ** 0.25
LN_EPS = 1e-5
NEG_INF = -1e30

COL_Q = 2 * D_RNN
COL_QM = COL_Q + N_QKV_HEADS * HEAD_DIM
COL_GATES = COL_QM + D_MEM
N_GATE_COLS = 3 * D_MODEL
ROUTER_ROWS = 32

VMEM_LIMIT = 56 * 1024 * 1024


def _cparams(sem):
    return pltpu.CompilerParams(dimension_semantics=sem, vmem_limit_bytes=VMEM_LIMIT)


def _gelu(x):
    return 0.5 * x * (1.0 + jnp.tanh(0.7978845608028654 * (x + 0.044715 * (x * x * x))))


def _layer_norm(x, g, b):
    mu = jnp.mean(x, axis=-1, keepdims=True)
    xc = x - mu
    var = jnp.mean(xc * xc, axis=-1, keepdims=True)
    return xc * lax.rsqrt(var + LN_EPS) * g + b


def _dot_nt(a, b):
    return lax.dot_general(a, b, (((1,), (1,)), ((), ())), preferred_element_type=F32)


def _tile(m, cap):
    t = min(m, cap)
    assert m % t == 0, (m, t)
    return t


def _mm_body(x_ref, w_ref, *rest, split, gate):
    acc = jnp.dot(x_ref[...].astype(BF16), w_ref[...].astype(BF16), preferred_element_type=F32)
    if gate:
        b_ref, o_ref = rest
        acc = jax.nn.sigmoid(acc + b_ref[...])
    else:
        (o_ref,) = rest
    if split == 1:
        o_ref[...] = acc.astype(o_ref.dtype)
    else:
        w = acc.shape[1] // split
        for s in range(split):
            o_ref[s] = acc[:, s * w:(s + 1) * w].astype(o_ref.dtype)


def _matmul(x, w, col_off, n_cols, out_dtype, tm, tn, name, bias=None, split=1):
    m, k = x.shape
    cb = col_off // tn
    in_specs = [pl.BlockSpec((tm, k), lambda i, j: (i, 0)),
                pl.BlockSpec((k, tn), lambda i, j: (0, j + cb))]
    args = [x, w]
    if bias is not None:
        in_specs.append(pl.BlockSpec((1, tn), lambda i, j: (0, j)))
        args.append(bias)
    if split == 1:
        out_shape = jax.ShapeDtypeStruct((m, n_cols), out_dtype)
        out_spec = pl.BlockSpec((tm, tn), lambda i, j: (i, j))
    else:
        out_shape = jax.ShapeDtypeStruct((n_cols * split // tn, m, tn // split), out_dtype)
        out_spec = pl.BlockSpec((split, tm, tn // split), lambda i, j: (j, i, 0))
    return pl.pallas_call(
        functools.partial(_mm_body, split=split, gate=bias is not None),
        out_shape=out_shape,
        grid=(m // tm, n_cols // tn),
        in_specs=in_specs,
        out_specs=out_spec,
        compiler_params=_cparams(("parallel", "arbitrary")),
        name=name,
    )(*args)


def _rglru_body(xr_ref, xg_ref, cbuf_ref, h0_ref, cw_ref, cb_ref, wax_ref, ba_ref, bx_ref, lam_ref,
                out_ref, hl_ref, ext_s, a_s, u_s, h_s, *, tt):
    t = pl.program_id(1)

    @pl.when(t == 0)
    def _():
        ext_s[0:8, :] = jnp.zeros((8, D_RNN), F32)
        ext_s[5:8, :] = cbuf_ref[0]
        h_s[...] = jnp.broadcast_to(h0_ref[0], (8, D_RNN))

    @pl.when(t > 0)
    def _():
        ext_s[0:8, :] = ext_s[tt:tt + 8, :]

    ext_s[8:8 + tt, :] = xr_ref[0]
    cw = cw_ref[...]
    xc = (cb_ref[...] + cw[3:4, :] * ext_s[8:8 + tt, :] + cw[2:3, :] * ext_s[7:7 + tt, :]
          + cw[1:2, :] * ext_s[6:6 + tt, :] + cw[0:1, :] * ext_s[5:5 + tt, :])
    xcb = xc.astype(BF16)
    r_parts, i_parts = [], []
    for c in range(N_RNN_CHUNKS):
        g = jnp.dot(xcb[:, c * RNN_CHUNK:(c + 1) * RNN_CHUNK], wax_ref[c], preferred_element_type=F32)
        r_parts.append(g[:, :RNN_CHUNK])
        i_parts.append(g[:, RNN_CHUNK:])
    r = jax.nn.sigmoid(jnp.concatenate(r_parts, axis=1) + ba_ref[...])
    gi = jax.nn.sigmoid(jnp.concatenate(i_parts, axis=1) + bx_ref[...])
    nl = -lam_ref[...]
    softplus = jnp.maximum(nl, 0.0) + jnp.log1p(jnp.exp(-jnp.abs(nl)))
    log_a = (-LRU_C) * r * softplus
    th = jnp.tanh(log_a)
    a_s[...] = jnp.exp(log_a)
    u_s[...] = jnp.sqrt(-2.0 * th / (1.0 - th)) * (gi * xc)

    rows = lax.broadcasted_iota(jnp.int32, (8, D_RNN), 0)

    def blk(i, h):
        r0 = pl.multiple_of(i * 8, 8)
        ab = a_s[pl.ds(r0, 8), :]
        ub = u_s[pl.ds(r0, 8), :]
        for s in (1, 2, 4):
            keep = rows >= s
            ub = ab * jnp.where(keep, pltpu.roll(ub, s, 0), 0.0) + ub
            ab = ab * jnp.where(keep, pltpu.roll(ab, s, 0), 1.0)
        hb = ab * h + ub
        u_s[pl.ds(r0, 8), :] = hb
        return jnp.broadcast_to(hb[7:8, :], (8, D_RNN))

    h_fin = lax.fori_loop(0, tt // 8, blk, h_s[...])
    h_s[...] = h_fin
    hl_ref[0] = h_fin[0:1, :]
    out_ref[0] = (u_s[...] * _gelu(xg_ref[0])).astype(out_ref.dtype)


def _rglru(xrg3, conv_buf, h0, conv_w, conv_b, wax, b_a, b_x, lam, tt):
    b, t, _ = xrg3.shape
    vec = lambda: pl.BlockSpec((1, D_RNN), lambda i, j: (0, 0))
    return pl.pallas_call(
        functools.partial(_rglru_body, tt=tt),
        out_shape=(jax.ShapeDtypeStruct((b, t, D_RNN), BF16), jax.ShapeDtypeStruct((b, 1, D_RNN), F32)),
        grid=(b, t // tt),
        in_specs=[pl.BlockSpec((1, tt, D_RNN), lambda i, j: (i, j, 0)),
                  pl.BlockSpec((1, tt, D_RNN), lambda i, j: (i, j, 1)),
                  pl.BlockSpec((1, CONV_W - 1, D_RNN), lambda i, j: (i, 0, 0)),
                  pl.BlockSpec((1, 1, D_RNN), lambda i, j: (i, 0, 0)),
                  pl.BlockSpec((CONV_W, D_RNN), lambda i, j: (0, 0)),
                  vec(),
                  pl.BlockSpec((N_RNN_CHUNKS, RNN_CHUNK, 2 * RNN_CHUNK), lambda i, j: (0, 0, 0)),
                  vec(), vec(), vec()],
        out_specs=(pl.BlockSpec((1, tt, D_RNN), lambda i, j: (i, j, 0)),
                   pl.BlockSpec((1, 1, D_RNN), lambda i, j: (i, 0, 0))),
        scratch_shapes=[pltpu.VMEM((tt + 8, D_RNN), F32), pltpu.VMEM((tt, D_RNN), F32),
                        pltpu.VMEM((tt, D_RNN), F32), pltpu.VMEM((8, D_RNN), F32)],
        compiler_params=_cparams(("parallel", "arbitrary")),
        name="rglru",
    )(xrg3, xrg3, conv_buf, h0, conv_w, conv_b, wax, b_a, b_x, lam)


def _band_block(q, k, v, has_prev):
    row = lax.broadcasted_iota(jnp.int32, (KEYS, 2 * KEYS), 0)
    col = lax.broadcasted_iota(jnp.int32, (KEYS, 2 * KEYS), 1)
    ok = jnp.logical_and(jnp.logical_and(col >= row, col <= row + KEYS), jnp.logical_or(col >= KEYS, has_prev))
    s = jnp.where(ok, _dot_nt(q, k) * ATT_SCALE, NEG_INF)
    m = jnp.max(s, axis=-1, keepdims=True)
    p = jnp.exp(s - m)
    l = jnp.sum(p, axis=-1, keepdims=True)
    o = jnp.dot(p.astype(BF16), v, preferred_element_type=F32) / l
    return o, m + jnp.log(l)


def _attn_prompt_body(*refs):
    q_refs, k_refs, v_refs, kh_refs, vh_refs = (refs[3 * i:3 * i + 3] for i in range(5))
    o_ref, og_s, lg_s = refs[15:]
    has_prev = pl.program_id(1) > 0

    def put(g, rows, o, lse):
        og_s[g, rows, :] = o
        lg_s[g, rows, :] = jnp.broadcast_to(lse, (KEYS, HEAD_DIM))

    for g, d in enumerate(DILATIONS):
        q_ref, k_ref, v_ref, kh_ref, vh_ref = q_refs[g], k_refs[g], v_refs[g], kh_refs[g], vh_refs[g]
        nqb = ATT_TILE // (d * KEYS)
        for r in range(d):
            own = pl.ds(r, KEYS, stride=d) if d > 1 else pl.ds(0, KEYS)
            kk = jnp.concatenate([kh_ref[0, own, :], k_ref[0, own, :]], axis=0).astype(BF16)
            vv = jnp.concatenate([vh_ref[0, own, :], v_ref[0, own, :]], axis=0).astype(BF16)
            o, lse = _band_block(q_ref[0, own, :].astype(BF16), kk, vv, has_prev)
            put(g, own, o, lse)
            if d == 1:
                def qb_body(qb, carry):
                    r0 = pl.multiple_of(qb * KEYS, KEYS)
                    rows = pl.ds(r0, KEYS)
                    keys = pl.ds(pl.multiple_of(r0 - KEYS, KEYS), 2 * KEYS)
                    o, lse = _band_block(q_ref[0, rows, :].astype(BF16), k_ref[0, keys, :].astype(BF16),
                                         v_ref[0, keys, :].astype(BF16), True)
                    put(g, rows, o, lse)
                    return carry
                lax.fori_loop(1, nqb, qb_body, 0)
            else:
                for qb in range(1, nqb):
                    rows = pl.ds(qb * KEYS * d + r, KEYS, stride=d)
                    keys = pl.ds((qb - 1) * KEYS * d + r, 2 * KEYS, stride=d)
                    o, lse = _band_block(q_ref[0, rows, :].astype(BF16), k_ref[0, keys, :].astype(BF16),
                                         v_ref[0, keys, :].astype(BF16), True)
                    put(g, rows, o, lse)

    l0, l1, l2 = lg_s[0], lg_s[1], lg_s[2]
    m = jnp.maximum(jnp.maximum(l0, l1), l2)
    e0, e1, e2 = jnp.exp(l0 - m), jnp.exp(l1 - m), jnp.exp(l2 - m)
    o_ref[...] = ((e0 * og_s[0] + e1 * og_s[1] + e2 * og_s[2]) / (e0 + e1 + e2)).astype(o_ref.dtype)


def _attn_prompt(qkv, b, s):
    assert s % ATT_TILE == 0
    nt = s // ATT_TILE
    n = b * s

    def cur(which, g):
        return pl.BlockSpec((1, ATT_TILE, HEAD_DIM),
                            lambda i, j, h: ((which * N_GROUPS + g) * HEADS + h, i * nt + j, 0))

    def halo(which, g):
        rows = KEYS * DILATIONS[g]
        per = ATT_TILE // rows
        return pl.BlockSpec((1, rows, HEAD_DIM),
                            lambda i, j, h: ((which * N_GROUPS + g) * HEADS + h,
                                             jnp.maximum((i * nt + j) * per - 1, 0), 0))

    in_specs = ([cur(0, g) for g in range(N_GROUPS)] + [cur(1, g) for g in range(N_GROUPS)]
                + [cur(2, g) for g in range(N_GROUPS)] + [halo(1, g) for g in range(N_GROUPS)]
                + [halo(2, g) for g in range(N_GROUPS)])
    return pl.pallas_call(
        _attn_prompt_body,
        out_shape=jax.ShapeDtypeStruct((n, D_ATT_OUT), BF16),
        grid=(b, nt, HEADS),
        in_specs=in_specs,
        out_specs=pl.BlockSpec((ATT_TILE, HEAD_DIM), lambda i, j, h: (i * nt + j, h)),
        scratch_shapes=[pltpu.VMEM((N_GROUPS, ATT_TILE, HEAD_DIM), F32),
                        pltpu.VMEM((N_GROUPS, ATT_TILE, HEAD_DIM), F32)],
        compiler_params=_cparams(("parallel", "parallel", "parallel")),
        name="attn_prompt",
    )(*([qkv] * 15))


def _attn_sample_group(q4, kn4, vn4, c_ref, d, t_new):
    wb = c_ref.shape[1]
    nr = HEADS * t_new
    zeros = jnp.zeros((t_new, HEAD_DIM), F32)
    qbd = jnp.concatenate(
        [jnp.concatenate([q4[h] if hh == h else zeros for hh in range(HEADS)], axis=1) for h in range(HEADS)],
        axis=0).astype(BF16)
    kn = jnp.concatenate([kn4[h] for h in range(HEADS)], axis=1).astype(BF16)
    vn = jnp.concatenate([vn4[h] for h in range(HEADS)], axis=1).astype(BF16)
    kc = c_ref[0, :, 0:D_ATT_OUT].astype(BF16)
    vc = c_ref[0, :, D_ATT_OUT:2 * D_ATT_OUT].astype(BF16)
    tq_c = lax.broadcasted_iota(jnp.int32, (nr, wb), 0) & (t_new - 1)
    dist_c = wb + tq_c - lax.broadcasted_iota(jnp.int32, (nr, wb), 1)
    ok_c = jnp.logical_and((dist_c & (d - 1)) == 0, dist_c <= KEYS * d)
    tq_n = lax.broadcasted_iota(jnp.int32, (nr, t_new), 0) & (t_new - 1)
    dist_n = tq_n - lax.broadcasted_iota(jnp.int32, (nr, t_new), 1)
    ok_n = jnp.logical_and(jnp.logical_and(dist_n >= 0, (dist_n & (d - 1)) == 0), dist_n <= KEYS * d)
    s_c = jnp.where(ok_c, _dot_nt(qbd, kc) * ATT_SCALE, NEG_INF)
    s_n = jnp.where(ok_n, _dot_nt(qbd, kn) * ATT_SCALE, NEG_INF)
    m = jnp.maximum(jnp.max(s_c, axis=-1, keepdims=True), jnp.max(s_n, axis=-1, keepdims=True))
    p_c = jnp.exp(s_c - m)
    p_n = jnp.exp(s_n - m)
    l = jnp.sum(p_c, axis=-1, keepdims=True) + jnp.sum(p_n, axis=-1, keepdims=True)
    o = (jnp.dot(p_c.astype(BF16), vc, preferred_element_type=F32)
         + jnp.dot(p_n.astype(BF16), vn, preferred_element_type=F32)) / l
    lse = m + jnp.log(l)
    o = jnp.concatenate(
        [o[h * t_new:(h + 1) * t_new, h * HEAD_DIM:(h + 1) * HEAD_DIM] for h in range(HEADS)], axis=1)
    lse = jnp.concatenate(
        [jnp.broadcast_to(lse[h * t_new:(h + 1) * t_new], (t_new, HEAD_DIM)) for h in range(HEADS)], axis=1)
    return o, lse


def _attn_sample_body(*refs, t_new):
    q_refs, k_refs, v_refs, c_refs = (refs[3 * i:3 * i + 3] for i in range(4))
    o_ref = refs[12]
    outs = [_attn_sample_group(q_refs[g][...], k_refs[g][...], v_refs[g][...], c_refs[g], d, t_new)
            for g, d in enumerate(DILATIONS)]
    (o0, l0), (o1, l1), (o2, l2) = outs
    m = jnp.maximum(jnp.maximum(l0, l1), l2)
    e0, e1, e2 = jnp.exp(l0 - m), jnp.exp(l1 - m), jnp.exp(l2 - m)
    o_ref[0] = (e0 * o0 + e1 * o1 + e2 * o2) / (e0 + e1 + e2)


def _attn_sample(qkv, caches, b, t_new):
    assert t_new & (t_new - 1) == 0
    new = lambda which, g: pl.BlockSpec((HEADS, t_new, HEAD_DIM), lambda i: (which * N_GROUPS + g, i, 0))
    caches2 = [c.reshape(b, c.shape[1], 2 * D_ATT_OUT) for c in caches]
    in_specs = ([new(0, g) for g in range(N_GROUPS)] + [new(1, g) for g in range(N_GROUPS)]
                + [new(2, g) for g in range(N_GROUPS)]
                + [pl.BlockSpec((1, c.shape[1], 2 * D_ATT_OUT), lambda i: (i, 0, 0)) for c in caches2])
    att = pl.pallas_call(
        functools.partial(_attn_sample_body, t_new=t_new),
        out_shape=jax.ShapeDtypeStruct((b, t_new, D_ATT_OUT), F32),
        grid=(b,),
        in_specs=in_specs,
        out_specs=pl.BlockSpec((1, t_new, D_ATT_OUT), lambda i: (i, 0, 0)),
        compiler_params=_cparams(("parallel",)),
        name="attn_sample",
    )(*([qkv] * 9), *caches2)
    return att.reshape(b * t_new, D_ATT_OUT)


def _mem_attn_body(q_ref, k_ref, v_ref, o_ref):
    s = _dot_nt(q_ref[0].astype(BF16), k_ref[0].astype(BF16)) * MEM_SCALE
    m = jnp.max(s, axis=-1, keepdims=True)
    p = jnp.exp(s - m)
    l = jnp.sum(p, axis=-1, keepdims=True)
    o = jnp.dot(p.astype(BF16), v_ref[0].astype(BF16), preferred_element_type=F32) / l
    o_ref[...] = o.astype(o_ref.dtype)


def _mem_attn(qm, mem_kv, b, t, tm, out_dtype):
    nt = t // tm
    return pl.pallas_call(
        _mem_attn_body,
        out_shape=jax.ShapeDtypeStruct((b * t, D_MEM), out_dtype),
        grid=(b, nt, MEM_HEADS),
        in_specs=[pl.BlockSpec((1, tm, MEM_HEAD_DIM), lambda i, j, h: (h, i * nt + j, 0)),
                  pl.BlockSpec((1, N_MEM, MEM_HEAD_DIM), lambda i, j, h: (i, 0, h)),
                  pl.BlockSpec((1, N_MEM, MEM_HEAD_DIM), lambda i, j, h: (i, 0, MEM_HEADS + h))],
        out_specs=pl.BlockSpec((tm, MEM_HEAD_DIM), lambda i, j, h: (i * nt + j, h)),
        compiler_params=_cparams(("parallel", "parallel", "arbitrary")),
        name="mem_attn",
    )(qm, mem_kv, mem_kv)


def _branch_body(al_ref, att_ref, mem_ref, gt_ref, wl_ref, wa_ref, wm_ref, out_ref):
    acc = gt_ref[:, 0:D_MODEL].astype(F32) * jnp.dot(al_ref[...].astype(BF16), wl_ref[...], preferred_element_type=F32)
    acc = acc + gt_ref[:, D_MODEL:2 * D_MODEL].astype(F32) * jnp.dot(att_ref[...].astype(BF16), wa_ref[...], preferred_element_type=F32)
    acc = acc + gt_ref[:, 2 * D_MODEL:3 * D_MODEL].astype(F32) * jnp.dot(mem_ref[...].astype(BF16), wm_ref[...], preferred_element_type=F32)
    out_ref[...] = acc.astype(out_ref.dtype)


def _branch_merge(a_lru, att, mem, gates, wl, wa, wm, tm):
    n = a_lru.shape[0]
    row = lambda w: pl.BlockSpec((tm, w), lambda i: (i, 0))
    full = lambda a: pl.BlockSpec(a.shape, lambda i: (0, 0))
    return pl.pallas_call(
        _branch_body,
        out_shape=jax.ShapeDtypeStruct((n, D_MODEL), BF16),
        grid=(n // tm,),
        in_specs=[row(D_RNN), row(D_ATT_OUT), row(D_MEM), row(N_GATE_COLS), full(wl), full(wa), full(wm)],
        out_specs=row(D_MODEL),
        compiler_params=_cparams(("parallel",)),
        name="branch_merge",
    )(a_lru, att, mem, gates, wl, wa, wm)


def _split_bf16(x):
    hi = x.astype(BF16)
    return hi, (x - hi.astype(F32)).astype(BF16)


def _route_rows(lg):
    g = [lg[i:i + 1, :] for i in range(N_EXPERT_GROUPS)]
    gmax = jnp.maximum(jnp.maximum(g[0], g[1]), jnp.maximum(g[2], g[3]))
    gidx = jnp.where(g[0] == gmax, 0.0, jnp.where(g[1] == gmax, 1.0, jnp.where(g[2] == gmax, 2.0, 3.0)))
    g_p = 1.0 / (jnp.exp(g[0] - gmax) + jnp.exp(g[1] - gmax) + jnp.exp(g[2] - gmax) + jnp.exp(g[3] - gmax))
    e = []
    for k in range(EXPERTS_PER_GROUP):
        rows = [lg[N_EXPERT_GROUPS + gg * EXPERTS_PER_GROUP + k:N_EXPERT_GROUPS + gg * EXPERTS_PER_GROUP + k + 1, :]
                for gg in range(N_EXPERT_GROUPS)]
        e.append(jnp.where(gidx == 0.0, rows[0], jnp.where(gidx == 1.0, rows[1], jnp.where(gidx == 2.0, rows[2], rows[3]))))

    def first_argmax(v):
        mx = jnp.maximum(jnp.maximum(v[0], v[1]), jnp.maximum(v[2], v[3]))
        ix = jnp.where(v[0] == mx, 0.0, jnp.where(v[1] == mx, 1.0, jnp.where(v[2] == mx, 2.0, 3.0)))
        return mx, ix

    v1, i1 = first_argmax(e)
    v2, i2 = first_argmax([jnp.where(i1 == float(k), -jnp.inf, e[k]) for k in range(EXPERTS_PER_GROUP)])
    ex = jnp.exp(v2 - v1)
    w1 = g_p / (1.0 + ex)
    w2 = g_p * ex / (1.0 + ex)
    base = gidx * float(EXPERTS_PER_GROUP)
    zero = jnp.zeros_like(w1)
    return jnp.concatenate([base + i1, base + i2, w1, w2, zero, zero, zero, zero], axis=0)


def _proj_ln_body(mg_ref, x_ref, wo_ref, g_ref, b_ref, wr_ref, br_ref, x1_ref, meta_ref):
    mix = jnp.dot(mg_ref[...], wo_ref[...], preferred_element_type=F32)
    x1 = _layer_norm(DN_ALPHA * x_ref[...] + mix, g_ref[...], b_ref[...])
    x1_ref[...] = x1
    xh, xl = _split_bf16(x1)
    wh, wl = _split_bf16(wr_ref[...])
    lg = _dot_nt(wh, xh) + (_dot_nt(wh, xl) + _dot_nt(wl, xh)) + br_ref[...]
    meta_ref[...] = _route_rows(lg)


def _proj_ln(merged, x, wo, g, b, wr, br, tm):
    n = merged.shape[0]
    row = lambda w: pl.BlockSpec((tm, w), lambda i: (i, 0))
    full = lambda a: pl.BlockSpec(a.shape, lambda i: (0, 0))
    return pl.pallas_call(
        _proj_ln_body,
        out_shape=(jax.ShapeDtypeStruct((n, D_MODEL), F32), jax.ShapeDtypeStruct((8, n), F32)),
        grid=(n // tm,),
        in_specs=[row(D_MODEL), row(D_MODEL), full(wo), full(g), full(b), full(wr), full(br)],
        out_specs=(row(D_MODEL), pl.BlockSpec((8, tm), lambda i: (0, i))),
        compiler_params=_cparams(("parallel",)),
        name="proj_ln_router",
    )(merged, x, wo, g, b, wr, br)


def _dispatch(meta_t, n):
    tm = MOE_TILE
    ids = meta_t[0:2].astype(jnp.int32).reshape(2 * n)
    onehot = (ids[:, None] == jnp.arange(N_EXPERTS, dtype=jnp.int32)[None, :]).astype(jnp.int32)
    csum = jnp.cumsum(onehot, axis=0)
    rank = jnp.take_along_axis(csum, ids[:, None], axis=1)[:, 0] - 1
    counts = csum[-1]
    padded = ((counts + tm - 1) // tm) * tm
    ends = jnp.cumsum(padded)
    pos = (ends - padded)[ids] + rank
    r_tot = -(-2 * n // tm) * tm + N_EXPERTS * tm
    nt = r_tot // tm
    pair = jnp.arange(2 * n, dtype=jnp.int32)
    src = jnp.zeros((r_tot,), jnp.int32).at[pos].set(pair % n, unique_indices=True)
    valid = jnp.zeros((r_tot,), jnp.int32).at[pos].set(1, unique_indices=True)
    dst = jnp.zeros((r_tot,), jnp.int32).at[pos].set(pair, unique_indices=True)
    dst = jnp.where(valid == 1, dst, 2 * n + jnp.cumsum(1 - valid) - 1)
    tile_start = jnp.arange(nt, dtype=jnp.int32) * tm
    tile_expert = jnp.minimum(jnp.sum((tile_start[:, None] >= ends[None, :]).astype(jnp.int32), axis=1), N_EXPERTS - 1)
    return tile_expert, src.reshape(nt, 1, tm), dst.reshape(nt, 1, tm)


def _moe_body(te_ref, src0_ref, srcn_ref, dst_ref, x_hbm, wg_ref, wu_ref, wd_ref, y_hbm,
              xbuf, obuf, gsem, ssem, *, nt):
    del te_ref
    tm = MOE_TILE
    i = pl.program_id(0)
    slot = lax.rem(i, 2)

    def gather(idx_ref, s):
        def body(r, c):
            pltpu.make_async_copy(x_hbm.at[pl.ds(idx_ref[0, 0, r], 1)], xbuf.at[s, pl.ds(r, 1)], gsem.at[s]).start()
            return c
        lax.fori_loop(0, tm, body, 0, unroll=8)

    def scatter_tile_copy(s):
        return pltpu.make_async_copy(obuf.at[s], y_hbm.at[pl.ds(0, tm)], ssem.at[s])

    @pl.when(i == 0)
    def _():
        gather(src0_ref, 0)

    @pl.when(i + 1 < nt)
    def _():
        gather(srcn_ref, 1 - slot)

    pltpu.make_async_copy(x_hbm.at[pl.ds(0, tm)], xbuf.at[slot], gsem.at[slot]).wait()

    @pl.when(i >= 2)
    def _():
        scatter_tile_copy(slot).wait()

    xb = xbuf[slot].astype(BF16)
    hid = _gelu(jnp.dot(xb, wg_ref[0], preferred_element_type=F32)) * jnp.dot(xb, wu_ref[0], preferred_element_type=F32)
    obuf[slot] = jnp.dot(hid.astype(BF16), wd_ref[0], preferred_element_type=F32)

    def sbody(r, c):
        pltpu.make_async_copy(obuf.at[slot, pl.ds(r, 1)], y_hbm.at[pl.ds(dst_ref[0, 0, r], 1)], ssem.at[slot]).start()
        return c
    lax.fori_loop(0, tm, sbody, 0, unroll=8)

    @pl.when(i == nt - 1)
    def _():
        scatter_tile_copy(slot).wait()
        if nt >= 2:
            scatter_tile_copy(1 - slot).wait()


def _moe_experts(x1, tile_expert, src, dst, wg, wu, wd):
    n = x1.shape[0]
    tm = MOE_TILE
    nt = src.shape[0]
    idx = lambda f: pl.BlockSpec((1, 1, tm), f, memory_space=pltpu.SMEM)
    grid_spec = pltpu.PrefetchScalarGridSpec(
        num_scalar_prefetch=1,
        grid=(nt,),
        in_specs=[idx(lambda i, te: (0, 0, 0)),
                  idx(lambda i, te: (jnp.minimum(i + 1, nt - 1), 0, 0)),
                  idx(lambda i, te: (i, 0, 0)),
                  pl.BlockSpec(memory_space=pl.ANY),
                  pl.BlockSpec((1, D_MODEL, D_EXPERT), lambda i, te: (te[i], 0, 0)),
                  pl.BlockSpec((1, D_MODEL, D_EXPERT), lambda i, te: (te[i], 0, 0)),
                  pl.BlockSpec((1, D_EXPERT, D_MODEL), lambda i, te: (te[i], 0, 0))],
        out_specs=pl.BlockSpec(memory_space=pl.ANY),
        scratch_shapes=[pltpu.VMEM((2, tm, D_MODEL), F32), pltpu.VMEM((2, tm, D_MODEL), F32),
                        pltpu.SemaphoreType.DMA((2,)), pltpu.SemaphoreType.DMA((2,))])
    return pl.pallas_call(
        functools.partial(_moe_body, nt=nt),
        out_shape=jax.ShapeDtypeStruct((nt * tm, D_MODEL), F32),
        grid_spec=grid_spec,
        compiler_params=_cparams(("arbitrary",)),
        name="moe_experts",
    )(tile_expert, src, src, dst, x1, wg, wu, wd)


def _final_body(x1_ref, y0_ref, y1_ref, meta_ref, g_ref, b_ref, o_ref):
    meta = meta_ref[...]
    moe = meta[:, 2:3] * y0_ref[...] + meta[:, 3:4] * y1_ref[...]
    o_ref[...] = _layer_norm(DN_ALPHA * x1_ref[...] + moe, g_ref[...], b_ref[...])


def _final_ln(x1, y, meta_n, g, b, tm):
    n = x1.shape[0]
    nb = n // tm
    vec = lambda: pl.BlockSpec((1, D_MODEL), lambda i: (0, 0))
    return pl.pallas_call(
        _final_body,
        out_shape=jax.ShapeDtypeStruct((n, D_MODEL), F32),
        grid=(nb,),
        in_specs=[pl.BlockSpec((tm, D_MODEL), lambda i: (i, 0)),
                  pl.BlockSpec((tm, D_MODEL), lambda i: (i, 0)),
                  pl.BlockSpec((tm, D_MODEL), lambda i: (nb + i, 0)),
                  pl.BlockSpec((tm, 8), lambda i: (i, 0)), vec(), vec()],
        out_specs=pl.BlockSpec((tm, D_MODEL), lambda i: (i, 0)),
        compiler_params=_cparams(("parallel",)),
        name="moe_combine_ln",
    )(x1, y, y, meta_n, g, b)


def _layer(x, conv_buf, h0, mem_kv, kv_bufs, p):
    b, t, _ = x.shape
    n = b * t
    x2 = x.reshape(n, D_MODEL)
    xb = x2.astype(BF16)
    tm_a = _tile(n, 1024)
    xrg = _matmul(xb, p["w_in"], 0, 2 * D_RNN, F32, tm_a, 512, "in_proj_rnn")
    qkv = _matmul(xb, p["w_in"], COL_Q, N_QKV_HEADS * HEAD_DIM, F32, tm_a, 512, "in_proj_qkv", split=4)
    qm = _matmul(xb, p["w_in"], COL_QM, D_MEM, BF16 if t % 16 == 0 else F32, tm_a, 512, "in_proj_qm", split=2)
    gates = _matmul(xb, p["w_in"], COL_GATES, N_GATE_COLS, BF16, tm_a, 512, "gate_proj", bias=p["b_gates"])

    xrg3 = xrg.reshape(b, t, 2 * D_RNN)
    a_lru, h_last = _rglru(xrg3, conv_buf, h0.reshape(b, 1, D_RNN), p["conv_w"], p["conv_b"], p["wax"],
                           p["b_a"], p["b_x"], p["lam"], _tile(t, 256))
    if kv_bufs is None:
        att = _attn_prompt(qkv, b, t)
    else:
        att = _attn_sample(qkv, kv_bufs, b, t)
    mem = _mem_attn(qm, mem_kv, b, t, _tile(t, 512), BF16 if t % 16 == 0 else F32)

    tm = _tile(n, 256)
    merged = _branch_merge(a_lru.reshape(n, D_RNN), att, mem, gates, p["w_br_lru"], p["w_br_att"], p["w_br_mem"], tm)
    x1, meta_t = _proj_ln(merged, x2, p["w_o"], p["ln1_g"], p["ln1_b"], p["w_router"], p["b_router"], tm)
    tile_expert, src, dst = _dispatch(meta_t, n)
    y2 = _moe_experts(x1, tile_expert, src, dst, p["w_gate"], p["w_up"], p["w_down"])
    y = _final_ln(x1, y2, meta_t.T, p["ln2_g"], p["ln2_b"], _tile(n, 512))
    return y.reshape(b, t, D_MODEL), xrg3, qkv, h_last.reshape(b, D_RNN)


def _kv_rows(qkv, g, b, t, lo):
    def pick(which):
        h0 = (which * N_GROUPS + g) * HEADS
        a = qkv[h0:h0 + HEADS].reshape(HEADS, b, t, HEAD_DIM)[:, :, lo:, :]
        return jnp.transpose(a, (1, 2, 0, 3))
    return jnp.stack([pick(1), pick(2)], axis=2)


def _block_diag_gates(w_a, w_x):
    per = RNN_CHUNK // RNN_BLOCK
    chunks = []
    for c in range(N_RNN_CHUNKS):
        halves = []
        for w in (w_a, w_x):
            m = jnp.zeros((RNN_CHUNK, RNN_CHUNK), F32)
            for i in range(per):
                m = lax.dynamic_update_slice(m, w[c * per + i], (i * RNN_BLOCK, i * RNN_BLOCK))
            halves.append(m)
        chunks.append(jnp.concatenate(halves, axis=1))
    return jnp.stack(chunks).astype(BF16)


def kernel(x_prompt, x_sample, cache_kv_w128, cache_kv_w512, cache_kv_w2048, cache_mem_kv, state_h, state_conv, mem_prompt, w_in, b_gates, conv_w, conv_b, w_a, b_a, w_x, b_x, lru_lambda, w_br_lru, w_br_att, w_br_mem, w_o, w_mem_kv, ln1_g, ln1_b, w_rg, b_rg, w_re, b_re, w_gate, w_up, w_down, ln2_g, ln2_b):
    row = lambda v: v.reshape(1, -1).astype(F32)
    w_router = jnp.zeros((ROUTER_ROWS, D_MODEL), F32)
    w_router = w_router.at[:N_EXPERT_GROUPS].set(w_rg.T).at[N_EXPERT_GROUPS:N_EXPERT_GROUPS + N_EXPERTS].set(w_re.T)
    b_router = jnp.zeros((ROUTER_ROWS, 1), F32)
    b_router = b_router.at[:N_EXPERT_GROUPS, 0].set(b_rg).at[N_EXPERT_GROUPS:N_EXPERT_GROUPS + N_EXPERTS, 0].set(b_re)
    p = dict(
        w_in=w_in, b_gates=row(b_gates), conv_w=conv_w, conv_b=row(conv_b), wax=_block_diag_gates(w_a, w_x),
        b_a=row(b_a), b_x=row(b_x), lam=row(lru_lambda),
        w_br_lru=w_br_lru.astype(BF16), w_br_att=w_br_att.astype(BF16), w_br_mem=w_br_mem.astype(BF16),
        w_o=w_o.astype(BF16), ln1_g=row(ln1_g), ln1_b=row(ln1_b), w_router=w_router, b_router=b_router,
        w_gate=w_gate.astype(BF16), w_up=w_up.astype(BF16), w_down=w_down.astype(BF16),
        ln2_g=row(ln2_g), ln2_b=row(ln2_b))

    bp, s, _ = x_prompt.shape
    bs, ts, _ = x_sample.shape
    mem_rows = mem_prompt.reshape(bp * N_MEM, D_MODEL)
    mem_kv_p = _matmul(mem_rows, w_mem_kv, 0, 2 * D_MEM, F32, _tile(bp * N_MEM, 512), 512, "mem_kv_proj")
    mem_kv_p = mem_kv_p.reshape(bp, N_MEM, 2 * D_MEM)

    y_p, xrg_p, qkv_p, h_p = _layer(x_prompt, jnp.zeros((bp, CONV_W - 1, D_RNN), F32), jnp.zeros((bp, D_RNN), F32),
                                    mem_kv_p, None, p)
    y_s, xrg_s, qkv_s, h_s = _layer(x_sample, state_conv, state_h, cache_mem_kv.reshape(bs, N_MEM, 2 * D_MEM),
                                    (cache_kv_w128, cache_kv_w512, cache_kv_w2048), p)

    conv_p = jnp.concatenate([jnp.zeros((bp, CONV_W - 1, D_RNN), F32), xrg_p[:, :, :D_RNN]], axis=1)[:, -(CONV_W - 1):]
    conv_s = jnp.concatenate([state_conv, xrg_s[:, :, :D_RNN]], axis=1)[:, -(CONV_W - 1):]
    kv_p = [_kv_rows(qkv_p, g, bp, s, s - min(KEYS * d, s)) for g, d in enumerate(DILATIONS)]
    kv_s = [_kv_rows(qkv_s, g, bs, ts, 0) for g in range(N_GROUPS)]
    return (y_p, y_s, kv_p[0], kv_p[1], kv_p[2],
            mem_kv_p.reshape(bp, N_MEM, 2, MEM_HEADS, MEM_HEAD_DIM), h_p, conv_p,
            kv_s[0], kv_s[1], kv_s[2], h_s, conv_s)
```

```python
import functools

import jax
import jax.numpy as jnp
from jax import lax
from jax.experimental import pallas as pl
from jax.experimental.pallas import tpu as pltpu

F32 = jnp.float32
BF16 = jnp.bfloat16

D_MODEL = 2048
D_RNN = 1536
N_RNN_BLOCKS = 16
RNN_BLOCK = D_RNN // N_RNN_BLOCKS
RNN_CHUNK = 384
N_RNN_CHUNKS = D_RNN // RNN_CHUNK
CONV_W = 4
LRU_C = 8.0
HEAD_DIM = 128
HEADS = 4
DILATIONS = (1, 4, 16)
KEYS = 128
N_GROUPS = 3
D_ATT_OUT = HEADS * HEAD_DIM
N_QKV_HEADS = 3 * N_GROUPS * HEADS
ATT_SCALE = HEAD_DIM ** -0.5
ATT_TILE = KEYS * max(DILATIONS)
N_MEM = 256
MEM_HEADS = 4
MEM_HEAD_DIM = 256
D_MEM = MEM_HEADS * MEM_HEAD_DIM
MEM_SCALE = MEM_HEAD_DIM ** -0.5
N_EXPERT_GROUPS = 4
EXPERTS_PER_GROUP = 4
N_EXPERTS = 16
D_EXPERT = 512
MOE_TILE = 256
DN_ALPHA = 2.0 ** 0.25
LN_EPS = 1e-5
NEG_INF = -1e30

COL_Q = 2 * D_RNN
COL_QM = COL_Q + N_QKV_HEADS * HEAD_DIM
COL_GATES = COL_QM + D_MEM
N_GATE_COLS = 3 * D_MODEL
ROUTER_ROWS = 32

VMEM_LIMIT = 56 * 1024 * 1024


def _cparams(sem):
    return pltpu.CompilerParams(dimension_semantics=sem, vmem_limit_bytes=VMEM_LIMIT)


def _gelu(x):
    return 0.5 * x * (1.0 + jnp.tanh(0.7978845608028654 * (x + 0.044715 * (x * x * x))))


def _layer_norm(x, g, b):
    mu = jnp.mean(x, axis=-1, keepdims=True)
    xc = x - mu
    var = jnp.mean(xc * xc, axis=-1, keepdims=True)
    return xc * lax.rsqrt(var + LN_EPS) * g + b


def _dot_nt(a, b):
    return lax.dot_general(a, b, (((1,), (1,)), ((), ())), preferred_element_type=F32)


def _tile(m, cap):
    t = min(m, cap)
    assert m % t == 0, (m, t)
    return t


def _mm_body(x_ref, w_ref, *rest, split, gate):
    acc = jnp.dot(x_ref[...].astype(BF16), w_ref[...].astype(BF16), preferred_element_type=F32)
    if gate:
        b_ref, o_ref = rest
        acc = jax.nn.sigmoid(acc + b_ref[...])
    else:
        (o_ref,) = rest
    if split == 1:
        o_ref[...] = acc.astype(o_ref.dtype)
    else:
        w = acc.shape[1] // split
        for s in range(split):
            o_ref[s] = acc[:, s * w:(s + 1) * w].astype(o_ref.dtype)


def _matmul(x, w, col_off, n_cols, out_dtype, tm, tn, name, bias=None, split=1):
    m, k = x.shape
    cb = col_off // tn
    in_specs = [pl.BlockSpec((tm, k), lambda i, j: (i, 0)),
                pl.BlockSpec((k, tn), lambda i, j: (0, j + cb))]
    args = [x, w]
    if bias is not None:
        in_specs.append(pl.BlockSpec((1, tn), lambda i, j: (0, j)))
        args.append(bias)
    if split == 1:
        out_shape = jax.ShapeDtypeStruct((m, n_cols), out_dtype)
        out_spec = pl.BlockSpec((tm, tn), lambda i, j: (i, j))
    else:
        out_shape = jax.ShapeDtypeStruct((n_cols * split // tn, m, tn // split), out_dtype)
        out_spec = pl.BlockSpec((split, tm, tn // split), lambda i, j: (j, i, 0))
    return pl.pallas_call(
        functools.partial(_mm_body, split=split, gate=bias is not None),
        out_shape=out_shape,
        grid=(m // tm, n_cols // tn),
        in_specs=in_specs,
        out_specs=out_spec,
        compiler_params=_cparams(("parallel", "arbitrary")),
        name=name,
    )(*args)


def _rglru_body(xr_ref, xg_ref, cbuf_ref, h0_ref, cw_ref, cb_ref, wax_ref, ba_ref, bx_ref, lam_ref,
                out_ref, hl_ref, ext_s, a_s, u_s, h_s, *, tt):
    t = pl.program_id(1)

    @pl.when(t == 0)
    def _():
        ext_s[0:8, :] = jnp.zeros((8, D_RNN), F32)
        ext_s[5:8, :] = cbuf_ref[0]
        h_s[...] = jnp.broadcast_to(h0_ref[0], (8, D_RNN))

    @pl.when(t > 0)
    def _():
        ext_s[0:8, :] = ext_s[tt:tt + 8, :]

    ext_s[8:8 + tt, :] = xr_ref[0]
    cw = cw_ref[...]
    xc = (cb_ref[...] + cw[3:4, :] * ext_s[8:8 + tt, :] + cw[2:3, :] * ext_s[7:7 + tt, :]
          + cw[1:2, :] * ext_s[6:6 + tt, :] + cw[0:1, :] * ext_s[5:5 + tt, :])
    xcb = xc.astype(BF16)
    r_parts, i_parts = [], []
    for c in range(N_RNN_CHUNKS):
        g = jnp.dot(xcb[:, c * RNN_CHUNK:(c + 1) * RNN_CHUNK], wax_ref[c], preferred_element_type=F32)
        r_parts.append(g[:, :RNN_CHUNK])
        i_parts.append(g[:, RNN_CHUNK:])
    r = jax.nn.sigmoid(jnp.concatenate(r_parts, axis=1) + ba_ref[...])
    gi = jax.nn.sigmoid(jnp.concatenate(i_parts, axis=1) + bx_ref[...])
    nl = -lam_ref[...]
    softplus = jnp.maximum(nl, 0.0) + jnp.log1p(jnp.exp(-jnp.abs(nl)))
    log_a = (-LRU_C) * r * softplus
    th = jnp.tanh(log_a)
    a_s[...] = jnp.exp(log_a)
    u_s[...] = jnp.sqrt(-2.0 * th / (1.0 - th)) * (gi * xc)

    rows = lax.broadcasted_iota(jnp.int32, (8, D_RNN), 0)

    def blk(i, h):
        r0 = pl.multiple_of(i * 8, 8)
        ab = a_s[pl.ds(r0, 8), :]
        ub = u_s[pl.ds(r0, 8), :]
        for s in (1, 2, 4):
            keep = rows >= s
            ub = ab * jnp.where(keep, pltpu.roll(ub, s, 0), 0.0) + ub
            ab = ab * jnp.where(keep, pltpu.roll(ab, s, 0), 1.0)
        hb = ab * h + ub
        u_s[pl.ds(r0, 8), :] = hb
        return jnp.broadcast_to(hb[7:8, :], (8, D_RNN))

    h_fin = lax.fori_loop(0, tt // 8, blk, h_s[...])
    h_s[...] = h_fin
    hl_ref[0] = h_fin[0:1, :]
    out_ref[0] = (u_s[...] * _gelu(xg_ref[0])).astype(out_ref.dtype)


def _rglru(xrg3, conv_buf, h0, conv_w, conv_b, wax, b_a, b_x, lam, tt):
    b, t, _ = xrg3.shape
    vec = lambda: pl.BlockSpec((1, D_RNN), lambda i, j: (0, 0))
    return pl.pallas_call(
        functools.partial(_rglru_body, tt=tt),
        out_shape=(jax.ShapeDtypeStruct((b, t, D_RNN), BF16), jax.ShapeDtypeStruct((b, 1, D_RNN), F32)),
        grid=(b, t // tt),
        in_specs=[pl.BlockSpec((1, tt, D_RNN), lambda i, j: (i, j, 0)),
                  pl.BlockSpec((1, tt, D_RNN), lambda i, j: (i, j, 1)),
                  pl.BlockSpec((1, CONV_W - 1, D_RNN), lambda i, j: (i, 0, 0)),
                  pl.BlockSpec((1, 1, D_RNN), lambda i, j: (i, 0, 0)),
                  pl.BlockSpec((CONV_W, D_RNN), lambda i, j: (0, 0)),
                  vec(),
                  pl.BlockSpec((N_RNN_CHUNKS, RNN_CHUNK, 2 * RNN_CHUNK), lambda i, j: (0, 0, 0)),
                  vec(), vec(), vec()],
        out_specs=(pl.BlockSpec((1, tt, D_RNN), lambda i, j: (i, j, 0)),
                   pl.BlockSpec((1, 1, D_RNN), lambda i, j: (i, 0, 0))),
        scratch_shapes=[pltpu.VMEM((tt + 8, D_RNN), F32), pltpu.VMEM((tt, D_RNN), F32),
                        pltpu.VMEM((tt, D_RNN), F32), pltpu.VMEM((8, D_RNN), F32)],
        compiler_params=_cparams(("parallel", "arbitrary")),
        name="rglru",
    )(xrg3, xrg3, conv_buf, h0, conv_w, conv_b, wax, b_a, b_x, lam)


def _band_block(q, k, v, has_prev):
    row = lax.broadcasted_iota(jnp.int32, (KEYS, 2 * KEYS), 0)
    col = lax.broadcasted_iota(jnp.int32, (KEYS, 2 * KEYS), 1)
    ok = jnp.logical_and(jnp.logical_and(col >= row, col <= row + KEYS), jnp.logical_or(col >= KEYS, has_prev))
    s = jnp.where(ok, _dot_nt(q, k) * ATT_SCALE, NEG_INF)
    m = jnp.max(s, axis=-1, keepdims=True)
    p = jnp.exp(s - m)
    l = jnp.sum(p, axis=-1, keepdims=True)
    o = jnp.dot(p.astype(BF16), v, preferred_element_type=F32) / l
    return o, m + jnp.log(l)


def _attn_prompt_body(*refs):
    q_refs, k_refs, v_refs, kh_refs, vh_refs = (refs[3 * i:3 * i + 3] for i in range(5))
    o_ref, og_s, lg_s = refs[15:]
    has_prev = pl.program_id(1) > 0

    def put(g, rows, o, lse):
        og_s[g, rows, :] = o
        lg_s[g, rows, :] = jnp.broadcast_to(lse, (KEYS, HEAD_DIM))

    for g, d in enumerate(DILATIONS):
        q_ref, k_ref, v_ref, kh_ref, vh_ref = q_refs[g], k_refs[g], v_refs[g], kh_refs[g], vh_refs[g]
        nqb = ATT_TILE // (d * KEYS)
        for r in range(d):
            own = pl.ds(r, KEYS, stride=d) if d > 1 else pl.ds(0, KEYS)
            kk = jnp.concatenate([kh_ref[0, own, :], k_ref[0, own, :]], axis=0).astype(BF16)
            vv = jnp.concatenate([vh_ref[0, own, :], v_ref[0, own, :]], axis=0).astype(BF16)
            o, lse = _band_block(q_ref[0, own, :].astype(BF16), kk, vv, has_prev)
            put(g, own, o, lse)
            if d == 1:
                def qb_body(qb, carry):
                    r0 = pl.multiple_of(qb * KEYS, KEYS)
                    rows = pl.ds(r0, KEYS)
                    keys = pl.ds(pl.multiple_of(r0 - KEYS, KEYS), 2 * KEYS)
                    o, lse = _band_block(q_ref[0, rows, :].astype(BF16), k_ref[0, keys, :].astype(BF16),
                                         v_ref[0, keys, :].astype(BF16), True)
                    put(g, rows, o, lse)
                    return carry
                lax.fori_loop(1, nqb, qb_body, 0)
            else:
                for qb in range(1, nqb):
                    rows = pl.ds(qb * KEYS * d + r, KEYS, stride=d)
                    keys = pl.ds((qb - 1) * KEYS * d + r, 2 * KEYS, stride=d)
                    o, lse = _band_block(q_ref[0, rows, :].astype(BF16), k_ref[0, keys, :].astype(BF16),
                                         v_ref[0, keys, :].astype(BF16), True)
                    put(g, rows, o, lse)

    l0, l1, l2 = lg_s[0], lg_s[1], lg_s[2]
    m = jnp.maximum(jnp.maximum(l0, l1), l2)
    e0, e1, e2 = jnp.exp(l0 - m), jnp.exp(l1 - m), jnp.exp(l2 - m)
    o_ref[...] = ((e0 * og_s[0] + e1 * og_s[1] + e2 * og_s[2]) / (e0 + e1 + e2)).astype(o_ref.dtype)


def _attn_prompt(qkv, b, s):
    assert s % ATT_TILE == 0
    nt = s // ATT_TILE
    n = b * s

    def cur(which, g):
        return pl.BlockSpec((1, ATT_TILE, HEAD_DIM),
                            lambda i, j, h: ((which * N_GROUPS + g) * HEADS + h, i * nt + j, 0))

    def halo(which, g):
        rows = KEYS * DILATIONS[g]
        per = ATT_TILE // rows
        return pl.BlockSpec((1, rows, HEAD_DIM),
                            lambda i, j, h: ((which * N_GROUPS + g) * HEADS + h,
                                             jnp.maximum((i * nt + j) * per - 1, 0), 0))

    in_specs = ([cur(0, g) for g in range(N_GROUPS)] + [cur(1, g) for g in range(N_GROUPS)]
                + [cur(2, g) for g in range(N_GROUPS)] + [halo(1, g) for g in range(N_GROUPS)]
                + [halo(2, g) for g in range(N_GROUPS)])
    return pl.pallas_call(
        _attn_prompt_body,
        out_shape=jax.ShapeDtypeStruct((n, D_ATT_OUT), BF16),
        grid=(b, nt, HEADS),
        in_specs=in_specs,
        out_specs=pl.BlockSpec((ATT_TILE, HEAD_DIM), lambda i, j, h: (i * nt + j, h)),
        scratch_shapes=[pltpu.VMEM((N_GROUPS, ATT_TILE, HEAD_DIM), F32),
                        pltpu.VMEM((N_GROUPS, ATT_TILE, HEAD_DIM), F32)],
        compiler_params=_cparams(("parallel", "parallel", "parallel")),
        name="attn_prompt",
    )(*([qkv] * 15))


def _attn_sample_group(q4, kn4, vn4, c_ref, d, t_new):
    wb = c_ref.shape[1] // (2 * HEADS)
    nr = HEADS * t_new
    zeros = jnp.zeros((t_new, HEAD_DIM), F32)
    qbd = jnp.concatenate(
        [jnp.concatenate([q4[h] if hh == h else zeros for hh in range(HEADS)], axis=1) for h in range(HEADS)],
        axis=0).astype(BF16)
    kn = jnp.concatenate([kn4[h] for h in range(HEADS)], axis=1).astype(BF16)
    vn = jnp.concatenate([vn4[h] for h in range(HEADS)], axis=1).astype(BF16)
    cache_rows = lambda kv, h: c_ref[0, pl.ds(kv * HEADS + h, wb, stride=2 * HEADS), :]
    kc = jnp.concatenate([cache_rows(0, h) for h in range(HEADS)], axis=1).astype(BF16)
    vc = jnp.concatenate([cache_rows(1, h) for h in range(HEADS)], axis=1).astype(BF16)
    tq_c = lax.broadcasted_iota(jnp.int32, (nr, wb), 0) & (t_new - 1)
    dist_c = wb + tq_c - lax.broadcasted_iota(jnp.int32, (nr, wb), 1)
    ok_c = jnp.logical_and((dist_c & (d - 1)) == 0, dist_c <= KEYS * d)
    tq_n = lax.broadcasted_iota(jnp.int32, (nr, t_new), 0) & (t_new - 1)
    dist_n = tq_n - lax.broadcasted_iota(jnp.int32, (nr, t_new), 1)
    ok_n = jnp.logical_and(jnp.logical_and(dist_n >= 0, (dist_n & (d - 1)) == 0), dist_n <= KEYS * d)
    s_c = jnp.where(ok_c, _dot_nt(qbd, kc) * ATT_SCALE, NEG_INF)
    s_n = jnp.where(ok_n, _dot_nt(qbd, kn) * ATT_SCALE, NEG_INF)
    m = jnp.maximum(jnp.max(s_c, axis=-1, keepdims=True), jnp.max(s_n, axis=-1, keepdims=True))
    p_c = jnp.exp(s_c - m)
    p_n = jnp.exp(s_n - m)
    l = jnp.sum(p_c, axis=-1, keepdims=True) + jnp.sum(p_n, axis=-1, keepdims=True)
    o = (jnp.dot(p_c.astype(BF16), vc, preferred_element_type=F32)
         + jnp.dot(p_n.astype(BF16), vn, preferred_element_type=F32)) / l
    lse = m + jnp.log(l)
    o = jnp.concatenate(
        [o[h * t_new:(h + 1) * t_new, h * HEAD_DIM:(h + 1) * HEAD_DIM] for h in range(HEADS)], axis=1)
    lse = jnp.concatenate(
        [jnp.broadcast_to(lse[h * t_new:(h + 1) * t_new], (t_new, HEAD_DIM)) for h in range(HEADS)], axis=1)
    return o, lse


def _attn_sample_body(*refs, t_new):
    q_refs, k_refs, v_refs, c_refs = (refs[3 * i:3 * i + 3] for i in range(4))
    o_ref = refs[12]
    outs = [_attn_sample_group(q_refs[g][...], k_refs[g][...], v_refs[g][...], c_refs[g], d, t_new)
            for g, d in enumerate(DILATIONS)]
    (o0, l0), (o1, l1), (o2, l2) = outs
    m = jnp.maximum(jnp.maximum(l0, l1), l2)
    e0, e1, e2 = jnp.exp(l0 - m), jnp.exp(l1 - m), jnp.exp(l2 - m)
    o_ref[0] = (e0 * o0 + e1 * o1 + e2 * o2) / (e0 + e1 + e2)


def _attn_sample(qkv, caches, b, t_new):
    assert t_new & (t_new - 1) == 0
    new = lambda which, g: pl.BlockSpec((HEADS, t_new, HEAD_DIM), lambda i: (which * N_GROUPS + g, i, 0))
    caches2 = [c.reshape(b, c.shape[1] * 2 * HEADS, HEAD_DIM) for c in caches]
    in_specs = ([new(0, g) for g in range(N_GROUPS)] + [new(1, g) for g in range(N_GROUPS)]
                + [new(2, g) for g in range(N_GROUPS)]
                + [pl.BlockSpec((1, c.shape[1], HEAD_DIM), lambda i: (i, 0, 0)) for c in caches2])
    att = pl.pallas_call(
        functools.partial(_attn_sample_body, t_new=t_new),
        out_shape=jax.ShapeDtypeStruct((b, t_new, D_ATT_OUT), F32),
        grid=(b,),
        in_specs=in_specs,
        out_specs=pl.BlockSpec((1, t_new, D_ATT_OUT), lambda i: (i, 0, 0)),
        compiler_params=_cparams(("parallel",)),
        name="attn_sample",
    )(*([qkv] * 9), *caches2)
    return att.reshape(b * t_new, D_ATT_OUT)


def _mem_attn_body(q_ref, k_ref, v_ref, o_ref):
    s = _dot_nt(q_ref[0].astype(BF16), k_ref[0].astype(BF16)) * MEM_SCALE
    m = jnp.max(s, axis=-1, keepdims=True)
    p = jnp.exp(s - m)
    l = jnp.sum(p, axis=-1, keepdims=True)
    o = jnp.dot(p.astype(BF16), v_ref[0].astype(BF16), preferred_element_type=F32) / l
    o_ref[...] = o.astype(o_ref.dtype)


def _mem_attn(qm, mem_kv, b, t, tm, out_dtype):
    nt = t // tm
    return pl.pallas_call(
        _mem_attn_body,
        out_shape=jax.ShapeDtypeStruct((b * t, D_MEM), out_dtype),
        grid=(b, nt, MEM_HEADS),
        in_specs=[pl.BlockSpec((1, tm, MEM_HEAD_DIM), lambda i, j, h: (h, i * nt + j, 0)),
                  pl.BlockSpec((1, N_MEM, MEM_HEAD_DIM), lambda i, j, h: (i, 0, h)),
                  pl.BlockSpec((1, N_MEM, MEM_HEAD_DIM), lambda i, j, h: (i, 0, MEM_HEADS + h))],
        out_specs=pl.BlockSpec((tm, MEM_HEAD_DIM), lambda i, j, h: (i * nt + j, h)),
        compiler_params=_cparams(("parallel", "parallel", "arbitrary")),
        name="mem_attn",
    )(qm, mem_kv, mem_kv)


def _branch_body(al_ref, att_ref, mem_ref, gt_ref, wl_ref, wa_ref, wm_ref, out_ref):
    acc = gt_ref[:, 0:D_MODEL].astype(F32) * jnp.dot(al_ref[...].astype(BF16), wl_ref[...], preferred_element_type=F32)
    acc = acc + gt_ref[:, D_MODEL:2 * D_MODEL].astype(F32) * jnp.dot(att_ref[...].astype(BF16), wa_ref[...], preferred_element_type=F32)
    acc = acc + gt_ref[:, 2 * D_MODEL:3 * D_MODEL].astype(F32) * jnp.dot(mem_ref[...].astype(BF16), wm_ref[...], preferred_element_type=F32)
    out_ref[...] = acc.astype(out_ref.dtype)


def _branch_merge(a_lru, att, mem, gates, wl, wa, wm, tm):
    n = a_lru.shape[0]
    row = lambda w: pl.BlockSpec((tm, w), lambda i: (i, 0))
    full = lambda a: pl.BlockSpec(a.shape, lambda i: (0, 0))
    return pl.pallas_call(
        _branch_body,
        out_shape=jax.ShapeDtypeStruct((n, D_MODEL), BF16),
        grid=(n // tm,),
        in_specs=[row(D_RNN), row(D_ATT_OUT), row(D_MEM), row(N_GATE_COLS), full(wl), full(wa), full(wm)],
        out_specs=row(D_MODEL),
        compiler_params=_cparams(("parallel",)),
        name="branch_merge",
    )(a_lru, att, mem, gates, wl, wa, wm)


def _split_bf16(x):
    hi = x.astype(BF16)
    return hi, (x - hi.astype(F32)).astype(BF16)


def _route_rows(lg):
    g = [lg[i:i + 1, :] for i in range(N_EXPERT_GROUPS)]
    gmax = jnp.maximum(jnp.maximum(g[0], g[1]), jnp.maximum(g[2], g[3]))
    gidx = jnp.where(g[0] == gmax, 0.0, jnp.where(g[1] == gmax, 1.0, jnp.where(g[2] == gmax, 2.0, 3.0)))
    g_p = 1.0 / (jnp.exp(g[0] - gmax) + jnp.exp(g[1] - gmax) + jnp.exp(g[2] - gmax) + jnp.exp(g[3] - gmax))
    e = []
    for k in range(EXPERTS_PER_GROUP):
        rows = [lg[N_EXPERT_GROUPS + gg * EXPERTS_PER_GROUP + k:N_EXPERT_GROUPS + gg * EXPERTS_PER_GROUP + k + 1, :]
                for gg in range(N_EXPERT_GROUPS)]
        e.append(jnp.where(gidx == 0.0, rows[0], jnp.where(gidx == 1.0, rows[1], jnp.where(gidx == 2.0, rows[2], rows[3]))))

    def first_argmax(v):
        mx = jnp.maximum(jnp.maximum(v[0], v[1]), jnp.maximum(v[2], v[3]))
        ix = jnp.where(v[0] == mx, 0.0, jnp.where(v[1] == mx, 1.0, jnp.where(v[2] == mx, 2.0, 3.0)))
        return mx, ix

    v1, i1 = first_argmax(e)
    v2, i2 = first_argmax([jnp.where(i1 == float(k), -jnp.inf, e[k]) for k in range(EXPERTS_PER_GROUP)])
    ex = jnp.exp(v2 - v1)
    w1 = g_p / (1.0 + ex)
    w2 = g_p * ex / (1.0 + ex)
    base = gidx * float(EXPERTS_PER_GROUP)
    zero = jnp.zeros_like(w1)
    return jnp.concatenate([base + i1, base + i2, w1, w2, zero, zero, zero, zero], axis=0)


def _proj_ln_body(mg_ref, x_ref, wo_ref, g_ref, b_ref, wr_ref, br_ref, x1_ref, meta_ref):
    mix = jnp.dot(mg_ref[...], wo_ref[...], preferred_element_type=F32)
    x1 = _layer_norm(DN_ALPHA * x_ref[...] + mix, g_ref[...], b_ref[...])
    x1_ref[...] = x1
    xh, xl = _split_bf16(x1)
    wh, wl = _split_bf16(wr_ref[...])
    lg = _dot_nt(wh, xh) + (_dot_nt(wh, xl) + _dot_nt(wl, xh)) + br_ref[...]
    meta_ref[...] = _route_rows(lg)


def _proj_ln(merged, x, wo, g, b, wr, br, tm):
    n = merged.shape[0]
    row = lambda w: pl.BlockSpec((tm, w), lambda i: (i, 0))
    full = lambda a: pl.BlockSpec(a.shape, lambda i: (0, 0))
    return pl.pallas_call(
        _proj_ln_body,
        out_shape=(jax.ShapeDtypeStruct((n, D_MODEL), F32), jax.ShapeDtypeStruct((8, n), F32)),
        grid=(n // tm,),
        in_specs=[row(D_MODEL), row(D_MODEL), full(wo), full(g), full(b), full(wr), full(br)],
        out_specs=(row(D_MODEL), pl.BlockSpec((8, tm), lambda i: (0, i))),
        compiler_params=_cparams(("parallel",)),
        name="proj_ln_router",
    )(merged, x, wo, g, b, wr, br)


MOE_CHUNK = 8


def _local_rows(tt):
    return -(-(2 * tt + N_EXPERTS * (MOE_CHUNK - 1)) // 128) * 128


def _dispatch(meta_t, n):
    tt = _tile(n, 256)
    n_t = n // tt
    ids = meta_t[0:2].astype(jnp.int32)
    onehot = (ids[:, :, None] == jnp.arange(N_EXPERTS, dtype=jnp.int32)).astype(jnp.int32).reshape(2, n_t, tt, N_EXPERTS)
    cnt_slot = jnp.sum(onehot, axis=2)
    cnt = cnt_slot[0] + cnt_slot[1]
    pc = (cnt + MOE_CHUNK - 1) // MOE_CHUNK * MOE_CHUNK
    lstart = jnp.cumsum(pc, axis=1) - pc
    rank = jnp.cumsum(onehot, axis=2) - onehot + jnp.stack([jnp.zeros_like(cnt), cnt_slot[0]])[:, :, None, :]
    lpos = jnp.sum(onehot * (lstart[None, :, None, :] + rank), axis=-1).reshape(2, n)
    seg = jnp.sum(pc, axis=0)
    pe = (seg + MOE_TILE - 1) // MOE_TILE * MOE_TILE
    ends = jnp.cumsum(pe)
    base = ends - pe
    gstart = base[None, :] + jnp.cumsum(pc, axis=0) - pc
    r_tot = -(-(2 * n + N_EXPERTS * (MOE_CHUNK - 1) * n_t) // MOE_TILE) * MOE_TILE + N_EXPERTS * MOE_TILE
    n_tiles = r_tot // MOE_TILE
    tile_start = jnp.arange(n_tiles, dtype=jnp.int32) * MOE_TILE
    tile_expert = jnp.minimum(jnp.sum((tile_start[:, None] >= ends[None, :]).astype(jnp.int32), axis=1), N_EXPERTS - 1)
    tables = dict(
        lstart=lstart.reshape(-1), gstart=gstart.reshape(-1), nchunk=(pc // MOE_CHUNK).reshape(-1),
        zstart=jnp.concatenate([base + seg, ends[-1:]]),
        zcount=jnp.concatenate([(pe - seg) // MOE_CHUNK, (r_tot - ends[-1:]) // MOE_TILE]),
        tile_expert=tile_expert, n_used=(ends[-1] // MOE_TILE).reshape(1))
    return tt, r_tot, lpos, tables


def _seg_loop(tab, t, fn):
    lstart_ref, gstart_ref, nchunk_ref = tab
    for e in range(N_EXPERTS):
        ls = lstart_ref[t * N_EXPERTS + e]
        gs = gstart_ref[t * N_EXPERTS + e]

        def body(j, c, ls=ls, gs=gs):
            fn(pl.multiple_of(ls + j * MOE_CHUNK, MOE_CHUNK), pl.multiple_of(gs + j * MOE_CHUNK, MOE_CHUNK))
            return c
        lax.fori_loop(0, nchunk_ref[t * N_EXPERTS + e], body, 0)


def _n_chunks(nchunk_ref, t):
    tot = nchunk_ref[t * N_EXPERTS]
    for e in range(1, N_EXPERTS):
        tot = tot + nchunk_ref[t * N_EXPERTS + e]
    return tot


def _sort_body(lstart_ref, gstart_ref, nchunk_ref, zstart_ref, zcount_ref, lpos_ref, x_ref, xs_hbm,
               xloc, zbuf, sem, zsem, *, n_t):
    t = pl.program_id(0)
    slot = lax.rem(t, 2)
    tab = (lstart_ref, gstart_ref, nchunk_ref)
    rows = xloc.shape[1]

    def chunk_copy(s, lrow, grow):
        return pltpu.make_async_copy(xloc.at[s, pl.ds(lrow, MOE_CHUNK)], xs_hbm.at[pl.ds(grow, MOE_CHUNK)], sem.at[s])

    def wait_tile(s, tile):
        def body(j, c):
            chunk_copy(s, 0, 0).wait()
            return c
        lax.fori_loop(0, _n_chunks(nchunk_ref, tile), body, 0)

    @pl.when(t == 0)
    def _():
        zbuf[...] = jnp.zeros_like(zbuf)
        zero_copy = lambda grow: pltpu.make_async_copy(
            zbuf.at[pl.ds(0, MOE_CHUNK)], xs_hbm.at[pl.ds(grow, MOE_CHUNK)], zsem.at[0])
        zero_tile = lambda grow: pltpu.make_async_copy(zbuf, xs_hbm.at[pl.ds(grow, MOE_TILE)], zsem.at[0])
        tail_start = zstart_ref[N_EXPERTS]
        for e in range(N_EXPERTS):
            def zb(j, c, e=e):
                zero_copy(pl.multiple_of(zstart_ref[e] + j * MOE_CHUNK, MOE_CHUNK)).start()
                return c
            lax.fori_loop(0, zcount_ref[e], zb, 0)

        def tb(j, c):
            zero_tile(pl.multiple_of(tail_start + j * MOE_TILE, MOE_TILE)).start()
            return c
        lax.fori_loop(0, zcount_ref[N_EXPERTS], tb, 0)
        for e in range(N_EXPERTS):
            def zw(j, c):
                zero_copy(0).wait()
                return c
            lax.fori_loop(0, zcount_ref[e], zw, 0)

        def tw(j, c):
            zero_tile(0).wait()
            return c
        lax.fori_loop(0, zcount_ref[N_EXPERTS], tw, 0)

    @pl.when(t >= 2)
    def _():
        wait_tile(slot, t - 2)

    l_iota = lax.broadcasted_iota(jnp.int32, (rows, x_ref.shape[0]), 0)
    perm = jnp.logical_or(l_iota == lpos_ref[0:1, :], l_iota == lpos_ref[1:2, :])
    perm = jnp.where(perm, 1.0, 0.0).astype(BF16)
    xloc[slot] = jnp.dot(perm, x_ref[...].astype(BF16), preferred_element_type=F32)
    _seg_loop(tab, t, lambda lrow, grow: chunk_copy(slot, lrow, grow).start())

    @pl.when(t == n_t - 1)
    def _():
        wait_tile(slot, t)
        if n_t >= 2:
            wait_tile(1 - slot, t - 1)


def _moe_sort(x1, lpos, tab, tt, r_tot):
    n = x1.shape[0]
    n_t = n // tt
    rows = _local_rows(tt)
    grid_spec = pltpu.PrefetchScalarGridSpec(
        num_scalar_prefetch=5,
        grid=(n_t,),
        in_specs=[pl.BlockSpec((2, tt), lambda t, *_: (0, t)),
                  pl.BlockSpec((tt, D_MODEL), lambda t, *_: (t, 0))],
        out_specs=pl.BlockSpec(memory_space=pl.ANY),
        scratch_shapes=[pltpu.VMEM((2, rows, D_MODEL), F32), pltpu.VMEM((MOE_TILE, D_MODEL), F32),
                        pltpu.SemaphoreType.DMA((2,)), pltpu.SemaphoreType.DMA((1,))])
    return pl.pallas_call(
        functools.partial(_sort_body, n_t=n_t),
        out_shape=jax.ShapeDtypeStruct((r_tot, D_MODEL), F32),
        grid_spec=grid_spec,
        compiler_params=_cparams(("arbitrary",)),
        name="moe_sort",
    )(tab["lstart"], tab["gstart"], tab["nchunk"], tab["zstart"], tab["zcount"], lpos, x1)


def _expert_body(te_ref, nu_ref, x_ref, wg_ref, wu_ref, wd_ref, o_ref):
    del te_ref
    used = pl.program_id(0) < nu_ref[0]

    @pl.when(used)
    def _():
        xb = x_ref[...].astype(BF16)
        hid = _gelu(jnp.dot(xb, wg_ref[0], preferred_element_type=F32)) * jnp.dot(xb, wu_ref[0], preferred_element_type=F32)
        o_ref[...] = jnp.dot(hid.astype(BF16), wd_ref[0], preferred_element_type=F32)

    @pl.when(jnp.logical_not(used))
    def _():
        o_ref[...] = jnp.zeros_like(o_ref)


def _moe_experts(xs, tab, wg, wu, wd):
    r_tot = xs.shape[0]
    last = lambda i, nu: jnp.minimum(i, nu[0] - 1)
    grid_spec = pltpu.PrefetchScalarGridSpec(
        num_scalar_prefetch=2,
        grid=(r_tot // MOE_TILE,),
        in_specs=[pl.BlockSpec((MOE_TILE, D_MODEL), lambda i, te, nu: (last(i, nu), 0)),
                  pl.BlockSpec((1, D_MODEL, D_EXPERT), lambda i, te, nu: (te[last(i, nu)], 0, 0)),
                  pl.BlockSpec((1, D_MODEL, D_EXPERT), lambda i, te, nu: (te[last(i, nu)], 0, 0)),
                  pl.BlockSpec((1, D_EXPERT, D_MODEL), lambda i, te, nu: (te[last(i, nu)], 0, 0))],
        out_specs=pl.BlockSpec((MOE_TILE, D_MODEL), lambda i, te, nu: (i, 0)))
    return pl.pallas_call(
        _expert_body,
        out_shape=jax.ShapeDtypeStruct((r_tot, D_MODEL), F32),
        grid_spec=grid_spec,
        compiler_params=_cparams(("arbitrary",)),
        name="moe_experts",
    )(tab["tile_expert"], tab["n_used"], xs, wg, wu, wd)


def _combine_body(lstart_ref, gstart_ref, nchunk_ref, ys_hbm, x1_ref, meta_ref, g_ref, b_ref, o_ref,
                  yloc, sem, *, n_t):
    t = pl.program_id(0)
    slot = lax.rem(t, 2)
    tab = (lstart_ref, gstart_ref, nchunk_ref)
    rows = yloc.shape[1]

    def chunk_copy(s, lrow, grow):
        return pltpu.make_async_copy(ys_hbm.at[pl.ds(grow, MOE_CHUNK)], yloc.at[s, pl.ds(lrow, MOE_CHUNK)], sem.at[s])

    def fetch(s, tile):
        _seg_loop(tab, tile, lambda lrow, grow: chunk_copy(s, lrow, grow).start())

    @pl.when(t == 0)
    def _():
        yloc[...] = jnp.zeros_like(yloc)
        fetch(0, 0)

    @pl.when(t + 1 < n_t)
    def _():
        fetch(1 - slot, t + 1)

    def wbody(j, c):
        chunk_copy(slot, 0, 0).wait()
        return c
    lax.fori_loop(0, _n_chunks(nchunk_ref, t), wbody, 0)

    meta = meta_ref[...]
    l_iota = lax.broadcasted_iota(jnp.int32, (x1_ref.shape[0], rows), 1).astype(F32)
    yb = yloc[slot].astype(BF16)
    pick = lambda k: jnp.dot(jnp.where(l_iota == meta[:, k:k + 1], 1.0, 0.0).astype(BF16), yb, preferred_element_type=F32)
    moe = meta[:, 2:3] * pick(0) + meta[:, 3:4] * pick(1)
    o_ref[...] = _layer_norm(DN_ALPHA * x1_ref[...] + moe, g_ref[...], b_ref[...])


def _moe_combine(ys, x1, meta_n, tab, g, b, tt):
    n = x1.shape[0]
    n_t = n // tt
    rows = _local_rows(tt)
    vec = lambda: pl.BlockSpec((1, D_MODEL), lambda t, *_: (0, 0))
    grid_spec = pltpu.PrefetchScalarGridSpec(
        num_scalar_prefetch=3,
        grid=(n_t,),
        in_specs=[pl.BlockSpec(memory_space=pl.ANY),
                  pl.BlockSpec((tt, D_MODEL), lambda t, *_: (t, 0)),
                  pl.BlockSpec((tt, 8), lambda t, *_: (t, 0)), vec(), vec()],
        out_specs=pl.BlockSpec((tt, D_MODEL), lambda t, *_: (t, 0)),
        scratch_shapes=[pltpu.VMEM((2, rows, D_MODEL), F32), pltpu.SemaphoreType.DMA((2,))])
    return pl.pallas_call(
        functools.partial(_combine_body, n_t=n_t),
        out_shape=jax.ShapeDtypeStruct((n, D_MODEL), F32),
        grid_spec=grid_spec,
        compiler_params=_cparams(("arbitrary",)),
        name="moe_combine_ln",
    )(tab["lstart"], tab["gstart"], tab["nchunk"], ys, x1, meta_n, g, b)


def _moe(x1, meta_t, p):
    n = x1.shape[0]
    tt, r_tot, lpos, tab = _dispatch(meta_t, n)
    xs = _moe_sort(x1, lpos, tab, tt, r_tot)
    ys = _moe_experts(xs, tab, p["w_gate"], p["w_up"], p["w_down"])
    meta_n = jnp.concatenate([lpos.astype(F32), meta_t[2:4], jnp.zeros((4, n), F32)], axis=0).T
    return _moe_combine(ys, x1, meta_n, tab, p["ln2_g"], p["ln2_b"], tt)


def _layer(x, conv_buf, h0, mem_kv, kv_bufs, p):
    b, t, _ = x.shape
    n = b * t
    x2 = x.reshape(n, D_MODEL)
    xb = x2.astype(BF16)
    tm_a = _tile(n, 1024)
    xrg = _matmul(xb, p["w_in"], 0, 2 * D_RNN, F32, tm_a, 512, "in_proj_rnn")
    qkv = _matmul(xb, p["w_in"], COL_Q, N_QKV_HEADS * HEAD_DIM, F32, tm_a, 512, "in_proj_qkv", split=4)
    qm = _matmul(xb, p["w_in"], COL_QM, D_MEM, BF16 if t % 16 == 0 else F32, tm_a, 512, "in_proj_qm", split=2)
    gates = _matmul(xb, p["w_in"], COL_GATES, N_GATE_COLS, BF16, tm_a, 512, "gate_proj", bias=p["b_gates"])

    xrg3 = xrg.reshape(b, t, 2 * D_RNN)
    a_lru, h_last = _rglru(xrg3, conv_buf, h0.reshape(b, 1, D_RNN), p["conv_w"], p["conv_b"], p["wax"],
                           p["b_a"], p["b_x"], p["lam"], _tile(t, 256))
    if kv_bufs is None:
        att = _attn_prompt(qkv, b, t)
    else:
        att = _attn_sample(qkv, kv_bufs, b, t)
    mem = _mem_attn(qm, mem_kv, b, t, _tile(t, 512), BF16 if t % 16 == 0 else F32)

    tm = _tile(n, 256)
    merged = _branch_merge(a_lru.reshape(n, D_RNN), att, mem, gates, p["w_br_lru"], p["w_br_att"], p["w_br_mem"], tm)
    x1, meta_t = _proj_ln(merged, x2, p["w_o"], p["ln1_g"], p["ln1_b"], p["w_router"], p["b_router"], tm)
    y = _moe(x1, meta_t, p)
    return y.reshape(b, t, D_MODEL), xrg3, qkv, h_last.reshape(b, D_RNN)


def _kv_rows(qkv, g, b, t, lo):
    q4 = qkv.reshape(N_QKV_HEADS, b, t, HEAD_DIM)

    def pick(which):
        h0 = (which * N_GROUPS + g) * HEADS
        a = lax.slice(q4, (h0, 0, lo, 0), (h0 + HEADS, b, t, HEAD_DIM))
        return jnp.transpose(a, (1, 2, 0, 3))
    return jnp.stack([pick(1), pick(2)], axis=2)


def _block_diag_gates(w_a, w_x):
    per = RNN_CHUNK // RNN_BLOCK
    chunks = []
    for c in range(N_RNN_CHUNKS):
        halves = []
        for w in (w_a, w_x):
            m = jnp.zeros((RNN_CHUNK, RNN_CHUNK), F32)
            for i in range(per):
                m = lax.dynamic_update_slice(m, w[c * per + i], (i * RNN_BLOCK, i * RNN_BLOCK))
            halves.append(m)
        chunks.append(jnp.concatenate(halves, axis=1))
    return jnp.stack(chunks).astype(BF16)


def kernel(x_prompt, x_sample, cache_kv_w128, cache_kv_w512, cache_kv_w2048, cache_mem_kv, state_h, state_conv, mem_prompt, w_in, b_gates, conv_w, conv_b, w_a, b_a, w_x, b_x, lru_lambda, w_br_lru, w_br_att, w_br_mem, w_o, w_mem_kv, ln1_g, ln1_b, w_rg, b_rg, w_re, b_re, w_gate, w_up, w_down, ln2_g, ln2_b):
    row = lambda v: v.reshape(1, -1).astype(F32)
    w_router = jnp.zeros((ROUTER_ROWS, D_MODEL), F32)
    w_router = w_router.at[:N_EXPERT_GROUPS].set(w_rg.T).at[N_EXPERT_GROUPS:N_EXPERT_GROUPS + N_EXPERTS].set(w_re.T)
    b_router = jnp.zeros((ROUTER_ROWS, 1), F32)
    b_router = b_router.at[:N_EXPERT_GROUPS, 0].set(b_rg).at[N_EXPERT_GROUPS:N_EXPERT_GROUPS + N_EXPERTS, 0].set(b_re)
    p = dict(
        w_in=w_in, b_gates=row(b_gates), conv_w=conv_w, conv_b=row(conv_b), wax=_block_diag_gates(w_a, w_x),
        b_a=row(b_a), b_x=row(b_x), lam=row(lru_lambda),
        w_br_lru=w_br_lru.astype(BF16), w_br_att=w_br_att.astype(BF16), w_br_mem=w_br_mem.astype(BF16),
        w_o=w_o.astype(BF16), ln1_g=row(ln1_g), ln1_b=row(ln1_b), w_router=w_router, b_router=b_router,
        w_gate=w_gate.astype(BF16), w_up=w_up.astype(BF16), w_down=w_down.astype(BF16),
        ln2_g=row(ln2_g), ln2_b=row(ln2_b))

    bp, s, _ = x_prompt.shape
    bs, ts, _ = x_sample.shape
    mem_rows = mem_prompt.reshape(bp * N_MEM, D_MODEL)
    mem_kv_p = _matmul(mem_rows, w_mem_kv, 0, 2 * D_MEM, F32, _tile(bp * N_MEM, 512), 512, "mem_kv_proj")
    mem_kv_p = mem_kv_p.reshape(bp, N_MEM, 2 * D_MEM)

    y_p, xrg_p, qkv_p, h_p = _layer(x_prompt, jnp.zeros((bp, CONV_W - 1, D_RNN), F32), jnp.zeros((bp, D_RNN), F32),
                                    mem_kv_p, None, p)
    y_s, xrg_s, qkv_s, h_s = _layer(x_sample, state_conv, state_h, cache_mem_kv.reshape(bs, N_MEM, 2 * D_MEM),
                                    (cache_kv_w128, cache_kv_w512, cache_kv_w2048), p)

    conv_p = jnp.concatenate([jnp.zeros((bp, CONV_W - 1, D_RNN), F32), xrg_p[:, :, :D_RNN]], axis=1)[:, -(CONV_W - 1):]
    conv_s = jnp.concatenate([state_conv, xrg_s[:, :, :D_RNN]], axis=1)[:, -(CONV_W - 1):]
    kv_p = [_kv_rows(qkv_p, g, bp, s, s - min(KEYS * d, s)) for g, d in enumerate(DILATIONS)]
    kv_s = [_kv_rows(qkv_s, g, bs, ts, 0) for g in range(N_GROUPS)]
    return (y_p, y_s, kv_p[0], kv_p[1], kv_p[2],
            mem_kv_p.reshape(bp, N_MEM, 2, MEM_HEADS, MEM_HEAD_DIM), h_p, conv_p,
            kv_s[0], kv_s[1], kv_s[2], h_s, conv_s)
```

```python
import functools

import jax
import jax.numpy as jnp
from jax import lax
from jax.experimental import pallas as pl
from jax.experimental.pallas import tpu as pltpu

F32 = jnp.float32
BF16 = jnp.bfloat16

D_MODEL = 2048
D_RNN = 1536
N_RNN_BLOCKS = 16
RNN_BLOCK = D_RNN // N_RNN_BLOCKS
RNN_CHUNK = 384
N_RNN_CHUNKS = D_RNN // RNN_CHUNK
CONV_W = 4
LRU_C = 8.0
HEAD_DIM = 128
HEADS = 4
DILATIONS = (1, 4, 16)
KEYS = 128
N_GROUPS = 3
D_ATT_OUT = HEADS * HEAD_DIM
N_QKV_HEADS = 3 * N_GROUPS * HEADS
ATT_SCALE = HEAD_DIM ** -0.5
ATT_TILE = KEYS * max(DILATIONS)
N_MEM = 256
MEM_HEADS = 4
MEM_HEAD_DIM = 256
D_MEM = MEM_HEADS * MEM_HEAD_DIM
MEM_SCALE = MEM_HEAD_DIM ** -0.5
N_EXPERT_GROUPS = 4
EXPERTS_PER_GROUP = 4
N_EXPERTS = 16
D_EXPERT = 512
MOE_TILE = 256
DN_ALPHA = 2.0 ** 0.25
LN_EPS = 1e-5
NEG_INF = -1e30

COL_Q = 2 * D_RNN
COL_QM = COL_Q + N_QKV_HEADS * HEAD_DIM
COL_GATES = COL_QM + D_MEM
N_GATE_COLS = 3 * D_MODEL
ROUTER_ROWS = 32

VMEM_LIMIT = 56 * 1024 * 1024


def _cparams(sem):
    return pltpu.CompilerParams(dimension_semantics=sem, vmem_limit_bytes=VMEM_LIMIT)


def _gelu(x):
    return 0.5 * x * (1.0 + jnp.tanh(0.7978845608028654 * (x + 0.044715 * (x * x * x))))


def _layer_norm(x, g, b):
    mu = jnp.mean(x, axis=-1, keepdims=True)
    xc = x - mu
    var = jnp.mean(xc * xc, axis=-1, keepdims=True)
    return xc * lax.rsqrt(var + LN_EPS) * g + b


def _dot_nt(a, b):
    return lax.dot_general(a, b, (((1,), (1,)), ((), ())), preferred_element_type=F32)


def _tile(m, cap):
    t = min(m, cap)
    assert m % t == 0, (m, t)
    return t


def _mm_body(x_ref, w_ref, *rest, split, gate):
    acc = jnp.dot(x_ref[...].astype(BF16), w_ref[...].astype(BF16), preferred_element_type=F32)
    if gate:
        b_ref, o_ref = rest
        acc = 0.5 * jnp.tanh(0.5 * (acc + b_ref[...])) + 0.5
    else:
        (o_ref,) = rest
    if split == 1:
        o_ref[...] = acc.astype(o_ref.dtype)
    else:
        w = acc.shape[1] // split
        for s in range(split):
            o_ref[s] = acc[:, s * w:(s + 1) * w].astype(o_ref.dtype)


def _matmul(x, w, col_off, n_cols, out_dtype, tm, tn, name, bias=None, split=1):
    m, k = x.shape
    cb = col_off // tn
    in_specs = [pl.BlockSpec((tm, k), lambda i, j: (i, 0)),
                pl.BlockSpec((k, tn), lambda i, j: (0, j + cb))]
    args = [x, w]
    if bias is not None:
        in_specs.append(pl.BlockSpec((1, tn), lambda i, j: (0, j)))
        args.append(bias)
    if split == 1:
        out_shape = jax.ShapeDtypeStruct((m, n_cols), out_dtype)
        out_spec = pl.BlockSpec((tm, tn), lambda i, j: (i, j))
    else:
        out_shape = jax.ShapeDtypeStruct((n_cols * split // tn, m, tn // split), out_dtype)
        out_spec = pl.BlockSpec((split, tm, tn // split), lambda i, j: (j, i, 0))
    return pl.pallas_call(
        functools.partial(_mm_body, split=split, gate=bias is not None),
        out_shape=out_shape,
        grid=(m // tm, n_cols // tn),
        in_specs=in_specs,
        out_specs=out_spec,
        compiler_params=_cparams(("parallel", "arbitrary")),
        name=name,
    )(*args)


def _rglru_body(xr_ref, xg_ref, cbuf_ref, h0_ref, cw_ref, cb_ref, wax_ref, ba_ref, bx_ref, lam_ref,
                out_ref, hl_ref, ext_s, a_s, u_s, h_s, *, tt):
    t = pl.program_id(1)

    @pl.when(t == 0)
    def _():
        ext_s[0:8, :] = jnp.zeros((8, D_RNN), F32)
        ext_s[5:8, :] = cbuf_ref[0]
        h_s[...] = jnp.broadcast_to(h0_ref[0], (8, D_RNN))

    @pl.when(t > 0)
    def _():
        ext_s[0:8, :] = ext_s[tt:tt + 8, :]

    ext_s[8:8 + tt, :] = xr_ref[0]
    cw = cw_ref[...]
    xc = (cb_ref[...] + cw[3:4, :] * ext_s[8:8 + tt, :] + cw[2:3, :] * ext_s[7:7 + tt, :]
          + cw[1:2, :] * ext_s[6:6 + tt, :] + cw[0:1, :] * ext_s[5:5 + tt, :])
    xcb = xc.astype(BF16)
    r_parts, i_parts = [], []
    for c in range(N_RNN_CHUNKS):
        g = jnp.dot(xcb[:, c * RNN_CHUNK:(c + 1) * RNN_CHUNK], wax_ref[c], preferred_element_type=F32)
        r_parts.append(g[:, :RNN_CHUNK])
        i_parts.append(g[:, RNN_CHUNK:])
    r = jax.nn.sigmoid(jnp.concatenate(r_parts, axis=1) + ba_ref[...])
    gi = jax.nn.sigmoid(jnp.concatenate(i_parts, axis=1) + bx_ref[...])
    nl = -lam_ref[...]
    softplus = jnp.maximum(nl, 0.0) + jnp.log1p(jnp.exp(-jnp.abs(nl)))
    log_a = (-LRU_C) * r * softplus
    th = jnp.tanh(log_a)
    a_s[...] = jnp.exp(log_a)
    u_s[...] = jnp.sqrt(-2.0 * th / (1.0 - th)) * (gi * xc)

    rows = lax.broadcasted_iota(jnp.int32, (8, D_RNN), 0)

    def blk(i, h):
        r0 = pl.multiple_of(i * 8, 8)
        ab = a_s[pl.ds(r0, 8), :]
        ub = u_s[pl.ds(r0, 8), :]
        for s in (1, 2, 4):
            keep = rows >= s
            ub = ab * jnp.where(keep, pltpu.roll(ub, s, 0), 0.0) + ub
            ab = ab * jnp.where(keep, pltpu.roll(ab, s, 0), 1.0)
        hb = ab * h + ub
        u_s[pl.ds(r0, 8), :] = hb
        return jnp.broadcast_to(hb[7:8, :], (8, D_RNN))

    h_fin = lax.fori_loop(0, tt // 8, blk, h_s[...])
    h_s[...] = h_fin
    hl_ref[0] = h_fin[0:1, :]
    out_ref[0] = (u_s[...] * _gelu(xg_ref[0])).astype(out_ref.dtype)


def _rglru(xrg3, conv_buf, h0, conv_w, conv_b, wax, b_a, b_x, lam, tt):
    b, t, _ = xrg3.shape
    vec = lambda: pl.BlockSpec((1, D_RNN), lambda i, j: (0, 0))
    return pl.pallas_call(
        functools.partial(_rglru_body, tt=tt),
        out_shape=(jax.ShapeDtypeStruct((b, t, D_RNN), BF16), jax.ShapeDtypeStruct((b, 1, D_RNN), F32)),
        grid=(b, t // tt),
        in_specs=[pl.BlockSpec((1, tt, D_RNN), lambda i, j: (i, j, 0)),
                  pl.BlockSpec((1, tt, D_RNN), lambda i, j: (i, j, 1)),
                  pl.BlockSpec((1, CONV_W - 1, D_RNN), lambda i, j: (i, 0, 0)),
                  pl.BlockSpec((1, 1, D_RNN), lambda i, j: (i, 0, 0)),
                  pl.BlockSpec((CONV_W, D_RNN), lambda i, j: (0, 0)),
                  vec(),
                  pl.BlockSpec((N_RNN_CHUNKS, RNN_CHUNK, 2 * RNN_CHUNK), lambda i, j: (0, 0, 0)),
                  vec(), vec(), vec()],
        out_specs=(pl.BlockSpec((1, tt, D_RNN), lambda i, j: (i, j, 0)),
                   pl.BlockSpec((1, 1, D_RNN), lambda i, j: (i, 0, 0))),
        scratch_shapes=[pltpu.VMEM((tt + 8, D_RNN), F32), pltpu.VMEM((tt, D_RNN), F32),
                        pltpu.VMEM((tt, D_RNN), F32), pltpu.VMEM((8, D_RNN), F32)],
        compiler_params=_cparams(("parallel", "arbitrary")),
        name="rglru",
    )(xrg3, xrg3, conv_buf, h0, conv_w, conv_b, wax, b_a, b_x, lam)


def _band_block(q, k, v, bias):
    s = _dot_nt(q, k) * ATT_SCALE + bias
    m = jnp.max(s, axis=-1, keepdims=True)
    p = jnp.exp(s - m)
    l = jnp.sum(p, axis=-1, keepdims=True)
    o = jnp.dot(p.astype(BF16), v, preferred_element_type=F32) / l
    return o, m + jnp.log(l)


def _attn_prompt_body(*refs):
    q_refs, k_refs, v_refs, kh_refs, vh_refs = (refs[3 * i:3 * i + 3] for i in range(5))
    o_ref, og_s, lg_s = refs[15:]
    row = lax.broadcasted_iota(jnp.int32, (KEYS, 2 * KEYS), 0)
    col = lax.broadcasted_iota(jnp.int32, (KEYS, 2 * KEYS), 1)
    band = jnp.logical_and(col >= row, col <= row + KEYS)
    bias = jnp.where(band, 0.0, NEG_INF)
    has_prev = pl.program_id(1) > 0
    bias_first = jnp.where(jnp.logical_and(band, jnp.logical_or(col >= KEYS, has_prev)), 0.0, NEG_INF)

    def put(g, rows, o, lse):
        og_s[g, rows, :] = o
        lg_s[g, rows, :] = jnp.broadcast_to(lse, (KEYS, HEAD_DIM))

    def rows_of(start, size, d):
        return pl.ds(start, size, stride=d) if d > 1 else pl.ds(start, size)

    for g, d in enumerate(DILATIONS):
        q_ref, k_ref, v_ref, kh_ref, vh_ref = q_refs[g], k_refs[g], v_refs[g], kh_refs[g], vh_refs[g]
        nqb = ATT_TILE // (d * KEYS)
        for r in range(d):
            own = rows_of(r, KEYS, d)
            kk = jnp.concatenate([kh_ref[0, own, :], k_ref[0, own, :]], axis=0).astype(BF16)
            vv = jnp.concatenate([vh_ref[0, own, :], v_ref[0, own, :]], axis=0).astype(BF16)
            o, lse = _band_block(q_ref[0, own, :].astype(BF16), kk, vv, bias_first)
            put(g, own, o, lse)
            for qb in range(1, nqb):
                rows = rows_of(qb * KEYS * d + r, KEYS, d)
                keys = rows_of((qb - 1) * KEYS * d + r, 2 * KEYS, d)
                o, lse = _band_block(q_ref[0, rows, :].astype(BF16), k_ref[0, keys, :].astype(BF16),
                                     v_ref[0, keys, :].astype(BF16), bias)
                put(g, rows, o, lse)

    l0, l1, l2 = lg_s[0], lg_s[1], lg_s[2]
    m = jnp.maximum(jnp.maximum(l0, l1), l2)
    e0, e1, e2 = jnp.exp(l0 - m), jnp.exp(l1 - m), jnp.exp(l2 - m)
    o_ref[...] = ((e0 * og_s[0] + e1 * og_s[1] + e2 * og_s[2]) / (e0 + e1 + e2)).astype(o_ref.dtype)


def _attn_prompt(qkv, b, s):
    assert s % ATT_TILE == 0
    nt = s // ATT_TILE
    n = b * s

    def cur(which, g):
        return pl.BlockSpec((1, ATT_TILE, HEAD_DIM),
                            lambda i, j, h: ((which * N_GROUPS + g) * HEADS + h, i * nt + j, 0))

    def halo(which, g):
        rows = KEYS * DILATIONS[g]
        per = ATT_TILE // rows
        return pl.BlockSpec((1, rows, HEAD_DIM),
                            lambda i, j, h: ((which * N_GROUPS + g) * HEADS + h,
                                             jnp.maximum((i * nt + j) * per - 1, 0), 0))

    in_specs = ([cur(0, g) for g in range(N_GROUPS)] + [cur(1, g) for g in range(N_GROUPS)]
                + [cur(2, g) for g in range(N_GROUPS)] + [halo(1, g) for g in range(N_GROUPS)]
                + [halo(2, g) for g in range(N_GROUPS)])
    return pl.pallas_call(
        _attn_prompt_body,
        out_shape=jax.ShapeDtypeStruct((n, D_ATT_OUT), BF16),
        grid=(b, nt, HEADS),
        in_specs=in_specs,
        out_specs=pl.BlockSpec((ATT_TILE, HEAD_DIM), lambda i, j, h: (i * nt + j, h)),
        scratch_shapes=[pltpu.VMEM((N_GROUPS, ATT_TILE, HEAD_DIM), F32),
                        pltpu.VMEM((N_GROUPS, ATT_TILE, HEAD_DIM), F32)],
        compiler_params=_cparams(("parallel", "parallel", "parallel")),
        name="attn_prompt",
    )(*([qkv] * 15))


def _attn_sample_group(q4, kn4, vn4, c_ref, d, t_new):
    wb = c_ref.shape[1] // (2 * HEADS)
    nr = HEADS * t_new
    zeros = jnp.zeros((t_new, HEAD_DIM), F32)
    qbd = jnp.concatenate(
        [jnp.concatenate([q4[h] if hh == h else zeros for hh in range(HEADS)], axis=1) for h in range(HEADS)],
        axis=0).astype(BF16)
    kn = jnp.concatenate([kn4[h] for h in range(HEADS)], axis=1).astype(BF16)
    vn = jnp.concatenate([vn4[h] for h in range(HEADS)], axis=1).astype(BF16)
    cache_rows = lambda kv, h: c_ref[0, pl.ds(kv * HEADS + h, wb, stride=2 * HEADS), :]
    kc = jnp.concatenate([cache_rows(0, h) for h in range(HEADS)], axis=1).astype(BF16)
    vc = jnp.concatenate([cache_rows(1, h) for h in range(HEADS)], axis=1).astype(BF16)
    tq_c = lax.broadcasted_iota(jnp.int32, (nr, wb), 0) & (t_new - 1)
    dist_c = wb + tq_c - lax.broadcasted_iota(jnp.int32, (nr, wb), 1)
    ok_c = jnp.logical_and((dist_c & (d - 1)) == 0, dist_c <= KEYS * d)
    tq_n = lax.broadcasted_iota(jnp.int32, (nr, t_new), 0) & (t_new - 1)
    dist_n = tq_n - lax.broadcasted_iota(jnp.int32, (nr, t_new), 1)
    ok_n = jnp.logical_and(jnp.logical_and(dist_n >= 0, (dist_n & (d - 1)) == 0), dist_n <= KEYS * d)
    s_c = jnp.where(ok_c, _dot_nt(qbd, kc) * ATT_SCALE, NEG_INF)
    s_n = jnp.where(ok_n, _dot_nt(qbd, kn) * ATT_SCALE, NEG_INF)
    m = jnp.maximum(jnp.max(s_c, axis=-1, keepdims=True), jnp.max(s_n, axis=-1, keepdims=True))
    p_c = jnp.exp(s_c - m)
    p_n = jnp.exp(s_n - m)
    l = jnp.sum(p_c, axis=-1, keepdims=True) + jnp.sum(p_n, axis=-1, keepdims=True)
    o = (jnp.dot(p_c.astype(BF16), vc, preferred_element_type=F32)
         + jnp.dot(p_n.astype(BF16), vn, preferred_element_type=F32)) / l
    lse = m + jnp.log(l)
    o = jnp.concatenate(
        [o[h * t_new:(h + 1) * t_new, h * HEAD_DIM:(h + 1) * HEAD_DIM] for h in range(HEADS)], axis=1)
    lse = jnp.concatenate(
        [jnp.broadcast_to(lse[h * t_new:(h + 1) * t_new], (t_new, HEAD_DIM)) for h in range(HEADS)], axis=1)
    return o, lse


def _attn_sample_body(*refs, t_new):
    q_refs, k_refs, v_refs, c_refs = (refs[3 * i:3 * i + 3] for i in range(4))
    o_ref = refs[12]
    outs = [_attn_sample_group(q_refs[g][...], k_refs[g][...], v_refs[g][...], c_refs[g], d, t_new)
            for g, d in enumerate(DILATIONS)]
    (o0, l0), (o1, l1), (o2, l2) = outs
    m = jnp.maximum(jnp.maximum(l0, l1), l2)
    e0, e1, e2 = jnp.exp(l0 - m), jnp.exp(l1 - m), jnp.exp(l2 - m)
    o_ref[0] = (e0 * o0 + e1 * o1 + e2 * o2) / (e0 + e1 + e2)


def _attn_sample(qkv, caches, b, t_new):
    assert t_new & (t_new - 1) == 0
    new = lambda which, g: pl.BlockSpec((HEADS, t_new, HEAD_DIM), lambda i: (which * N_GROUPS + g, i, 0))
    caches2 = [c.reshape(b, c.shape[1] * 2 * HEADS, HEAD_DIM) for c in caches]
    in_specs = ([new(0, g) for g in range(N_GROUPS)] + [new(1, g) for g in range(N_GROUPS)]
                + [new(2, g) for g in range(N_GROUPS)]
                + [pl.BlockSpec((1, c.shape[1], HEAD_DIM), lambda i: (i, 0, 0)) for c in caches2])
    att = pl.pallas_call(
        functools.partial(_attn_sample_body, t_new=t_new),
        out_shape=jax.ShapeDtypeStruct((b, t_new, D_ATT_OUT), F32),
        grid=(b,),
        in_specs=in_specs,
        out_specs=pl.BlockSpec((1, t_new, D_ATT_OUT), lambda i: (i, 0, 0)),
        compiler_params=_cparams(("parallel",)),
        name="attn_sample",
    )(*([qkv] * 9), *caches2)
    return att.reshape(b * t_new, D_ATT_OUT)


def _mem_attn_body(q_ref, k_ref, v_ref, o_ref):
    s = _dot_nt(q_ref[0].astype(BF16), k_ref[0].astype(BF16)) * MEM_SCALE
    m = jnp.max(s, axis=-1, keepdims=True)
    p = jnp.exp(s - m)
    l = jnp.sum(p, axis=-1, keepdims=True)
    o = jnp.dot(p.astype(BF16), v_ref[0].astype(BF16), preferred_element_type=F32) / l
    o_ref[...] = o.astype(o_ref.dtype)


def _mem_attn(qm, mem_kv, b, t, tm, out_dtype):
    nt = t // tm
    return pl.pallas_call(
        _mem_attn_body,
        out_shape=jax.ShapeDtypeStruct((b * t, D_MEM), out_dtype),
        grid=(b, nt, MEM_HEADS),
        in_specs=[pl.BlockSpec((1, tm, MEM_HEAD_DIM), lambda i, j, h: (h, i * nt + j, 0)),
                  pl.BlockSpec((1, N_MEM, MEM_HEAD_DIM), lambda i, j, h: (i, 0, h)),
                  pl.BlockSpec((1, N_MEM, MEM_HEAD_DIM), lambda i, j, h: (i, 0, MEM_HEADS + h))],
        out_specs=pl.BlockSpec((tm, MEM_HEAD_DIM), lambda i, j, h: (i * nt + j, h)),
        compiler_params=_cparams(("parallel", "parallel", "arbitrary")),
        name="mem_attn",
    )(qm, mem_kv, mem_kv)


def _branch_body(al_ref, att_ref, mem_ref, gt_ref, wl_ref, wa_ref, wm_ref, out_ref):
    acc = gt_ref[:, 0:D_MODEL].astype(F32) * jnp.dot(al_ref[...].astype(BF16), wl_ref[...], preferred_element_type=F32)
    acc = acc + gt_ref[:, D_MODEL:2 * D_MODEL].astype(F32) * jnp.dot(att_ref[...].astype(BF16), wa_ref[...], preferred_element_type=F32)
    acc = acc + gt_ref[:, 2 * D_MODEL:3 * D_MODEL].astype(F32) * jnp.dot(mem_ref[...].astype(BF16), wm_ref[...], preferred_element_type=F32)
    out_ref[...] = acc.astype(out_ref.dtype)


def _branch_merge(a_lru, att, mem, gates, wl, wa, wm, tm):
    n = a_lru.shape[0]
    row = lambda w: pl.BlockSpec((tm, w), lambda i: (i, 0))
    full = lambda a: pl.BlockSpec(a.shape, lambda i: (0, 0))
    return pl.pallas_call(
        _branch_body,
        out_shape=jax.ShapeDtypeStruct((n, D_MODEL), BF16),
        grid=(n // tm,),
        in_specs=[row(D_RNN), row(D_ATT_OUT), row(D_MEM), row(N_GATE_COLS), full(wl), full(wa), full(wm)],
        out_specs=row(D_MODEL),
        compiler_params=_cparams(("parallel",)),
        name="branch_merge",
    )(a_lru, att, mem, gates, wl, wa, wm)


def _split_bf16(x):
    hi = x.astype(BF16)
    return hi, (x - hi.astype(F32)).astype(BF16)


def _route_rows(lg):
    g = [lg[i:i + 1, :] for i in range(N_EXPERT_GROUPS)]
    gmax = jnp.maximum(jnp.maximum(g[0], g[1]), jnp.maximum(g[2], g[3]))
    gidx = jnp.where(g[0] == gmax, 0.0, jnp.where(g[1] == gmax, 1.0, jnp.where(g[2] == gmax, 2.0, 3.0)))
    g_p = 1.0 / (jnp.exp(g[0] - gmax) + jnp.exp(g[1] - gmax) + jnp.exp(g[2] - gmax) + jnp.exp(g[3] - gmax))
    e = []
    for k in range(EXPERTS_PER_GROUP):
        rows = [lg[N_EXPERT_GROUPS + gg * EXPERTS_PER_GROUP + k:N_EXPERT_GROUPS + gg * EXPERTS_PER_GROUP + k + 1, :]
                for gg in range(N_EXPERT_GROUPS)]
        e.append(jnp.where(gidx == 0.0, rows[0], jnp.where(gidx == 1.0, rows[1], jnp.where(gidx == 2.0, rows[2], rows[3]))))

    def first_argmax(v):
        mx = jnp.maximum(jnp.maximum(v[0], v[1]), jnp.maximum(v[2], v[3]))
        ix = jnp.where(v[0] == mx, 0.0, jnp.where(v[1] == mx, 1.0, jnp.where(v[2] == mx, 2.0, 3.0)))
        return mx, ix

    v1, i1 = first_argmax(e)
    v2, i2 = first_argmax([jnp.where(i1 == float(k), -jnp.inf, e[k]) for k in range(EXPERTS_PER_GROUP)])
    ex = jnp.exp(v2 - v1)
    w1 = g_p / (1.0 + ex)
    w2 = g_p * ex / (1.0 + ex)
    base = gidx * float(EXPERTS_PER_GROUP)
    zero = jnp.zeros_like(w1)
    return jnp.concatenate([base + i1, base + i2, w1, w2, zero, zero, zero, zero], axis=0)


def _proj_ln_body(mg_ref, x_ref, wo_ref, g_ref, b_ref, wr_ref, br_ref, x1_ref, meta_ref, mix_s):
    @pl.when(pl.program_id(0) == 0)
    def _():
        mix_s[...] = jnp.zeros(mix_s.shape, F32)

    wh, wl = _split_bf16(wr_ref[...])
    tm = x_ref.shape[0]
    sub = min(256, tm)
    for r0 in range(0, tm, sub):
        rs = pl.ds(r0, sub)
        x1 = _layer_norm(DN_ALPHA * x_ref[rs, :] + mix_s[rs, :], g_ref[...], b_ref[...])
        x1_ref[rs, :] = x1
        xh, xl = _split_bf16(x1)
        lg = _dot_nt(wh, xh) + (_dot_nt(wh, xl) + _dot_nt(wl, xh)) + br_ref[...]
        meta_ref[:, rs] = _route_rows(lg)
    mix_s[...] = jnp.dot(mg_ref[...], wo_ref[...], preferred_element_type=F32)


def _proj_ln(merged, x, wo, g, b, wr, br, tm):
    n = merged.shape[0]
    nb = n // tm
    prev = lambda w: pl.BlockSpec((tm, w), lambda i: (jnp.maximum(i - 1, 0), 0))
    full = lambda a: pl.BlockSpec(a.shape, lambda i: (0, 0))
    return pl.pallas_call(
        _proj_ln_body,
        out_shape=(jax.ShapeDtypeStruct((n, D_MODEL), F32), jax.ShapeDtypeStruct((8, n), F32)),
        grid=(nb + 1,),
        in_specs=[pl.BlockSpec((tm, D_MODEL), lambda i: (jnp.minimum(i, nb - 1), 0)), prev(D_MODEL),
                  full(wo), full(g), full(b), full(wr), full(br)],
        out_specs=(prev(D_MODEL), pl.BlockSpec((8, tm), lambda i: (0, jnp.maximum(i - 1, 0)))),
        scratch_shapes=[pltpu.VMEM((tm, D_MODEL), F32)],
        compiler_params=_cparams(("arbitrary",)),
        name="proj_ln_router",
    )(merged, x, wo, g, b, wr, br)


MOE_CHUNK = 8


def _local_rows(tt):
    return -(-(2 * tt + N_EXPERTS * (MOE_CHUNK - 1)) // 128) * 128


def _dispatch(meta_t, n):
    tt = _tile(n, 256)
    n_t = n // tt
    ids = meta_t[0:2].astype(jnp.int32)
    onehot = (ids[:, :, None] == jnp.arange(N_EXPERTS, dtype=jnp.int32)).astype(jnp.int32).reshape(2, n_t, tt, N_EXPERTS)
    cnt_slot = jnp.sum(onehot, axis=2)
    cnt = cnt_slot[0] + cnt_slot[1]
    pc = (cnt + MOE_CHUNK - 1) // MOE_CHUNK * MOE_CHUNK
    lstart = jnp.cumsum(pc, axis=1) - pc
    rank = jnp.cumsum(onehot, axis=2) - onehot + jnp.stack([jnp.zeros_like(cnt), cnt_slot[0]])[:, :, None, :]
    lpos = jnp.sum(onehot * (lstart[None, :, None, :] + rank), axis=-1).reshape(2, n)
    seg = jnp.sum(pc, axis=0)
    pe = (seg + MOE_TILE - 1) // MOE_TILE * MOE_TILE
    ends = jnp.cumsum(pe)
    base = ends - pe
    gstart = base[None, :] + jnp.cumsum(pc, axis=0) - pc
    r_tot = -(-(2 * n + N_EXPERTS * (MOE_CHUNK - 1) * n_t) // MOE_TILE) * MOE_TILE + N_EXPERTS * MOE_TILE
    n_tiles = r_tot // MOE_TILE
    tile_start = jnp.arange(n_tiles, dtype=jnp.int32) * MOE_TILE
    tile_expert = jnp.minimum(jnp.sum((tile_start[:, None] >= ends[None, :]).astype(jnp.int32), axis=1), N_EXPERTS - 1)
    tables = dict(
        lstart=lstart.reshape(-1), gstart=gstart.reshape(-1), nchunk=(pc // MOE_CHUNK).reshape(-1),
        zstart=jnp.concatenate([base + seg, ends[-1:]]),
        zcount=jnp.concatenate([(pe - seg) // MOE_CHUNK, (r_tot - ends[-1:]) // MOE_TILE]),
        tile_expert=tile_expert, n_used=(ends[-1] // MOE_TILE).reshape(1))
    return tt, r_tot, lpos, tables


def _seg_loop(tab, t, fn):
    lstart_ref, gstart_ref, nchunk_ref = tab
    for e in range(N_EXPERTS):
        ls = lstart_ref[t * N_EXPERTS + e]
        gs = gstart_ref[t * N_EXPERTS + e]

        def body(j, c, ls=ls, gs=gs):
            fn(pl.multiple_of(ls + j * MOE_CHUNK, MOE_CHUNK), pl.multiple_of(gs + j * MOE_CHUNK, MOE_CHUNK))
            return c
        lax.fori_loop(0, nchunk_ref[t * N_EXPERTS + e], body, 0)


def _n_chunks(nchunk_ref, t):
    tot = nchunk_ref[t * N_EXPERTS]
    for e in range(1, N_EXPERTS):
        tot = tot + nchunk_ref[t * N_EXPERTS + e]
    return tot


def _sort_body(lstart_ref, gstart_ref, nchunk_ref, zstart_ref, zcount_ref, lpos_ref, x_ref, xs_hbm,
               xloc, zbuf, sem, zsem, *, n_t):
    t = pl.program_id(0)
    slot = lax.rem(t, 2)
    tab = (lstart_ref, gstart_ref, nchunk_ref)
    rows = xloc.shape[1]

    def chunk_copy(s, lrow, grow):
        return pltpu.make_async_copy(xloc.at[s, pl.ds(lrow, MOE_CHUNK)], xs_hbm.at[pl.ds(grow, MOE_CHUNK)], sem.at[s])

    def wait_tile(s, tile):
        def body(j, c):
            chunk_copy(s, 0, 0).wait()
            return c
        lax.fori_loop(0, _n_chunks(nchunk_ref, tile), body, 0)

    @pl.when(t == 0)
    def _():
        zbuf[...] = jnp.zeros_like(zbuf)
        zero_copy = lambda grow: pltpu.make_async_copy(
            zbuf.at[pl.ds(0, MOE_CHUNK)], xs_hbm.at[pl.ds(grow, MOE_CHUNK)], zsem.at[0])
        zero_tile = lambda grow: pltpu.make_async_copy(zbuf, xs_hbm.at[pl.ds(grow, MOE_TILE)], zsem.at[0])
        tail_start = zstart_ref[N_EXPERTS]
        for e in range(N_EXPERTS):
            def zb(j, c, e=e):
                zero_copy(pl.multiple_of(zstart_ref[e] + j * MOE_CHUNK, MOE_CHUNK)).start()
                return c
            lax.fori_loop(0, zcount_ref[e], zb, 0)

        def tb(j, c):
            zero_tile(pl.multiple_of(tail_start + j * MOE_TILE, MOE_TILE)).start()
            return c
        lax.fori_loop(0, zcount_ref[N_EXPERTS], tb, 0)
        for e in range(N_EXPERTS):
            def zw(j, c):
                zero_copy(0).wait()
                return c
            lax.fori_loop(0, zcount_ref[e], zw, 0)

        def tw(j, c):
            zero_tile(0).wait()
            return c
        lax.fori_loop(0, zcount_ref[N_EXPERTS], tw, 0)

    @pl.when(t >= 2)
    def _():
        wait_tile(slot, t - 2)

    l_iota = lax.broadcasted_iota(jnp.int32, (rows, x_ref.shape[0]), 0)
    perm = jnp.logical_or(l_iota == lpos_ref[0:1, :], l_iota == lpos_ref[1:2, :])
    perm = jnp.where(perm, 1.0, 0.0).astype(BF16)
    xloc[slot] = jnp.dot(perm, x_ref[...].astype(BF16), preferred_element_type=F32)
    _seg_loop(tab, t, lambda lrow, grow: chunk_copy(slot, lrow, grow).start())

    @pl.when(t == n_t - 1)
    def _():
        wait_tile(slot, t)
        if n_t >= 2:
            wait_tile(1 - slot, t - 1)


def _moe_sort(x1, lpos, tab, tt, r_tot):
    n = x1.shape[0]
    n_t = n // tt
    rows = _local_rows(tt)
    grid_spec = pltpu.PrefetchScalarGridSpec(
        num_scalar_prefetch=5,
        grid=(n_t,),
        in_specs=[pl.BlockSpec((2, tt), lambda t, *_: (0, t)),
                  pl.BlockSpec((tt, D_MODEL), lambda t, *_: (t, 0))],
        out_specs=pl.BlockSpec(memory_space=pl.ANY),
        scratch_shapes=[pltpu.VMEM((2, rows, D_MODEL), F32), pltpu.VMEM((MOE_TILE, D_MODEL), F32),
                        pltpu.SemaphoreType.DMA((2,)), pltpu.SemaphoreType.DMA((1,))])
    return pl.pallas_call(
        functools.partial(_sort_body, n_t=n_t),
        out_shape=jax.ShapeDtypeStruct((r_tot, D_MODEL), F32),
        grid_spec=grid_spec,
        compiler_params=_cparams(("arbitrary",)),
        name="moe_sort",
    )(tab["lstart"], tab["gstart"], tab["nchunk"], tab["zstart"], tab["zcount"], lpos, x1)


def _expert_body(te_ref, nu_ref, x_ref, wg_ref, wu_ref, wd_ref, o_ref):
    del te_ref
    used = pl.program_id(0) < nu_ref[0]

    @pl.when(used)
    def _():
        xb = x_ref[...].astype(BF16)
        hid = _gelu(jnp.dot(xb, wg_ref[0], preferred_element_type=F32)) * jnp.dot(xb, wu_ref[0], preferred_element_type=F32)
        o_ref[...] = jnp.dot(hid.astype(BF16), wd_ref[0], preferred_element_type=F32)

    @pl.when(jnp.logical_not(used))
    def _():
        o_ref[...] = jnp.zeros_like(o_ref)


def _moe_experts(xs, tab, wg, wu, wd):
    r_tot = xs.shape[0]
    last = lambda i, nu: jnp.minimum(i, nu[0] - 1)
    grid_spec = pltpu.PrefetchScalarGridSpec(
        num_scalar_prefetch=2,
        grid=(r_tot // MOE_TILE,),
        in_specs=[pl.BlockSpec((MOE_TILE, D_MODEL), lambda i, te, nu: (last(i, nu), 0)),
                  pl.BlockSpec((1, D_MODEL, D_EXPERT), lambda i, te, nu: (te[last(i, nu)], 0, 0)),
                  pl.BlockSpec((1, D_MODEL, D_EXPERT), lambda i, te, nu: (te[last(i, nu)], 0, 0)),
                  pl.BlockSpec((1, D_EXPERT, D_MODEL), lambda i, te, nu: (te[last(i, nu)], 0, 0))],
        out_specs=pl.BlockSpec((MOE_TILE, D_MODEL), lambda i, te, nu: (i, 0)))
    return pl.pallas_call(
        _expert_body,
        out_shape=jax.ShapeDtypeStruct((r_tot, D_MODEL), F32),
        grid_spec=grid_spec,
        compiler_params=_cparams(("arbitrary",)),
        name="moe_experts",
    )(tab["tile_expert"], tab["n_used"], xs, wg, wu, wd)


def _combine_body(lstart_ref, gstart_ref, nchunk_ref, ys_hbm, x1_ref, meta_ref, g_ref, b_ref, o_ref,
                  yloc, sem, *, n_t):
    t = pl.program_id(0)
    slot = lax.rem(t, 2)
    tab = (lstart_ref, gstart_ref, nchunk_ref)
    rows = yloc.shape[1]

    def chunk_copy(s, lrow, grow):
        return pltpu.make_async_copy(ys_hbm.at[pl.ds(grow, MOE_CHUNK)], yloc.at[s, pl.ds(lrow, MOE_CHUNK)], sem.at[s])

    def fetch(s, tile):
        _seg_loop(tab, tile, lambda lrow, grow: chunk_copy(s, lrow, grow).start())

    @pl.when(t == 0)
    def _():
        yloc[...] = jnp.zeros_like(yloc)
        fetch(0, 0)

    @pl.when(t + 1 < n_t)
    def _():
        fetch(1 - slot, t + 1)

    def wbody(j, c):
        chunk_copy(slot, 0, 0).wait()
        return c
    lax.fori_loop(0, _n_chunks(nchunk_ref, t), wbody, 0)

    meta = meta_ref[...]
    l_iota = lax.broadcasted_iota(jnp.int32, (x1_ref.shape[0], rows), 1).astype(F32)
    yb = yloc[slot].astype(BF16)
    pick = lambda k: jnp.dot(jnp.where(l_iota == meta[:, k:k + 1], 1.0, 0.0).astype(BF16), yb, preferred_element_type=F32)
    moe = meta[:, 2:3] * pick(0) + meta[:, 3:4] * pick(1)
    o_ref[...] = _layer_norm(DN_ALPHA * x1_ref[...] + moe, g_ref[...], b_ref[...])


def _moe_combine(ys, x1, meta_n, tab, g, b, tt):
    n = x1.shape[0]
    n_t = n // tt
    rows = _local_rows(tt)
    vec = lambda: pl.BlockSpec((1, D_MODEL), lambda t, *_: (0, 0))
    grid_spec = pltpu.PrefetchScalarGridSpec(
        num_scalar_prefetch=3,
        grid=(n_t,),
        in_specs=[pl.BlockSpec(memory_space=pl.ANY),
                  pl.BlockSpec((tt, D_MODEL), lambda t, *_: (t, 0)),
                  pl.BlockSpec((tt, 8), lambda t, *_: (t, 0)), vec(), vec()],
        out_specs=pl.BlockSpec((tt, D_MODEL), lambda t, *_: (t, 0)),
        scratch_shapes=[pltpu.VMEM((2, rows, D_MODEL), F32), pltpu.SemaphoreType.DMA((2,))])
    return pl.pallas_call(
        functools.partial(_combine_body, n_t=n_t),
        out_shape=jax.ShapeDtypeStruct((n, D_MODEL), F32),
        grid_spec=grid_spec,
        compiler_params=_cparams(("arbitrary",)),
        name="moe_combine_ln",
    )(tab["lstart"], tab["gstart"], tab["nchunk"], ys, x1, meta_n, g, b)


def _moe(x1, meta_t, p):
    n = x1.shape[0]
    tt, r_tot, lpos, tab = _dispatch(meta_t, n)
    xs = _moe_sort(x1, lpos, tab, tt, r_tot)
    ys = _moe_experts(xs, tab, p["w_gate"], p["w_up"], p["w_down"])
    meta_n = jnp.concatenate([lpos.astype(F32), meta_t[2:4], jnp.zeros((4, n), F32)], axis=0).T
    return _moe_combine(ys, x1, meta_n, tab, p["ln2_g"], p["ln2_b"], tt)


def _layer(x, conv_buf, h0, mem_kv, kv_bufs, p):
    b, t, _ = x.shape
    n = b * t
    x2 = x.reshape(n, D_MODEL)
    xb = x2.astype(BF16)
    tm_a = _tile(n, 2048)
    xrg = _matmul(xb, p["w_in"], 0, 2 * D_RNN, F32, tm_a, 512, "in_proj_rnn")
    qkv = _matmul(xb, p["w_in"], COL_Q, N_QKV_HEADS * HEAD_DIM, F32, tm_a, 512, "in_proj_qkv", split=4)
    qm = _matmul(xb, p["w_in"], COL_QM, D_MEM, BF16 if t % 16 == 0 else F32, tm_a, 512, "in_proj_qm", split=2)
    gates = _matmul(xb, p["w_in"], COL_GATES, N_GATE_COLS, BF16, tm_a, 512, "gate_proj", bias=p["b_gates"])

    xrg3 = xrg.reshape(b, t, 2 * D_RNN)
    a_lru, h_last = _rglru(xrg3, conv_buf, h0.reshape(b, 1, D_RNN), p["conv_w"], p["conv_b"], p["wax"],
                           p["b_a"], p["b_x"], p["lam"], _tile(t, 256))
    if kv_bufs is None:
        att = _attn_prompt(qkv, b, t)
    else:
        att = _attn_sample(qkv, kv_bufs, b, t)
    mem = _mem_attn(qm, mem_kv, b, t, _tile(t, 512), BF16 if t % 16 == 0 else F32)

    tm = _tile(n, 256)
    merged = _branch_merge(a_lru.reshape(n, D_RNN), att, mem, gates, p["w_br_lru"], p["w_br_att"], p["w_br_mem"], tm)
    x1, meta_t = _proj_ln(merged, x2, p["w_o"], p["ln1_g"], p["ln1_b"], p["w_router"], p["b_router"], _tile(n, 512))
    y = _moe(x1, meta_t, p)
    return y.reshape(b, t, D_MODEL), xrg3, qkv, h_last.reshape(b, D_RNN)


def _kv_rows(qkv, g, b, t, lo):
    q4 = qkv.reshape(N_QKV_HEADS, b, t, HEAD_DIM)

    def pick(which):
        h0 = (which * N_GROUPS + g) * HEADS
        a = lax.slice(q4, (h0, 0, lo, 0), (h0 + HEADS, b, t, HEAD_DIM))
        return jnp.transpose(a, (1, 2, 0, 3))
    return jnp.stack([pick(1), pick(2)], axis=2)


def _block_diag_gates(w_a, w_x):
    per = RNN_CHUNK // RNN_BLOCK
    chunks = []
    for c in range(N_RNN_CHUNKS):
        halves = []
        for w in (w_a, w_x):
            m = jnp.zeros((RNN_CHUNK, RNN_CHUNK), F32)
            for i in range(per):
                m = lax.dynamic_update_slice(m, w[c * per + i], (i * RNN_BLOCK, i * RNN_BLOCK))
            halves.append(m)
        chunks.append(jnp.concatenate(halves, axis=1))
    return jnp.stack(chunks).astype(BF16)


def kernel(x_prompt, x_sample, cache_kv_w128, cache_kv_w512, cache_kv_w2048, cache_mem_kv, state_h, state_conv, mem_prompt, w_in, b_gates, conv_w, conv_b, w_a, b_a, w_x, b_x, lru_lambda, w_br_lru, w_br_att, w_br_mem, w_o, w_mem_kv, ln1_g, ln1_b, w_rg, b_rg, w_re, b_re, w_gate, w_up, w_down, ln2_g, ln2_b):
    row = lambda v: v.reshape(1, -1).astype(F32)
    w_router = jnp.zeros((ROUTER_ROWS, D_MODEL), F32)
    w_router = w_router.at[:N_EXPERT_GROUPS].set(w_rg.T).at[N_EXPERT_GROUPS:N_EXPERT_GROUPS + N_EXPERTS].set(w_re.T)
    b_router = jnp.zeros((ROUTER_ROWS, 1), F32)
    b_router = b_router.at[:N_EXPERT_GROUPS, 0].set(b_rg).at[N_EXPERT_GROUPS:N_EXPERT_GROUPS + N_EXPERTS, 0].set(b_re)
    p = dict(
        w_in=w_in.astype(BF16), b_gates=row(b_gates), conv_w=conv_w, conv_b=row(conv_b), wax=_block_diag_gates(w_a, w_x),
        b_a=row(b_a), b_x=row(b_x), lam=row(lru_lambda),
        w_br_lru=w_br_lru.astype(BF16), w_br_att=w_br_att.astype(BF16), w_br_mem=w_br_mem.astype(BF16),
        w_o=w_o.astype(BF16), ln1_g=row(ln1_g), ln1_b=row(ln1_b), w_router=w_router, b_router=b_router,
        w_gate=w_gate.astype(BF16), w_up=w_up.astype(BF16), w_down=w_down.astype(BF16),
        ln2_g=row(ln2_g), ln2_b=row(ln2_b))

    bp, s, _ = x_prompt.shape
    bs, ts, _ = x_sample.shape
    mem_rows = mem_prompt.reshape(bp * N_MEM, D_MODEL)
    mem_kv_p = _matmul(mem_rows, w_mem_kv, 0, 2 * D_MEM, F32, _tile(bp * N_MEM, 512), 512, "mem_kv_proj")
    mem_kv_p = mem_kv_p.reshape(bp, N_MEM, 2 * D_MEM)

    y_p, xrg_p, qkv_p, h_p = _layer(x_prompt, jnp.zeros((bp, CONV_W - 1, D_RNN), F32), jnp.zeros((bp, D_RNN), F32),
                                    mem_kv_p, None, p)
    y_s, xrg_s, qkv_s, h_s = _layer(x_sample, state_conv, state_h, cache_mem_kv.reshape(bs, N_MEM, 2 * D_MEM),
                                    (cache_kv_w128, cache_kv_w512, cache_kv_w2048), p)

    conv_p = jnp.concatenate([jnp.zeros((bp, CONV_W - 1, D_RNN), F32), xrg_p[:, :, :D_RNN]], axis=1)[:, -(CONV_W - 1):]
    conv_s = jnp.concatenate([state_conv, xrg_s[:, :, :D_RNN]], axis=1)[:, -(CONV_W - 1):]
    kv_p = [_kv_rows(qkv_p, g, bp, s, s - min(KEYS * d, s)) for g, d in enumerate(DILATIONS)]
    kv_s = [_kv_rows(qkv_s, g, bs, ts, 0) for g in range(N_GROUPS)]
    return (y_p, y_s, kv_p[0], kv_p[1], kv_p[2],
            mem_kv_p.reshape(bp, N_MEM, 2, MEM_HEADS, MEM_HEAD_DIM), h_p, conv_p,
            kv_s[0], kv_s[1], kv_s[2], h_s, conv_s)
```

```python
import functools

import jax
import jax.numpy as jnp
from jax import lax
from jax.experimental import pallas as pl
from jax.experimental.pallas import tpu as pltpu

F32 = jnp.float32
BF16 = jnp.bfloat16

D_MODEL = 2048
D_RNN = 1536
N_RNN_BLOCKS = 16
RNN_BLOCK = D_RNN // N_RNN_BLOCKS
RNN_CHUNK = 384
N_RNN_CHUNKS = D_RNN // RNN_CHUNK
CONV_W = 4
LRU_C = 8.0
HEAD_DIM = 128
HEADS = 4
DILATIONS = (1, 4, 16)
KEYS = 128
N_GROUPS = 3
D_ATT_OUT = HEADS * HEAD_DIM
N_QKV_HEADS = 3 * N_GROUPS * HEADS
ATT_SCALE = HEAD_DIM ** -0.5
ATT_TILE = KEYS * max(DILATIONS)
N_MEM = 256
MEM_HEADS = 4
MEM_HEAD_DIM = 256
D_MEM = MEM_HEADS * MEM_HEAD_DIM
MEM_SCALE = MEM_HEAD_DIM ** -0.5
N_EXPERT_GROUPS = 4
EXPERTS_PER_GROUP = 4
N_EXPERTS = 16
D_EXPERT = 512
MOE_TILE = 512
DN_ALPHA = 2.0 ** 0.25
LN_EPS = 1e-5
NEG_INF = -1e30

COL_Q = 2 * D_RNN
COL_QM = COL_Q + N_QKV_HEADS * HEAD_DIM
COL_GATES = COL_QM + D_MEM
N_GATE_COLS = 3 * D_MODEL
ROUTER_ROWS = 32

VMEM_LIMIT = 56 * 1024 * 1024


def _cparams(sem):
    return pltpu.CompilerParams(dimension_semantics=sem, vmem_limit_bytes=VMEM_LIMIT)


def _gelu(x):
    return 0.5 * x * (1.0 + jnp.tanh(0.7978845608028654 * (x + 0.044715 * (x * x * x))))


def _layer_norm(x, g, b):
    mu = jnp.mean(x, axis=-1, keepdims=True)
    xc = x - mu
    var = jnp.mean(xc * xc, axis=-1, keepdims=True)
    return xc * lax.rsqrt(var + LN_EPS) * g + b


def _dot_nt(a, b):
    return lax.dot_general(a, b, (((1,), (1,)), ((), ())), preferred_element_type=F32)


def _tile(m, cap):
    t = min(m, cap)
    assert m % t == 0, (m, t)
    return t


def _mm_body(x_ref, w_ref, *rest, split, gate):
    acc = jnp.dot(x_ref[...].astype(BF16), w_ref[...].astype(BF16), preferred_element_type=F32)
    if gate:
        b_ref, o_ref = rest
        acc = 0.5 * jnp.tanh(0.5 * (acc + b_ref[...])) + 0.5
    else:
        (o_ref,) = rest
    if split == 1:
        o_ref[...] = acc.astype(o_ref.dtype)
    else:
        w = acc.shape[1] // split
        for s in range(split):
            o_ref[s] = acc[:, s * w:(s + 1) * w].astype(o_ref.dtype)


def _matmul(x, w, col_off, n_cols, out_dtype, tm, tn, name, bias=None, split=1):
    m, k = x.shape
    cb = col_off // tn
    in_specs = [pl.BlockSpec((tm, k), lambda i, j: (i, 0)),
                pl.BlockSpec((k, tn), lambda i, j: (0, j + cb))]
    args = [x, w]
    if bias is not None:
        in_specs.append(pl.BlockSpec((1, tn), lambda i, j: (0, j)))
        args.append(bias)
    if split == 1:
        out_shape = jax.ShapeDtypeStruct((m, n_cols), out_dtype)
        out_spec = pl.BlockSpec((tm, tn), lambda i, j: (i, j))
    else:
        out_shape = jax.ShapeDtypeStruct((n_cols * split // tn, m, tn // split), out_dtype)
        out_spec = pl.BlockSpec((split, tm, tn // split), lambda i, j: (j, i, 0))
    return pl.pallas_call(
        functools.partial(_mm_body, split=split, gate=bias is not None),
        out_shape=out_shape,
        grid=(m // tm, n_cols // tn),
        in_specs=in_specs,
        out_specs=out_spec,
        compiler_params=_cparams(("parallel", "arbitrary")),
        name=name,
    )(*args)


def _rglru_body(xr_ref, xg_ref, cbuf_ref, h0_ref, cw_ref, cb_ref, wax_ref, ba_ref, bx_ref, lam_ref,
                out_ref, hl_ref, ext_s, a_s, u_s, h_s, *, tt):
    t = pl.program_id(1)

    @pl.when(t == 0)
    def _():
        ext_s[0:8, :] = jnp.zeros((8, D_RNN), F32)
        ext_s[5:8, :] = cbuf_ref[0]
        h_s[...] = jnp.broadcast_to(h0_ref[0], (8, D_RNN))

    @pl.when(t > 0)
    def _():
        ext_s[0:8, :] = ext_s[tt:tt + 8, :]

    ext_s[8:8 + tt, :] = xr_ref[0]
    cw = cw_ref[...]
    xc = (cb_ref[...] + cw[3:4, :] * ext_s[8:8 + tt, :] + cw[2:3, :] * ext_s[7:7 + tt, :]
          + cw[1:2, :] * ext_s[6:6 + tt, :] + cw[0:1, :] * ext_s[5:5 + tt, :])
    xcb = xc.astype(BF16)
    r_parts, i_parts = [], []
    for c in range(N_RNN_CHUNKS):
        g = jnp.dot(xcb[:, c * RNN_CHUNK:(c + 1) * RNN_CHUNK], wax_ref[c], preferred_element_type=F32)
        r_parts.append(g[:, :RNN_CHUNK])
        i_parts.append(g[:, RNN_CHUNK:])
    sigmoid = lambda z: 0.5 * jnp.tanh(0.5 * z) + 0.5
    r = sigmoid(jnp.concatenate(r_parts, axis=1) + ba_ref[...])
    gi = sigmoid(jnp.concatenate(i_parts, axis=1) + bx_ref[...])
    nl = -lam_ref[...]
    softplus = jnp.maximum(nl, 0.0) + jnp.log1p(jnp.exp(-jnp.abs(nl)))
    log_a = (-LRU_C) * r * softplus
    th = jnp.tanh(log_a)
    a_s[...] = jnp.exp(log_a)
    u_s[...] = jnp.sqrt(-2.0 * th / (1.0 - th)) * (gi * xc)

    rows = lax.broadcasted_iota(jnp.int32, (8, D_RNN), 0)

    def blk(i, h):
        r0 = pl.multiple_of(i * 8, 8)
        ab = a_s[pl.ds(r0, 8), :]
        ub = u_s[pl.ds(r0, 8), :]
        for s in (1, 2, 4):
            keep = rows >= s
            ub = ab * jnp.where(keep, pltpu.roll(ub, s, 0), 0.0) + ub
            ab = ab * jnp.where(keep, pltpu.roll(ab, s, 0), 1.0)
        hb = ab * h + ub
        u_s[pl.ds(r0, 8), :] = hb
        return jnp.broadcast_to(hb[7:8, :], (8, D_RNN))

    h_fin = lax.fori_loop(0, tt // 8, blk, h_s[...])
    h_s[...] = h_fin
    hl_ref[0] = h_fin[0:1, :]
    out_ref[0] = (u_s[...] * _gelu(xg_ref[0])).astype(out_ref.dtype)


def _rglru(xrg3, conv_buf, h0, conv_w, conv_b, wax, b_a, b_x, lam, tt):
    b, t, _ = xrg3.shape
    vec = lambda: pl.BlockSpec((1, D_RNN), lambda i, j: (0, 0))
    return pl.pallas_call(
        functools.partial(_rglru_body, tt=tt),
        out_shape=(jax.ShapeDtypeStruct((b, t, D_RNN), BF16), jax.ShapeDtypeStruct((b, 1, D_RNN), F32)),
        grid=(b, t // tt),
        in_specs=[pl.BlockSpec((1, tt, D_RNN), lambda i, j: (i, j, 0)),
                  pl.BlockSpec((1, tt, D_RNN), lambda i, j: (i, j, 1)),
                  pl.BlockSpec((1, CONV_W - 1, D_RNN), lambda i, j: (i, 0, 0)),
                  pl.BlockSpec((1, 1, D_RNN), lambda i, j: (i, 0, 0)),
                  pl.BlockSpec((CONV_W, D_RNN), lambda i, j: (0, 0)),
                  vec(),
                  pl.BlockSpec((N_RNN_CHUNKS, RNN_CHUNK, 2 * RNN_CHUNK), lambda i, j: (0, 0, 0)),
                  vec(), vec(), vec()],
        out_specs=(pl.BlockSpec((1, tt, D_RNN), lambda i, j: (i, j, 0)),
                   pl.BlockSpec((1, 1, D_RNN), lambda i, j: (i, 0, 0))),
        scratch_shapes=[pltpu.VMEM((tt + 8, D_RNN), F32), pltpu.VMEM((tt, D_RNN), F32),
                        pltpu.VMEM((tt, D_RNN), F32), pltpu.VMEM((8, D_RNN), F32)],
        compiler_params=_cparams(("parallel", "arbitrary")),
        name="rglru",
    )(xrg3, xrg3, conv_buf, h0, conv_w, conv_b, wax, b_a, b_x, lam)


def _band_block(q, k, v, bias):
    s = _dot_nt(q, k) * ATT_SCALE + bias
    m = jnp.max(s, axis=-1, keepdims=True)
    p = jnp.exp(s - m)
    l = jnp.sum(p, axis=-1, keepdims=True)
    o = jnp.dot(p.astype(BF16), v, preferred_element_type=F32) / l
    return o, m + jnp.log(l)


def _attn_prompt_body(*refs):
    q_refs, k_refs, v_refs, kh_refs, vh_refs = (refs[3 * i:3 * i + 3] for i in range(5))
    o_ref, og_s, lg_s = refs[15:]
    row = lax.broadcasted_iota(jnp.int32, (KEYS, 2 * KEYS), 0)
    col = lax.broadcasted_iota(jnp.int32, (KEYS, 2 * KEYS), 1)
    band = jnp.logical_and(col >= row, col <= row + KEYS)
    bias = jnp.where(band, 0.0, NEG_INF)
    has_prev = pl.program_id(1) > 0
    bias_first = jnp.where(jnp.logical_and(band, jnp.logical_or(col >= KEYS, has_prev)), 0.0, NEG_INF)

    def put(g, rows, o, lse):
        og_s[g, rows, :] = o
        lg_s[g, rows, :] = jnp.broadcast_to(lse, (KEYS, HEAD_DIM))

    def rows_of(start, size, d):
        return pl.ds(start, size, stride=d) if d > 1 else pl.ds(start, size)

    for g, d in enumerate(DILATIONS):
        q_ref, k_ref, v_ref, kh_ref, vh_ref = q_refs[g], k_refs[g], v_refs[g], kh_refs[g], vh_refs[g]
        nqb = ATT_TILE // (d * KEYS)
        for r in range(d):
            own = rows_of(r, KEYS, d)
            kk = jnp.concatenate([kh_ref[0, own, :], k_ref[0, own, :]], axis=0).astype(BF16)
            vv = jnp.concatenate([vh_ref[0, own, :], v_ref[0, own, :]], axis=0).astype(BF16)
            o, lse = _band_block(q_ref[0, own, :].astype(BF16), kk, vv, bias_first)
            put(g, own, o, lse)
            for qb in range(1, nqb):
                rows = rows_of(qb * KEYS * d + r, KEYS, d)
                keys = rows_of((qb - 1) * KEYS * d + r, 2 * KEYS, d)
                o, lse = _band_block(q_ref[0, rows, :].astype(BF16), k_ref[0, keys, :].astype(BF16),
                                     v_ref[0, keys, :].astype(BF16), bias)
                put(g, rows, o, lse)

    l0, l1, l2 = lg_s[0], lg_s[1], lg_s[2]
    m = jnp.maximum(jnp.maximum(l0, l1), l2)
    e0, e1, e2 = jnp.exp(l0 - m), jnp.exp(l1 - m), jnp.exp(l2 - m)
    o_ref[...] = ((e0 * og_s[0] + e1 * og_s[1] + e2 * og_s[2]) / (e0 + e1 + e2)).astype(o_ref.dtype)


def _attn_prompt(qkv, b, s):
    assert s % ATT_TILE == 0
    nt = s // ATT_TILE
    n = b * s

    def cur(which, g):
        return pl.BlockSpec((1, ATT_TILE, HEAD_DIM),
                            lambda i, j, h: ((which * N_GROUPS + g) * HEADS + h, i * nt + j, 0))

    def halo(which, g):
        rows = KEYS * DILATIONS[g]
        per = ATT_TILE // rows
        return pl.BlockSpec((1, rows, HEAD_DIM),
                            lambda i, j, h: ((which * N_GROUPS + g) * HEADS + h,
                                             jnp.maximum((i * nt + j) * per - 1, 0), 0))

    in_specs = ([cur(0, g) for g in range(N_GROUPS)] + [cur(1, g) for g in range(N_GROUPS)]
                + [cur(2, g) for g in range(N_GROUPS)] + [halo(1, g) for g in range(N_GROUPS)]
                + [halo(2, g) for g in range(N_GROUPS)])
    return pl.pallas_call(
        _attn_prompt_body,
        out_shape=jax.ShapeDtypeStruct((n, D_ATT_OUT), BF16),
        grid=(b, nt, HEADS),
        in_specs=in_specs,
        out_specs=pl.BlockSpec((ATT_TILE, HEAD_DIM), lambda i, j, h: (i * nt + j, h)),
        scratch_shapes=[pltpu.VMEM((N_GROUPS, ATT_TILE, HEAD_DIM), F32),
                        pltpu.VMEM((N_GROUPS, ATT_TILE, HEAD_DIM), F32)],
        compiler_params=_cparams(("parallel", "parallel", "parallel")),
        name="attn_prompt",
    )(*([qkv] * 15))


def _attn_sample_group(q4, kn4, vn4, c_ref, d, t_new):
    wb = c_ref.shape[1] // (2 * HEADS)
    nr = HEADS * t_new
    zeros = jnp.zeros((t_new, HEAD_DIM), F32)
    qbd = jnp.concatenate(
        [jnp.concatenate([q4[h] if hh == h else zeros for hh in range(HEADS)], axis=1) for h in range(HEADS)],
        axis=0).astype(BF16)
    kn = jnp.concatenate([kn4[h] for h in range(HEADS)], axis=1).astype(BF16)
    vn = jnp.concatenate([vn4[h] for h in range(HEADS)], axis=1).astype(BF16)
    cache_rows = lambda kv, h: c_ref[0, pl.ds(kv * HEADS + h, wb, stride=2 * HEADS), :]
    kc = jnp.concatenate([cache_rows(0, h) for h in range(HEADS)], axis=1).astype(BF16)
    vc = jnp.concatenate([cache_rows(1, h) for h in range(HEADS)], axis=1).astype(BF16)
    tq_c = lax.broadcasted_iota(jnp.int32, (nr, wb), 0) & (t_new - 1)
    dist_c = wb + tq_c - lax.broadcasted_iota(jnp.int32, (nr, wb), 1)
    ok_c = jnp.logical_and((dist_c & (d - 1)) == 0, dist_c <= KEYS * d)
    tq_n = lax.broadcasted_iota(jnp.int32, (nr, t_new), 0) & (t_new - 1)
    dist_n = tq_n - lax.broadcasted_iota(jnp.int32, (nr, t_new), 1)
    ok_n = jnp.logical_and(jnp.logical_and(dist_n >= 0, (dist_n & (d - 1)) == 0), dist_n <= KEYS * d)
    s_c = jnp.where(ok_c, _dot_nt(qbd, kc) * ATT_SCALE, NEG_INF)
    s_n = jnp.where(ok_n, _dot_nt(qbd, kn) * ATT_SCALE, NEG_INF)
    m = jnp.maximum(jnp.max(s_c, axis=-1, keepdims=True), jnp.max(s_n, axis=-1, keepdims=True))
    p_c = jnp.exp(s_c - m)
    p_n = jnp.exp(s_n - m)
    l = jnp.sum(p_c, axis=-1, keepdims=True) + jnp.sum(p_n, axis=-1, keepdims=True)
    o = (jnp.dot(p_c.astype(BF16), vc, preferred_element_type=F32)
         + jnp.dot(p_n.astype(BF16), vn, preferred_element_type=F32)) / l
    lse = m + jnp.log(l)
    o = jnp.concatenate(
        [o[h * t_new:(h + 1) * t_new, h * HEAD_DIM:(h + 1) * HEAD_DIM] for h in range(HEADS)], axis=1)
    lse = jnp.concatenate(
        [jnp.broadcast_to(lse[h * t_new:(h + 1) * t_new], (t_new, HEAD_DIM)) for h in range(HEADS)], axis=1)
    return o, lse


def _attn_sample_body(*refs, t_new):
    q_refs, k_refs, v_refs, c_refs = (refs[3 * i:3 * i + 3] for i in range(4))
    o_ref = refs[12]
    outs = [_attn_sample_group(q_refs[g][...], k_refs[g][...], v_refs[g][...], c_refs[g], d, t_new)
            for g, d in enumerate(DILATIONS)]
    (o0, l0), (o1, l1), (o2, l2) = outs
    m = jnp.maximum(jnp.maximum(l0, l1), l2)
    e0, e1, e2 = jnp.exp(l0 - m), jnp.exp(l1 - m), jnp.exp(l2 - m)
    o_ref[0] = (e0 * o0 + e1 * o1 + e2 * o2) / (e0 + e1 + e2)


def _attn_sample(qkv, caches, b, t_new):
    assert t_new & (t_new - 1) == 0
    new = lambda which, g: pl.BlockSpec((HEADS, t_new, HEAD_DIM), lambda i: (which * N_GROUPS + g, i, 0))
    caches2 = [c.reshape(b, c.shape[1] * 2 * HEADS, HEAD_DIM) for c in caches]
    in_specs = ([new(0, g) for g in range(N_GROUPS)] + [new(1, g) for g in range(N_GROUPS)]
                + [new(2, g) for g in range(N_GROUPS)]
                + [pl.BlockSpec((1, c.shape[1], HEAD_DIM), lambda i: (i, 0, 0)) for c in caches2])
    att = pl.pallas_call(
        functools.partial(_attn_sample_body, t_new=t_new),
        out_shape=jax.ShapeDtypeStruct((b, t_new, D_ATT_OUT), F32),
        grid=(b,),
        in_specs=in_specs,
        out_specs=pl.BlockSpec((1, t_new, D_ATT_OUT), lambda i: (i, 0, 0)),
        compiler_params=_cparams(("parallel",)),
        name="attn_sample",
    )(*([qkv] * 9), *caches2)
    return att.reshape(b * t_new, D_ATT_OUT)


def _mem_attn_body(q_ref, k_ref, v_ref, o_ref):
    s = _dot_nt(q_ref[0].astype(BF16), k_ref[0].astype(BF16)) * MEM_SCALE
    m = jnp.max(s, axis=-1, keepdims=True)
    p = jnp.exp(s - m)
    l = jnp.sum(p, axis=-1, keepdims=True)
    o = jnp.dot(p.astype(BF16), v_ref[0].astype(BF16), preferred_element_type=F32) / l
    o_ref[...] = o.astype(o_ref.dtype)


def _mem_attn(qm, mem_kv, b, t, tm, out_dtype):
    nt = t // tm
    return pl.pallas_call(
        _mem_attn_body,
        out_shape=jax.ShapeDtypeStruct((b * t, D_MEM), out_dtype),
        grid=(b, nt, MEM_HEADS),
        in_specs=[pl.BlockSpec((1, tm, MEM_HEAD_DIM), lambda i, j, h: (h, i * nt + j, 0)),
                  pl.BlockSpec((1, N_MEM, MEM_HEAD_DIM), lambda i, j, h: (i, 0, h)),
                  pl.BlockSpec((1, N_MEM, MEM_HEAD_DIM), lambda i, j, h: (i, 0, MEM_HEADS + h))],
        out_specs=pl.BlockSpec((tm, MEM_HEAD_DIM), lambda i, j, h: (i * nt + j, h)),
        compiler_params=_cparams(("parallel", "parallel", "arbitrary")),
        name="mem_attn",
    )(qm, mem_kv, mem_kv)


def _mem_attn_cache_body(q_ref, c_ref, o_ref):
    outs = []
    for h in range(MEM_HEADS):
        k = c_ref[0, :, 0, h, :].astype(BF16)
        v = c_ref[0, :, 1, h, :].astype(BF16)
        s = _dot_nt(q_ref[h].astype(BF16), k) * MEM_SCALE
        m = jnp.max(s, axis=-1, keepdims=True)
        p = jnp.exp(s - m)
        l = jnp.sum(p, axis=-1, keepdims=True)
        outs.append(jnp.dot(p.astype(BF16), v, preferred_element_type=F32) / l)
    o_ref[...] = jnp.concatenate(outs, axis=1)


def _mem_attn_cache(qm, cache, b, t):
    return pl.pallas_call(
        _mem_attn_cache_body,
        out_shape=jax.ShapeDtypeStruct((b * t, D_MEM), F32),
        grid=(b,),
        in_specs=[pl.BlockSpec((MEM_HEADS, t, MEM_HEAD_DIM), lambda i: (0, i, 0)),
                  pl.BlockSpec((1, N_MEM, 2, MEM_HEADS, MEM_HEAD_DIM), lambda i: (i, 0, 0, 0, 0))],
        out_specs=pl.BlockSpec((t, D_MEM), lambda i: (i, 0)),
        compiler_params=_cparams(("parallel",)),
        name="mem_attn_cache",
    )(qm, cache)


def _branch_body(al_ref, att_ref, mem_ref, gt_ref, wl_ref, wa_ref, wm_ref, out_ref):
    acc = gt_ref[:, 0:D_MODEL].astype(F32) * jnp.dot(al_ref[...].astype(BF16), wl_ref[...], preferred_element_type=F32)
    acc = acc + gt_ref[:, D_MODEL:2 * D_MODEL].astype(F32) * jnp.dot(att_ref[...].astype(BF16), wa_ref[...], preferred_element_type=F32)
    acc = acc + gt_ref[:, 2 * D_MODEL:3 * D_MODEL].astype(F32) * jnp.dot(mem_ref[...].astype(BF16), wm_ref[...], preferred_element_type=F32)
    out_ref[...] = acc.astype(out_ref.dtype)


def _branch_merge(a_lru, att, mem, gates, wl, wa, wm, tm):
    n = a_lru.shape[0]
    row = lambda w: pl.BlockSpec((tm, w), lambda i: (i, 0))
    full = lambda a: pl.BlockSpec(a.shape, lambda i: (0, 0))
    return pl.pallas_call(
        _branch_body,
        out_shape=jax.ShapeDtypeStruct((n, D_MODEL), BF16),
        grid=(n // tm,),
        in_specs=[row(D_RNN), row(D_ATT_OUT), row(D_MEM), row(N_GATE_COLS), full(wl), full(wa), full(wm)],
        out_specs=row(D_MODEL),
        compiler_params=_cparams(("parallel",)),
        name="branch_merge",
    )(a_lru, att, mem, gates, wl, wa, wm)


def _split_bf16(x):
    hi = x.astype(BF16)
    return hi, (x - hi.astype(F32)).astype(BF16)


def _route_rows(lg):
    g = [lg[i:i + 1, :] for i in range(N_EXPERT_GROUPS)]
    gmax = jnp.maximum(jnp.maximum(g[0], g[1]), jnp.maximum(g[2], g[3]))
    gidx = jnp.where(g[0] == gmax, 0.0, jnp.where(g[1] == gmax, 1.0, jnp.where(g[2] == gmax, 2.0, 3.0)))
    g_p = 1.0 / (jnp.exp(g[0] - gmax) + jnp.exp(g[1] - gmax) + jnp.exp(g[2] - gmax) + jnp.exp(g[3] - gmax))
    e = []
    for k in range(EXPERTS_PER_GROUP):
        rows = [lg[N_EXPERT_GROUPS + gg * EXPERTS_PER_GROUP + k:N_EXPERT_GROUPS + gg * EXPERTS_PER_GROUP + k + 1, :]
                for gg in range(N_EXPERT_GROUPS)]
        e.append(jnp.where(gidx == 0.0, rows[0], jnp.where(gidx == 1.0, rows[1], jnp.where(gidx == 2.0, rows[2], rows[3]))))

    def first_argmax(v):
        mx = jnp.maximum(jnp.maximum(v[0], v[1]), jnp.maximum(v[2], v[3]))
        ix = jnp.where(v[0] == mx, 0.0, jnp.where(v[1] == mx, 1.0, jnp.where(v[2] == mx, 2.0, 3.0)))
        return mx, ix

    v1, i1 = first_argmax(e)
    v2, i2 = first_argmax([jnp.where(i1 == float(k), -jnp.inf, e[k]) for k in range(EXPERTS_PER_GROUP)])
    ex = jnp.exp(v2 - v1)
    w1 = g_p / (1.0 + ex)
    w2 = g_p * ex / (1.0 + ex)
    base = gidx * float(EXPERTS_PER_GROUP)
    zero = jnp.zeros_like(w1)
    return jnp.concatenate([base + i1, base + i2, w1, w2, zero, zero, zero, zero], axis=0)


def _proj_ln_body(mg_ref, x_ref, wo_ref, g_ref, b_ref, wr_ref, br_ref, x1_ref, meta_ref, mix_s):
    @pl.when(pl.program_id(0) == 0)
    def _():
        mix_s[...] = jnp.zeros(mix_s.shape, F32)

    wh, wl = _split_bf16(wr_ref[...])
    tm = x_ref.shape[0]
    sub = min(256, tm)
    for r0 in range(0, tm, sub):
        rs = pl.ds(r0, sub)
        x1 = _layer_norm(DN_ALPHA * x_ref[rs, :] + mix_s[rs, :], g_ref[...], b_ref[...])
        x1_ref[rs, :] = x1
        xh, xl = _split_bf16(x1)
        lg = _dot_nt(wh, xh) + (_dot_nt(wh, xl) + _dot_nt(wl, xh)) + br_ref[...]
        meta_ref[:, rs] = _route_rows(lg)
    mix_s[...] = jnp.dot(mg_ref[...], wo_ref[...], preferred_element_type=F32)


def _proj_ln(merged, x, wo, g, b, wr, br, tm):
    n = merged.shape[0]
    nb = n // tm
    prev = lambda w: pl.BlockSpec((tm, w), lambda i: (jnp.maximum(i - 1, 0), 0))
    full = lambda a: pl.BlockSpec(a.shape, lambda i: (0, 0))
    return pl.pallas_call(
        _proj_ln_body,
        out_shape=(jax.ShapeDtypeStruct((n, D_MODEL), F32), jax.ShapeDtypeStruct((8, n), F32)),
        grid=(nb + 1,),
        in_specs=[pl.BlockSpec((tm, D_MODEL), lambda i: (jnp.minimum(i, nb - 1), 0)), prev(D_MODEL),
                  full(wo), full(g), full(b), full(wr), full(br)],
        out_specs=(prev(D_MODEL), pl.BlockSpec((8, tm), lambda i: (0, jnp.maximum(i - 1, 0)))),
        scratch_shapes=[pltpu.VMEM((tm, D_MODEL), F32)],
        compiler_params=_cparams(("arbitrary",)),
        name="proj_ln_router",
    )(merged, x, wo, g, b, wr, br)


MOE_CHUNK = 8


def _local_rows(tt):
    return -(-(2 * tt + N_EXPERTS * (MOE_CHUNK - 1)) // 128) * 128


def _dispatch(meta_t, n):
    tt = _tile(n, 256)
    n_t = n // tt
    ids = meta_t[0:2].astype(jnp.int32)
    onehot = (ids[:, :, None] == jnp.arange(N_EXPERTS, dtype=jnp.int32)).astype(jnp.int32).reshape(2, n_t, tt, N_EXPERTS)
    cnt_slot = jnp.sum(onehot, axis=2)
    cnt = cnt_slot[0] + cnt_slot[1]
    pc = (cnt + MOE_CHUNK - 1) // MOE_CHUNK * MOE_CHUNK
    lstart = jnp.cumsum(pc, axis=1) - pc
    rank = jnp.cumsum(onehot, axis=2) - onehot + jnp.stack([jnp.zeros_like(cnt), cnt_slot[0]])[:, :, None, :]
    lpos = jnp.sum(onehot * (lstart[None, :, None, :] + rank), axis=-1).reshape(2, n)
    seg = jnp.sum(pc, axis=0)
    pe = (seg + MOE_TILE - 1) // MOE_TILE * MOE_TILE
    ends = jnp.cumsum(pe)
    base = ends - pe
    gstart = base[None, :] + jnp.cumsum(pc, axis=0) - pc
    r_tot = -(-(2 * n + N_EXPERTS * (MOE_CHUNK - 1) * n_t) // MOE_TILE) * MOE_TILE + N_EXPERTS * MOE_TILE
    n_tiles = r_tot // MOE_TILE
    tile_start = jnp.arange(n_tiles, dtype=jnp.int32) * MOE_TILE
    tile_expert = jnp.minimum(jnp.sum((tile_start[:, None] >= ends[None, :]).astype(jnp.int32), axis=1), N_EXPERTS - 1)
    tables = dict(
        lstart=lstart.reshape(-1), gstart=gstart.reshape(-1), nchunk=(pc // MOE_CHUNK).reshape(-1),
        zstart=jnp.concatenate([base + seg, ends[-1:]]),
        zcount=jnp.concatenate([(pe - seg) // MOE_CHUNK, (r_tot - ends[-1:]) // MOE_TILE]),
        tile_expert=tile_expert, n_used=(ends[-1] // MOE_TILE).reshape(1))
    return tt, r_tot, lpos, tables


def _seg_loop(tab, t, fn):
    lstart_ref, gstart_ref, nchunk_ref = tab
    for e in range(N_EXPERTS):
        ls = lstart_ref[t * N_EXPERTS + e]
        gs = gstart_ref[t * N_EXPERTS + e]

        def body(j, c, ls=ls, gs=gs):
            fn(pl.multiple_of(ls + j * MOE_CHUNK, MOE_CHUNK), pl.multiple_of(gs + j * MOE_CHUNK, MOE_CHUNK))
            return c
        lax.fori_loop(0, nchunk_ref[t * N_EXPERTS + e], body, 0)


def _n_chunks(nchunk_ref, t):
    tot = nchunk_ref[t * N_EXPERTS]
    for e in range(1, N_EXPERTS):
        tot = tot + nchunk_ref[t * N_EXPERTS + e]
    return tot


def _sort_body(lstart_ref, gstart_ref, nchunk_ref, zstart_ref, zcount_ref, lpos_ref, x_ref, xs_hbm,
               xloc, zbuf, sem, zsem, *, n_t):
    t = pl.program_id(0)
    slot = lax.rem(t, 2)
    tab = (lstart_ref, gstart_ref, nchunk_ref)
    rows = xloc.shape[1]

    def chunk_copy(s, lrow, grow):
        return pltpu.make_async_copy(xloc.at[s, pl.ds(lrow, MOE_CHUNK)], xs_hbm.at[pl.ds(grow, MOE_CHUNK)], sem.at[s])

    def wait_tile(s, tile):
        def body(j, c):
            chunk_copy(s, 0, 0).wait()
            return c
        lax.fori_loop(0, _n_chunks(nchunk_ref, tile), body, 0)

    @pl.when(t == 0)
    def _():
        zbuf[...] = jnp.zeros_like(zbuf)
        zero_copy = lambda grow: pltpu.make_async_copy(
            zbuf.at[pl.ds(0, MOE_CHUNK)], xs_hbm.at[pl.ds(grow, MOE_CHUNK)], zsem.at[0])
        zero_tile = lambda grow: pltpu.make_async_copy(zbuf, xs_hbm.at[pl.ds(grow, MOE_TILE)], zsem.at[0])
        tail_start = zstart_ref[N_EXPERTS]
        for e in range(N_EXPERTS):
            def zb(j, c, e=e):
                zero_copy(pl.multiple_of(zstart_ref[e] + j * MOE_CHUNK, MOE_CHUNK)).start()
                return c
            lax.fori_loop(0, zcount_ref[e], zb, 0)

        def tb(j, c):
            zero_tile(pl.multiple_of(tail_start + j * MOE_TILE, MOE_TILE)).start()
            return c
        lax.fori_loop(0, zcount_ref[N_EXPERTS], tb, 0)
        for e in range(N_EXPERTS):
            def zw(j, c):
                zero_copy(0).wait()
                return c
            lax.fori_loop(0, zcount_ref[e], zw, 0)

        def tw(j, c):
            zero_tile(0).wait()
            return c
        lax.fori_loop(0, zcount_ref[N_EXPERTS], tw, 0)

    @pl.when(t >= 2)
    def _():
        wait_tile(slot, t - 2)

    l_iota = lax.broadcasted_iota(jnp.int32, (rows, x_ref.shape[0]), 0)
    perm = jnp.logical_or(l_iota == lpos_ref[0:1, :], l_iota == lpos_ref[1:2, :])
    perm = jnp.where(perm, 1.0, 0.0).astype(BF16)
    xloc[slot] = jnp.dot(perm, x_ref[...].astype(BF16), preferred_element_type=F32)
    _seg_loop(tab, t, lambda lrow, grow: chunk_copy(slot, lrow, grow).start())

    @pl.when(t == n_t - 1)
    def _():
        wait_tile(slot, t)
        if n_t >= 2:
            wait_tile(1 - slot, t - 1)


def _moe_sort(x1, lpos, tab, tt, r_tot):
    n = x1.shape[0]
    n_t = n // tt
    rows = _local_rows(tt)
    grid_spec = pltpu.PrefetchScalarGridSpec(
        num_scalar_prefetch=5,
        grid=(n_t,),
        in_specs=[pl.BlockSpec((2, tt), lambda t, *_: (0, t)),
                  pl.BlockSpec((tt, D_MODEL), lambda t, *_: (t, 0))],
        out_specs=pl.BlockSpec(memory_space=pl.ANY),
        scratch_shapes=[pltpu.VMEM((2, rows, D_MODEL), F32), pltpu.VMEM((MOE_TILE, D_MODEL), F32),
                        pltpu.SemaphoreType.DMA((2,)), pltpu.SemaphoreType.DMA((1,))])
    return pl.pallas_call(
        functools.partial(_sort_body, n_t=n_t),
        out_shape=jax.ShapeDtypeStruct((r_tot, D_MODEL), F32),
        grid_spec=grid_spec,
        compiler_params=_cparams(("arbitrary",)),
        name="moe_sort",
    )(tab["lstart"], tab["gstart"], tab["nchunk"], tab["zstart"], tab["zcount"], lpos, x1)


def _expert_body(te_ref, nu_ref, x_ref, wg_ref, wu_ref, wd_ref, o_ref):
    del te_ref
    used = pl.program_id(0) < nu_ref[0]

    @pl.when(used)
    def _():
        xb = x_ref[...].astype(BF16)
        hid = _gelu(jnp.dot(xb, wg_ref[0], preferred_element_type=F32)) * jnp.dot(xb, wu_ref[0], preferred_element_type=F32)
        o_ref[...] = jnp.dot(hid.astype(BF16), wd_ref[0], preferred_element_type=F32)

    @pl.when(jnp.logical_not(used))
    def _():
        o_ref[...] = jnp.zeros_like(o_ref)


def _moe_experts(xs, tab, wg, wu, wd):
    r_tot = xs.shape[0]
    last = lambda i, nu: jnp.minimum(i, nu[0] - 1)
    grid_spec = pltpu.PrefetchScalarGridSpec(
        num_scalar_prefetch=2,
        grid=(r_tot // MOE_TILE,),
        in_specs=[pl.BlockSpec((MOE_TILE, D_MODEL), lambda i, te, nu: (last(i, nu), 0)),
                  pl.BlockSpec((1, D_MODEL, D_EXPERT), lambda i, te, nu: (te[last(i, nu)], 0, 0)),
                  pl.BlockSpec((1, D_MODEL, D_EXPERT), lambda i, te, nu: (te[last(i, nu)], 0, 0)),
                  pl.BlockSpec((1, D_EXPERT, D_MODEL), lambda i, te, nu: (te[last(i, nu)], 0, 0))],
        out_specs=pl.BlockSpec((MOE_TILE, D_MODEL), lambda i, te, nu: (i, 0)))
    return pl.pallas_call(
        _expert_body,
        out_shape=jax.ShapeDtypeStruct((r_tot, D_MODEL), F32),
        grid_spec=grid_spec,
        compiler_params=_cparams(("arbitrary",)),
        name="moe_experts",
    )(tab["tile_expert"], tab["n_used"], xs, wg, wu, wd)


def _combine_body(lstart_ref, gstart_ref, nchunk_ref, ys_hbm, x1_ref, meta_ref, g_ref, b_ref, o_ref,
                  yloc, sem, *, n_t):
    t = pl.program_id(0)
    slot = lax.rem(t, 2)
    tab = (lstart_ref, gstart_ref, nchunk_ref)
    rows = yloc.shape[1]

    def chunk_copy(s, lrow, grow):
        return pltpu.make_async_copy(ys_hbm.at[pl.ds(grow, MOE_CHUNK)], yloc.at[s, pl.ds(lrow, MOE_CHUNK)], sem.at[s])

    def fetch(s, tile):
        _seg_loop(tab, tile, lambda lrow, grow: chunk_copy(s, lrow, grow).start())

    @pl.when(t == 0)
    def _():
        yloc[...] = jnp.zeros_like(yloc)
        fetch(0, 0)

    @pl.when(t + 1 < n_t)
    def _():
        fetch(1 - slot, t + 1)

    def wbody(j, c):
        chunk_copy(slot, 0, 0).wait()
        return c
    lax.fori_loop(0, _n_chunks(nchunk_ref, t), wbody, 0)

    meta = meta_ref[...]
    l_iota = lax.broadcasted_iota(jnp.int32, (x1_ref.shape[0], rows), 1).astype(F32)
    yb = yloc[slot].astype(BF16)
    pick = lambda k: jnp.dot(jnp.where(l_iota == meta[:, k:k + 1], 1.0, 0.0).astype(BF16), yb, preferred_element_type=F32)
    moe = meta[:, 2:3] * pick(0) + meta[:, 3:4] * pick(1)
    o_ref[...] = _layer_norm(DN_ALPHA * x1_ref[...] + moe, g_ref[...], b_ref[...])


def _moe_combine(ys, x1, meta_n, tab, g, b, tt):
    n = x1.shape[0]
    n_t = n // tt
    rows = _local_rows(tt)
    vec = lambda: pl.BlockSpec((1, D_MODEL), lambda t, *_: (0, 0))
    grid_spec = pltpu.PrefetchScalarGridSpec(
        num_scalar_prefetch=3,
        grid=(n_t,),
        in_specs=[pl.BlockSpec(memory_space=pl.ANY),
                  pl.BlockSpec((tt, D_MODEL), lambda t, *_: (t, 0)),
                  pl.BlockSpec((tt, 8), lambda t, *_: (t, 0)), vec(), vec()],
        out_specs=pl.BlockSpec((tt, D_MODEL), lambda t, *_: (t, 0)),
        scratch_shapes=[pltpu.VMEM((2, rows, D_MODEL), F32), pltpu.SemaphoreType.DMA((2,))])
    return pl.pallas_call(
        functools.partial(_combine_body, n_t=n_t),
        out_shape=jax.ShapeDtypeStruct((n, D_MODEL), F32),
        grid_spec=grid_spec,
        compiler_params=_cparams(("arbitrary",)),
        name="moe_combine_ln",
    )(tab["lstart"], tab["gstart"], tab["nchunk"], ys, x1, meta_n, g, b)


def _moe(x1, meta_t, p):
    n = x1.shape[0]
    tt, r_tot, lpos, tab = _dispatch(meta_t, n)
    xs = _moe_sort(x1, lpos, tab, tt, r_tot)
    ys = _moe_experts(xs, tab, p["w_gate"], p["w_up"], p["w_down"])
    meta_n = jnp.concatenate([lpos.astype(F32), meta_t[2:4], jnp.zeros((4, n), F32)], axis=0).T
    return _moe_combine(ys, x1, meta_n, tab, p["ln2_g"], p["ln2_b"], tt)


def _layer(x, conv_buf, h0, mem_kv, kv_bufs, p):
    b, t, _ = x.shape
    n = b * t
    x2 = x.reshape(n, D_MODEL)
    xb = x2.astype(BF16)
    tm_a = _tile(n, 2048)
    xrg = _matmul(xb, p["w_in"], 0, 2 * D_RNN, F32, tm_a, 512, "in_proj_rnn")
    qkv = _matmul(xb, p["w_in"], COL_Q, N_QKV_HEADS * HEAD_DIM, F32, tm_a, 512, "in_proj_qkv", split=4)
    qm = _matmul(xb, p["w_in"], COL_QM, D_MEM, BF16 if t % 16 == 0 else F32, tm_a, 512, "in_proj_qm", split=2)
    gates = _matmul(xb, p["w_in"], COL_GATES, N_GATE_COLS, BF16, tm_a, 512, "gate_proj", bias=p["b_gates"])

    xrg3 = xrg.reshape(b, t, 2 * D_RNN)
    a_lru, h_last = _rglru(xrg3, conv_buf, h0.reshape(b, 1, D_RNN), p["conv_w"], p["conv_b"], p["wax"],
                           p["b_a"], p["b_x"], p["lam"], _tile(t, 256))
    if kv_bufs is None:
        att = _attn_prompt(qkv, b, t)
    else:
        att = _attn_sample(qkv, kv_bufs, b, t)
    if mem_kv.ndim == 5:
        mem = _mem_attn_cache(qm, mem_kv, b, t)
    else:
        mem = _mem_attn(qm, mem_kv, b, t, _tile(t, 512), BF16 if t % 16 == 0 else F32)

    tm = _tile(n, 256)
    merged = _branch_merge(a_lru.reshape(n, D_RNN), att, mem, gates, p["w_br_lru"], p["w_br_att"], p["w_br_mem"], tm)
    x1, meta_t = _proj_ln(merged, x2, p["w_o"], p["ln1_g"], p["ln1_b"], p["w_router"], p["b_router"], _tile(n, 512))
    y = _moe(x1, meta_t, p)
    return y.reshape(b, t, D_MODEL), xrg3, qkv, h_last.reshape(b, D_RNN)


def _kv_rows(qkv, g, b, t, lo):
    q4 = qkv.reshape(N_QKV_HEADS, b, t, HEAD_DIM)

    def pick(which):
        h0 = (which * N_GROUPS + g) * HEADS
        a = lax.slice(q4, (h0, 0, lo, 0), (h0 + HEADS, b, t, HEAD_DIM))
        return jnp.transpose(a, (1, 2, 0, 3))
    return jnp.stack([pick(1), pick(2)], axis=2)


def _block_diag_gates(w_a, w_x):
    per = RNN_CHUNK // RNN_BLOCK
    chunks = []
    for c in range(N_RNN_CHUNKS):
        halves = []
        for w in (w_a, w_x):
            m = jnp.zeros((RNN_CHUNK, RNN_CHUNK), F32)
            for i in range(per):
                m = lax.dynamic_update_slice(m, w[c * per + i], (i * RNN_BLOCK, i * RNN_BLOCK))
            halves.append(m)
        chunks.append(jnp.concatenate(halves, axis=1))
    return jnp.stack(chunks).astype(BF16)


def kernel(x_prompt, x_sample, cache_kv_w128, cache_kv_w512, cache_kv_w2048, cache_mem_kv, state_h, state_conv, mem_prompt, w_in, b_gates, conv_w, conv_b, w_a, b_a, w_x, b_x, lru_lambda, w_br_lru, w_br_att, w_br_mem, w_o, w_mem_kv, ln1_g, ln1_b, w_rg, b_rg, w_re, b_re, w_gate, w_up, w_down, ln2_g, ln2_b):
    row = lambda v: v.reshape(1, -1).astype(F32)
    w_router = jnp.zeros((ROUTER_ROWS, D_MODEL), F32)
    w_router = w_router.at[:N_EXPERT_GROUPS].set(w_rg.T).at[N_EXPERT_GROUPS:N_EXPERT_GROUPS + N_EXPERTS].set(w_re.T)
    b_router = jnp.zeros((ROUTER_ROWS, 1), F32)
    b_router = b_router.at[:N_EXPERT_GROUPS, 0].set(b_rg).at[N_EXPERT_GROUPS:N_EXPERT_GROUPS + N_EXPERTS, 0].set(b_re)
    p = dict(
        w_in=w_in.astype(BF16), b_gates=row(b_gates), conv_w=conv_w, conv_b=row(conv_b), wax=_block_diag_gates(w_a, w_x),
        b_a=row(b_a), b_x=row(b_x), lam=row(lru_lambda),
        w_br_lru=w_br_lru.astype(BF16), w_br_att=w_br_att.astype(BF16), w_br_mem=w_br_mem.astype(BF16),
        w_o=w_o.astype(BF16), ln1_g=row(ln1_g), ln1_b=row(ln1_b), w_router=w_router, b_router=b_router,
        w_gate=w_gate.astype(BF16), w_up=w_up.astype(BF16), w_down=w_down.astype(BF16),
        ln2_g=row(ln2_g), ln2_b=row(ln2_b))

    bp, s, _ = x_prompt.shape
    bs, ts, _ = x_sample.shape
    mem_rows = mem_prompt.reshape(bp * N_MEM, D_MODEL)
    mem_kv_p = _matmul(mem_rows, w_mem_kv, 0, 2 * D_MEM, F32, _tile(bp * N_MEM, 512), 512, "mem_kv_proj")
    mem_kv_p = mem_kv_p.reshape(bp, N_MEM, 2 * D_MEM)

    y_p, xrg_p, qkv_p, h_p = _layer(x_prompt, jnp.zeros((bp, CONV_W - 1, D_RNN), F32), jnp.zeros((bp, D_RNN), F32),
                                    mem_kv_p, None, p)
    y_s, xrg_s, qkv_s, h_s = _layer(x_sample, state_conv, state_h, cache_mem_kv,
                                    (cache_kv_w128, cache_kv_w512, cache_kv_w2048), p)

    conv_p = jnp.concatenate([jnp.zeros((bp, CONV_W - 1, D_RNN), F32), xrg_p[:, :, :D_RNN]], axis=1)[:, -(CONV_W - 1):]
    conv_s = jnp.concatenate([state_conv, xrg_s[:, :, :D_RNN]], axis=1)[:, -(CONV_W - 1):]
    kv_p = [_kv_rows(qkv_p, g, bp, s, s - min(KEYS * d, s)) for g, d in enumerate(DILATIONS)]
    kv_s = [_kv_rows(qkv_s, g, bs, ts, 0) for g in range(N_GROUPS)]
    return (y_p, y_s, kv_p[0], kv_p[1], kv_p[2],
            mem_kv_p.reshape(bp, N_MEM, 2, MEM_HEADS, MEM_HEAD_DIM), h_p, conv_p,
            kv_s[0], kv_s[1], kv_s[2], h_s, conv_s)
```

```python
import functools

import jax
import jax.numpy as jnp
from jax import lax
from jax.experimental import pallas as pl
from jax.experimental.pallas import tpu as pltpu

F32 = jnp.float32
BF16 = jnp.bfloat16

D_MODEL = 2048
D_RNN = 1536
N_RNN_BLOCKS = 16
RNN_BLOCK = D_RNN // N_RNN_BLOCKS
RNN_CHUNK = 384
N_RNN_CHUNKS = D_RNN // RNN_CHUNK
CONV_W = 4
LRU_C = 8.0
HEAD_DIM = 128
HEADS = 4
DILATIONS = (1, 4, 16)
KEYS = 128
N_GROUPS = 3
D_ATT_OUT = HEADS * HEAD_DIM
N_QKV_HEADS = 3 * N_GROUPS * HEADS
ATT_SCALE = HEAD_DIM ** -0.5
ATT_TILE = KEYS * max(DILATIONS)
N_MEM = 256
MEM_HEADS = 4
MEM_HEAD_DIM = 256
D_MEM = MEM_HEADS * MEM_HEAD_DIM
MEM_SCALE = MEM_HEAD_DIM ** -0.5
N_EXPERT_GROUPS = 4
EXPERTS_PER_GROUP = 4
N_EXPERTS = 16
D_EXPERT = 512
MOE_TILE = 512
DN_ALPHA = 2.0 ** 0.25
LN_EPS = 1e-5
NEG_INF = -1e30

COL_Q = 2 * D_RNN
COL_QM = COL_Q + N_QKV_HEADS * HEAD_DIM
COL_GATES = COL_QM + D_MEM
N_GATE_COLS = 3 * D_MODEL
ROUTER_ROWS = 32

VMEM_LIMIT = 56 * 1024 * 1024


def _cparams(sem):
    return pltpu.CompilerParams(dimension_semantics=sem, vmem_limit_bytes=VMEM_LIMIT)


def _gelu(x):
    return 0.5 * x * (1.0 + jnp.tanh(0.7978845608028654 * (x + 0.044715 * (x * x * x))))


def _layer_norm(x, g, b):
    mu = jnp.mean(x, axis=-1, keepdims=True)
    xc = x - mu
    var = jnp.mean(xc * xc, axis=-1, keepdims=True)
    return xc * lax.rsqrt(var + LN_EPS) * g + b


def _dot_nt(a, b):
    return lax.dot_general(a, b, (((1,), (1,)), ((), ())), preferred_element_type=F32)


def _tile(m, cap):
    t = min(m, cap)
    assert m % t == 0, (m, t)
    return t


def _mm_body(x_ref, w_ref, *rest, split, gate):
    acc = jnp.dot(x_ref[...].astype(BF16), w_ref[...].astype(BF16), preferred_element_type=F32)
    if gate:
        b_ref, o_ref = rest
        acc = 0.5 * jnp.tanh(0.5 * (acc + b_ref[...])) + 0.5
    else:
        (o_ref,) = rest
    if split == 1:
        o_ref[...] = acc.astype(o_ref.dtype)
    else:
        w = acc.shape[1] // split
        for s in range(split):
            o_ref[s] = acc[:, s * w:(s + 1) * w].astype(o_ref.dtype)


def _matmul(x, w, col_off, n_cols, out_dtype, tm, tn, name, bias=None, split=1):
    m, k = x.shape
    cb = col_off // tn
    in_specs = [pl.BlockSpec((tm, k), lambda i, j: (i, 0)),
                pl.BlockSpec((k, tn), lambda i, j: (0, j + cb))]
    args = [x, w]
    if bias is not None:
        in_specs.append(pl.BlockSpec((1, tn), lambda i, j: (0, j)))
        args.append(bias)
    if split == 1:
        out_shape = jax.ShapeDtypeStruct((m, n_cols), out_dtype)
        out_spec = pl.BlockSpec((tm, tn), lambda i, j: (i, j))
    else:
        out_shape = jax.ShapeDtypeStruct((n_cols * split // tn, m, tn // split), out_dtype)
        out_spec = pl.BlockSpec((split, tm, tn // split), lambda i, j: (j, i, 0))
    return pl.pallas_call(
        functools.partial(_mm_body, split=split, gate=bias is not None),
        out_shape=out_shape,
        grid=(m // tm, n_cols // tn),
        in_specs=in_specs,
        out_specs=out_spec,
        compiler_params=_cparams(("parallel", "arbitrary")),
        name=name,
    )(*args)


def _rglru_body(xr_ref, xg_ref, cbuf_ref, h0_ref, cw_ref, cb_ref, wax_ref, ba_ref, bx_ref, lam_ref,
                out_ref, hl_ref, ext_s, a_s, u_s, h_s, *, tt):
    t = pl.program_id(1)

    @pl.when(t == 0)
    def _():
        ext_s[0:8, :] = jnp.zeros((8, D_RNN), F32)
        ext_s[5:8, :] = cbuf_ref[0]
        h_s[...] = jnp.broadcast_to(h0_ref[0], (8, D_RNN))

    @pl.when(t > 0)
    def _():
        ext_s[0:8, :] = ext_s[tt:tt + 8, :]

    ext_s[8:8 + tt, :] = xr_ref[...]
    cw = cw_ref[...]
    xc = (cb_ref[...] + cw[3:4, :] * ext_s[8:8 + tt, :] + cw[2:3, :] * ext_s[7:7 + tt, :]
          + cw[1:2, :] * ext_s[6:6 + tt, :] + cw[0:1, :] * ext_s[5:5 + tt, :])
    xcb = xc.astype(BF16)
    r_parts, i_parts = [], []
    for c in range(N_RNN_CHUNKS):
        g = jnp.dot(xcb[:, c * RNN_CHUNK:(c + 1) * RNN_CHUNK], wax_ref[c], preferred_element_type=F32)
        r_parts.append(g[:, :RNN_CHUNK])
        i_parts.append(g[:, RNN_CHUNK:])
    sigmoid = lambda z: 0.5 * jnp.tanh(0.5 * z) + 0.5
    r = sigmoid(jnp.concatenate(r_parts, axis=1) + ba_ref[...])
    gi = sigmoid(jnp.concatenate(i_parts, axis=1) + bx_ref[...])
    nl = -lam_ref[...]
    softplus = jnp.maximum(nl, 0.0) + jnp.log1p(jnp.exp(-jnp.abs(nl)))
    log_a = (-LRU_C) * r * softplus
    th = jnp.tanh(log_a)
    a_s[...] = jnp.exp(log_a)
    u_s[...] = jnp.sqrt(-2.0 * th / (1.0 - th)) * (gi * xc)

    rows = lax.broadcasted_iota(jnp.int32, (8, D_RNN), 0)

    def blk(i, h):
        r0 = pl.multiple_of(i * 8, 8)
        ab = a_s[pl.ds(r0, 8), :]
        ub = u_s[pl.ds(r0, 8), :]
        for s in (1, 2, 4):
            keep = rows >= s
            ub = ab * jnp.where(keep, pltpu.roll(ub, s, 0), 0.0) + ub
            ab = ab * jnp.where(keep, pltpu.roll(ab, s, 0), 1.0)
        hb = ab * h + ub
        u_s[pl.ds(r0, 8), :] = hb
        return jnp.broadcast_to(hb[7:8, :], (8, D_RNN))

    h_fin = lax.fori_loop(0, tt // 8, blk, h_s[...])
    h_s[...] = h_fin
    hl_ref[0] = h_fin[0:1, :]
    out_ref[...] = (u_s[...] * _gelu(xg_ref[...])).astype(out_ref.dtype)


def _rglru(xrg, off, b, t, conv_buf, h0, conv_w, conv_b, wax, b_a, b_x, lam, tt, out_dtype):
    nt = t // tt
    ob = off // tt
    vec = lambda: pl.BlockSpec((1, D_RNN), lambda i, j: (0, 0))
    return pl.pallas_call(
        functools.partial(_rglru_body, tt=tt),
        out_shape=(jax.ShapeDtypeStruct((b * t, D_RNN), out_dtype), jax.ShapeDtypeStruct((b, 1, D_RNN), F32)),
        grid=(b, nt),
        in_specs=[pl.BlockSpec((tt, D_RNN), lambda i, j: (ob + i * nt + j, 0)),
                  pl.BlockSpec((tt, D_RNN), lambda i, j: (ob + i * nt + j, 1)),
                  pl.BlockSpec((1, CONV_W - 1, D_RNN), lambda i, j: (i, 0, 0)),
                  pl.BlockSpec((1, 1, D_RNN), lambda i, j: (i, 0, 0)),
                  pl.BlockSpec((CONV_W, D_RNN), lambda i, j: (0, 0)),
                  vec(),
                  pl.BlockSpec((N_RNN_CHUNKS, RNN_CHUNK, 2 * RNN_CHUNK), lambda i, j: (0, 0, 0)),
                  vec(), vec(), vec()],
        out_specs=(pl.BlockSpec((tt, D_RNN), lambda i, j: (i * nt + j, 0)),
                   pl.BlockSpec((1, 1, D_RNN), lambda i, j: (i, 0, 0))),
        scratch_shapes=[pltpu.VMEM((tt + 8, D_RNN), F32), pltpu.VMEM((tt, D_RNN), F32),
                        pltpu.VMEM((tt, D_RNN), F32), pltpu.VMEM((8, D_RNN), F32)],
        compiler_params=_cparams(("parallel", "arbitrary")),
        name="rglru",
    )(xrg, xrg, conv_buf, h0.reshape(b, 1, D_RNN), conv_w, conv_b, wax, b_a, b_x, lam)


def _band_block(q, k, v, bias):
    s = _dot_nt(q, k) * ATT_SCALE + bias
    m = jnp.max(s, axis=-1, keepdims=True)
    p = jnp.exp(s - m)
    l = jnp.sum(p, axis=-1, keepdims=True)
    o = jnp.dot(p.astype(BF16), v, preferred_element_type=F32) / l
    return o, m + jnp.log(l)


def _attn_prompt_body(*refs):
    q_refs, k_refs, v_refs, kh_refs, vh_refs = (refs[3 * i:3 * i + 3] for i in range(5))
    o_ref, og_s, lg_s = refs[15:]
    row = lax.broadcasted_iota(jnp.int32, (KEYS, 2 * KEYS), 0)
    col = lax.broadcasted_iota(jnp.int32, (KEYS, 2 * KEYS), 1)
    band = jnp.logical_and(col >= row, col <= row + KEYS)
    bias = jnp.where(band, 0.0, NEG_INF)
    has_prev = pl.program_id(1) > 0
    bias_first = jnp.where(jnp.logical_and(band, jnp.logical_or(col >= KEYS, has_prev)), 0.0, NEG_INF)

    def put(g, rows, o, lse):
        og_s[g, rows, :] = o
        lg_s[g, rows, :] = jnp.broadcast_to(lse, (KEYS, HEAD_DIM))

    def rows_of(start, size, d):
        return pl.ds(start, size, stride=d) if d > 1 else pl.ds(start, size)

    for g, d in enumerate(DILATIONS):
        q_ref, k_ref, v_ref, kh_ref, vh_ref = q_refs[g], k_refs[g], v_refs[g], kh_refs[g], vh_refs[g]
        nqb = ATT_TILE // (d * KEYS)
        for r in range(d):
            own = rows_of(r, KEYS, d)
            kk = jnp.concatenate([kh_ref[0, own, :], k_ref[0, own, :]], axis=0).astype(BF16)
            vv = jnp.concatenate([vh_ref[0, own, :], v_ref[0, own, :]], axis=0).astype(BF16)
            o, lse = _band_block(q_ref[0, own, :].astype(BF16), kk, vv, bias_first)
            put(g, own, o, lse)
            for qb in range(1, nqb):
                rows = rows_of(qb * KEYS * d + r, KEYS, d)
                keys = rows_of((qb - 1) * KEYS * d + r, 2 * KEYS, d)
                o, lse = _band_block(q_ref[0, rows, :].astype(BF16), k_ref[0, keys, :].astype(BF16),
                                     v_ref[0, keys, :].astype(BF16), bias)
                put(g, rows, o, lse)

    l0, l1, l2 = lg_s[0], lg_s[1], lg_s[2]
    m = jnp.maximum(jnp.maximum(l0, l1), l2)
    e0, e1, e2 = jnp.exp(l0 - m), jnp.exp(l1 - m), jnp.exp(l2 - m)
    o_ref[...] = ((e0 * og_s[0] + e1 * og_s[1] + e2 * og_s[2]) / (e0 + e1 + e2)).astype(o_ref.dtype)


def _attn_prompt(qkv, b, s):
    assert s % ATT_TILE == 0
    nt = s // ATT_TILE
    n = b * s

    def cur(which, g):
        return pl.BlockSpec((1, ATT_TILE, HEAD_DIM),
                            lambda i, j, h: ((which * N_GROUPS + g) * HEADS + h, i * nt + j, 0))

    def halo(which, g):
        rows = KEYS * DILATIONS[g]
        per = ATT_TILE // rows
        return pl.BlockSpec((1, rows, HEAD_DIM),
                            lambda i, j, h: ((which * N_GROUPS + g) * HEADS + h,
                                             jnp.maximum((i * nt + j) * per - 1, 0), 0))

    in_specs = ([cur(0, g) for g in range(N_GROUPS)] + [cur(1, g) for g in range(N_GROUPS)]
                + [cur(2, g) for g in range(N_GROUPS)] + [halo(1, g) for g in range(N_GROUPS)]
                + [halo(2, g) for g in range(N_GROUPS)])
    return pl.pallas_call(
        _attn_prompt_body,
        out_shape=jax.ShapeDtypeStruct((n, D_ATT_OUT), BF16),
        grid=(b, nt, HEADS),
        in_specs=in_specs,
        out_specs=pl.BlockSpec((ATT_TILE, HEAD_DIM), lambda i, j, h: (i * nt + j, h)),
        scratch_shapes=[pltpu.VMEM((N_GROUPS, ATT_TILE, HEAD_DIM), F32),
                        pltpu.VMEM((N_GROUPS, ATT_TILE, HEAD_DIM), F32)],
        compiler_params=_cparams(("parallel", "parallel", "parallel")),
        name="attn_prompt",
    )(*([qkv] * 15))


def _attn_sample_group(q4, kn4, vn4, c_ref, d, t_new):
    wb = c_ref.shape[1] // (2 * HEADS)
    nr = HEADS * t_new
    zeros = jnp.zeros((t_new, HEAD_DIM), F32)
    qbd = jnp.concatenate(
        [jnp.concatenate([q4[h] if hh == h else zeros for hh in range(HEADS)], axis=1) for h in range(HEADS)],
        axis=0).astype(BF16)
    kn = jnp.concatenate([kn4[h] for h in range(HEADS)], axis=1).astype(BF16)
    vn = jnp.concatenate([vn4[h] for h in range(HEADS)], axis=1).astype(BF16)
    cache_rows = lambda kv, h: c_ref[0, pl.ds(kv * HEADS + h, wb, stride=2 * HEADS), :]
    kc = jnp.concatenate([cache_rows(0, h) for h in range(HEADS)], axis=1).astype(BF16)
    vc = jnp.concatenate([cache_rows(1, h) for h in range(HEADS)], axis=1).astype(BF16)
    tq_c = lax.broadcasted_iota(jnp.int32, (nr, wb), 0) & (t_new - 1)
    dist_c = wb + tq_c - lax.broadcasted_iota(jnp.int32, (nr, wb), 1)
    ok_c = jnp.logical_and((dist_c & (d - 1)) == 0, dist_c <= KEYS * d)
    tq_n = lax.broadcasted_iota(jnp.int32, (nr, t_new), 0) & (t_new - 1)
    dist_n = tq_n - lax.broadcasted_iota(jnp.int32, (nr, t_new), 1)
    ok_n = jnp.logical_and(jnp.logical_and(dist_n >= 0, (dist_n & (d - 1)) == 0), dist_n <= KEYS * d)
    s_c = jnp.where(ok_c, _dot_nt(qbd, kc) * ATT_SCALE, NEG_INF)
    s_n = jnp.where(ok_n, _dot_nt(qbd, kn) * ATT_SCALE, NEG_INF)
    m = jnp.maximum(jnp.max(s_c, axis=-1, keepdims=True), jnp.max(s_n, axis=-1, keepdims=True))
    p_c = jnp.exp(s_c - m)
    p_n = jnp.exp(s_n - m)
    l = jnp.sum(p_c, axis=-1, keepdims=True) + jnp.sum(p_n, axis=-1, keepdims=True)
    o = (jnp.dot(p_c.astype(BF16), vc, preferred_element_type=F32)
         + jnp.dot(p_n.astype(BF16), vn, preferred_element_type=F32)) / l
    lse = m + jnp.log(l)
    o = jnp.concatenate(
        [o[h * t_new:(h + 1) * t_new, h * HEAD_DIM:(h + 1) * HEAD_DIM] for h in range(HEADS)], axis=1)
    lse = jnp.concatenate(
        [jnp.broadcast_to(lse[h * t_new:(h + 1) * t_new], (t_new, HEAD_DIM)) for h in range(HEADS)], axis=1)
    return o, lse


def _attn_sample_body(*refs, t_new):
    q_refs, k_refs, v_refs, c_refs = (refs[3 * i:3 * i + 3] for i in range(4))
    o_ref = refs[12]
    outs = [_attn_sample_group(q_refs[g][...], k_refs[g][...], v_refs[g][...], c_refs[g], d, t_new)
            for g, d in enumerate(DILATIONS)]
    (o0, l0), (o1, l1), (o2, l2) = outs
    m = jnp.maximum(jnp.maximum(l0, l1), l2)
    e0, e1, e2 = jnp.exp(l0 - m), jnp.exp(l1 - m), jnp.exp(l2 - m)
    o_ref[0] = (e0 * o0 + e1 * o1 + e2 * o2) / (e0 + e1 + e2)


def _attn_sample(qkv, caches, b, t_new, off):
    assert t_new & (t_new - 1) == 0 and off % t_new == 0
    ob = off // t_new
    new = lambda which, g: pl.BlockSpec((HEADS, t_new, HEAD_DIM), lambda i: (which * N_GROUPS + g, ob + i, 0))
    caches2 = [c.reshape(b, c.shape[1] * 2 * HEADS, HEAD_DIM) for c in caches]
    in_specs = ([new(0, g) for g in range(N_GROUPS)] + [new(1, g) for g in range(N_GROUPS)]
                + [new(2, g) for g in range(N_GROUPS)]
                + [pl.BlockSpec((1, c.shape[1], HEAD_DIM), lambda i: (i, 0, 0)) for c in caches2])
    att = pl.pallas_call(
        functools.partial(_attn_sample_body, t_new=t_new),
        out_shape=jax.ShapeDtypeStruct((b, t_new, D_ATT_OUT), F32),
        grid=(b,),
        in_specs=in_specs,
        out_specs=pl.BlockSpec((1, t_new, D_ATT_OUT), lambda i: (i, 0, 0)),
        compiler_params=_cparams(("parallel",)),
        name="attn_sample",
    )(*([qkv] * 9), *caches2)
    return att.reshape(b * t_new, D_ATT_OUT)


def _mem_attn_body(q_ref, k_ref, v_ref, o_ref):
    s = _dot_nt(q_ref[0].astype(BF16), k_ref[0].astype(BF16)) * MEM_SCALE
    m = jnp.max(s, axis=-1, keepdims=True)
    p = jnp.exp(s - m)
    l = jnp.sum(p, axis=-1, keepdims=True)
    o = jnp.dot(p.astype(BF16), v_ref[0].astype(BF16), preferred_element_type=F32) / l
    o_ref[...] = o.astype(o_ref.dtype)


def _mem_attn(qm, mem_kv, b, t, tm, out_dtype):
    nt = t // tm
    return pl.pallas_call(
        _mem_attn_body,
        out_shape=jax.ShapeDtypeStruct((b * t, D_MEM), out_dtype),
        grid=(b, nt, MEM_HEADS),
        in_specs=[pl.BlockSpec((1, tm, MEM_HEAD_DIM), lambda i, j, h: (h, i * nt + j, 0)),
                  pl.BlockSpec((1, N_MEM, MEM_HEAD_DIM), lambda i, j, h: (i, 0, h)),
                  pl.BlockSpec((1, N_MEM, MEM_HEAD_DIM), lambda i, j, h: (i, 0, MEM_HEADS + h))],
        out_specs=pl.BlockSpec((tm, MEM_HEAD_DIM), lambda i, j, h: (i * nt + j, h)),
        compiler_params=_cparams(("parallel", "parallel", "arbitrary")),
        name="mem_attn",
    )(qm, mem_kv, mem_kv)


def _mem_attn_cache_body(q_ref, c_ref, o_ref):
    outs = []
    for h in range(MEM_HEADS):
        k = c_ref[0, :, 0, h, :].astype(BF16)
        v = c_ref[0, :, 1, h, :].astype(BF16)
        s = _dot_nt(q_ref[h].astype(BF16), k) * MEM_SCALE
        m = jnp.max(s, axis=-1, keepdims=True)
        p = jnp.exp(s - m)
        l = jnp.sum(p, axis=-1, keepdims=True)
        outs.append(jnp.dot(p.astype(BF16), v, preferred_element_type=F32) / l)
    o_ref[...] = jnp.concatenate(outs, axis=1)


def _mem_attn_cache(qm, cache, b, t, off):
    ob = off // t
    return pl.pallas_call(
        _mem_attn_cache_body,
        out_shape=jax.ShapeDtypeStruct((b * t, D_MEM), F32),
        grid=(b,),
        in_specs=[pl.BlockSpec((MEM_HEADS, t, MEM_HEAD_DIM), lambda i: (0, ob + i, 0)),
                  pl.BlockSpec((1, N_MEM, 2, MEM_HEADS, MEM_HEAD_DIM), lambda i: (i, 0, 0, 0, 0))],
        out_specs=pl.BlockSpec((t, D_MEM), lambda i: (i, 0)),
        compiler_params=_cparams(("parallel",)),
        name="mem_attn_cache",
    )(qm, cache)


def _branch_body(al_ref, att_ref, mem_ref, gt_ref, wl_ref, wa_ref, wm_ref, out_ref):
    acc = gt_ref[:, 0:D_MODEL].astype(F32) * jnp.dot(al_ref[...].astype(BF16), wl_ref[...], preferred_element_type=F32)
    acc = acc + gt_ref[:, D_MODEL:2 * D_MODEL].astype(F32) * jnp.dot(att_ref[...].astype(BF16), wa_ref[...], preferred_element_type=F32)
    acc = acc + gt_ref[:, 2 * D_MODEL:3 * D_MODEL].astype(F32) * jnp.dot(mem_ref[...].astype(BF16), wm_ref[...], preferred_element_type=F32)
    out_ref[...] = acc.astype(out_ref.dtype)


def _branch_merge(a_lru, att, mem, gates, goff, wl, wa, wm, tm):
    n = a_lru.shape[0]
    gb = goff // tm
    row = lambda w: pl.BlockSpec((tm, w), lambda i: (i, 0))
    full = lambda a: pl.BlockSpec(a.shape, lambda i: (0, 0))
    return pl.pallas_call(
        _branch_body,
        out_shape=jax.ShapeDtypeStruct((n, D_MODEL), BF16),
        grid=(n // tm,),
        in_specs=[row(D_RNN), row(D_ATT_OUT), row(D_MEM), pl.BlockSpec((tm, N_GATE_COLS), lambda i: (gb + i, 0)),
                  full(wl), full(wa), full(wm)],
        out_specs=row(D_MODEL),
        compiler_params=_cparams(("parallel",)),
        name="branch_merge",
    )(a_lru, att, mem, gates, wl, wa, wm)


def _split_bf16(x):
    hi = x.astype(BF16)
    return hi, (x - hi.astype(F32)).astype(BF16)


def _route_rows(lg):
    g = [lg[i:i + 1, :] for i in range(N_EXPERT_GROUPS)]
    gmax = jnp.maximum(jnp.maximum(g[0], g[1]), jnp.maximum(g[2], g[3]))
    gidx = jnp.where(g[0] == gmax, 0.0, jnp.where(g[1] == gmax, 1.0, jnp.where(g[2] == gmax, 2.0, 3.0)))
    g_p = 1.0 / (jnp.exp(g[0] - gmax) + jnp.exp(g[1] - gmax) + jnp.exp(g[2] - gmax) + jnp.exp(g[3] - gmax))
    e = []
    for k in range(EXPERTS_PER_GROUP):
        rows = [lg[N_EXPERT_GROUPS + gg * EXPERTS_PER_GROUP + k:N_EXPERT_GROUPS + gg * EXPERTS_PER_GROUP + k + 1, :]
                for gg in range(N_EXPERT_GROUPS)]
        e.append(jnp.where(gidx == 0.0, rows[0], jnp.where(gidx == 1.0, rows[1], jnp.where(gidx == 2.0, rows[2], rows[3]))))

    def first_argmax(v):
        mx = jnp.maximum(jnp.maximum(v[0], v[1]), jnp.maximum(v[2], v[3]))
        ix = jnp.where(v[0] == mx, 0.0, jnp.where(v[1] == mx, 1.0, jnp.where(v[2] == mx, 2.0, 3.0)))
        return mx, ix

    v1, i1 = first_argmax(e)
    v2, i2 = first_argmax([jnp.where(i1 == float(k), -jnp.inf, e[k]) for k in range(EXPERTS_PER_GROUP)])
    ex = jnp.exp(v2 - v1)
    w1 = g_p / (1.0 + ex)
    w2 = g_p * ex / (1.0 + ex)
    base = gidx * float(EXPERTS_PER_GROUP)
    zero = jnp.zeros_like(w1)
    return jnp.concatenate([base + i1, base + i2, w1, w2, zero, zero, zero, zero], axis=0)


def _proj_ln_body(mg_ref, x_ref, wo_ref, g_ref, b_ref, wr_ref, br_ref, x1_ref, meta_ref, mix_s):
    @pl.when(pl.program_id(0) == 0)
    def _():
        mix_s[...] = jnp.zeros(mix_s.shape, F32)

    wh, wl = _split_bf16(wr_ref[...])
    tm = x_ref.shape[0]
    sub = min(256, tm)
    for r0 in range(0, tm, sub):
        rs = pl.ds(r0, sub)
        x1 = _layer_norm(DN_ALPHA * x_ref[rs, :] + mix_s[rs, :], g_ref[...], b_ref[...])
        x1_ref[rs, :] = x1
        xh, xl = _split_bf16(x1)
        lg = _dot_nt(wh, xh) + (_dot_nt(wh, xl) + _dot_nt(wl, xh)) + br_ref[...]
        meta_ref[:, rs] = _route_rows(lg)
    mix_s[...] = jnp.dot(mg_ref[...], wo_ref[...], preferred_element_type=F32)


def _proj_ln(merged, x, wo, g, b, wr, br, tm):
    n = merged.shape[0]
    nb = n // tm
    prev = lambda w: pl.BlockSpec((tm, w), lambda i: (jnp.maximum(i - 1, 0), 0))
    full = lambda a: pl.BlockSpec(a.shape, lambda i: (0, 0))
    return pl.pallas_call(
        _proj_ln_body,
        out_shape=(jax.ShapeDtypeStruct((n, D_MODEL), F32), jax.ShapeDtypeStruct((8, n), F32)),
        grid=(nb + 1,),
        in_specs=[pl.BlockSpec((tm, D_MODEL), lambda i: (jnp.minimum(i, nb - 1), 0)), prev(D_MODEL),
                  full(wo), full(g), full(b), full(wr), full(br)],
        out_specs=(prev(D_MODEL), pl.BlockSpec((8, tm), lambda i: (0, jnp.maximum(i - 1, 0)))),
        scratch_shapes=[pltpu.VMEM((tm, D_MODEL), F32)],
        compiler_params=_cparams(("arbitrary",)),
        name="proj_ln_router",
    )(merged, x, wo, g, b, wr, br)


MOE_CHUNK = 8


def _local_rows(tt):
    return -(-(2 * tt + N_EXPERTS * (MOE_CHUNK - 1)) // 128) * 128


def _dispatch(meta_t, n, tt):
    n_t = n // tt
    ids = meta_t[0:2].astype(jnp.int32)
    onehot = (ids[:, :, None] == jnp.arange(N_EXPERTS, dtype=jnp.int32)).astype(jnp.int32).reshape(2, n_t, tt, N_EXPERTS)
    cnt_slot = jnp.sum(onehot, axis=2)
    cnt = cnt_slot[0] + cnt_slot[1]
    pc = (cnt + MOE_CHUNK - 1) // MOE_CHUNK * MOE_CHUNK
    lstart = jnp.cumsum(pc, axis=1) - pc
    tri = (jnp.arange(tt)[:, None] >= jnp.arange(tt)[None, :]).astype(F32)
    csum = jnp.einsum("ut,snte->snue", tri, onehot.astype(F32)).astype(jnp.int32)
    rank = csum - onehot + jnp.stack([jnp.zeros_like(cnt), cnt_slot[0]])[:, :, None, :]
    lpos = jnp.sum(onehot * (lstart[None, :, None, :] + rank), axis=-1).reshape(2, n)
    seg = jnp.sum(pc, axis=0)
    pe = (seg + MOE_TILE - 1) // MOE_TILE * MOE_TILE
    ends = jnp.cumsum(pe)
    base = ends - pe
    gstart = base[None, :] + jnp.cumsum(pc, axis=0) - pc
    r_tot = -(-(2 * n + N_EXPERTS * (MOE_CHUNK - 1) * n_t) // MOE_TILE) * MOE_TILE + N_EXPERTS * MOE_TILE
    n_tiles = r_tot // MOE_TILE
    tile_start = jnp.arange(n_tiles, dtype=jnp.int32) * MOE_TILE
    tile_expert = jnp.minimum(jnp.sum((tile_start[:, None] >= ends[None, :]).astype(jnp.int32), axis=1), N_EXPERTS - 1)
    tables = dict(
        lstart=lstart.reshape(-1), gstart=gstart.reshape(-1), nchunk=(pc // MOE_CHUNK).reshape(-1),
        zstart=jnp.concatenate([base + seg, ends[-1:]]),
        zcount=jnp.concatenate([(pe - seg) // MOE_CHUNK, (r_tot - ends[-1:]) // MOE_TILE]),
        tile_expert=tile_expert, n_used=(ends[-1] // MOE_TILE).reshape(1))
    return r_tot, lpos, tables


def _seg_loop(tab, t, fn):
    lstart_ref, gstart_ref, nchunk_ref = tab
    for e in range(N_EXPERTS):
        ls = lstart_ref[t * N_EXPERTS + e]
        gs = gstart_ref[t * N_EXPERTS + e]

        def body(j, c, ls=ls, gs=gs):
            fn(pl.multiple_of(ls + j * MOE_CHUNK, MOE_CHUNK), pl.multiple_of(gs + j * MOE_CHUNK, MOE_CHUNK))
            return c
        lax.fori_loop(0, nchunk_ref[t * N_EXPERTS + e], body, 0)


def _n_chunks(nchunk_ref, t):
    tot = nchunk_ref[t * N_EXPERTS]
    for e in range(1, N_EXPERTS):
        tot = tot + nchunk_ref[t * N_EXPERTS + e]
    return tot


def _sort_body(lstart_ref, gstart_ref, nchunk_ref, zstart_ref, zcount_ref, lpos_ref, xa_ref, xb_ref, xs_hbm,
               xloc, zbuf, sem, zsem, *, n_t, n_ta):
    t = pl.program_id(0)
    x_tile = jnp.where(t < n_ta, xa_ref[...], xb_ref[...]).astype(BF16)
    slot = lax.rem(t, 2)
    tab = (lstart_ref, gstart_ref, nchunk_ref)
    rows = xloc.shape[1]

    def chunk_copy(s, lrow, grow):
        return pltpu.make_async_copy(xloc.at[s, pl.ds(lrow, MOE_CHUNK)], xs_hbm.at[pl.ds(grow, MOE_CHUNK)], sem.at[s])

    def wait_tile(s, tile):
        def body(j, c):
            chunk_copy(s, 0, 0).wait()
            return c
        lax.fori_loop(0, _n_chunks(nchunk_ref, tile), body, 0)

    @pl.when(t == 0)
    def _():
        zbuf[...] = jnp.zeros_like(zbuf)
        zero_copy = lambda grow: pltpu.make_async_copy(
            zbuf.at[pl.ds(0, MOE_CHUNK)], xs_hbm.at[pl.ds(grow, MOE_CHUNK)], zsem.at[0])
        zero_tile = lambda grow: pltpu.make_async_copy(zbuf, xs_hbm.at[pl.ds(grow, MOE_TILE)], zsem.at[0])
        tail_start = zstart_ref[N_EXPERTS]
        for e in range(N_EXPERTS):
            def zb(j, c, e=e):
                zero_copy(pl.multiple_of(zstart_ref[e] + j * MOE_CHUNK, MOE_CHUNK)).start()
                return c
            lax.fori_loop(0, zcount_ref[e], zb, 0)

        def tb(j, c):
            zero_tile(pl.multiple_of(tail_start + j * MOE_TILE, MOE_TILE)).start()
            return c
        lax.fori_loop(0, zcount_ref[N_EXPERTS], tb, 0)
        for e in range(N_EXPERTS):
            def zw(j, c):
                zero_copy(0).wait()
                return c
            lax.fori_loop(0, zcount_ref[e], zw, 0)

        def tw(j, c):
            zero_tile(0).wait()
            return c
        lax.fori_loop(0, zcount_ref[N_EXPERTS], tw, 0)

    @pl.when(t >= 2)
    def _():
        wait_tile(slot, t - 2)

    l_iota = lax.broadcasted_iota(jnp.int32, (rows, xa_ref.shape[0]), 0)
    perm = jnp.logical_or(l_iota == lpos_ref[0:1, :], l_iota == lpos_ref[1:2, :])
    perm = jnp.where(perm, 1.0, 0.0).astype(BF16)
    xloc[slot] = jnp.dot(perm, x_tile, preferred_element_type=F32)
    _seg_loop(tab, t, lambda lrow, grow: chunk_copy(slot, lrow, grow).start())

    @pl.when(t == n_t - 1)
    def _():
        wait_tile(slot, t)
        if n_t >= 2:
            wait_tile(1 - slot, t - 1)


def _moe_sort(xa, xb, lpos, tab, tt, r_tot):
    n_ta, n_tb = xa.shape[0] // tt, xb.shape[0] // tt
    n_t = n_ta + n_tb
    rows = _local_rows(tt)
    grid_spec = pltpu.PrefetchScalarGridSpec(
        num_scalar_prefetch=5,
        grid=(n_t,),
        in_specs=[pl.BlockSpec((2, tt), lambda t, *_: (0, t)),
                  pl.BlockSpec((tt, D_MODEL), lambda t, *_: (jnp.minimum(t, n_ta - 1), 0)),
                  pl.BlockSpec((tt, D_MODEL), lambda t, *_: (jnp.maximum(t - n_ta, 0), 0))],
        out_specs=pl.BlockSpec(memory_space=pl.ANY),
        scratch_shapes=[pltpu.VMEM((2, rows, D_MODEL), F32), pltpu.VMEM((MOE_TILE, D_MODEL), F32),
                        pltpu.SemaphoreType.DMA((2,)), pltpu.SemaphoreType.DMA((1,))])
    return pl.pallas_call(
        functools.partial(_sort_body, n_t=n_t, n_ta=n_ta),
        out_shape=jax.ShapeDtypeStruct((r_tot, D_MODEL), F32),
        grid_spec=grid_spec,
        compiler_params=_cparams(("arbitrary",)),
        name="moe_sort",
    )(tab["lstart"], tab["gstart"], tab["nchunk"], tab["zstart"], tab["zcount"], lpos, xa, xb)


def _expert_body(te_ref, nu_ref, x_ref, wg_ref, wu_ref, wd_ref, o_ref):
    del te_ref
    used = pl.program_id(0) < nu_ref[0]

    @pl.when(used)
    def _():
        xb = x_ref[...].astype(BF16)
        hid = _gelu(jnp.dot(xb, wg_ref[0], preferred_element_type=F32)) * jnp.dot(xb, wu_ref[0], preferred_element_type=F32)
        o_ref[...] = jnp.dot(hid.astype(BF16), wd_ref[0], preferred_element_type=F32)

    @pl.when(jnp.logical_not(used))
    def _():
        o_ref[...] = jnp.zeros_like(o_ref)


def _moe_experts(xs, tab, wg, wu, wd):
    r_tot = xs.shape[0]
    last = lambda i, nu: jnp.minimum(i, nu[0] - 1)
    grid_spec = pltpu.PrefetchScalarGridSpec(
        num_scalar_prefetch=2,
        grid=(r_tot // MOE_TILE,),
        in_specs=[pl.BlockSpec((MOE_TILE, D_MODEL), lambda i, te, nu: (last(i, nu), 0)),
                  pl.BlockSpec((1, D_MODEL, D_EXPERT), lambda i, te, nu: (te[last(i, nu)], 0, 0)),
                  pl.BlockSpec((1, D_MODEL, D_EXPERT), lambda i, te, nu: (te[last(i, nu)], 0, 0)),
                  pl.BlockSpec((1, D_EXPERT, D_MODEL), lambda i, te, nu: (te[last(i, nu)], 0, 0))],
        out_specs=pl.BlockSpec((MOE_TILE, D_MODEL), lambda i, te, nu: (i, 0)))
    return pl.pallas_call(
        _expert_body,
        out_shape=jax.ShapeDtypeStruct((r_tot, D_MODEL), F32),
        grid_spec=grid_spec,
        compiler_params=_cparams(("arbitrary",)),
        name="moe_experts",
    )(tab["tile_expert"], tab["n_used"], xs, wg, wu, wd)


def _combine_body(lstart_ref, gstart_ref, nchunk_ref, ys_hbm, xa_ref, xb_ref, meta_ref, g_ref, b_ref,
                  oa_ref, ob_ref, yloc, sem, *, n_t, n_ta):
    t = pl.program_id(0)
    first = t < n_ta
    x1 = jnp.where(first, xa_ref[...], xb_ref[...])
    slot = lax.rem(t, 2)
    tab = (lstart_ref, gstart_ref, nchunk_ref)
    rows = yloc.shape[1]

    def chunk_copy(s, lrow, grow):
        return pltpu.make_async_copy(ys_hbm.at[pl.ds(grow, MOE_CHUNK)], yloc.at[s, pl.ds(lrow, MOE_CHUNK)], sem.at[s])

    def fetch(s, tile):
        _seg_loop(tab, tile, lambda lrow, grow: chunk_copy(s, lrow, grow).start())

    @pl.when(t == 0)
    def _():
        yloc[...] = jnp.zeros_like(yloc)
        fetch(0, 0)

    @pl.when(t + 1 < n_t)
    def _():
        fetch(1 - slot, t + 1)

    def wbody(j, c):
        chunk_copy(slot, 0, 0).wait()
        return c
    lax.fori_loop(0, _n_chunks(nchunk_ref, t), wbody, 0)

    meta = meta_ref[...]
    l_iota = lax.broadcasted_iota(jnp.int32, (x1.shape[0], rows), 1).astype(F32)
    yb = yloc[slot].astype(BF16)
    pick = lambda k: jnp.dot(jnp.where(l_iota == meta[:, k:k + 1], 1.0, 0.0).astype(BF16), yb, preferred_element_type=F32)
    moe = meta[:, 2:3] * pick(0) + meta[:, 3:4] * pick(1)
    y = _layer_norm(DN_ALPHA * x1 + moe, g_ref[...], b_ref[...])

    @pl.when(first)
    def _():
        oa_ref[...] = y

    @pl.when(jnp.logical_not(first))
    def _():
        ob_ref[...] = y


def _moe_combine(ys, xa, xb, meta_n, tab, g, b, tt):
    n_ta, n_tb = xa.shape[0] // tt, xb.shape[0] // tt
    n_t = n_ta + n_tb
    rows = _local_rows(tt)
    vec = lambda: pl.BlockSpec((1, D_MODEL), lambda t, *_: (0, 0))
    in_a = pl.BlockSpec((tt, D_MODEL), lambda t, *_: (jnp.minimum(t, n_ta - 1), 0))
    in_b = pl.BlockSpec((tt, D_MODEL), lambda t, *_: (jnp.maximum(t - n_ta, 0), 0))
    grid_spec = pltpu.PrefetchScalarGridSpec(
        num_scalar_prefetch=3,
        grid=(n_t,),
        in_specs=[pl.BlockSpec(memory_space=pl.ANY), in_a, in_b,
                  pl.BlockSpec((tt, 8), lambda t, *_: (t, 0)), vec(), vec()],
        out_specs=(in_a, in_b),
        scratch_shapes=[pltpu.VMEM((2, rows, D_MODEL), F32), pltpu.SemaphoreType.DMA((2,))])
    return pl.pallas_call(
        functools.partial(_combine_body, n_t=n_t, n_ta=n_ta),
        out_shape=(jax.ShapeDtypeStruct(xa.shape, F32), jax.ShapeDtypeStruct(xb.shape, F32)),
        grid_spec=grid_spec,
        compiler_params=_cparams(("arbitrary",)),
        name="moe_combine_ln",
    )(tab["lstart"], tab["gstart"], tab["nchunk"], ys, xa, xb, meta_n, g, b)


def _moe(xa, xb, meta_t, p):
    n = xa.shape[0] + xb.shape[0]
    tt = 256
    while xa.shape[0] % tt or xb.shape[0] % tt:
        tt //= 2
    r_tot, lpos, tab = _dispatch(meta_t, n, tt)
    xs = _moe_sort(xa, xb, lpos, tab, tt, r_tot)
    ys = _moe_experts(xs, tab, p["w_gate"], p["w_up"], p["w_down"])
    meta_n = jnp.concatenate([lpos.astype(F32), meta_t[2:4], jnp.zeros((4, n), F32)], axis=0).T
    return _moe_combine(ys, xa, xb, meta_n, tab, p["ln2_g"], p["ln2_b"], tt)


def _largest_tile(n, cap, mult):
    best = None
    for d in range(mult, min(n, cap) + 1, mult):
        if n % d == 0:
            best = d
    assert best is not None, (n, cap, mult)
    return best


def _seq_tails(x, off, b, t, k, c0, c1):
    if b <= 8:
        return jnp.stack([lax.slice(x, (off + i * t + t - k, c0), (off + (i + 1) * t, c1)) for i in range(b)])
    return lax.slice(x, (off, c0), (off + b * t, c1)).reshape(b, t, c1 - c0)[:, t - k:, :]


def _kv_rows(qkv, g, off, b, t, k):
    def pick(which):
        h0 = (which * N_GROUPS + g) * HEADS
        if b <= 8:
            a = jnp.stack([lax.slice(qkv, (h0, off + i * t + t - k, 0), (h0 + HEADS, off + (i + 1) * t, HEAD_DIM))
                           for i in range(b)], axis=1)
        else:
            a = lax.slice(qkv, (h0, off, 0), (h0 + HEADS, off + b * t, HEAD_DIM)).reshape(HEADS, b, t, HEAD_DIM)[:, :, t - k:]
        return jnp.transpose(a, (1, 2, 0, 3))
    return jnp.stack([pick(1), pick(2)], axis=2)


def _block_diag_gates(w_a, w_x):
    per = RNN_CHUNK // RNN_BLOCK
    chunks = []
    for c in range(N_RNN_CHUNKS):
        halves = []
        for w in (w_a, w_x):
            m = jnp.zeros((RNN_CHUNK, RNN_CHUNK), F32)
            for i in range(per):
                m = lax.dynamic_update_slice(m, w[c * per + i], (i * RNN_BLOCK, i * RNN_BLOCK))
            halves.append(m)
        chunks.append(jnp.concatenate(halves, axis=1))
    return jnp.stack(chunks).astype(BF16)


def kernel(x_prompt, x_sample, cache_kv_w128, cache_kv_w512, cache_kv_w2048, cache_mem_kv, state_h, state_conv, mem_prompt, w_in, b_gates, conv_w, conv_b, w_a, b_a, w_x, b_x, lru_lambda, w_br_lru, w_br_att, w_br_mem, w_o, w_mem_kv, ln1_g, ln1_b, w_rg, b_rg, w_re, b_re, w_gate, w_up, w_down, ln2_g, ln2_b):
    row = lambda v: v.reshape(1, -1).astype(F32)
    w_router = jnp.zeros((ROUTER_ROWS, D_MODEL), F32)
    w_router = w_router.at[:N_EXPERT_GROUPS].set(w_rg.T).at[N_EXPERT_GROUPS:N_EXPERT_GROUPS + N_EXPERTS].set(w_re.T)
    b_router = jnp.zeros((ROUTER_ROWS, 1), F32)
    b_router = b_router.at[:N_EXPERT_GROUPS, 0].set(b_rg).at[N_EXPERT_GROUPS:N_EXPERT_GROUPS + N_EXPERTS, 0].set(b_re)
    p = dict(
        w_in=w_in.astype(BF16), b_gates=row(b_gates), conv_w=conv_w, conv_b=row(conv_b), wax=_block_diag_gates(w_a, w_x),
        b_a=row(b_a), b_x=row(b_x), lam=row(lru_lambda),
        w_br_lru=w_br_lru.astype(BF16), w_br_att=w_br_att.astype(BF16), w_br_mem=w_br_mem.astype(BF16),
        w_o=w_o.astype(BF16), ln1_g=row(ln1_g), ln1_b=row(ln1_b), w_router=w_router, b_router=b_router,
        w_gate=w_gate.astype(BF16), w_up=w_up.astype(BF16), w_down=w_down.astype(BF16),
        ln2_g=row(ln2_g), ln2_b=row(ln2_b))

    bp, s, _ = x_prompt.shape
    bs, ts, _ = x_sample.shape
    n_p, n_s = bp * s, bs * ts
    xp2, xs2 = x_prompt.reshape(n_p, D_MODEL), x_sample.reshape(n_s, D_MODEL)

    xb = jnp.concatenate([xp2.astype(BF16), xs2.astype(BF16)], axis=0)
    tm_a = _largest_tile(n_p + n_s, 2304, 16)
    xrg = _matmul(xb, p["w_in"], 0, 2 * D_RNN, F32, tm_a, 512, "in_proj_rnn")
    qkv = _matmul(xb, p["w_in"], COL_Q, N_QKV_HEADS * HEAD_DIM, F32, tm_a, 512, "in_proj_qkv", split=4)
    qm = _matmul(xb, p["w_in"], COL_QM, D_MEM, F32, tm_a, 512, "in_proj_qm", split=2)
    gates = _matmul(xb, p["w_in"], COL_GATES, N_GATE_COLS, BF16, tm_a, 512, "gate_proj", bias=p["b_gates"])
    lru = (p["conv_w"], p["conv_b"], p["wax"], p["b_a"], p["b_x"], p["lam"])
    branch_w = (p["w_br_lru"], p["w_br_att"], p["w_br_mem"])
    ln1 = (p["w_o"], p["ln1_g"], p["ln1_b"], p["w_router"], p["b_router"])

    mem_rows = mem_prompt.reshape(bp * N_MEM, D_MODEL)
    mem_kv_p = _matmul(mem_rows, w_mem_kv, 0, 2 * D_MEM, F32, _tile(bp * N_MEM, 512), 512, "mem_kv_proj")
    a_p, h_p = _rglru(xrg, 0, bp, s, jnp.zeros((bp, CONV_W - 1, D_RNN), F32), jnp.zeros((bp, D_RNN), F32), *lru,
                      _tile(s, 256), BF16)
    att_p = _attn_prompt(qkv, bp, s)
    mem_p = _mem_attn(qm, mem_kv_p.reshape(bp, N_MEM, 2 * D_MEM), bp, s, _tile(s, 512), BF16)
    merged_p = _branch_merge(a_p, att_p, mem_p, gates, 0, *branch_w, _tile(n_p, 256))
    x1_p, meta_p = _proj_ln(merged_p, xp2, *ln1, _tile(n_p, 512))

    a_s, h_s = _rglru(xrg, n_p, bs, ts, state_conv, state_h, *lru, ts, F32)
    att_s = _attn_sample(qkv, (cache_kv_w128, cache_kv_w512, cache_kv_w2048), bs, ts, n_p)
    mem_s = _mem_attn_cache(qm, cache_mem_kv, bs, ts, n_p)
    tm_s = _tile(n_s, 256)
    assert n_p % tm_s == 0
    merged_s = _branch_merge(a_s, att_s, mem_s, gates, n_p, *branch_w, tm_s)
    x1_s, meta_s = _proj_ln(merged_s, xs2, *ln1, _tile(n_s, 512))

    y_p, y_s = _moe(x1_p, x1_s, jnp.concatenate([meta_p, meta_s], axis=1), p)

    kc = CONV_W - 1
    conv_p = jnp.concatenate([jnp.zeros((bp, kc, D_RNN), F32), _seq_tails(xrg, 0, bp, s, min(kc, s), 0, D_RNN)], axis=1)[:, -kc:]
    conv_s = jnp.concatenate([state_conv, _seq_tails(xrg, n_p, bs, ts, min(kc, ts), 0, D_RNN)], axis=1)[:, -kc:]
    kv_p = [_kv_rows(qkv, g, 0, bp, s, min(KEYS * d, s)) for g, d in enumerate(DILATIONS)]
    kv_s = [_kv_rows(qkv, g, n_p, bs, ts, ts) for g in range(N_GROUPS)]
    return (y_p.reshape(bp, s, D_MODEL), y_s.reshape(bs, ts, D_MODEL), kv_p[0], kv_p[1], kv_p[2],
            mem_kv_p.reshape(bp, N_MEM, 2, MEM_HEADS, MEM_HEAD_DIM), h_p.reshape(bp, D_RNN), conv_p,
            kv_s[0], kv_s[1], kv_s[2], h_s.reshape(bs, D_RNN), conv_s)
```

```python
import functools

import jax
import jax.numpy as jnp
from jax import lax
from jax.experimental import pallas as pl
from jax.experimental.pallas import tpu as pltpu

F32 = jnp.float32
BF16 = jnp.bfloat16

D_MODEL = 2048
D_RNN = 1536
N_RNN_BLOCKS = 16
RNN_BLOCK = D_RNN // N_RNN_BLOCKS
RNN_CHUNK = 384
N_RNN_CHUNKS = D_RNN // RNN_CHUNK
CONV_W = 4
LRU_C = 8.0
HEAD_DIM = 128
HEADS = 4
DILATIONS = (1, 4, 16)
KEYS = 128
N_GROUPS = 3
D_ATT_OUT = HEADS * HEAD_DIM
N_QKV_HEADS = 3 * N_GROUPS * HEADS
ATT_SCALE = HEAD_DIM ** -0.5
ATT_TILE = KEYS * max(DILATIONS)
N_MEM = 256
MEM_HEADS = 4
MEM_HEAD_DIM = 256
D_MEM = MEM_HEADS * MEM_HEAD_DIM
MEM_SCALE = MEM_HEAD_DIM ** -0.5
N_EXPERT_GROUPS = 4
EXPERTS_PER_GROUP = 4
N_EXPERTS = 16
D_EXPERT = 512
MOE_TILE = 512
DN_ALPHA = 2.0 ** 0.25
LN_EPS = 1e-5
NEG_INF = -1e30

COL_Q = 2 * D_RNN
COL_QM = COL_Q + N_QKV_HEADS * HEAD_DIM
COL_GATES = COL_QM + D_MEM
N_GATE_COLS = 3 * D_MODEL
ROUTER_ROWS = 32

VMEM_LIMIT = 56 * 1024 * 1024


def _cparams(sem):
    return pltpu.CompilerParams(dimension_semantics=sem, vmem_limit_bytes=VMEM_LIMIT)


def _gelu(x):
    return 0.5 * x * (1.0 + jnp.tanh(0.7978845608028654 * (x + 0.044715 * (x * x * x))))


def _layer_norm(x, g, b):
    mu = jnp.mean(x, axis=-1, keepdims=True)
    xc = x - mu
    var = jnp.mean(xc * xc, axis=-1, keepdims=True)
    return xc * lax.rsqrt(var + LN_EPS) * g + b


def _dot_nt(a, b):
    return lax.dot_general(a, b, (((1,), (1,)), ((), ())), preferred_element_type=F32)


def _tile(m, cap):
    t = min(m, cap)
    assert m % t == 0, (m, t)
    return t


def _mm_body(x_ref, w_ref, *rest, split, gate):
    acc = jnp.dot(x_ref[...].astype(BF16), w_ref[...].astype(BF16), preferred_element_type=F32)
    if gate:
        b_ref, o_ref = rest
        acc = 0.5 * jnp.tanh(0.5 * (acc + b_ref[...])) + 0.5
    else:
        (o_ref,) = rest
    if split == 1:
        o_ref[...] = acc.astype(o_ref.dtype)
    else:
        w = acc.shape[1] // split
        for s in range(split):
            o_ref[s] = acc[:, s * w:(s + 1) * w].astype(o_ref.dtype)


def _matmul(x, w, col_off, n_cols, out_dtype, tm, tn, name, bias=None, split=1):
    m, k = x.shape
    cb = col_off // tn
    in_specs = [pl.BlockSpec((tm, k), lambda i, j: (i, 0)),
                pl.BlockSpec((k, tn), lambda i, j: (0, j + cb))]
    args = [x, w]
    if bias is not None:
        in_specs.append(pl.BlockSpec((1, tn), lambda i, j: (0, j)))
        args.append(bias)
    if split == 1:
        out_shape = jax.ShapeDtypeStruct((m, n_cols), out_dtype)
        out_spec = pl.BlockSpec((tm, tn), lambda i, j: (i, j))
    else:
        out_shape = jax.ShapeDtypeStruct((n_cols * split // tn, m, tn // split), out_dtype)
        out_spec = pl.BlockSpec((split, tm, tn // split), lambda i, j: (j, i, 0))
    return pl.pallas_call(
        functools.partial(_mm_body, split=split, gate=bias is not None),
        out_shape=out_shape,
        grid=(m // tm, n_cols // tn),
        in_specs=in_specs,
        out_specs=out_spec,
        compiler_params=_cparams(("parallel", "arbitrary")),
        name=name,
    )(*args)


def _rglru_body(xr_ref, xg_ref, cbuf_ref, h0_ref, cw_ref, cb_ref, wax_ref, ba_ref, bx_ref, lam_ref,
                out_ref, hl_ref, ext_s, a_s, u_s, h_s, *, tt):
    t = pl.program_id(1)

    @pl.when(t == 0)
    def _():
        ext_s[0:8, :] = jnp.zeros((8, D_RNN), F32)
        ext_s[5:8, :] = cbuf_ref[0]
        h_s[...] = jnp.broadcast_to(h0_ref[0], (8, D_RNN))

    @pl.when(t > 0)
    def _():
        ext_s[0:8, :] = ext_s[tt:tt + 8, :]

    ext_s[8:8 + tt, :] = xr_ref[...]
    cw = cw_ref[...]
    xc = (cb_ref[...] + cw[3:4, :] * ext_s[8:8 + tt, :] + cw[2:3, :] * ext_s[7:7 + tt, :]
          + cw[1:2, :] * ext_s[6:6 + tt, :] + cw[0:1, :] * ext_s[5:5 + tt, :])
    xcb = xc.astype(BF16)
    r_parts, i_parts = [], []
    for c in range(N_RNN_CHUNKS):
        g = jnp.dot(xcb[:, c * RNN_CHUNK:(c + 1) * RNN_CHUNK], wax_ref[c], preferred_element_type=F32)
        r_parts.append(g[:, :RNN_CHUNK])
        i_parts.append(g[:, RNN_CHUNK:])
    sigmoid = lambda z: 0.5 * jnp.tanh(0.5 * z) + 0.5
    r = sigmoid(jnp.concatenate(r_parts, axis=1) + ba_ref[...])
    gi = sigmoid(jnp.concatenate(i_parts, axis=1) + bx_ref[...])
    nl = -lam_ref[...]
    softplus = jnp.maximum(nl, 0.0) + jnp.log1p(jnp.exp(-jnp.abs(nl)))
    log_a = (-LRU_C) * r * softplus
    th = jnp.tanh(log_a)
    a_s[...] = jnp.exp(log_a)
    u_s[...] = jnp.sqrt(-2.0 * th / (1.0 - th)) * (gi * xc)

    rows = lax.broadcasted_iota(jnp.int32, (8, D_RNN), 0)

    def blk(i, h):
        r0 = pl.multiple_of(i * 8, 8)
        ab = a_s[pl.ds(r0, 8), :]
        ub = u_s[pl.ds(r0, 8), :]
        for s in (1, 2, 4):
            keep = rows >= s
            ub = ab * jnp.where(keep, pltpu.roll(ub, s, 0), 0.0) + ub
            ab = ab * jnp.where(keep, pltpu.roll(ab, s, 0), 1.0)
        hb = ab * h + ub
        u_s[pl.ds(r0, 8), :] = hb
        return jnp.broadcast_to(hb[7:8, :], (8, D_RNN))

    h_fin = lax.fori_loop(0, tt // 8, blk, h_s[...])
    h_s[...] = h_fin
    hl_ref[0] = h_fin[0:1, :]
    out_ref[...] = (u_s[...] * _gelu(xg_ref[...])).astype(out_ref.dtype)


def _rglru(xrg, off, b, t, conv_buf, h0, conv_w, conv_b, wax, b_a, b_x, lam, tt, out_dtype):
    nt = t // tt
    ob = off // tt
    vec = lambda: pl.BlockSpec((1, D_RNN), lambda i, j: (0, 0))
    return pl.pallas_call(
        functools.partial(_rglru_body, tt=tt),
        out_shape=(jax.ShapeDtypeStruct((b * t, D_RNN), out_dtype), jax.ShapeDtypeStruct((b, 1, D_RNN), F32)),
        grid=(b, nt),
        in_specs=[pl.BlockSpec((tt, D_RNN), lambda i, j: (ob + i * nt + j, 0)),
                  pl.BlockSpec((tt, D_RNN), lambda i, j: (ob + i * nt + j, 1)),
                  pl.BlockSpec((1, CONV_W - 1, D_RNN), lambda i, j: (i, 0, 0)),
                  pl.BlockSpec((1, 1, D_RNN), lambda i, j: (i, 0, 0)),
                  pl.BlockSpec((CONV_W, D_RNN), lambda i, j: (0, 0)),
                  vec(),
                  pl.BlockSpec((N_RNN_CHUNKS, RNN_CHUNK, 2 * RNN_CHUNK), lambda i, j: (0, 0, 0)),
                  vec(), vec(), vec()],
        out_specs=(pl.BlockSpec((tt, D_RNN), lambda i, j: (i * nt + j, 0)),
                   pl.BlockSpec((1, 1, D_RNN), lambda i, j: (i, 0, 0))),
        scratch_shapes=[pltpu.VMEM((tt + 8, D_RNN), F32), pltpu.VMEM((tt, D_RNN), F32),
                        pltpu.VMEM((tt, D_RNN), F32), pltpu.VMEM((8, D_RNN), F32)],
        compiler_params=_cparams(("parallel", "arbitrary")),
        name="rglru",
    )(xrg, xrg, conv_buf, h0.reshape(b, 1, D_RNN), conv_w, conv_b, wax, b_a, b_x, lam)


def _band_block(q, k, v, bias):
    s = _dot_nt(q, k) * ATT_SCALE + bias
    m = jnp.max(s, axis=-1, keepdims=True)
    p = jnp.exp(s - m)
    l = jnp.sum(p, axis=-1, keepdims=True)
    o = jnp.dot(p.astype(BF16), v, preferred_element_type=F32) / l
    return o, m + jnp.log(l)


def _attn_prompt_body(*refs):
    q_refs, k_refs, v_refs, kh_refs, vh_refs = (refs[3 * i:3 * i + 3] for i in range(5))
    o_ref, og_s, lg_s = refs[15:]
    row = lax.broadcasted_iota(jnp.int32, (KEYS, 2 * KEYS), 0)
    col = lax.broadcasted_iota(jnp.int32, (KEYS, 2 * KEYS), 1)
    band = jnp.logical_and(col >= row, col <= row + KEYS)
    bias = jnp.where(band, 0.0, NEG_INF)
    has_prev = pl.program_id(1) > 0
    bias_first = jnp.where(jnp.logical_and(band, jnp.logical_or(col >= KEYS, has_prev)), 0.0, NEG_INF)

    def put(g, rows, o, lse):
        og_s[g, rows, :] = o
        lg_s[g, rows, :] = jnp.broadcast_to(lse, (KEYS, HEAD_DIM))

    def rows_of(start, size, d):
        return pl.ds(start, size, stride=d) if d > 1 else pl.ds(start, size)

    for g, d in enumerate(DILATIONS):
        q_ref, k_ref, v_ref, kh_ref, vh_ref = q_refs[g], k_refs[g], v_refs[g], kh_refs[g], vh_refs[g]
        nqb = ATT_TILE // (d * KEYS)
        for r in range(d):
            own = rows_of(r, KEYS, d)
            kk = jnp.concatenate([kh_ref[0, own, :], k_ref[0, own, :]], axis=0).astype(BF16)
            vv = jnp.concatenate([vh_ref[0, own, :], v_ref[0, own, :]], axis=0).astype(BF16)
            o, lse = _band_block(q_ref[0, own, :].astype(BF16), kk, vv, bias_first)
            put(g, own, o, lse)
            for qb in range(1, nqb):
                rows = rows_of(qb * KEYS * d + r, KEYS, d)
                keys = rows_of((qb - 1) * KEYS * d + r, 2 * KEYS, d)
                o, lse = _band_block(q_ref[0, rows, :].astype(BF16), k_ref[0, keys, :].astype(BF16),
                                     v_ref[0, keys, :].astype(BF16), bias)
                put(g, rows, o, lse)

    l0, l1, l2 = lg_s[0], lg_s[1], lg_s[2]
    m = jnp.maximum(jnp.maximum(l0, l1), l2)
    e0, e1, e2 = jnp.exp(l0 - m), jnp.exp(l1 - m), jnp.exp(l2 - m)
    o_ref[...] = ((e0 * og_s[0] + e1 * og_s[1] + e2 * og_s[2]) / (e0 + e1 + e2)).astype(o_ref.dtype)


def _attn_prompt(qkv, b, s):
    assert s % ATT_TILE == 0
    nt = s // ATT_TILE
    n = b * s

    def cur(which, g):
        return pl.BlockSpec((1, ATT_TILE, HEAD_DIM),
                            lambda i, j, h: ((which * N_GROUPS + g) * HEADS + h, i * nt + j, 0))

    def halo(which, g):
        rows = KEYS * DILATIONS[g]
        per = ATT_TILE // rows
        return pl.BlockSpec((1, rows, HEAD_DIM),
                            lambda i, j, h: ((which * N_GROUPS + g) * HEADS + h,
                                             jnp.maximum((i * nt + j) * per - 1, 0), 0))

    in_specs = ([cur(0, g) for g in range(N_GROUPS)] + [cur(1, g) for g in range(N_GROUPS)]
                + [cur(2, g) for g in range(N_GROUPS)] + [halo(1, g) for g in range(N_GROUPS)]
                + [halo(2, g) for g in range(N_GROUPS)])
    return pl.pallas_call(
        _attn_prompt_body,
        out_shape=jax.ShapeDtypeStruct((n, D_ATT_OUT), BF16),
        grid=(b, nt, HEADS),
        in_specs=in_specs,
        out_specs=pl.BlockSpec((ATT_TILE, HEAD_DIM), lambda i, j, h: (i * nt + j, h)),
        scratch_shapes=[pltpu.VMEM((N_GROUPS, ATT_TILE, HEAD_DIM), F32),
                        pltpu.VMEM((N_GROUPS, ATT_TILE, HEAD_DIM), F32)],
        compiler_params=_cparams(("parallel", "parallel", "parallel")),
        name="attn_prompt",
    )(*([qkv] * 15))


def _attn_sample_group(q4, kn4, vn4, c_ref, d, t_new):
    wb = c_ref.shape[1] // (2 * HEADS)
    nr = HEADS * t_new
    zeros = jnp.zeros((t_new, HEAD_DIM), F32)
    qbd = jnp.concatenate(
        [jnp.concatenate([q4[h] if hh == h else zeros for hh in range(HEADS)], axis=1) for h in range(HEADS)],
        axis=0).astype(BF16)
    kn = jnp.concatenate([kn4[h] for h in range(HEADS)], axis=1).astype(BF16)
    vn = jnp.concatenate([vn4[h] for h in range(HEADS)], axis=1).astype(BF16)
    cache_rows = lambda kv, h: c_ref[0, pl.ds(kv * HEADS + h, wb, stride=2 * HEADS), :]
    kc = jnp.concatenate([cache_rows(0, h) for h in range(HEADS)], axis=1).astype(BF16)
    vc = jnp.concatenate([cache_rows(1, h) for h in range(HEADS)], axis=1).astype(BF16)
    tq_c = lax.broadcasted_iota(jnp.int32, (nr, wb), 0) & (t_new - 1)
    dist_c = wb + tq_c - lax.broadcasted_iota(jnp.int32, (nr, wb), 1)
    ok_c = jnp.logical_and((dist_c & (d - 1)) == 0, dist_c <= KEYS * d)
    tq_n = lax.broadcasted_iota(jnp.int32, (nr, t_new), 0) & (t_new - 1)
    dist_n = tq_n - lax.broadcasted_iota(jnp.int32, (nr, t_new), 1)
    ok_n = jnp.logical_and(jnp.logical_and(dist_n >= 0, (dist_n & (d - 1)) == 0), dist_n <= KEYS * d)
    s_c = jnp.where(ok_c, _dot_nt(qbd, kc) * ATT_SCALE, NEG_INF)
    s_n = jnp.where(ok_n, _dot_nt(qbd, kn) * ATT_SCALE, NEG_INF)
    m = jnp.maximum(jnp.max(s_c, axis=-1, keepdims=True), jnp.max(s_n, axis=-1, keepdims=True))
    p_c = jnp.exp(s_c - m)
    p_n = jnp.exp(s_n - m)
    l = jnp.sum(p_c, axis=-1, keepdims=True) + jnp.sum(p_n, axis=-1, keepdims=True)
    o = (jnp.dot(p_c.astype(BF16), vc, preferred_element_type=F32)
         + jnp.dot(p_n.astype(BF16), vn, preferred_element_type=F32)) / l
    lse = m + jnp.log(l)
    o = jnp.concatenate(
        [o[h * t_new:(h + 1) * t_new, h * HEAD_DIM:(h + 1) * HEAD_DIM] for h in range(HEADS)], axis=1)
    lse = jnp.concatenate(
        [jnp.broadcast_to(lse[h * t_new:(h + 1) * t_new], (t_new, HEAD_DIM)) for h in range(HEADS)], axis=1)
    return o, lse


def _attn_sample_body(*refs, t_new):
    q_refs, k_refs, v_refs, c_refs = (refs[3 * i:3 * i + 3] for i in range(4))
    o_ref = refs[12]
    outs = [_attn_sample_group(q_refs[g][...], k_refs[g][...], v_refs[g][...], c_refs[g], d, t_new)
            for g, d in enumerate(DILATIONS)]
    (o0, l0), (o1, l1), (o2, l2) = outs
    m = jnp.maximum(jnp.maximum(l0, l1), l2)
    e0, e1, e2 = jnp.exp(l0 - m), jnp.exp(l1 - m), jnp.exp(l2 - m)
    o_ref[0] = (e0 * o0 + e1 * o1 + e2 * o2) / (e0 + e1 + e2)


def _attn_sample(qkv, caches, b, t_new, off):
    assert t_new & (t_new - 1) == 0 and off % t_new == 0
    ob = off // t_new
    new = lambda which, g: pl.BlockSpec((HEADS, t_new, HEAD_DIM), lambda i: (which * N_GROUPS + g, ob + i, 0))
    caches2 = [c.reshape(b, c.shape[1] * 2 * HEADS, HEAD_DIM) for c in caches]
    in_specs = ([new(0, g) for g in range(N_GROUPS)] + [new(1, g) for g in range(N_GROUPS)]
                + [new(2, g) for g in range(N_GROUPS)]
                + [pl.BlockSpec((1, c.shape[1], HEAD_DIM), lambda i: (i, 0, 0)) for c in caches2])
    att = pl.pallas_call(
        functools.partial(_attn_sample_body, t_new=t_new),
        out_shape=jax.ShapeDtypeStruct((b, t_new, D_ATT_OUT), F32),
        grid=(b,),
        in_specs=in_specs,
        out_specs=pl.BlockSpec((1, t_new, D_ATT_OUT), lambda i: (i, 0, 0)),
        compiler_params=_cparams(("parallel",)),
        name="attn_sample",
    )(*([qkv] * 9), *caches2)
    return att.reshape(b * t_new, D_ATT_OUT)


def _mem_attn_body(q_ref, kv_ref, o_ref):
    for h in range(MEM_HEADS):
        k = kv_ref[0, :, h * MEM_HEAD_DIM:(h + 1) * MEM_HEAD_DIM].astype(BF16)
        v = kv_ref[0, :, D_MEM + h * MEM_HEAD_DIM:D_MEM + (h + 1) * MEM_HEAD_DIM].astype(BF16)
        s = _dot_nt(q_ref[h].astype(BF16), k) * MEM_SCALE
        m = jnp.max(s, axis=-1, keepdims=True)
        p = jnp.exp(s - m)
        l = jnp.sum(p, axis=-1, keepdims=True)
        o = jnp.dot(p.astype(BF16), v, preferred_element_type=F32) / l
        o_ref[:, h * MEM_HEAD_DIM:(h + 1) * MEM_HEAD_DIM] = o.astype(o_ref.dtype)


def _mem_attn(qm, mem_kv, b, t, tm, out_dtype):
    nt = t // tm
    return pl.pallas_call(
        _mem_attn_body,
        out_shape=jax.ShapeDtypeStruct((b * t, D_MEM), out_dtype),
        grid=(b, nt),
        in_specs=[pl.BlockSpec((MEM_HEADS, tm, MEM_HEAD_DIM), lambda i, j: (0, i * nt + j, 0)),
                  pl.BlockSpec((1, N_MEM, 2 * D_MEM), lambda i, j: (i, 0, 0))],
        out_specs=pl.BlockSpec((tm, D_MEM), lambda i, j: (i * nt + j, 0)),
        compiler_params=_cparams(("parallel", "parallel")),
        name="mem_attn",
    )(qm, mem_kv)


def _mem_attn_cache_body(q_ref, c_ref, o_ref):
    outs = []
    for h in range(MEM_HEADS):
        k = c_ref[0, :, 0, h, :].astype(BF16)
        v = c_ref[0, :, 1, h, :].astype(BF16)
        s = _dot_nt(q_ref[h].astype(BF16), k) * MEM_SCALE
        m = jnp.max(s, axis=-1, keepdims=True)
        p = jnp.exp(s - m)
        l = jnp.sum(p, axis=-1, keepdims=True)
        outs.append(jnp.dot(p.astype(BF16), v, preferred_element_type=F32) / l)
    o_ref[...] = jnp.concatenate(outs, axis=1)


def _mem_attn_cache(qm, cache, b, t, off):
    ob = off // t
    return pl.pallas_call(
        _mem_attn_cache_body,
        out_shape=jax.ShapeDtypeStruct((b * t, D_MEM), F32),
        grid=(b,),
        in_specs=[pl.BlockSpec((MEM_HEADS, t, MEM_HEAD_DIM), lambda i: (0, ob + i, 0)),
                  pl.BlockSpec((1, N_MEM, 2, MEM_HEADS, MEM_HEAD_DIM), lambda i: (i, 0, 0, 0, 0))],
        out_specs=pl.BlockSpec((t, D_MEM), lambda i: (i, 0)),
        compiler_params=_cparams(("parallel",)),
        name="mem_attn_cache",
    )(qm, cache)


def _branch_body(al_ref, att_ref, mem_ref, gt_ref, wl_ref, wa_ref, wm_ref, out_ref):
    acc = gt_ref[:, 0:D_MODEL].astype(F32) * jnp.dot(al_ref[...].astype(BF16), wl_ref[...], preferred_element_type=F32)
    acc = acc + gt_ref[:, D_MODEL:2 * D_MODEL].astype(F32) * jnp.dot(att_ref[...].astype(BF16), wa_ref[...], preferred_element_type=F32)
    acc = acc + gt_ref[:, 2 * D_MODEL:3 * D_MODEL].astype(F32) * jnp.dot(mem_ref[...].astype(BF16), wm_ref[...], preferred_element_type=F32)
    out_ref[...] = acc.astype(out_ref.dtype)


def _branch_merge(a_lru, att, mem, gates, goff, wl, wa, wm, tm):
    n = a_lru.shape[0]
    gb = goff // tm
    row = lambda w: pl.BlockSpec((tm, w), lambda i: (i, 0))
    full = lambda a: pl.BlockSpec(a.shape, lambda i: (0, 0))
    return pl.pallas_call(
        _branch_body,
        out_shape=jax.ShapeDtypeStruct((n, D_MODEL), BF16),
        grid=(n // tm,),
        in_specs=[row(D_RNN), row(D_ATT_OUT), row(D_MEM), pl.BlockSpec((tm, N_GATE_COLS), lambda i: (gb + i, 0)),
                  full(wl), full(wa), full(wm)],
        out_specs=row(D_MODEL),
        compiler_params=_cparams(("parallel",)),
        name="branch_merge",
    )(a_lru, att, mem, gates, wl, wa, wm)


def _split_bf16(x):
    hi = x.astype(BF16)
    return hi, (x - hi.astype(F32)).astype(BF16)


def _route_rows(lg):
    g = [lg[i:i + 1, :] for i in range(N_EXPERT_GROUPS)]
    gmax = jnp.maximum(jnp.maximum(g[0], g[1]), jnp.maximum(g[2], g[3]))
    gidx = jnp.where(g[0] == gmax, 0.0, jnp.where(g[1] == gmax, 1.0, jnp.where(g[2] == gmax, 2.0, 3.0)))
    g_p = 1.0 / (jnp.exp(g[0] - gmax) + jnp.exp(g[1] - gmax) + jnp.exp(g[2] - gmax) + jnp.exp(g[3] - gmax))
    e = []
    for k in range(EXPERTS_PER_GROUP):
        rows = [lg[N_EXPERT_GROUPS + gg * EXPERTS_PER_GROUP + k:N_EXPERT_GROUPS + gg * EXPERTS_PER_GROUP + k + 1, :]
                for gg in range(N_EXPERT_GROUPS)]
        e.append(jnp.where(gidx == 0.0, rows[0], jnp.where(gidx == 1.0, rows[1], jnp.where(gidx == 2.0, rows[2], rows[3]))))

    def first_argmax(v):
        mx = jnp.maximum(jnp.maximum(v[0], v[1]), jnp.maximum(v[2], v[3]))
        ix = jnp.where(v[0] == mx, 0.0, jnp.where(v[1] == mx, 1.0, jnp.where(v[2] == mx, 2.0, 3.0)))
        return mx, ix

    v1, i1 = first_argmax(e)
    v2, i2 = first_argmax([jnp.where(i1 == float(k), -jnp.inf, e[k]) for k in range(EXPERTS_PER_GROUP)])
    ex = jnp.exp(v2 - v1)
    w1 = g_p / (1.0 + ex)
    w2 = g_p * ex / (1.0 + ex)
    base = gidx * float(EXPERTS_PER_GROUP)
    zero = jnp.zeros_like(w1)
    return jnp.concatenate([base + i1, base + i2, w1, w2, zero, zero, zero, zero], axis=0)


def _proj_ln_body(mg_ref, x_ref, wo_ref, g_ref, b_ref, wr_ref, br_ref, x1_ref, meta_ref, mix_s):
    @pl.when(pl.program_id(0) == 0)
    def _():
        mix_s[...] = jnp.zeros(mix_s.shape, F32)

    wh, wl = _split_bf16(wr_ref[...])
    tm = x_ref.shape[0]
    sub = min(256, tm)
    for r0 in range(0, tm, sub):
        rs = pl.ds(r0, sub)
        x1 = _layer_norm(DN_ALPHA * x_ref[rs, :] + mix_s[rs, :], g_ref[...], b_ref[...])
        x1_ref[rs, :] = x1
        xh, xl = _split_bf16(x1)
        lg = _dot_nt(wh, xh) + (_dot_nt(wh, xl) + _dot_nt(wl, xh)) + br_ref[...]
        meta_ref[:, rs] = _route_rows(lg)
    mix_s[...] = jnp.dot(mg_ref[...], wo_ref[...], preferred_element_type=F32)


def _proj_ln(merged, x, wo, g, b, wr, br, tm):
    n = merged.shape[0]
    nb = n // tm
    prev = lambda w: pl.BlockSpec((tm, w), lambda i: (jnp.maximum(i - 1, 0), 0))
    full = lambda a: pl.BlockSpec(a.shape, lambda i: (0, 0))
    return pl.pallas_call(
        _proj_ln_body,
        out_shape=(jax.ShapeDtypeStruct((n, D_MODEL), F32), jax.ShapeDtypeStruct((8, n), F32)),
        grid=(nb + 1,),
        in_specs=[pl.BlockSpec((tm, D_MODEL), lambda i: (jnp.minimum(i, nb - 1), 0)), prev(D_MODEL),
                  full(wo), full(g), full(b), full(wr), full(br)],
        out_specs=(prev(D_MODEL), pl.BlockSpec((8, tm), lambda i: (0, jnp.maximum(i - 1, 0)))),
        scratch_shapes=[pltpu.VMEM((tm, D_MODEL), F32)],
        compiler_params=_cparams(("arbitrary",)),
        name="proj_ln_router",
    )(merged, x, wo, g, b, wr, br)


MOE_CHUNK = 8


def _local_rows(tt):
    return -(-(2 * tt + N_EXPERTS * (MOE_CHUNK - 1)) // 128) * 128


def _dispatch(meta_t, n, tt):
    n_t = n // tt
    ids = meta_t[0:2].astype(jnp.int32)
    onehot = (ids[:, :, None] == jnp.arange(N_EXPERTS, dtype=jnp.int32)).astype(jnp.int32).reshape(2, n_t, tt, N_EXPERTS)
    cnt_slot = jnp.sum(onehot, axis=2)
    cnt = cnt_slot[0] + cnt_slot[1]
    pc = (cnt + MOE_CHUNK - 1) // MOE_CHUNK * MOE_CHUNK
    lstart = jnp.cumsum(pc, axis=1) - pc
    tri = (jnp.arange(tt)[:, None] >= jnp.arange(tt)[None, :]).astype(F32)
    csum = jnp.einsum("ut,snte->snue", tri, onehot.astype(F32)).astype(jnp.int32)
    rank = csum - onehot + jnp.stack([jnp.zeros_like(cnt), cnt_slot[0]])[:, :, None, :]
    lpos = jnp.sum(onehot * (lstart[None, :, None, :] + rank), axis=-1).reshape(2, n)
    seg = jnp.sum(pc, axis=0)
    pe = (seg + MOE_TILE - 1) // MOE_TILE * MOE_TILE
    ends = jnp.cumsum(pe)
    base = ends - pe
    gstart = base[None, :] + jnp.cumsum(pc, axis=0) - pc
    r_tot = -(-(2 * n + N_EXPERTS * (MOE_CHUNK - 1) * n_t) // MOE_TILE) * MOE_TILE + N_EXPERTS * MOE_TILE
    n_tiles = r_tot // MOE_TILE
    tile_start = jnp.arange(n_tiles, dtype=jnp.int32) * MOE_TILE
    tile_expert = jnp.minimum(jnp.sum((tile_start[:, None] >= ends[None, :]).astype(jnp.int32), axis=1), N_EXPERTS - 1)
    tables = dict(
        lstart=lstart.reshape(-1), gstart=gstart.reshape(-1), nchunk=(pc // MOE_CHUNK).reshape(-1),
        zstart=jnp.concatenate([base + seg, ends[-1:]]),
        zcount=jnp.concatenate([(pe - seg) // MOE_CHUNK, (r_tot - ends[-1:]) // MOE_TILE]),
        tile_expert=tile_expert, n_used=(ends[-1] // MOE_TILE).reshape(1))
    return r_tot, lpos, tables


def _seg_loop(tab, t, fn):
    lstart_ref, gstart_ref, nchunk_ref = tab
    for e in range(N_EXPERTS):
        ls = lstart_ref[t * N_EXPERTS + e]
        gs = gstart_ref[t * N_EXPERTS + e]

        def body(j, c, ls=ls, gs=gs):
            fn(pl.multiple_of(ls + j * MOE_CHUNK, MOE_CHUNK), pl.multiple_of(gs + j * MOE_CHUNK, MOE_CHUNK))
            return c
        lax.fori_loop(0, nchunk_ref[t * N_EXPERTS + e], body, 0)


def _n_chunks(nchunk_ref, t):
    tot = nchunk_ref[t * N_EXPERTS]
    for e in range(1, N_EXPERTS):
        tot = tot + nchunk_ref[t * N_EXPERTS + e]
    return tot


def _sort_body(lstart_ref, gstart_ref, nchunk_ref, zstart_ref, zcount_ref, lpos_ref, xa_ref, xb_ref, xs_hbm,
               xloc, zbuf, sem, zsem, *, n_t, n_ta):
    t = pl.program_id(0)
    x_tile = jnp.where(t < n_ta, xa_ref[...], xb_ref[...]).astype(BF16)
    slot = lax.rem(t, 2)
    tab = (lstart_ref, gstart_ref, nchunk_ref)
    rows = xloc.shape[1]

    def chunk_copy(s, lrow, grow):
        return pltpu.make_async_copy(xloc.at[s, pl.ds(lrow, MOE_CHUNK)], xs_hbm.at[pl.ds(grow, MOE_CHUNK)], sem.at[s])

    def wait_tile(s, tile):
        def body(j, c):
            chunk_copy(s, 0, 0).wait()
            return c
        lax.fori_loop(0, _n_chunks(nchunk_ref, tile), body, 0)

    @pl.when(t == 0)
    def _():
        zbuf[...] = jnp.zeros_like(zbuf)
        zero_copy = lambda grow: pltpu.make_async_copy(
            zbuf.at[pl.ds(0, MOE_CHUNK)], xs_hbm.at[pl.ds(grow, MOE_CHUNK)], zsem.at[0])
        zero_tile = lambda grow: pltpu.make_async_copy(zbuf, xs_hbm.at[pl.ds(grow, MOE_TILE)], zsem.at[0])
        tail_start = zstart_ref[N_EXPERTS]
        for e in range(N_EXPERTS):
            def zb(j, c, e=e):
                zero_copy(pl.multiple_of(zstart_ref[e] + j * MOE_CHUNK, MOE_CHUNK)).start()
                return c
            lax.fori_loop(0, zcount_ref[e], zb, 0)

        def tb(j, c):
            zero_tile(pl.multiple_of(tail_start + j * MOE_TILE, MOE_TILE)).start()
            return c
        lax.fori_loop(0, zcount_ref[N_EXPERTS], tb, 0)
        for e in range(N_EXPERTS):
            def zw(j, c):
                zero_copy(0).wait()
                return c
            lax.fori_loop(0, zcount_ref[e], zw, 0)

        def tw(j, c):
            zero_tile(0).wait()
            return c
        lax.fori_loop(0, zcount_ref[N_EXPERTS], tw, 0)

    @pl.when(t >= 2)
    def _():
        wait_tile(slot, t - 2)

    l_iota = lax.broadcasted_iota(jnp.int32, (rows, xa_ref.shape[0]), 0)
    perm = jnp.logical_or(l_iota == lpos_ref[0:1, :], l_iota == lpos_ref[1:2, :])
    perm = jnp.where(perm, 1.0, 0.0).astype(BF16)
    xloc[slot] = jnp.dot(perm, x_tile, preferred_element_type=F32)
    _seg_loop(tab, t, lambda lrow, grow: chunk_copy(slot, lrow, grow).start())

    @pl.when(t == n_t - 1)
    def _():
        wait_tile(slot, t)
        if n_t >= 2:
            wait_tile(1 - slot, t - 1)


def _moe_sort(xa, xb, lpos, tab, tt, r_tot):
    n_ta, n_tb = xa.shape[0] // tt, xb.shape[0] // tt
    n_t = n_ta + n_tb
    rows = _local_rows(tt)
    grid_spec = pltpu.PrefetchScalarGridSpec(
        num_scalar_prefetch=5,
        grid=(n_t,),
        in_specs=[pl.BlockSpec((2, tt), lambda t, *_: (0, t)),
                  pl.BlockSpec((tt, D_MODEL), lambda t, *_: (jnp.minimum(t, n_ta - 1), 0)),
                  pl.BlockSpec((tt, D_MODEL), lambda t, *_: (jnp.maximum(t - n_ta, 0), 0))],
        out_specs=pl.BlockSpec(memory_space=pl.ANY),
        scratch_shapes=[pltpu.VMEM((2, rows, D_MODEL), F32), pltpu.VMEM((MOE_TILE, D_MODEL), F32),
                        pltpu.SemaphoreType.DMA((2,)), pltpu.SemaphoreType.DMA((1,))])
    return pl.pallas_call(
        functools.partial(_sort_body, n_t=n_t, n_ta=n_ta),
        out_shape=jax.ShapeDtypeStruct((r_tot, D_MODEL), F32),
        grid_spec=grid_spec,
        compiler_params=_cparams(("arbitrary",)),
        name="moe_sort",
    )(tab["lstart"], tab["gstart"], tab["nchunk"], tab["zstart"], tab["zcount"], lpos, xa, xb)


def _expert_body(te_ref, nu_ref, x_ref, wg_ref, wu_ref, wd_ref, o_ref):
    del te_ref
    used = pl.program_id(0) < nu_ref[0]

    @pl.when(used)
    def _():
        xb = x_ref[...].astype(BF16)
        hid = _gelu(jnp.dot(xb, wg_ref[0], preferred_element_type=F32)) * jnp.dot(xb, wu_ref[0], preferred_element_type=F32)
        o_ref[...] = jnp.dot(hid.astype(BF16), wd_ref[0], preferred_element_type=F32)

    @pl.when(jnp.logical_not(used))
    def _():
        o_ref[...] = jnp.zeros_like(o_ref)


def _moe_experts(xs, tab, wg, wu, wd):
    r_tot = xs.shape[0]
    last = lambda i, nu: jnp.minimum(i, nu[0] - 1)
    grid_spec = pltpu.PrefetchScalarGridSpec(
        num_scalar_prefetch=2,
        grid=(r_tot // MOE_TILE,),
        in_specs=[pl.BlockSpec((MOE_TILE, D_MODEL), lambda i, te, nu: (last(i, nu), 0)),
                  pl.BlockSpec((1, D_MODEL, D_EXPERT), lambda i, te, nu: (te[last(i, nu)], 0, 0)),
                  pl.BlockSpec((1, D_MODEL, D_EXPERT), lambda i, te, nu: (te[last(i, nu)], 0, 0)),
                  pl.BlockSpec((1, D_EXPERT, D_MODEL), lambda i, te, nu: (te[last(i, nu)], 0, 0))],
        out_specs=pl.BlockSpec((MOE_TILE, D_MODEL), lambda i, te, nu: (i, 0)))
    return pl.pallas_call(
        _expert_body,
        out_shape=jax.ShapeDtypeStruct((r_tot, D_MODEL), F32),
        grid_spec=grid_spec,
        compiler_params=_cparams(("arbitrary",)),
        name="moe_experts",
    )(tab["tile_expert"], tab["n_used"], xs, wg, wu, wd)


def _combine_body(lstart_ref, gstart_ref, nchunk_ref, ys_hbm, xa_ref, xb_ref, meta_ref, g_ref, b_ref,
                  oa_ref, ob_ref, yloc, moe_s, sem, *, n_t, n_ta):
    t = pl.program_id(0)
    first = t < n_ta
    slot = lax.rem(t, 2)
    tab = (lstart_ref, gstart_ref, nchunk_ref)
    rows = yloc.shape[1]

    def chunk_copy(s, lrow, grow):
        return pltpu.make_async_copy(ys_hbm.at[pl.ds(grow, MOE_CHUNK)], yloc.at[s, pl.ds(lrow, MOE_CHUNK)], sem.at[s])

    def fetch(s, tile):
        _seg_loop(tab, tile, lambda lrow, grow: chunk_copy(s, lrow, grow).start())

    @pl.when(t == 0)
    def _():
        yloc[...] = jnp.zeros_like(yloc)
        fetch(0, 0)

    @pl.when(t + 1 < n_t)
    def _():
        fetch(1 - slot, t + 1)

    def wbody(j, c):
        chunk_copy(slot, 0, 0).wait()
        return c
    lax.fori_loop(0, _n_chunks(nchunk_ref, t), wbody, 0)

    meta = meta_ref[...]
    tt = meta.shape[0]
    l_iota = lax.broadcasted_iota(jnp.int32, (tt, rows), 1).astype(F32)
    sel = (jnp.where(l_iota == meta[:, 0:1], meta[:, 2:3], 0.0) + jnp.where(l_iota == meta[:, 1:2], meta[:, 3:4], 0.0))
    sel_hi, sel_lo = _split_bf16(sel)
    yb = yloc[slot].astype(BF16)
    moe_s[...] = jnp.dot(sel_hi, yb, preferred_element_type=F32) + jnp.dot(sel_lo, yb, preferred_element_type=F32)

    def finish(x_ref, o_ref):
        def rows8(i, c):
            r = pl.ds(pl.multiple_of(i * 8, 8), 8)
            o_ref[r, :] = _layer_norm(DN_ALPHA * x_ref[r, :] + moe_s[r, :], g_ref[...], b_ref[...])
            return c
        lax.fori_loop(0, tt // 8, rows8, 0, unroll=2)

    @pl.when(first)
    def _():
        finish(xa_ref, oa_ref)

    @pl.when(jnp.logical_not(first))
    def _():
        finish(xb_ref, ob_ref)


def _moe_combine(ys, xa, xb, meta_n, tab, g, b, tt):
    n_ta, n_tb = xa.shape[0] // tt, xb.shape[0] // tt
    n_t = n_ta + n_tb
    rows = _local_rows(tt)
    vec = lambda: pl.BlockSpec((1, D_MODEL), lambda t, *_: (0, 0))
    in_a = pl.BlockSpec((tt, D_MODEL), lambda t, *_: (jnp.minimum(t, n_ta - 1), 0))
    in_b = pl.BlockSpec((tt, D_MODEL), lambda t, *_: (jnp.maximum(t - n_ta, 0), 0))
    grid_spec = pltpu.PrefetchScalarGridSpec(
        num_scalar_prefetch=3,
        grid=(n_t,),
        in_specs=[pl.BlockSpec(memory_space=pl.ANY), in_a, in_b,
                  pl.BlockSpec((tt, 8), lambda t, *_: (t, 0)), vec(), vec()],
        out_specs=(in_a, in_b),
        scratch_shapes=[pltpu.VMEM((2, rows, D_MODEL), F32), pltpu.VMEM((tt, D_MODEL), F32),
                        pltpu.SemaphoreType.DMA((2,))])
    return pl.pallas_call(
        functools.partial(_combine_body, n_t=n_t, n_ta=n_ta),
        out_shape=(jax.ShapeDtypeStruct(xa.shape, F32), jax.ShapeDtypeStruct(xb.shape, F32)),
        grid_spec=grid_spec,
        compiler_params=_cparams(("arbitrary",)),
        name="moe_combine_ln",
    )(tab["lstart"], tab["gstart"], tab["nchunk"], ys, xa, xb, meta_n, g, b)


def _moe(xa, xb, meta_t, p):
    n = xa.shape[0] + xb.shape[0]
    tt = 256
    while xa.shape[0] % tt or xb.shape[0] % tt:
        tt //= 2
    r_tot, lpos, tab = _dispatch(meta_t, n, tt)
    xs = _moe_sort(xa, xb, lpos, tab, tt, r_tot)
    ys = _moe_experts(xs, tab, p["w_gate"], p["w_up"], p["w_down"])
    meta_n = jnp.concatenate([lpos.astype(F32), meta_t[2:4], jnp.zeros((4, n), F32)], axis=0).T
    return _moe_combine(ys, xa, xb, meta_n, tab, p["ln2_g"], p["ln2_b"], tt)


def _largest_tile(n, cap, mult):
    best = None
    for d in range(mult, min(n, cap) + 1, mult):
        if n % d == 0:
            best = d
    assert best is not None, (n, cap, mult)
    return best


def _seq_tails(x, off, b, t, k, c0, c1):
    if b <= 8:
        return jnp.stack([lax.slice(x, (off + i * t + t - k, c0), (off + (i + 1) * t, c1)) for i in range(b)])
    return lax.slice(x, (off, c0), (off + b * t, c1)).reshape(b, t, c1 - c0)[:, t - k:, :]


def _kv_rows(qkv, g, off, b, t, k):
    def pick(which):
        h0 = (which * N_GROUPS + g) * HEADS
        if b <= 8:
            a = jnp.stack([lax.slice(qkv, (h0, off + i * t + t - k, 0), (h0 + HEADS, off + (i + 1) * t, HEAD_DIM))
                           for i in range(b)], axis=1)
        else:
            a = lax.slice(qkv, (h0, off, 0), (h0 + HEADS, off + b * t, HEAD_DIM)).reshape(HEADS, b, t, HEAD_DIM)[:, :, t - k:]
        return jnp.transpose(a, (1, 2, 0, 3))
    return jnp.stack([pick(1), pick(2)], axis=2)


def _block_diag_gates(w_a, w_x):
    per = RNN_CHUNK // RNN_BLOCK
    chunks = []
    for c in range(N_RNN_CHUNKS):
        halves = []
        for w in (w_a, w_x):
            m = jnp.zeros((RNN_CHUNK, RNN_CHUNK), F32)
            for i in range(per):
                m = lax.dynamic_update_slice(m, w[c * per + i], (i * RNN_BLOCK, i * RNN_BLOCK))
            halves.append(m)
        chunks.append(jnp.concatenate(halves, axis=1))
    return jnp.stack(chunks).astype(BF16)


def kernel(x_prompt, x_sample, cache_kv_w128, cache_kv_w512, cache_kv_w2048, cache_mem_kv, state_h, state_conv, mem_prompt, w_in, b_gates, conv_w, conv_b, w_a, b_a, w_x, b_x, lru_lambda, w_br_lru, w_br_att, w_br_mem, w_o, w_mem_kv, ln1_g, ln1_b, w_rg, b_rg, w_re, b_re, w_gate, w_up, w_down, ln2_g, ln2_b):
    row = lambda v: v.reshape(1, -1).astype(F32)
    w_router = jnp.zeros((ROUTER_ROWS, D_MODEL), F32)
    w_router = w_router.at[:N_EXPERT_GROUPS].set(w_rg.T).at[N_EXPERT_GROUPS:N_EXPERT_GROUPS + N_EXPERTS].set(w_re.T)
    b_router = jnp.zeros((ROUTER_ROWS, 1), F32)
    b_router = b_router.at[:N_EXPERT_GROUPS, 0].set(b_rg).at[N_EXPERT_GROUPS:N_EXPERT_GROUPS + N_EXPERTS, 0].set(b_re)
    p = dict(
        b_gates=row(b_gates), conv_w=conv_w, conv_b=row(conv_b), wax=_block_diag_gates(w_a, w_x),
        b_a=row(b_a), b_x=row(b_x), lam=row(lru_lambda),
        w_br_lru=w_br_lru.astype(BF16), w_br_att=w_br_att.astype(BF16), w_br_mem=w_br_mem.astype(BF16),
        w_o=w_o.astype(BF16), ln1_g=row(ln1_g), ln1_b=row(ln1_b), w_router=w_router, b_router=b_router,
        w_gate=w_gate.astype(BF16), w_up=w_up.astype(BF16), w_down=w_down.astype(BF16),
        ln2_g=row(ln2_g), ln2_b=row(ln2_b))

    bp, s, _ = x_prompt.shape
    bs, ts, _ = x_sample.shape
    n_p, n_s = bp * s, bs * ts
    xp2, xs2 = x_prompt.reshape(n_p, D_MODEL), x_sample.reshape(n_s, D_MODEL)

    xb = lax.dynamic_update_slice(jnp.pad(xp2.astype(BF16), ((0, n_s), (0, 0))), xs2.astype(BF16), (n_p, 0))
    tm_a = _largest_tile(n_p + n_s, 2304, 16)
    wb = lambda c0, c1: w_in[:, c0:c1].astype(BF16)
    xrg = _matmul(xb, wb(0, COL_Q), 0, 2 * D_RNN, F32, tm_a, 768, "in_proj_rnn")
    qkv = _matmul(xb, wb(COL_Q, COL_QM), 0, N_QKV_HEADS * HEAD_DIM, F32, tm_a, 768, "in_proj_qkv", split=6)
    qm = _matmul(xb, wb(COL_QM, COL_GATES), 0, D_MEM, F32, tm_a, 512, "in_proj_qm", split=2)
    gates = _matmul(xb, wb(COL_GATES, COL_GATES + N_GATE_COLS), 0, N_GATE_COLS, BF16, tm_a, 1024, "gate_proj",
                    bias=p["b_gates"])
    lru = (p["conv_w"], p["conv_b"], p["wax"], p["b_a"], p["b_x"], p["lam"])
    branch_w = (p["w_br_lru"], p["w_br_att"], p["w_br_mem"])
    ln1 = (p["w_o"], p["ln1_g"], p["ln1_b"], p["w_router"], p["b_router"])

    mem_rows = mem_prompt.reshape(bp * N_MEM, D_MODEL)
    mem_kv_p = _matmul(mem_rows, w_mem_kv, 0, 2 * D_MEM, F32, _tile(bp * N_MEM, 512), 512, "mem_kv_proj")
    a_p, h_p = _rglru(xrg, 0, bp, s, jnp.zeros((bp, CONV_W - 1, D_RNN), F32), jnp.zeros((bp, D_RNN), F32), *lru,
                      _tile(s, 256), BF16)
    att_p = _attn_prompt(qkv, bp, s)
    mem_p = _mem_attn(qm, mem_kv_p.reshape(bp, N_MEM, 2 * D_MEM), bp, s, _tile(s, 1024), BF16)
    merged_p = _branch_merge(a_p, att_p, mem_p, gates, 0, *branch_w, _tile(n_p, 256))
    x1_p, meta_p = _proj_ln(merged_p, xp2, *ln1, _tile(n_p, 512))

    a_s, h_s = _rglru(xrg, n_p, bs, ts, state_conv, state_h, *lru, ts, F32)
    att_s = _attn_sample(qkv, (cache_kv_w128, cache_kv_w512, cache_kv_w2048), bs, ts, n_p)
    mem_s = _mem_attn_cache(qm, cache_mem_kv, bs, ts, n_p)
    tm_s = _tile(n_s, 256)
    assert n_p % tm_s == 0
    merged_s = _branch_merge(a_s, att_s, mem_s, gates, n_p, *branch_w, tm_s)
    x1_s, meta_s = _proj_ln(merged_s, xs2, *ln1, _tile(n_s, 512))

    y_p, y_s = _moe(x1_p, x1_s, jnp.concatenate([meta_p, meta_s], axis=1), p)

    kc = CONV_W - 1
    conv_p = jnp.concatenate([jnp.zeros((bp, kc, D_RNN), F32), _seq_tails(xrg, 0, bp, s, min(kc, s), 0, D_RNN)], axis=1)[:, -kc:]
    conv_s = jnp.concatenate([state_conv, _seq_tails(xrg, n_p, bs, ts, min(kc, ts), 0, D_RNN)], axis=1)[:, -kc:]
    kv_p = [_kv_rows(qkv, g, 0, bp, s, min(KEYS * d, s)) for g, d in enumerate(DILATIONS)]
    kv_s = [_kv_rows(qkv, g, n_p, bs, ts, ts) for g in range(N_GROUPS)]
    return (y_p.reshape(bp, s, D_MODEL), y_s.reshape(bs, ts, D_MODEL), kv_p[0], kv_p[1], kv_p[2],
            mem_kv_p.reshape(bp, N_MEM, 2, MEM_HEADS, MEM_HEAD_DIM), h_p.reshape(bp, D_RNN), conv_p,
            kv_s[0], kv_s[1], kv_s[2], h_s.reshape(bs, D_RNN), conv_s)
```

```python
import functools

import jax
import jax.numpy as jnp
from jax import lax
from jax.experimental import pallas as pl
from jax.experimental.pallas import tpu as pltpu

F32 = jnp.float32
BF16 = jnp.bfloat16

D_MODEL = 2048
D_RNN = 1536
N_RNN_BLOCKS = 16
RNN_BLOCK = D_RNN // N_RNN_BLOCKS
RNN_CHUNK = 384
N_RNN_CHUNKS = D_RNN // RNN_CHUNK
CONV_W = 4
LRU_C = 8.0
HEAD_DIM = 128
HEADS = 4
DILATIONS = (1, 4, 16)
KEYS = 128
N_GROUPS = 3
D_ATT_OUT = HEADS * HEAD_DIM
N_QKV_HEADS = 3 * N_GROUPS * HEADS
ATT_SCALE = HEAD_DIM ** -0.5
ATT_TILE = KEYS * max(DILATIONS)
N_MEM = 256
MEM_HEADS = 4
MEM_HEAD_DIM = 256
D_MEM = MEM_HEADS * MEM_HEAD_DIM
MEM_SCALE = MEM_HEAD_DIM ** -0.5
N_EXPERT_GROUPS = 4
EXPERTS_PER_GROUP = 4
N_EXPERTS = 16
D_EXPERT = 512
MOE_TILE = 512
DN_ALPHA = 2.0 ** 0.25
LN_EPS = 1e-5
NEG_INF = -1e30

COL_Q = 2 * D_RNN
COL_QM = COL_Q + N_QKV_HEADS * HEAD_DIM
COL_GATES = COL_QM + D_MEM
N_GATE_COLS = 3 * D_MODEL
ROUTER_ROWS = 32

VMEM_LIMIT = 56 * 1024 * 1024


def _cparams(sem):
    return pltpu.CompilerParams(dimension_semantics=sem, vmem_limit_bytes=VMEM_LIMIT)


def _gelu(x):
    return 0.5 * x * (1.0 + jnp.tanh(0.7978845608028654 * (x + 0.044715 * (x * x * x))))


def _layer_norm(x, g, b):
    mu = jnp.mean(x, axis=-1, keepdims=True)
    xc = x - mu
    var = jnp.mean(xc * xc, axis=-1, keepdims=True)
    return xc * lax.rsqrt(var + LN_EPS) * g + b


def _dot_nt(a, b):
    return lax.dot_general(a, b, (((1,), (1,)), ((), ())), preferred_element_type=F32)


def _tile(m, cap):
    t = min(m, cap)
    assert m % t == 0, (m, t)
    return t


def _mm_body(x_ref, w_ref, *rest, split, gate):
    acc = jnp.dot(x_ref[...].astype(BF16), w_ref[...].astype(BF16), preferred_element_type=F32)
    if gate:
        b_ref, o_ref = rest
        acc = 0.5 * jnp.tanh(0.5 * (acc + b_ref[...])) + 0.5
    else:
        (o_ref,) = rest
    if split == 1:
        o_ref[...] = acc.astype(o_ref.dtype)
    else:
        w = acc.shape[1] // split
        for s in range(split):
            o_ref[s] = acc[:, s * w:(s + 1) * w].astype(o_ref.dtype)


def _matmul(x, w, col_off, n_cols, out_dtype, tm, tn, name, bias=None, split=1):
    m, k = x.shape
    cb = col_off // tn
    in_specs = [pl.BlockSpec((tm, k), lambda i, j: (i, 0)),
                pl.BlockSpec((k, tn), lambda i, j: (0, j + cb))]
    args = [x, w]
    if bias is not None:
        in_specs.append(pl.BlockSpec((1, tn), lambda i, j: (0, j)))
        args.append(bias)
    if split == 1:
        out_shape = jax.ShapeDtypeStruct((m, n_cols), out_dtype)
        out_spec = pl.BlockSpec((tm, tn), lambda i, j: (i, j))
    else:
        out_shape = jax.ShapeDtypeStruct((n_cols * split // tn, m, tn // split), out_dtype)
        out_spec = pl.BlockSpec((split, tm, tn // split), lambda i, j: (j, i, 0))
    return pl.pallas_call(
        functools.partial(_mm_body, split=split, gate=bias is not None),
        out_shape=out_shape,
        grid=(m // tm, n_cols // tn),
        in_specs=in_specs,
        out_specs=out_spec,
        compiler_params=_cparams(("parallel", "arbitrary")),
        name=name,
    )(*args)


def _cast_rows_body(a_ref, b_ref, o_ref, *, n_ta):
    o_ref[...] = jnp.where(pl.program_id(0) < n_ta, a_ref[...], b_ref[...]).astype(o_ref.dtype)


def _cast_rows(xa, xb, dtype, tr):
    n_ta, n_tb = xa.shape[0] // tr, xb.shape[0] // tr
    cols = xa.shape[1]
    return pl.pallas_call(
        functools.partial(_cast_rows_body, n_ta=n_ta),
        out_shape=jax.ShapeDtypeStruct((xa.shape[0] + xb.shape[0], cols), dtype),
        grid=(n_ta + n_tb,),
        in_specs=[pl.BlockSpec((tr, cols), lambda i: (jnp.minimum(i, n_ta - 1), 0)),
                  pl.BlockSpec((tr, cols), lambda i: (jnp.maximum(i - n_ta, 0), 0))],
        out_specs=pl.BlockSpec((tr, cols), lambda i: (i, 0)),
        compiler_params=_cparams(("parallel",)),
        name="cast_rows",
    )(xa, xb)


def _rglru_body(xr_ref, xg_ref, cbuf_ref, h0_ref, cw_ref, cb_ref, wax_ref, ba_ref, bx_ref, lam_ref,
                out_ref, hl_ref, ext_s, a_s, u_s, h_s, *, tt):
    t = pl.program_id(1)

    @pl.when(t == 0)
    def _():
        ext_s[0:8, :] = jnp.zeros((8, D_RNN), F32)
        ext_s[5:8, :] = cbuf_ref[0]
        h_s[...] = jnp.broadcast_to(h0_ref[0], (8, D_RNN))

    @pl.when(t > 0)
    def _():
        ext_s[0:8, :] = ext_s[tt:tt + 8, :]

    ext_s[8:8 + tt, :] = xr_ref[...]
    cw = cw_ref[...]
    xc = (cb_ref[...] + cw[3:4, :] * ext_s[8:8 + tt, :] + cw[2:3, :] * ext_s[7:7 + tt, :]
          + cw[1:2, :] * ext_s[6:6 + tt, :] + cw[0:1, :] * ext_s[5:5 + tt, :])
    xcb = xc.astype(BF16)
    r_parts, i_parts = [], []
    for c in range(N_RNN_CHUNKS):
        g = jnp.dot(xcb[:, c * RNN_CHUNK:(c + 1) * RNN_CHUNK], wax_ref[c], preferred_element_type=F32)
        r_parts.append(g[:, :RNN_CHUNK])
        i_parts.append(g[:, RNN_CHUNK:])
    sigmoid = lambda z: 0.5 * jnp.tanh(0.5 * z) + 0.5
    r = sigmoid(jnp.concatenate(r_parts, axis=1) + ba_ref[...])
    gi = sigmoid(jnp.concatenate(i_parts, axis=1) + bx_ref[...])
    nl = -lam_ref[...]
    softplus = jnp.maximum(nl, 0.0) + jnp.log1p(jnp.exp(-jnp.abs(nl)))
    log_a = (-LRU_C) * r * softplus
    th = jnp.tanh(log_a)
    a_s[...] = jnp.exp(log_a)
    u_s[...] = jnp.sqrt(-2.0 * th / (1.0 - th)) * (gi * xc)

    rows = lax.broadcasted_iota(jnp.int32, (8, D_RNN), 0)

    def blk(i, h):
        r0 = pl.multiple_of(i * 8, 8)
        ab = a_s[pl.ds(r0, 8), :]
        ub = u_s[pl.ds(r0, 8), :]
        for s in (1, 2, 4):
            keep = rows >= s
            ub = ab * jnp.where(keep, pltpu.roll(ub, s, 0), 0.0) + ub
            ab = ab * jnp.where(keep, pltpu.roll(ab, s, 0), 1.0)
        hb = ab * h + ub
        u_s[pl.ds(r0, 8), :] = hb
        return jnp.broadcast_to(hb[7:8, :], (8, D_RNN))

    h_fin = lax.fori_loop(0, tt // 8, blk, h_s[...])
    h_s[...] = h_fin
    hl_ref[0] = h_fin[0:1, :]
    out_ref[...] = (u_s[...] * _gelu(xg_ref[...])).astype(out_ref.dtype)


def _rglru(xrg, off, b, t, conv_buf, h0, conv_w, conv_b, wax, b_a, b_x, lam, tt, out_dtype):
    nt = t // tt
    ob = off // tt
    vec = lambda: pl.BlockSpec((1, D_RNN), lambda i, j: (0, 0))
    return pl.pallas_call(
        functools.partial(_rglru_body, tt=tt),
        out_shape=(jax.ShapeDtypeStruct((b * t, D_RNN), out_dtype), jax.ShapeDtypeStruct((b, 1, D_RNN), F32)),
        grid=(b, nt),
        in_specs=[pl.BlockSpec((tt, D_RNN), lambda i, j: (ob + i * nt + j, 0)),
                  pl.BlockSpec((tt, D_RNN), lambda i, j: (ob + i * nt + j, 1)),
                  pl.BlockSpec((1, CONV_W - 1, D_RNN), lambda i, j: (i, 0, 0)),
                  pl.BlockSpec((1, 1, D_RNN), lambda i, j: (i, 0, 0)),
                  pl.BlockSpec((CONV_W, D_RNN), lambda i, j: (0, 0)),
                  vec(),
                  pl.BlockSpec((N_RNN_CHUNKS, RNN_CHUNK, 2 * RNN_CHUNK), lambda i, j: (0, 0, 0)),
                  vec(), vec(), vec()],
        out_specs=(pl.BlockSpec((tt, D_RNN), lambda i, j: (i * nt + j, 0)),
                   pl.BlockSpec((1, 1, D_RNN), lambda i, j: (i, 0, 0))),
        scratch_shapes=[pltpu.VMEM((tt + 8, D_RNN), F32), pltpu.VMEM((tt, D_RNN), F32),
                        pltpu.VMEM((tt, D_RNN), F32), pltpu.VMEM((8, D_RNN), F32)],
        compiler_params=_cparams(("parallel", "arbitrary")),
        name="rglru",
    )(xrg, xrg, conv_buf, h0.reshape(b, 1, D_RNN), conv_w, conv_b, wax, b_a, b_x, lam)


def _band_block(q, k, v, bias):
    s = _dot_nt(q, k) * ATT_SCALE + bias
    m = jnp.max(s, axis=-1, keepdims=True)
    p = jnp.exp(s - m)
    l = jnp.sum(p, axis=-1, keepdims=True)
    o = jnp.dot(p.astype(BF16), v, preferred_element_type=F32) / l
    return o, m + jnp.log(l)


def _attn_prompt_body(*refs):
    q_refs, k_refs, v_refs, kh_refs, vh_refs = (refs[3 * i:3 * i + 3] for i in range(5))
    o_ref, og_s, lg_s = refs[15:]
    row = lax.broadcasted_iota(jnp.int32, (KEYS, 2 * KEYS), 0)
    col = lax.broadcasted_iota(jnp.int32, (KEYS, 2 * KEYS), 1)
    band = jnp.logical_and(col >= row, col <= row + KEYS)
    bias = jnp.where(band, 0.0, NEG_INF)
    has_prev = pl.program_id(1) > 0
    bias_first = jnp.where(jnp.logical_and(band, jnp.logical_or(col >= KEYS, has_prev)), 0.0, NEG_INF)

    def put(g, rows, o, lse):
        og_s[g, rows, :] = o
        lg_s[g, rows, :] = jnp.broadcast_to(lse, (KEYS, HEAD_DIM))

    def rows_of(start, size, d):
        return pl.ds(start, size, stride=d) if d > 1 else pl.ds(start, size)

    for g, d in enumerate(DILATIONS):
        q_ref, k_ref, v_ref, kh_ref, vh_ref = q_refs[g], k_refs[g], v_refs[g], kh_refs[g], vh_refs[g]
        nqb = ATT_TILE // (d * KEYS)
        for r in range(d):
            own = rows_of(r, KEYS, d)
            kk = jnp.concatenate([kh_ref[0, own, :], k_ref[0, own, :]], axis=0).astype(BF16)
            vv = jnp.concatenate([vh_ref[0, own, :], v_ref[0, own, :]], axis=0).astype(BF16)
            o, lse = _band_block(q_ref[0, own, :].astype(BF16), kk, vv, bias_first)
            put(g, own, o, lse)
            for qb in range(1, nqb):
                rows = rows_of(qb * KEYS * d + r, KEYS, d)
                keys = rows_of((qb - 1) * KEYS * d + r, 2 * KEYS, d)
                o, lse = _band_block(q_ref[0, rows, :].astype(BF16), k_ref[0, keys, :].astype(BF16),
                                     v_ref[0, keys, :].astype(BF16), bias)
                put(g, rows, o, lse)

    l0, l1, l2 = lg_s[0], lg_s[1], lg_s[2]
    m = jnp.maximum(jnp.maximum(l0, l1), l2)
    e0, e1, e2 = jnp.exp(l0 - m), jnp.exp(l1 - m), jnp.exp(l2 - m)
    o_ref[...] = ((e0 * og_s[0] + e1 * og_s[1] + e2 * og_s[2]) / (e0 + e1 + e2)).astype(o_ref.dtype)


def _attn_prompt(qkv, b, s):
    assert s % ATT_TILE == 0
    nt = s // ATT_TILE
    n = b * s

    def cur(which, g):
        return pl.BlockSpec((1, ATT_TILE, HEAD_DIM),
                            lambda i, j, h: ((which * N_GROUPS + g) * HEADS + h, i * nt + j, 0))

    def halo(which, g):
        rows = KEYS * DILATIONS[g]
        per = ATT_TILE // rows
        return pl.BlockSpec((1, rows, HEAD_DIM),
                            lambda i, j, h: ((which * N_GROUPS + g) * HEADS + h,
                                             jnp.maximum((i * nt + j) * per - 1, 0), 0))

    in_specs = ([cur(0, g) for g in range(N_GROUPS)] + [cur(1, g) for g in range(N_GROUPS)]
                + [cur(2, g) for g in range(N_GROUPS)] + [halo(1, g) for g in range(N_GROUPS)]
                + [halo(2, g) for g in range(N_GROUPS)])
    return pl.pallas_call(
        _attn_prompt_body,
        out_shape=jax.ShapeDtypeStruct((n, D_ATT_OUT), BF16),
        grid=(b, nt, HEADS),
        in_specs=in_specs,
        out_specs=pl.BlockSpec((ATT_TILE, HEAD_DIM), lambda i, j, h: (i * nt + j, h)),
        scratch_shapes=[pltpu.VMEM((N_GROUPS, ATT_TILE, HEAD_DIM), F32),
                        pltpu.VMEM((N_GROUPS, ATT_TILE, HEAD_DIM), F32)],
        compiler_params=_cparams(("parallel", "parallel", "parallel")),
        name="attn_prompt",
    )(*([qkv] * 15))


def _attn_sample_group(q4, kn4, vn4, c_ref, d, t_new):
    half = len(c_ref.shape) == 4
    wb = c_ref.shape[1] * 16 if half else c_ref.shape[1] // (2 * HEADS)
    nk = wb // 2 if half else wb
    nr = HEADS * t_new
    zeros = jnp.zeros((t_new, HEAD_DIM), F32)
    qbd = jnp.concatenate(
        [jnp.concatenate([q4[h] if hh == h else zeros for hh in range(HEADS)], axis=1) for h in range(HEADS)],
        axis=0).astype(BF16)
    kn = jnp.concatenate([kn4[h] for h in range(HEADS)], axis=1).astype(BF16)
    vn = jnp.concatenate([vn4[h] for h in range(HEADS)], axis=1).astype(BF16)
    if half:
        cache_rows = lambda kv, h: c_ref[0, :, pl.ds(kv * HEADS + h, 8, stride=2 * HEADS), :].reshape(nk, HEAD_DIM)
    else:
        cache_rows = lambda kv, h: c_ref[0, pl.ds(kv * HEADS + h, wb, stride=2 * HEADS), :]
    kc = jnp.concatenate([cache_rows(0, h) for h in range(HEADS)], axis=1).astype(BF16)
    vc = jnp.concatenate([cache_rows(1, h) for h in range(HEADS)], axis=1).astype(BF16)
    tq_c = lax.broadcasted_iota(jnp.int32, (nr, nk), 0) & (t_new - 1)
    e_c = lax.broadcasted_iota(jnp.int32, (nr, nk), 1)
    if half:
        e_c = ((e_c >> 3) << 4) + (e_c & 7)
    dist_c = wb + tq_c - e_c
    ok_c = jnp.logical_and((dist_c & (d - 1)) == 0, dist_c <= KEYS * d)
    tq_n = lax.broadcasted_iota(jnp.int32, (nr, t_new), 0) & (t_new - 1)
    dist_n = tq_n - lax.broadcasted_iota(jnp.int32, (nr, t_new), 1)
    ok_n = jnp.logical_and(jnp.logical_and(dist_n >= 0, (dist_n & (d - 1)) == 0), dist_n <= KEYS * d)
    s_c = jnp.where(ok_c, _dot_nt(qbd, kc) * ATT_SCALE, NEG_INF)
    s_n = jnp.where(ok_n, _dot_nt(qbd, kn) * ATT_SCALE, NEG_INF)
    m = jnp.maximum(jnp.max(s_c, axis=-1, keepdims=True), jnp.max(s_n, axis=-1, keepdims=True))
    p_c = jnp.exp(s_c - m)
    p_n = jnp.exp(s_n - m)
    l = jnp.sum(p_c, axis=-1, keepdims=True) + jnp.sum(p_n, axis=-1, keepdims=True)
    o = (jnp.dot(p_c.astype(BF16), vc, preferred_element_type=F32)
         + jnp.dot(p_n.astype(BF16), vn, preferred_element_type=F32)) / l
    lse = m + jnp.log(l)
    o = jnp.concatenate(
        [o[h * t_new:(h + 1) * t_new, h * HEAD_DIM:(h + 1) * HEAD_DIM] for h in range(HEADS)], axis=1)
    lse = jnp.concatenate(
        [jnp.broadcast_to(lse[h * t_new:(h + 1) * t_new], (t_new, HEAD_DIM)) for h in range(HEADS)], axis=1)
    return o, lse


def _attn_sample_body(*refs, t_new):
    q_refs, k_refs, v_refs, c_refs = (refs[3 * i:3 * i + 3] for i in range(4))
    o_ref = refs[12]
    outs = [_attn_sample_group(q_refs[g][...], k_refs[g][...], v_refs[g][...], c_refs[g], d, t_new)
            for g, d in enumerate(DILATIONS)]
    (o0, l0), (o1, l1), (o2, l2) = outs
    m = jnp.maximum(jnp.maximum(l0, l1), l2)
    e0, e1, e2 = jnp.exp(l0 - m), jnp.exp(l1 - m), jnp.exp(l2 - m)
    o_ref[0] = (e0 * o0 + e1 * o1 + e2 * o2) / (e0 + e1 + e2)


def _attn_sample(qkv, caches, b, t_new, off):
    assert t_new & (t_new - 1) == 0 and off % t_new == 0
    ob = off // t_new
    new = lambda which, g: pl.BlockSpec((HEADS, t_new, HEAD_DIM), lambda i: (which * N_GROUPS + g, ob + i, 0))
    caches2, cache_specs = [], []
    for c, d in zip(caches, DILATIONS):
        wb = c.shape[1]
        if d == 16 and wb % 16 == 0 and t_new <= 8:
            caches2.append(c.reshape(b, wb // 16, 16 * 2 * HEADS, HEAD_DIM))
            cache_specs.append(pl.BlockSpec((1, wb // 16, 8 * 2 * HEADS, HEAD_DIM), lambda i: (i, 0, 0, 0)))
        else:
            caches2.append(c.reshape(b, wb * 2 * HEADS, HEAD_DIM))
            cache_specs.append(pl.BlockSpec((1, wb * 2 * HEADS, HEAD_DIM), lambda i: (i, 0, 0)))
    in_specs = ([new(0, g) for g in range(N_GROUPS)] + [new(1, g) for g in range(N_GROUPS)]
                + [new(2, g) for g in range(N_GROUPS)] + cache_specs)
    att = pl.pallas_call(
        functools.partial(_attn_sample_body, t_new=t_new),
        out_shape=jax.ShapeDtypeStruct((b, t_new, D_ATT_OUT), F32),
        grid=(b,),
        in_specs=in_specs,
        out_specs=pl.BlockSpec((1, t_new, D_ATT_OUT), lambda i: (i, 0, 0)),
        compiler_params=_cparams(("parallel",)),
        name="attn_sample",
    )(*([qkv] * 9), *caches2)
    return att.reshape(b * t_new, D_ATT_OUT)


def _mem_attn_body(q_ref, kv_ref, o_ref):
    for h in range(MEM_HEADS):
        k = kv_ref[0, :, h * MEM_HEAD_DIM:(h + 1) * MEM_HEAD_DIM].astype(BF16)
        v = kv_ref[0, :, D_MEM + h * MEM_HEAD_DIM:D_MEM + (h + 1) * MEM_HEAD_DIM].astype(BF16)
        s = _dot_nt(q_ref[h].astype(BF16), k) * MEM_SCALE
        m = jnp.max(s, axis=-1, keepdims=True)
        p = jnp.exp(s - m)
        l = jnp.sum(p, axis=-1, keepdims=True)
        o = jnp.dot(p.astype(BF16), v, preferred_element_type=F32) / l
        o_ref[:, h * MEM_HEAD_DIM:(h + 1) * MEM_HEAD_DIM] = o.astype(o_ref.dtype)


def _mem_attn(qm, mem_kv, b, t, tm, out_dtype):
    nt = t // tm
    return pl.pallas_call(
        _mem_attn_body,
        out_shape=jax.ShapeDtypeStruct((b * t, D_MEM), out_dtype),
        grid=(b, nt),
        in_specs=[pl.BlockSpec((MEM_HEADS, tm, MEM_HEAD_DIM), lambda i, j: (0, i * nt + j, 0)),
                  pl.BlockSpec((1, N_MEM, 2 * D_MEM), lambda i, j: (i, 0, 0))],
        out_specs=pl.BlockSpec((tm, D_MEM), lambda i, j: (i * nt + j, 0)),
        compiler_params=_cparams(("parallel", "parallel")),
        name="mem_attn",
    )(qm, mem_kv)


def _mem_attn_cache_body(q_ref, c_ref, o_ref):
    outs = []
    for h in range(MEM_HEADS):
        k = c_ref[0, :, 0, h, :].astype(BF16)
        v = c_ref[0, :, 1, h, :].astype(BF16)
        s = _dot_nt(q_ref[h].astype(BF16), k) * MEM_SCALE
        m = jnp.max(s, axis=-1, keepdims=True)
        p = jnp.exp(s - m)
        l = jnp.sum(p, axis=-1, keepdims=True)
        outs.append(jnp.dot(p.astype(BF16), v, preferred_element_type=F32) / l)
    o_ref[...] = jnp.concatenate(outs, axis=1)


def _mem_attn_cache(qm, cache, b, t, off):
    ob = off // t
    return pl.pallas_call(
        _mem_attn_cache_body,
        out_shape=jax.ShapeDtypeStruct((b * t, D_MEM), F32),
        grid=(b,),
        in_specs=[pl.BlockSpec((MEM_HEADS, t, MEM_HEAD_DIM), lambda i: (0, ob + i, 0)),
                  pl.BlockSpec((1, N_MEM, 2, MEM_HEADS, MEM_HEAD_DIM), lambda i: (i, 0, 0, 0, 0))],
        out_specs=pl.BlockSpec((t, D_MEM), lambda i: (i, 0)),
        compiler_params=_cparams(("parallel",)),
        name="mem_attn_cache",
    )(qm, cache)


def _branch_body(al_ref, att_ref, mem_ref, gt_ref, wl_ref, wa_ref, wm_ref, out_ref):
    acc = gt_ref[:, 0:D_MODEL].astype(F32) * jnp.dot(al_ref[...].astype(BF16), wl_ref[...], preferred_element_type=F32)
    acc = acc + gt_ref[:, D_MODEL:2 * D_MODEL].astype(F32) * jnp.dot(att_ref[...].astype(BF16), wa_ref[...], preferred_element_type=F32)
    acc = acc + gt_ref[:, 2 * D_MODEL:3 * D_MODEL].astype(F32) * jnp.dot(mem_ref[...].astype(BF16), wm_ref[...], preferred_element_type=F32)
    out_ref[...] = acc.astype(out_ref.dtype)


def _branch_merge(a_lru, att, mem, gates, goff, wl, wa, wm, tm):
    n = a_lru.shape[0]
    gb = goff // tm
    row = lambda w: pl.BlockSpec((tm, w), lambda i: (i, 0))
    full = lambda a: pl.BlockSpec(a.shape, lambda i: (0, 0))
    return pl.pallas_call(
        _branch_body,
        out_shape=jax.ShapeDtypeStruct((n, D_MODEL), BF16),
        grid=(n // tm,),
        in_specs=[row(D_RNN), row(D_ATT_OUT), row(D_MEM), pl.BlockSpec((tm, N_GATE_COLS), lambda i: (gb + i, 0)),
                  full(wl), full(wa), full(wm)],
        out_specs=row(D_MODEL),
        compiler_params=_cparams(("parallel",)),
        name="branch_merge",
    )(a_lru, att, mem, gates, wl, wa, wm)


def _split_bf16(x):
    hi = x.astype(BF16)
    return hi, (x - hi.astype(F32)).astype(BF16)


def _route_rows(lg):
    g = [lg[i:i + 1, :] for i in range(N_EXPERT_GROUPS)]
    gmax = jnp.maximum(jnp.maximum(g[0], g[1]), jnp.maximum(g[2], g[3]))
    gidx = jnp.where(g[0] == gmax, 0.0, jnp.where(g[1] == gmax, 1.0, jnp.where(g[2] == gmax, 2.0, 3.0)))
    g_p = 1.0 / (jnp.exp(g[0] - gmax) + jnp.exp(g[1] - gmax) + jnp.exp(g[2] - gmax) + jnp.exp(g[3] - gmax))
    e = []
    for k in range(EXPERTS_PER_GROUP):
        rows = [lg[N_EXPERT_GROUPS + gg * EXPERTS_PER_GROUP + k:N_EXPERT_GROUPS + gg * EXPERTS_PER_GROUP + k + 1, :]
                for gg in range(N_EXPERT_GROUPS)]
        e.append(jnp.where(gidx == 0.0, rows[0], jnp.where(gidx == 1.0, rows[1], jnp.where(gidx == 2.0, rows[2], rows[3]))))

    def first_argmax(v):
        mx = jnp.maximum(jnp.maximum(v[0], v[1]), jnp.maximum(v[2], v[3]))
        ix = jnp.where(v[0] == mx, 0.0, jnp.where(v[1] == mx, 1.0, jnp.where(v[2] == mx, 2.0, 3.0)))
        return mx, ix

    v1, i1 = first_argmax(e)
    v2, i2 = first_argmax([jnp.where(i1 == float(k), -jnp.inf, e[k]) for k in range(EXPERTS_PER_GROUP)])
    ex = jnp.exp(v2 - v1)
    w1 = g_p / (1.0 + ex)
    w2 = g_p * ex / (1.0 + ex)
    base = gidx * float(EXPERTS_PER_GROUP)
    zero = jnp.zeros_like(w1)
    return jnp.concatenate([base + i1, base + i2, w1, w2, zero, zero, zero, zero], axis=0)


def _proj_ln_body(mg_ref, x_ref, wo_ref, g_ref, b_ref, wr_ref, br_ref, x1_ref, meta_ref, mix_s):
    @pl.when(pl.program_id(0) == 0)
    def _():
        mix_s[...] = jnp.zeros(mix_s.shape, F32)

    wh, wl = _split_bf16(wr_ref[...])
    tm = x_ref.shape[0]
    sub = min(256, tm)
    for r0 in range(0, tm, sub):
        rs = pl.ds(r0, sub)
        x1 = _layer_norm(DN_ALPHA * x_ref[rs, :] + mix_s[rs, :], g_ref[...], b_ref[...])
        x1_ref[rs, :] = x1
        xh, xl = _split_bf16(x1)
        lg = _dot_nt(wh, xh) + (_dot_nt(wh, xl) + _dot_nt(wl, xh)) + br_ref[...]
        meta_ref[:, rs] = _route_rows(lg)
    mix_s[...] = jnp.dot(mg_ref[...], wo_ref[...], preferred_element_type=F32)


def _proj_ln(merged, x, wo, g, b, wr, br, tm):
    n = merged.shape[0]
    nb = n // tm
    prev = lambda w: pl.BlockSpec((tm, w), lambda i: (jnp.maximum(i - 1, 0), 0))
    full = lambda a: pl.BlockSpec(a.shape, lambda i: (0, 0))
    return pl.pallas_call(
        _proj_ln_body,
        out_shape=(jax.ShapeDtypeStruct((n, D_MODEL), F32), jax.ShapeDtypeStruct((8, n), F32)),
        grid=(nb + 1,),
        in_specs=[pl.BlockSpec((tm, D_MODEL), lambda i: (jnp.minimum(i, nb - 1), 0)), prev(D_MODEL),
                  full(wo), full(g), full(b), full(wr), full(br)],
        out_specs=(prev(D_MODEL), pl.BlockSpec((8, tm), lambda i: (0, jnp.maximum(i - 1, 0)))),
        scratch_shapes=[pltpu.VMEM((tm, D_MODEL), F32)],
        compiler_params=_cparams(("arbitrary",)),
        name="proj_ln_router",
    )(merged, x, wo, g, b, wr, br)


MOE_CHUNK = 8


def _local_rows(tt):
    return -(-(2 * tt + N_EXPERTS * (MOE_CHUNK - 1)) // 128) * 128


def _dispatch(meta_t, n, tt):
    n_t = n // tt
    ids = meta_t[0:2].astype(jnp.int32)
    onehot = (ids[:, :, None] == jnp.arange(N_EXPERTS, dtype=jnp.int32)).astype(jnp.int32).reshape(2, n_t, tt, N_EXPERTS)
    cnt_slot = jnp.sum(onehot, axis=2)
    cnt = cnt_slot[0] + cnt_slot[1]
    pc = (cnt + MOE_CHUNK - 1) // MOE_CHUNK * MOE_CHUNK
    lstart = jnp.cumsum(pc, axis=1) - pc
    tri = (jnp.arange(tt)[:, None] >= jnp.arange(tt)[None, :]).astype(F32)
    csum = jnp.einsum("ut,snte->snue", tri, onehot.astype(F32)).astype(jnp.int32)
    rank = csum - onehot + jnp.stack([jnp.zeros_like(cnt), cnt_slot[0]])[:, :, None, :]
    lpos = jnp.sum(onehot * (lstart[None, :, None, :] + rank), axis=-1).reshape(2, n)
    seg = jnp.sum(pc, axis=0)
    pe = (seg + MOE_TILE - 1) // MOE_TILE * MOE_TILE
    ends = jnp.cumsum(pe)
    base = ends - pe
    gstart = base[None, :] + jnp.cumsum(pc, axis=0) - pc
    r_tot = -(-(2 * n + N_EXPERTS * (MOE_CHUNK - 1) * n_t) // MOE_TILE) * MOE_TILE + N_EXPERTS * MOE_TILE
    n_tiles = r_tot // MOE_TILE
    tile_start = jnp.arange(n_tiles, dtype=jnp.int32) * MOE_TILE
    tile_expert = jnp.minimum(jnp.sum((tile_start[:, None] >= ends[None, :]).astype(jnp.int32), axis=1), N_EXPERTS - 1)
    tables = dict(
        lstart=lstart.reshape(-1), gstart=gstart.reshape(-1), nchunk=(pc // MOE_CHUNK).reshape(-1),
        zstart=jnp.concatenate([base + seg, ends[-1:]]),
        zcount=jnp.concatenate([(pe - seg) // MOE_CHUNK, (r_tot - ends[-1:]) // MOE_TILE]),
        tile_expert=tile_expert, n_used=(ends[-1] // MOE_TILE).reshape(1))
    return r_tot, lpos, tables


def _seg_loop(tab, t, fn):
    lstart_ref, gstart_ref, nchunk_ref = tab
    for e in range(N_EXPERTS):
        ls = lstart_ref[t * N_EXPERTS + e]
        gs = gstart_ref[t * N_EXPERTS + e]

        def body(j, c, ls=ls, gs=gs):
            fn(pl.multiple_of(ls + j * MOE_CHUNK, MOE_CHUNK), pl.multiple_of(gs + j * MOE_CHUNK, MOE_CHUNK))
            return c
        lax.fori_loop(0, nchunk_ref[t * N_EXPERTS + e], body, 0)


def _n_chunks(nchunk_ref, t):
    tot = nchunk_ref[t * N_EXPERTS]
    for e in range(1, N_EXPERTS):
        tot = tot + nchunk_ref[t * N_EXPERTS + e]
    return tot


def _sort_body(lstart_ref, gstart_ref, nchunk_ref, zstart_ref, zcount_ref, lpos_ref, xa_ref, xb_ref, xs_hbm,
               xloc, zbuf, sem, zsem, *, n_t, n_ta):
    t = pl.program_id(0)
    x_tile = jnp.where(t < n_ta, xa_ref[...], xb_ref[...]).astype(BF16)
    slot = lax.rem(t, 2)
    tab = (lstart_ref, gstart_ref, nchunk_ref)
    rows = xloc.shape[1]

    def chunk_copy(s, lrow, grow):
        return pltpu.make_async_copy(xloc.at[s, pl.ds(lrow, MOE_CHUNK)], xs_hbm.at[pl.ds(grow, MOE_CHUNK)], sem.at[s])

    def wait_tile(s, tile):
        def body(j, c):
            chunk_copy(s, 0, 0).wait()
            return c
        lax.fori_loop(0, _n_chunks(nchunk_ref, tile), body, 0)

    @pl.when(t == 0)
    def _():
        zbuf[...] = jnp.zeros_like(zbuf)
        zero_copy = lambda grow: pltpu.make_async_copy(
            zbuf.at[pl.ds(0, MOE_CHUNK)], xs_hbm.at[pl.ds(grow, MOE_CHUNK)], zsem.at[0])
        zero_tile = lambda grow: pltpu.make_async_copy(zbuf, xs_hbm.at[pl.ds(grow, MOE_TILE)], zsem.at[0])
        tail_start = zstart_ref[N_EXPERTS]
        for e in range(N_EXPERTS):
            def zb(j, c, e=e):
                zero_copy(pl.multiple_of(zstart_ref[e] + j * MOE_CHUNK, MOE_CHUNK)).start()
                return c
            lax.fori_loop(0, zcount_ref[e], zb, 0)

        def tb(j, c):
            zero_tile(pl.multiple_of(tail_start + j * MOE_TILE, MOE_TILE)).start()
            return c
        lax.fori_loop(0, zcount_ref[N_EXPERTS], tb, 0)
        for e in range(N_EXPERTS):
            def zw(j, c):
                zero_copy(0).wait()
                return c
            lax.fori_loop(0, zcount_ref[e], zw, 0)

        def tw(j, c):
            zero_tile(0).wait()
            return c
        lax.fori_loop(0, zcount_ref[N_EXPERTS], tw, 0)

    @pl.when(t >= 2)
    def _():
        wait_tile(slot, t - 2)

    l_iota = lax.broadcasted_iota(jnp.int32, (rows, xa_ref.shape[0]), 0)
    perm = jnp.logical_or(l_iota == lpos_ref[0:1, :], l_iota == lpos_ref[1:2, :])
    perm = jnp.where(perm, 1.0, 0.0).astype(BF16)
    xloc[slot] = jnp.dot(perm, x_tile, preferred_element_type=F32)
    _seg_loop(tab, t, lambda lrow, grow: chunk_copy(slot, lrow, grow).start())

    @pl.when(t == n_t - 1)
    def _():
        wait_tile(slot, t)
        if n_t >= 2:
            wait_tile(1 - slot, t - 1)


def _moe_sort(xa, xb, lpos, tab, tt, r_tot):
    n_ta, n_tb = xa.shape[0] // tt, xb.shape[0] // tt
    n_t = n_ta + n_tb
    rows = _local_rows(tt)
    grid_spec = pltpu.PrefetchScalarGridSpec(
        num_scalar_prefetch=5,
        grid=(n_t,),
        in_specs=[pl.BlockSpec((2, tt), lambda t, *_: (0, t)),
                  pl.BlockSpec((tt, D_MODEL), lambda t, *_: (jnp.minimum(t, n_ta - 1), 0)),
                  pl.BlockSpec((tt, D_MODEL), lambda t, *_: (jnp.maximum(t - n_ta, 0), 0))],
        out_specs=pl.BlockSpec(memory_space=pl.ANY),
        scratch_shapes=[pltpu.VMEM((2, rows, D_MODEL), F32), pltpu.VMEM((MOE_TILE, D_MODEL), F32),
                        pltpu.SemaphoreType.DMA((2,)), pltpu.SemaphoreType.DMA((1,))])
    return pl.pallas_call(
        functools.partial(_sort_body, n_t=n_t, n_ta=n_ta),
        out_shape=jax.ShapeDtypeStruct((r_tot, D_MODEL), F32),
        grid_spec=grid_spec,
        compiler_params=_cparams(("arbitrary",)),
        name="moe_sort",
    )(tab["lstart"], tab["gstart"], tab["nchunk"], tab["zstart"], tab["zcount"], lpos, xa, xb)


def _expert_body(te_ref, nu_ref, x_ref, wg_ref, wu_ref, wd_ref, o_ref):
    del te_ref
    used = pl.program_id(0) < nu_ref[0]

    @pl.when(used)
    def _():
        xb = x_ref[...].astype(BF16)
        hid = _gelu(jnp.dot(xb, wg_ref[0], preferred_element_type=F32)) * jnp.dot(xb, wu_ref[0], preferred_element_type=F32)
        o_ref[...] = jnp.dot(hid.astype(BF16), wd_ref[0], preferred_element_type=F32)

    @pl.when(jnp.logical_not(used))
    def _():
        o_ref[...] = jnp.zeros_like(o_ref)


def _moe_experts(xs, tab, wg, wu, wd):
    r_tot = xs.shape[0]
    last = lambda i, nu: jnp.minimum(i, nu[0] - 1)
    grid_spec = pltpu.PrefetchScalarGridSpec(
        num_scalar_prefetch=2,
        grid=(r_tot // MOE_TILE,),
        in_specs=[pl.BlockSpec((MOE_TILE, D_MODEL), lambda i, te, nu: (last(i, nu), 0)),
                  pl.BlockSpec((1, D_MODEL, D_EXPERT), lambda i, te, nu: (te[last(i, nu)], 0, 0)),
                  pl.BlockSpec((1, D_MODEL, D_EXPERT), lambda i, te, nu: (te[last(i, nu)], 0, 0)),
                  pl.BlockSpec((1, D_EXPERT, D_MODEL), lambda i, te, nu: (te[last(i, nu)], 0, 0))],
        out_specs=pl.BlockSpec((MOE_TILE, D_MODEL), lambda i, te, nu: (i, 0)))
    return pl.pallas_call(
        _expert_body,
        out_shape=jax.ShapeDtypeStruct((r_tot, D_MODEL), F32),
        grid_spec=grid_spec,
        compiler_params=_cparams(("arbitrary",)),
        name="moe_experts",
    )(tab["tile_expert"], tab["n_used"], xs, wg, wu, wd)


def _combine_body(lstart_ref, gstart_ref, nchunk_ref, ys_hbm, xa_ref, xb_ref, meta_ref, g_ref, b_ref,
                  oa_ref, ob_ref, yloc, moe_s, sem, *, n_t, n_ta):
    t = pl.program_id(0)
    first = t < n_ta
    slot = lax.rem(t, 2)
    tab = (lstart_ref, gstart_ref, nchunk_ref)
    rows = yloc.shape[1]

    def chunk_copy(s, lrow, grow):
        return pltpu.make_async_copy(ys_hbm.at[pl.ds(grow, MOE_CHUNK)], yloc.at[s, pl.ds(lrow, MOE_CHUNK)], sem.at[s])

    def fetch(s, tile):
        _seg_loop(tab, tile, lambda lrow, grow: chunk_copy(s, lrow, grow).start())

    @pl.when(t == 0)
    def _():
        yloc[...] = jnp.zeros_like(yloc)
        fetch(0, 0)

    @pl.when(t + 1 < n_t)
    def _():
        fetch(1 - slot, t + 1)

    def wbody(j, c):
        chunk_copy(slot, 0, 0).wait()
        return c
    lax.fori_loop(0, _n_chunks(nchunk_ref, t), wbody, 0)

    meta = meta_ref[...]
    tt = meta.shape[0]
    l_iota = lax.broadcasted_iota(jnp.int32, (tt, rows), 1).astype(F32)
    sel = (jnp.where(l_iota == meta[:, 0:1], meta[:, 2:3], 0.0) + jnp.where(l_iota == meta[:, 1:2], meta[:, 3:4], 0.0))
    sel_hi, sel_lo = _split_bf16(sel)
    yb = yloc[slot].astype(BF16)
    moe_s[...] = jnp.dot(sel_hi, yb, preferred_element_type=F32) + jnp.dot(sel_lo, yb, preferred_element_type=F32)

    @pl.when(first)
    def _():
        oa_ref[...] = _layer_norm(DN_ALPHA * xa_ref[...] + moe_s[...], g_ref[...], b_ref[...])

    @pl.when(jnp.logical_not(first))
    def _():
        ob_ref[...] = _layer_norm(DN_ALPHA * xb_ref[...] + moe_s[...], g_ref[...], b_ref[...])


def _moe_combine(ys, xa, xb, meta_n, tab, g, b, tt):
    n_ta, n_tb = xa.shape[0] // tt, xb.shape[0] // tt
    n_t = n_ta + n_tb
    rows = _local_rows(tt)
    vec = lambda: pl.BlockSpec((1, D_MODEL), lambda t, *_: (0, 0))
    in_a = pl.BlockSpec((tt, D_MODEL), lambda t, *_: (jnp.minimum(t, n_ta - 1), 0))
    in_b = pl.BlockSpec((tt, D_MODEL), lambda t, *_: (jnp.maximum(t - n_ta, 0), 0))
    grid_spec = pltpu.PrefetchScalarGridSpec(
        num_scalar_prefetch=3,
        grid=(n_t,),
        in_specs=[pl.BlockSpec(memory_space=pl.ANY), in_a, in_b,
                  pl.BlockSpec((tt, 8), lambda t, *_: (t, 0)), vec(), vec()],
        out_specs=(in_a, in_b),
        scratch_shapes=[pltpu.VMEM((2, rows, D_MODEL), F32), pltpu.VMEM((tt, D_MODEL), F32),
                        pltpu.SemaphoreType.DMA((2,))])
    return pl.pallas_call(
        functools.partial(_combine_body, n_t=n_t, n_ta=n_ta),
        out_shape=(jax.ShapeDtypeStruct(xa.shape, F32), jax.ShapeDtypeStruct(xb.shape, F32)),
        grid_spec=grid_spec,
        compiler_params=_cparams(("arbitrary",)),
        name="moe_combine_ln",
    )(tab["lstart"], tab["gstart"], tab["nchunk"], ys, xa, xb, meta_n, g, b)


def _moe(xa, xb, meta_t, p):
    n = xa.shape[0] + xb.shape[0]
    tt = 256
    while xa.shape[0] % tt or xb.shape[0] % tt:
        tt //= 2
    r_tot, lpos, tab = _dispatch(meta_t, n, tt)
    xs = _moe_sort(xa, xb, lpos, tab, tt, r_tot)
    ys = _moe_experts(xs, tab, p["w_gate"], p["w_up"], p["w_down"])
    meta_n = jnp.concatenate([lpos.astype(F32), meta_t[2:4], jnp.zeros((4, n), F32)], axis=0).T
    return _moe_combine(ys, xa, xb, meta_n, tab, p["ln2_g"], p["ln2_b"], tt)


def _largest_tile(n, cap, mult):
    best = None
    for d in range(mult, min(n, cap) + 1, mult):
        if n % d == 0:
            best = d
    assert best is not None, (n, cap, mult)
    return best


def _seq_tails(x, off, b, t, k, c0, c1):
    if b <= 8:
        return jnp.stack([lax.slice(x, (off + i * t + t - k, c0), (off + (i + 1) * t, c1)) for i in range(b)])
    return lax.slice(x, (off, c0), (off + b * t, c1)).reshape(b, t, c1 - c0)[:, t - k:, :]


def _kv_rows(qkv, g, off, b, t, k):
    def pick(which):
        h0 = (which * N_GROUPS + g) * HEADS
        if b <= 8:
            a = jnp.stack([lax.slice(qkv, (h0, off + i * t + t - k, 0), (h0 + HEADS, off + (i + 1) * t, HEAD_DIM))
                           for i in range(b)], axis=1)
        else:
            a = lax.slice(qkv, (h0, off, 0), (h0 + HEADS, off + b * t, HEAD_DIM)).reshape(HEADS, b, t, HEAD_DIM)[:, :, t - k:]
        return jnp.transpose(a, (1, 2, 0, 3))
    return jnp.stack([pick(1), pick(2)], axis=2)


def _block_diag_gates(w_a, w_x):
    per = RNN_CHUNK // RNN_BLOCK
    chunks = []
    for c in range(N_RNN_CHUNKS):
        halves = []
        for w in (w_a, w_x):
            m = jnp.zeros((RNN_CHUNK, RNN_CHUNK), F32)
            for i in range(per):
                m = lax.dynamic_update_slice(m, w[c * per + i], (i * RNN_BLOCK, i * RNN_BLOCK))
            halves.append(m)
        chunks.append(jnp.concatenate(halves, axis=1))
    return jnp.stack(chunks).astype(BF16)


def kernel(x_prompt, x_sample, cache_kv_w128, cache_kv_w512, cache_kv_w2048, cache_mem_kv, state_h, state_conv, mem_prompt, w_in, b_gates, conv_w, conv_b, w_a, b_a, w_x, b_x, lru_lambda, w_br_lru, w_br_att, w_br_mem, w_o, w_mem_kv, ln1_g, ln1_b, w_rg, b_rg, w_re, b_re, w_gate, w_up, w_down, ln2_g, ln2_b):
    row = lambda v: v.reshape(1, -1).astype(F32)
    w_router = jnp.zeros((ROUTER_ROWS, D_MODEL), F32)
    w_router = w_router.at[:N_EXPERT_GROUPS].set(w_rg.T).at[N_EXPERT_GROUPS:N_EXPERT_GROUPS + N_EXPERTS].set(w_re.T)
    b_router = jnp.zeros((ROUTER_ROWS, 1), F32)
    b_router = b_router.at[:N_EXPERT_GROUPS, 0].set(b_rg).at[N_EXPERT_GROUPS:N_EXPERT_GROUPS + N_EXPERTS, 0].set(b_re)
    p = dict(
        b_gates=row(b_gates), conv_w=conv_w, conv_b=row(conv_b), wax=_block_diag_gates(w_a, w_x),
        b_a=row(b_a), b_x=row(b_x), lam=row(lru_lambda),
        w_br_lru=w_br_lru.astype(BF16), w_br_att=w_br_att.astype(BF16), w_br_mem=w_br_mem.astype(BF16),
        w_o=w_o.astype(BF16), ln1_g=row(ln1_g), ln1_b=row(ln1_b), w_router=w_router, b_router=b_router,
        w_gate=w_gate.astype(BF16), w_up=w_up.astype(BF16), w_down=w_down.astype(BF16),
        ln2_g=row(ln2_g), ln2_b=row(ln2_b))

    bp, s, _ = x_prompt.shape
    bs, ts, _ = x_sample.shape
    n_p, n_s = bp * s, bs * ts
    xp2, xs2 = x_prompt.reshape(n_p, D_MODEL), x_sample.reshape(n_s, D_MODEL)

    tr = 512
    while n_p % tr or n_s % tr:
        tr //= 2
    xb = _cast_rows(xp2, xs2, BF16, tr)
    tm_a = _largest_tile(n_p + n_s, 2304, 16)
    wb = lambda c0, c1: w_in[:, c0:c1].astype(BF16)
    xrg = _matmul(xb, wb(0, COL_Q), 0, 2 * D_RNN, F32, tm_a, 768, "in_proj_rnn")
    qkv = _matmul(xb, wb(COL_Q, COL_QM), 0, N_QKV_HEADS * HEAD_DIM, F32, tm_a, 768, "in_proj_qkv", split=6)
    qm = _matmul(xb, wb(COL_QM, COL_GATES), 0, D_MEM, F32, tm_a, 512, "in_proj_qm", split=2)
    gates = _matmul(xb, wb(COL_GATES, COL_GATES + N_GATE_COLS), 0, N_GATE_COLS, BF16, tm_a, 1024, "gate_proj",
                    bias=p["b_gates"])
    lru = (p["conv_w"], p["conv_b"], p["wax"], p["b_a"], p["b_x"], p["lam"])
    branch_w = (p["w_br_lru"], p["w_br_att"], p["w_br_mem"])
    ln1 = (p["w_o"], p["ln1_g"], p["ln1_b"], p["w_router"], p["b_router"])

    mem_rows = mem_prompt.reshape(bp * N_MEM, D_MODEL)
    mem_kv_p = _matmul(mem_rows, w_mem_kv, 0, 2 * D_MEM, F32, _tile(bp * N_MEM, 512), 512, "mem_kv_proj")
    a_p, h_p = _rglru(xrg, 0, bp, s, jnp.zeros((bp, CONV_W - 1, D_RNN), F32), jnp.zeros((bp, D_RNN), F32), *lru,
                      _tile(s, 256), BF16)
    att_p = _attn_prompt(qkv, bp, s)
    mem_p = _mem_attn(qm, mem_kv_p.reshape(bp, N_MEM, 2 * D_MEM), bp, s, _tile(s, 1024), BF16)
    merged_p = _branch_merge(a_p, att_p, mem_p, gates, 0, *branch_w, _tile(n_p, 256))
    x1_p, meta_p = _proj_ln(merged_p, xp2, *ln1, _tile(n_p, 512))

    a_s, h_s = _rglru(xrg, n_p, bs, ts, state_conv, state_h, *lru, ts, F32)
    att_s = _attn_sample(qkv, (cache_kv_w128, cache_kv_w512, cache_kv_w2048), bs, ts, n_p)
    mem_s = _mem_attn_cache(qm, cache_mem_kv, bs, ts, n_p)
    tm_s = _tile(n_s, 256)
    assert n_p % tm_s == 0
    merged_s = _branch_merge(a_s, att_s, mem_s, gates, n_p, *branch_w, tm_s)
    x1_s, meta_s = _proj_ln(merged_s, xs2, *ln1, _tile(n_s, 512))

    y_p, y_s = _moe(x1_p, x1_s, jnp.concatenate([meta_p, meta_s], axis=1), p)

    kc = CONV_W - 1
    conv_p = jnp.concatenate([jnp.zeros((bp, kc, D_RNN), F32), _seq_tails(xrg, 0, bp, s, min(kc, s), 0, D_RNN)], axis=1)[:, -kc:]
    conv_s = jnp.concatenate([state_conv, _seq_tails(xrg, n_p, bs, ts, min(kc, ts), 0, D_RNN)], axis=1)[:, -kc:]
    kv_p = [_kv_rows(qkv, g, 0, bp, s, min(KEYS * d, s)) for g, d in enumerate(DILATIONS)]
    kv_s = [_kv_rows(qkv, g, n_p, bs, ts, ts) for g in range(N_GROUPS)]
    return (y_p.reshape(bp, s, D_MODEL), y_s.reshape(bs, ts, D_MODEL), kv_p[0], kv_p[1], kv_p[2],
            mem_kv_p.reshape(bp, N_MEM, 2, MEM_HEADS, MEM_HEAD_DIM), h_p.reshape(bp, D_RNN), conv_p,
            kv_s[0], kv_s[1], kv_s[2], h_s.reshape(bs, D_RNN), conv_s)
```

```python
import functools

import jax
import jax.numpy as jnp
from jax import lax
from jax.experimental import pallas as pl
from jax.experimental.pallas import tpu as pltpu

F32 = jnp.float32
BF16 = jnp.bfloat16

D_MODEL = 2048
D_RNN = 1536
N_RNN_BLOCKS = 16
RNN_BLOCK = D_RNN // N_RNN_BLOCKS
RNN_CHUNK = 384
N_RNN_CHUNKS = D_RNN // RNN_CHUNK
CONV_W = 4
LRU_C = 8.0
HEAD_DIM = 128
HEADS = 4
DILATIONS = (1, 4, 16)
KEYS = 128
N_GROUPS = 3
D_ATT_OUT = HEADS * HEAD_DIM
N_QKV_HEADS = 3 * N_GROUPS * HEADS
ATT_SCALE = HEAD_DIM ** -0.5
ATT_TILE = KEYS * max(DILATIONS)
N_MEM = 256
MEM_HEADS = 4
MEM_HEAD_DIM = 256
D_MEM = MEM_HEADS * MEM_HEAD_DIM
MEM_SCALE = MEM_HEAD_DIM ** -0.5
N_EXPERT_GROUPS = 4
EXPERTS_PER_GROUP = 4
N_EXPERTS = 16
D_EXPERT = 512
MOE_TILE = 512
DN_ALPHA = 2.0 ** 0.25
LN_EPS = 1e-5
NEG_INF = -1e30

COL_Q = 2 * D_RNN
COL_QM = COL_Q + N_QKV_HEADS * HEAD_DIM
COL_GATES = COL_QM + D_MEM
N_GATE_COLS = 3 * D_MODEL
ROUTER_ROWS = 32

VMEM_LIMIT = 56 * 1024 * 1024


def _cparams(sem):
    return pltpu.CompilerParams(dimension_semantics=sem, vmem_limit_bytes=VMEM_LIMIT)


def _gelu(x):
    return 0.5 * x * (1.0 + jnp.tanh(0.7978845608028654 * (x + 0.044715 * (x * x * x))))


def _layer_norm(x, g, b):
    mu = jnp.mean(x, axis=-1, keepdims=True)
    xc = x - mu
    var = jnp.mean(xc * xc, axis=-1, keepdims=True)
    return xc * lax.rsqrt(var + LN_EPS) * g + b


def _dot_nt(a, b):
    return lax.dot_general(a, b, (((1,), (1,)), ((), ())), preferred_element_type=F32)


def _tile(m, cap):
    t = min(m, cap)
    assert m % t == 0, (m, t)
    return t


def _mm_body(x_ref, w_ref, *rest, split, gate):
    acc = jnp.dot(x_ref[...].astype(BF16), w_ref[...].astype(BF16), preferred_element_type=F32)
    if gate:
        b_ref, o_ref = rest
        acc = 0.5 * jnp.tanh(0.5 * (acc + b_ref[...])) + 0.5
    else:
        (o_ref,) = rest
    if split == 1:
        o_ref[...] = acc.astype(o_ref.dtype)
    else:
        w = acc.shape[1] // split
        for s in range(split):
            o_ref[s] = acc[:, s * w:(s + 1) * w].astype(o_ref.dtype)


def _matmul(x, w, col_off, n_cols, out_dtype, tm, tn, name, bias=None, split=1):
    m, k = x.shape
    cb = col_off // tn
    in_specs = [pl.BlockSpec((tm, k), lambda i, j: (i, 0)),
                pl.BlockSpec((k, tn), lambda i, j: (0, j + cb))]
    args = [x, w]
    if bias is not None:
        in_specs.append(pl.BlockSpec((1, tn), lambda i, j: (0, j)))
        args.append(bias)
    if split == 1:
        out_shape = jax.ShapeDtypeStruct((m, n_cols), out_dtype)
        out_spec = pl.BlockSpec((tm, tn), lambda i, j: (i, j))
    else:
        out_shape = jax.ShapeDtypeStruct((n_cols * split // tn, m, tn // split), out_dtype)
        out_spec = pl.BlockSpec((split, tm, tn // split), lambda i, j: (j, i, 0))
    return pl.pallas_call(
        functools.partial(_mm_body, split=split, gate=bias is not None),
        out_shape=out_shape,
        grid=(m // tm, n_cols // tn),
        in_specs=in_specs,
        out_specs=out_spec,
        compiler_params=_cparams(("parallel", "arbitrary")),
        name=name,
    )(*args)


def _cast_rows_body(a_ref, b_ref, o_ref, *, n_ta):
    o_ref[...] = jnp.where(pl.program_id(0) < n_ta, a_ref[...], b_ref[...]).astype(o_ref.dtype)


def _cast_rows(xa, xb, dtype, tr):
    n_ta, n_tb = xa.shape[0] // tr, xb.shape[0] // tr
    cols = xa.shape[1]
    return pl.pallas_call(
        functools.partial(_cast_rows_body, n_ta=n_ta),
        out_shape=jax.ShapeDtypeStruct((xa.shape[0] + xb.shape[0], cols), dtype),
        grid=(n_ta + n_tb,),
        in_specs=[pl.BlockSpec((tr, cols), lambda i: (jnp.minimum(i, n_ta - 1), 0)),
                  pl.BlockSpec((tr, cols), lambda i: (jnp.maximum(i - n_ta, 0), 0))],
        out_specs=pl.BlockSpec((tr, cols), lambda i: (i, 0)),
        compiler_params=_cparams(("parallel",)),
        name="cast_rows",
    )(xa, xb)


def _rglru_body(xr_ref, cbuf_ref, h0_ref, cw_ref, cb_ref, wax_ref, ba_ref, bx_ref, lam_ref,
                out_ref, hl_ref, ext_s, a_s, u_s, h_s, *, tt):
    t = pl.program_id(1)

    @pl.when(t == 0)
    def _():
        ext_s[0:8, :] = jnp.zeros((8, D_RNN), F32)
        ext_s[5:8, :] = cbuf_ref[0]
        h_s[...] = jnp.broadcast_to(h0_ref[0], (8, D_RNN))

    @pl.when(t > 0)
    def _():
        ext_s[0:8, :] = ext_s[tt:tt + 8, :]

    ext_s[8:8 + tt, :] = xr_ref[...]
    cw = cw_ref[...]
    xc = (cb_ref[...] + cw[3:4, :] * ext_s[8:8 + tt, :] + cw[2:3, :] * ext_s[7:7 + tt, :]
          + cw[1:2, :] * ext_s[6:6 + tt, :] + cw[0:1, :] * ext_s[5:5 + tt, :])
    xcb = xc.astype(BF16)
    r_parts, i_parts = [], []
    for c in range(N_RNN_CHUNKS):
        g = jnp.dot(xcb[:, c * RNN_CHUNK:(c + 1) * RNN_CHUNK], wax_ref[c], preferred_element_type=F32)
        r_parts.append(g[:, :RNN_CHUNK])
        i_parts.append(g[:, RNN_CHUNK:])
    sigmoid = lambda z: 0.5 * jnp.tanh(0.5 * z) + 0.5
    r = sigmoid(jnp.concatenate(r_parts, axis=1) + ba_ref[...])
    gi = sigmoid(jnp.concatenate(i_parts, axis=1) + bx_ref[...])
    nl = -lam_ref[...]
    softplus = jnp.maximum(nl, 0.0) + jnp.log1p(jnp.exp(-jnp.abs(nl)))
    log_a = (-LRU_C) * r * softplus
    th = jnp.tanh(log_a)
    a_s[...] = jnp.exp(log_a)
    u_s[...] = jnp.sqrt(-2.0 * th / (1.0 - th)) * (gi * xc)

    rows = lax.broadcasted_iota(jnp.int32, (8, D_RNN), 0)

    def blk(i, h):
        r0 = pl.multiple_of(i * 8, 8)
        ab = a_s[pl.ds(r0, 8), :]
        ub = u_s[pl.ds(r0, 8), :]
        for s in (1, 2, 4):
            keep = rows >= s
            ub = ab * jnp.where(keep, pltpu.roll(ub, s, 0), 0.0) + ub
            ab = ab * jnp.where(keep, pltpu.roll(ab, s, 0), 1.0)
        hb = ab * h + ub
        u_s[pl.ds(r0, 8), :] = hb
        return jnp.broadcast_to(hb[7:8, :], (8, D_RNN))

    h_fin = lax.fori_loop(0, tt // 8, blk, h_s[...])
    h_s[...] = h_fin
    hl_ref[0] = h_fin[0:1, :]
    out_ref[...] = u_s[...]


def _rglru(xrg, off, b, t, conv_buf, h0, conv_w, conv_b, wax, b_a, b_x, lam, tt):
    nt = t // tt
    ob = off // tt
    vec = lambda: pl.BlockSpec((1, D_RNN), lambda i, j: (0, 0))
    return pl.pallas_call(
        functools.partial(_rglru_body, tt=tt),
        out_shape=(jax.ShapeDtypeStruct((b * t, D_RNN), F32), jax.ShapeDtypeStruct((b, 1, D_RNN), F32)),
        grid=(b, nt),
        in_specs=[pl.BlockSpec((tt, D_RNN), lambda i, j: (ob + i * nt + j, 0)),
                  pl.BlockSpec((1, CONV_W - 1, D_RNN), lambda i, j: (i, 0, 0)),
                  pl.BlockSpec((1, 1, D_RNN), lambda i, j: (i, 0, 0)),
                  pl.BlockSpec((CONV_W, D_RNN), lambda i, j: (0, 0)),
                  vec(),
                  pl.BlockSpec((N_RNN_CHUNKS, RNN_CHUNK, 2 * RNN_CHUNK), lambda i, j: (0, 0, 0)),
                  vec(), vec(), vec()],
        out_specs=(pl.BlockSpec((tt, D_RNN), lambda i, j: (i * nt + j, 0)),
                   pl.BlockSpec((1, 1, D_RNN), lambda i, j: (i, 0, 0))),
        scratch_shapes=[pltpu.VMEM((tt + 8, D_RNN), F32), pltpu.VMEM((tt, D_RNN), F32),
                        pltpu.VMEM((tt, D_RNN), F32), pltpu.VMEM((8, D_RNN), F32)],
        compiler_params=_cparams(("parallel", "arbitrary")),
        name="rglru",
    )(xrg, conv_buf, h0.reshape(b, 1, D_RNN), conv_w, conv_b, wax, b_a, b_x, lam)


def _band_block(q, k, v, bias):
    s = _dot_nt(q, k) * ATT_SCALE + bias
    m = jnp.max(s, axis=-1, keepdims=True)
    p = jnp.exp(s - m)
    l = jnp.sum(p, axis=-1, keepdims=True)
    o = jnp.dot(p.astype(BF16), v, preferred_element_type=F32) / l
    return o, m + jnp.log(l)


def _attn_prompt_body(*refs):
    q_refs, k_refs, v_refs, kh_refs, vh_refs = (refs[3 * i:3 * i + 3] for i in range(5))
    o_ref, og_s, lg_s = refs[15:]
    row = lax.broadcasted_iota(jnp.int32, (KEYS, 2 * KEYS), 0)
    col = lax.broadcasted_iota(jnp.int32, (KEYS, 2 * KEYS), 1)
    band = jnp.logical_and(col >= row, col <= row + KEYS)
    bias = jnp.where(band, 0.0, NEG_INF)
    has_prev = pl.program_id(1) > 0
    bias_first = jnp.where(jnp.logical_and(band, jnp.logical_or(col >= KEYS, has_prev)), 0.0, NEG_INF)

    def put(g, rows, o, lse):
        og_s[g, rows, :] = o
        lg_s[g, rows, :] = jnp.broadcast_to(lse, (KEYS, HEAD_DIM))

    def rows_of(start, size, d):
        return pl.ds(start, size, stride=d) if d > 1 else pl.ds(start, size)

    for g, d in enumerate(DILATIONS):
        q_ref, k_ref, v_ref, kh_ref, vh_ref = q_refs[g], k_refs[g], v_refs[g], kh_refs[g], vh_refs[g]
        nqb = ATT_TILE // (d * KEYS)
        for r in range(d):
            own = rows_of(r, KEYS, d)
            kk = jnp.concatenate([kh_ref[0, own, :], k_ref[0, own, :]], axis=0).astype(BF16)
            vv = jnp.concatenate([vh_ref[0, own, :], v_ref[0, own, :]], axis=0).astype(BF16)
            o, lse = _band_block(q_ref[0, own, :].astype(BF16), kk, vv, bias_first)
            put(g, own, o, lse)
            for qb in range(1, nqb):
                rows = rows_of(qb * KEYS * d + r, KEYS, d)
                keys = rows_of((qb - 1) * KEYS * d + r, 2 * KEYS, d)
                o, lse = _band_block(q_ref[0, rows, :].astype(BF16), k_ref[0, keys, :].astype(BF16),
                                     v_ref[0, keys, :].astype(BF16), bias)
                put(g, rows, o, lse)

    l0, l1, l2 = lg_s[0], lg_s[1], lg_s[2]
    m = jnp.maximum(jnp.maximum(l0, l1), l2)
    e0, e1, e2 = jnp.exp(l0 - m), jnp.exp(l1 - m), jnp.exp(l2 - m)
    o_ref[...] = ((e0 * og_s[0] + e1 * og_s[1] + e2 * og_s[2]) / (e0 + e1 + e2)).astype(o_ref.dtype)


def _attn_prompt(qkv, b, s):
    assert s % ATT_TILE == 0
    nt = s // ATT_TILE
    n = b * s

    def cur(which, g):
        return pl.BlockSpec((1, ATT_TILE, HEAD_DIM),
                            lambda i, j, h: ((which * N_GROUPS + g) * HEADS + h, i * nt + j, 0))

    def halo(which, g):
        rows = KEYS * DILATIONS[g]
        per = ATT_TILE // rows
        return pl.BlockSpec((1, rows, HEAD_DIM),
                            lambda i, j, h: ((which * N_GROUPS + g) * HEADS + h,
                                             jnp.maximum((i * nt + j) * per - 1, 0), 0))

    in_specs = ([cur(0, g) for g in range(N_GROUPS)] + [cur(1, g) for g in range(N_GROUPS)]
                + [cur(2, g) for g in range(N_GROUPS)] + [halo(1, g) for g in range(N_GROUPS)]
                + [halo(2, g) for g in range(N_GROUPS)])
    return pl.pallas_call(
        _attn_prompt_body,
        out_shape=jax.ShapeDtypeStruct((n, D_ATT_OUT), BF16),
        grid=(b, nt, HEADS),
        in_specs=in_specs,
        out_specs=pl.BlockSpec((ATT_TILE, HEAD_DIM), lambda i, j, h: (i * nt + j, h)),
        scratch_shapes=[pltpu.VMEM((N_GROUPS, ATT_TILE, HEAD_DIM), F32),
                        pltpu.VMEM((N_GROUPS, ATT_TILE, HEAD_DIM), F32)],
        compiler_params=_cparams(("parallel", "parallel", "parallel")),
        name="attn_prompt",
    )(*([qkv] * 15))


def _attn_sample_group(q4, kn4, vn4, c_ref, d, t_new):
    half = len(c_ref.shape) == 4
    wb = c_ref.shape[1] * 16 if half else c_ref.shape[1] // (2 * HEADS)
    nk = wb // 2 if half else wb
    nr = HEADS * t_new
    zeros = jnp.zeros((t_new, HEAD_DIM), F32)
    qbd = jnp.concatenate(
        [jnp.concatenate([q4[h] if hh == h else zeros for hh in range(HEADS)], axis=1) for h in range(HEADS)],
        axis=0).astype(BF16)
    kn = jnp.concatenate([kn4[h] for h in range(HEADS)], axis=1).astype(BF16)
    vn = jnp.concatenate([vn4[h] for h in range(HEADS)], axis=1).astype(BF16)
    if half:
        cache_rows = lambda kv, h: c_ref[0, :, pl.ds(kv * HEADS + h, 8, stride=2 * HEADS), :].reshape(nk, HEAD_DIM)
    else:
        cache_rows = lambda kv, h: c_ref[0, pl.ds(kv * HEADS + h, wb, stride=2 * HEADS), :]
    kc = jnp.concatenate([cache_rows(0, h) for h in range(HEADS)], axis=1).astype(BF16)
    vc = jnp.concatenate([cache_rows(1, h) for h in range(HEADS)], axis=1).astype(BF16)
    tq_c = lax.broadcasted_iota(jnp.int32, (nr, nk), 0) & (t_new - 1)
    e_c = lax.broadcasted_iota(jnp.int32, (nr, nk), 1)
    if half:
        e_c = ((e_c >> 3) << 4) + (e_c & 7)
    dist_c = wb + tq_c - e_c
    ok_c = jnp.logical_and((dist_c & (d - 1)) == 0, dist_c <= KEYS * d)
    tq_n = lax.broadcasted_iota(jnp.int32, (nr, t_new), 0) & (t_new - 1)
    dist_n = tq_n - lax.broadcasted_iota(jnp.int32, (nr, t_new), 1)
    ok_n = jnp.logical_and(jnp.logical_and(dist_n >= 0, (dist_n & (d - 1)) == 0), dist_n <= KEYS * d)
    s_c = jnp.where(ok_c, _dot_nt(qbd, kc) * ATT_SCALE, NEG_INF)
    s_n = jnp.where(ok_n, _dot_nt(qbd, kn) * ATT_SCALE, NEG_INF)
    m = jnp.maximum(jnp.max(s_c, axis=-1, keepdims=True), jnp.max(s_n, axis=-1, keepdims=True))
    p_c = jnp.exp(s_c - m)
    p_n = jnp.exp(s_n - m)
    l = jnp.sum(p_c, axis=-1, keepdims=True) + jnp.sum(p_n, axis=-1, keepdims=True)
    o = (jnp.dot(p_c.astype(BF16), vc, preferred_element_type=F32)
         + jnp.dot(p_n.astype(BF16), vn, preferred_element_type=F32)) / l
    lse = m + jnp.log(l)
    o = jnp.concatenate(
        [o[h * t_new:(h + 1) * t_new, h * HEAD_DIM:(h + 1) * HEAD_DIM] for h in range(HEADS)], axis=1)
    lse = jnp.concatenate(
        [jnp.broadcast_to(lse[h * t_new:(h + 1) * t_new], (t_new, HEAD_DIM)) for h in range(HEADS)], axis=1)
    return o, lse


def _attn_sample_body(*refs, t_new):
    q_refs, k_refs, v_refs, c_refs = (refs[3 * i:3 * i + 3] for i in range(4))
    o_ref = refs[12]
    outs = [_attn_sample_group(q_refs[g][...], k_refs[g][...], v_refs[g][...], c_refs[g], d, t_new)
            for g, d in enumerate(DILATIONS)]
    (o0, l0), (o1, l1), (o2, l2) = outs
    m = jnp.maximum(jnp.maximum(l0, l1), l2)
    e0, e1, e2 = jnp.exp(l0 - m), jnp.exp(l1 - m), jnp.exp(l2 - m)
    o_ref[0] = (e0 * o0 + e1 * o1 + e2 * o2) / (e0 + e1 + e2)


def _attn_sample(qkv, caches, b, t_new, off):
    assert t_new & (t_new - 1) == 0 and off % t_new == 0
    ob = off // t_new
    new = lambda which, g: pl.BlockSpec((HEADS, t_new, HEAD_DIM), lambda i: (which * N_GROUPS + g, ob + i, 0))
    caches2, cache_specs = [], []
    for c, d in zip(caches, DILATIONS):
        wb = c.shape[1]
        if d == 16 and wb % 16 == 0 and t_new <= 8:
            caches2.append(c.reshape(b, wb // 16, 16 * 2 * HEADS, HEAD_DIM))
            cache_specs.append(pl.BlockSpec((1, wb // 16, 8 * 2 * HEADS, HEAD_DIM), lambda i: (i, 0, 0, 0)))
        else:
            caches2.append(c.reshape(b, wb * 2 * HEADS, HEAD_DIM))
            cache_specs.append(pl.BlockSpec((1, wb * 2 * HEADS, HEAD_DIM), lambda i: (i, 0, 0)))
    in_specs = ([new(0, g) for g in range(N_GROUPS)] + [new(1, g) for g in range(N_GROUPS)]
                + [new(2, g) for g in range(N_GROUPS)] + cache_specs)
    att = pl.pallas_call(
        functools.partial(_attn_sample_body, t_new=t_new),
        out_shape=jax.ShapeDtypeStruct((b, t_new, D_ATT_OUT), F32),
        grid=(b,),
        in_specs=in_specs,
        out_specs=pl.BlockSpec((1, t_new, D_ATT_OUT), lambda i: (i, 0, 0)),
        compiler_params=_cparams(("parallel",)),
        name="attn_sample",
    )(*([qkv] * 9), *caches2)
    return att.reshape(b * t_new, D_ATT_OUT)


def _mem_attn_body(q_ref, kv_ref, o_ref):
    for h in range(MEM_HEADS):
        k = kv_ref[0, :, h * MEM_HEAD_DIM:(h + 1) * MEM_HEAD_DIM].astype(BF16)
        v = kv_ref[0, :, D_MEM + h * MEM_HEAD_DIM:D_MEM + (h + 1) * MEM_HEAD_DIM].astype(BF16)
        s = _dot_nt(q_ref[h].astype(BF16), k) * MEM_SCALE
        m = jnp.max(s, axis=-1, keepdims=True)
        p = jnp.exp(s - m)
        l = jnp.sum(p, axis=-1, keepdims=True)
        o = jnp.dot(p.astype(BF16), v, preferred_element_type=F32) / l
        o_ref[:, h * MEM_HEAD_DIM:(h + 1) * MEM_HEAD_DIM] = o.astype(o_ref.dtype)


def _mem_attn(qm, mem_kv, b, t, tm, out_dtype):
    nt = t // tm
    return pl.pallas_call(
        _mem_attn_body,
        out_shape=jax.ShapeDtypeStruct((b * t, D_MEM), out_dtype),
        grid=(b, nt),
        in_specs=[pl.BlockSpec((MEM_HEADS, tm, MEM_HEAD_DIM), lambda i, j: (0, i * nt + j, 0)),
                  pl.BlockSpec((1, N_MEM, 2 * D_MEM), lambda i, j: (i, 0, 0))],
        out_specs=pl.BlockSpec((tm, D_MEM), lambda i, j: (i * nt + j, 0)),
        compiler_params=_cparams(("parallel", "parallel")),
        name="mem_attn",
    )(qm, mem_kv)


def _mem_attn_cache_body(q_ref, c_ref, o_ref):
    t = q_ref.shape[1]
    n_rows = N_MEM * 2 * MEM_HEADS
    flat = c_ref[0].reshape(n_rows, MEM_HEAD_DIM).astype(BF16)
    q = jnp.concatenate([q_ref[h] for h in range(MEM_HEADS)], axis=0).astype(BF16)
    assert t & (t - 1) == 0
    head = lax.broadcasted_iota(jnp.int32, (MEM_HEADS * t, n_rows), 0) >> (t.bit_length() - 1)
    col = lax.broadcasted_iota(jnp.int32, (MEM_HEADS * t, n_rows), 1)
    s = jnp.where((col & (2 * MEM_HEADS - 1)) == head, _dot_nt(q, flat) * MEM_SCALE, NEG_INF)
    m = jnp.max(s, axis=-1, keepdims=True)
    p = jnp.exp(s - m)
    l = jnp.sum(p, axis=-1, keepdims=True)
    o = jnp.dot(pltpu.roll(p, MEM_HEADS, 1).astype(BF16), flat, preferred_element_type=F32) / l
    o_ref[...] = jnp.concatenate([o[h * t:(h + 1) * t, :] for h in range(MEM_HEADS)], axis=1)


def _mem_attn_cache(qm, cache, b, t, off):
    ob = off // t
    return pl.pallas_call(
        _mem_attn_cache_body,
        out_shape=jax.ShapeDtypeStruct((b * t, D_MEM), F32),
        grid=(b,),
        in_specs=[pl.BlockSpec((MEM_HEADS, t, MEM_HEAD_DIM), lambda i: (0, ob + i, 0)),
                  pl.BlockSpec((1, N_MEM, 2, MEM_HEADS, MEM_HEAD_DIM), lambda i: (i, 0, 0, 0, 0))],
        out_specs=pl.BlockSpec((t, D_MEM), lambda i: (i, 0)),
        compiler_params=_cparams(("parallel",)),
        name="mem_attn_cache",
    )(qm, cache)


def _branch_body(hs_ref, xg_ref, att_ref, mem_ref, gt_ref, wl_ref, wa_ref, wm_ref, out_ref):
    a_lru = (hs_ref[...] * _gelu(xg_ref[...])).astype(BF16)
    acc = gt_ref[:, 0:D_MODEL].astype(F32) * jnp.dot(a_lru, wl_ref[...], preferred_element_type=F32)
    acc = acc + gt_ref[:, D_MODEL:2 * D_MODEL].astype(F32) * jnp.dot(att_ref[...].astype(BF16), wa_ref[...], preferred_element_type=F32)
    acc = acc + gt_ref[:, 2 * D_MODEL:3 * D_MODEL].astype(F32) * jnp.dot(mem_ref[...].astype(BF16), wm_ref[...], preferred_element_type=F32)
    out_ref[...] = acc.astype(out_ref.dtype)


def _branch_merge(hs, xrg, att, mem, gates, goff, wl, wa, wm, tm):
    n = hs.shape[0]
    gb = goff // tm
    row = lambda w: pl.BlockSpec((tm, w), lambda i: (i, 0))
    full = lambda a: pl.BlockSpec(a.shape, lambda i: (0, 0))
    return pl.pallas_call(
        _branch_body,
        out_shape=jax.ShapeDtypeStruct((n, D_MODEL), BF16),
        grid=(n // tm,),
        in_specs=[row(D_RNN), pl.BlockSpec((tm, D_RNN), lambda i: (gb + i, 1)), row(D_ATT_OUT), row(D_MEM),
                  pl.BlockSpec((tm, N_GATE_COLS), lambda i: (gb + i, 0)), full(wl), full(wa), full(wm)],
        out_specs=row(D_MODEL),
        compiler_params=_cparams(("parallel",)),
        name="branch_merge",
    )(hs, xrg, att, mem, gates, wl, wa, wm)


def _split_bf16(x):
    hi = x.astype(BF16)
    return hi, (x - hi.astype(F32)).astype(BF16)


def _route_rows(lg):
    g = [lg[i:i + 1, :] for i in range(N_EXPERT_GROUPS)]
    gmax = jnp.maximum(jnp.maximum(g[0], g[1]), jnp.maximum(g[2], g[3]))
    gidx = jnp.where(g[0] == gmax, 0.0, jnp.where(g[1] == gmax, 1.0, jnp.where(g[2] == gmax, 2.0, 3.0)))
    g_p = 1.0 / (jnp.exp(g[0] - gmax) + jnp.exp(g[1] - gmax) + jnp.exp(g[2] - gmax) + jnp.exp(g[3] - gmax))
    e = []
    for k in range(EXPERTS_PER_GROUP):
        rows = [lg[N_EXPERT_GROUPS + gg * EXPERTS_PER_GROUP + k:N_EXPERT_GROUPS + gg * EXPERTS_PER_GROUP + k + 1, :]
                for gg in range(N_EXPERT_GROUPS)]
        e.append(jnp.where(gidx == 0.0, rows[0], jnp.where(gidx == 1.0, rows[1], jnp.where(gidx == 2.0, rows[2], rows[3]))))

    def first_argmax(v):
        mx = jnp.maximum(jnp.maximum(v[0], v[1]), jnp.maximum(v[2], v[3]))
        ix = jnp.where(v[0] == mx, 0.0, jnp.where(v[1] == mx, 1.0, jnp.where(v[2] == mx, 2.0, 3.0)))
        return mx, ix

    v1, i1 = first_argmax(e)
    v2, i2 = first_argmax([jnp.where(i1 == float(k), -jnp.inf, e[k]) for k in range(EXPERTS_PER_GROUP)])
    ex = jnp.exp(v2 - v1)
    w1 = g_p / (1.0 + ex)
    w2 = g_p * ex / (1.0 + ex)
    base = gidx * float(EXPERTS_PER_GROUP)
    zero = jnp.zeros_like(w1)
    return jnp.concatenate([base + i1, base + i2, w1, w2, zero, zero, zero, zero], axis=0)


def _proj_ln_body(mg_ref, x_ref, wo_ref, g_ref, b_ref, wr_ref, br_ref, x1_ref, meta_ref, mix_s):
    @pl.when(pl.program_id(0) == 0)
    def _():
        mix_s[...] = jnp.zeros(mix_s.shape, F32)

    wh, wl = _split_bf16(wr_ref[...])
    tm = x_ref.shape[0]
    sub = min(256, tm)
    for r0 in range(0, tm, sub):
        rs = pl.ds(r0, sub)
        x1 = _layer_norm(DN_ALPHA * x_ref[rs, :] + mix_s[rs, :], g_ref[...], b_ref[...])
        x1_ref[rs, :] = x1
        xh, xl = _split_bf16(x1)
        lg = _dot_nt(wh, xh) + (_dot_nt(wh, xl) + _dot_nt(wl, xh)) + br_ref[...]
        meta_ref[:, rs] = _route_rows(lg)
    mix_s[...] = jnp.dot(mg_ref[...], wo_ref[...], preferred_element_type=F32)


def _proj_ln(merged, x, wo, g, b, wr, br, tm):
    n = merged.shape[0]
    nb = n // tm
    prev = lambda w: pl.BlockSpec((tm, w), lambda i: (jnp.maximum(i - 1, 0), 0))
    full = lambda a: pl.BlockSpec(a.shape, lambda i: (0, 0))
    return pl.pallas_call(
        _proj_ln_body,
        out_shape=(jax.ShapeDtypeStruct((n, D_MODEL), F32), jax.ShapeDtypeStruct((8, n), F32)),
        grid=(nb + 1,),
        in_specs=[pl.BlockSpec((tm, D_MODEL), lambda i: (jnp.minimum(i, nb - 1), 0)), prev(D_MODEL),
                  full(wo), full(g), full(b), full(wr), full(br)],
        out_specs=(prev(D_MODEL), pl.BlockSpec((8, tm), lambda i: (0, jnp.maximum(i - 1, 0)))),
        scratch_shapes=[pltpu.VMEM((tm, D_MODEL), F32)],
        compiler_params=_cparams(("arbitrary",)),
        name="proj_ln_router",
    )(merged, x, wo, g, b, wr, br)


MOE_CHUNK = 8


def _local_rows(tt):
    return -(-(2 * tt + N_EXPERTS * (MOE_CHUNK - 1)) // 128) * 128


def _dispatch(meta_t, n, tt):
    n_t = n // tt
    ids = meta_t[0:2].astype(jnp.int32)
    onehot = (ids[:, :, None] == jnp.arange(N_EXPERTS, dtype=jnp.int32)).astype(jnp.int32).reshape(2, n_t, tt, N_EXPERTS)
    cnt_slot = jnp.sum(onehot, axis=2)
    cnt = cnt_slot[0] + cnt_slot[1]
    pc = (cnt + MOE_CHUNK - 1) // MOE_CHUNK * MOE_CHUNK
    lstart = jnp.cumsum(pc, axis=1) - pc
    tri = (jnp.arange(tt)[:, None] >= jnp.arange(tt)[None, :]).astype(F32)
    csum = jnp.einsum("ut,snte->snue", tri, onehot.astype(F32)).astype(jnp.int32)
    rank = csum - onehot + jnp.stack([jnp.zeros_like(cnt), cnt_slot[0]])[:, :, None, :]
    lpos = jnp.sum(onehot * (lstart[None, :, None, :] + rank), axis=-1).reshape(2, n)
    seg = jnp.sum(pc, axis=0)
    pe = (seg + MOE_TILE - 1) // MOE_TILE * MOE_TILE
    ends = jnp.cumsum(pe)
    base = ends - pe
    gstart = base[None, :] + jnp.cumsum(pc, axis=0) - pc
    r_tot = -(-(2 * n + N_EXPERTS * (MOE_CHUNK - 1) * n_t) // MOE_TILE) * MOE_TILE + N_EXPERTS * MOE_TILE
    n_tiles = r_tot // MOE_TILE
    tile_start = jnp.arange(n_tiles, dtype=jnp.int32) * MOE_TILE
    tile_expert = jnp.minimum(jnp.sum((tile_start[:, None] >= ends[None, :]).astype(jnp.int32), axis=1), N_EXPERTS - 1)
    tables = dict(
        lstart=lstart.reshape(-1), gstart=gstart.reshape(-1), nchunk=(pc // MOE_CHUNK).reshape(-1),
        zstart=jnp.concatenate([base + seg, ends[-1:]]),
        zcount=jnp.concatenate([(pe - seg) // MOE_CHUNK, (r_tot - ends[-1:]) // MOE_TILE]),
        tile_expert=tile_expert, n_used=(ends[-1] // MOE_TILE).reshape(1))
    return r_tot, lpos, tables


def _seg_loop(tab, t, fn):
    lstart_ref, gstart_ref, nchunk_ref = tab
    for e in range(N_EXPERTS):
        ls = lstart_ref[t * N_EXPERTS + e]
        gs = gstart_ref[t * N_EXPERTS + e]

        def body(j, c, ls=ls, gs=gs):
            fn(pl.multiple_of(ls + j * MOE_CHUNK, MOE_CHUNK), pl.multiple_of(gs + j * MOE_CHUNK, MOE_CHUNK))
            return c
        lax.fori_loop(0, nchunk_ref[t * N_EXPERTS + e], body, 0)


def _n_chunks(nchunk_ref, t):
    tot = nchunk_ref[t * N_EXPERTS]
    for e in range(1, N_EXPERTS):
        tot = tot + nchunk_ref[t * N_EXPERTS + e]
    return tot


def _sort_body(lstart_ref, gstart_ref, nchunk_ref, zstart_ref, zcount_ref, lpos_ref, xa_ref, xb_ref, xs_hbm,
               xloc, zbuf, sem, zsem, *, n_t, n_ta):
    t = pl.program_id(0)
    x_tile = jnp.where(t < n_ta, xa_ref[...], xb_ref[...]).astype(BF16)
    slot = lax.rem(t, 2)
    tab = (lstart_ref, gstart_ref, nchunk_ref)
    rows = xloc.shape[1]

    def chunk_copy(s, lrow, grow):
        return pltpu.make_async_copy(xloc.at[s, pl.ds(lrow, MOE_CHUNK)], xs_hbm.at[pl.ds(grow, MOE_CHUNK)], sem.at[s])

    def wait_tile(s, tile):
        def body(j, c):
            chunk_copy(s, 0, 0).wait()
            return c
        lax.fori_loop(0, _n_chunks(nchunk_ref, tile), body, 0)

    @pl.when(t == 0)
    def _():
        zbuf[...] = jnp.zeros_like(zbuf)
        zero_copy = lambda grow: pltpu.make_async_copy(
            zbuf.at[pl.ds(0, MOE_CHUNK)], xs_hbm.at[pl.ds(grow, MOE_CHUNK)], zsem.at[0])
        zero_tile = lambda grow: pltpu.make_async_copy(zbuf, xs_hbm.at[pl.ds(grow, MOE_TILE)], zsem.at[0])
        tail_start = zstart_ref[N_EXPERTS]
        for e in range(N_EXPERTS):
            def zb(j, c, e=e):
                zero_copy(pl.multiple_of(zstart_ref[e] + j * MOE_CHUNK, MOE_CHUNK)).start()
                return c
            lax.fori_loop(0, zcount_ref[e], zb, 0)

        def tb(j, c):
            zero_tile(pl.multiple_of(tail_start + j * MOE_TILE, MOE_TILE)).start()
            return c
        lax.fori_loop(0, zcount_ref[N_EXPERTS], tb, 0)
        for e in range(N_EXPERTS):
            def zw(j, c):
                zero_copy(0).wait()
                return c
            lax.fori_loop(0, zcount_ref[e], zw, 0)

        def tw(j, c):
            zero_tile(0).wait()
            return c
        lax.fori_loop(0, zcount_ref[N_EXPERTS], tw, 0)

    @pl.when(t >= 2)
    def _():
        wait_tile(slot, t - 2)

    l_iota = lax.broadcasted_iota(jnp.int32, (rows, xa_ref.shape[0]), 0)
    perm = jnp.logical_or(l_iota == lpos_ref[0:1, :], l_iota == lpos_ref[1:2, :])
    perm = jnp.where(perm, 1.0, 0.0).astype(BF16)
    xloc[slot] = jnp.dot(perm, x_tile, preferred_element_type=F32)
    _seg_loop(tab, t, lambda lrow, grow: chunk_copy(slot, lrow, grow).start())

    @pl.when(t == n_t - 1)
    def _():
        wait_tile(slot, t)
        if n_t >= 2:
            wait_tile(1 - slot, t - 1)


def _moe_sort(xa, xb, lpos, tab, tt, r_tot):
    n_ta, n_tb = xa.shape[0] // tt, xb.shape[0] // tt
    n_t = n_ta + n_tb
    rows = _local_rows(tt)
    grid_spec = pltpu.PrefetchScalarGridSpec(
        num_scalar_prefetch=5,
        grid=(n_t,),
        in_specs=[pl.BlockSpec((2, tt), lambda t, *_: (0, t)),
                  pl.BlockSpec((tt, D_MODEL), lambda t, *_: (jnp.minimum(t, n_ta - 1), 0)),
                  pl.BlockSpec((tt, D_MODEL), lambda t, *_: (jnp.maximum(t - n_ta, 0), 0))],
        out_specs=pl.BlockSpec(memory_space=pl.ANY),
        scratch_shapes=[pltpu.VMEM((2, rows, D_MODEL), F32), pltpu.VMEM((MOE_TILE, D_MODEL), F32),
                        pltpu.SemaphoreType.DMA((2,)), pltpu.SemaphoreType.DMA((1,))])
    return pl.pallas_call(
        functools.partial(_sort_body, n_t=n_t, n_ta=n_ta),
        out_shape=jax.ShapeDtypeStruct((r_tot, D_MODEL), F32),
        grid_spec=grid_spec,
        compiler_params=_cparams(("arbitrary",)),
        name="moe_sort",
    )(tab["lstart"], tab["gstart"], tab["nchunk"], tab["zstart"], tab["zcount"], lpos, xa, xb)


def _expert_body(te_ref, nu_ref, x_ref, wg_ref, wu_ref, wd_ref, o_ref):
    del te_ref
    used = pl.program_id(0) < nu_ref[0]

    @pl.when(used)
    def _():
        xb = x_ref[...].astype(BF16)
        hid = _gelu(jnp.dot(xb, wg_ref[0], preferred_element_type=F32)) * jnp.dot(xb, wu_ref[0], preferred_element_type=F32)
        o_ref[...] = jnp.dot(hid.astype(BF16), wd_ref[0], preferred_element_type=F32)

    @pl.when(jnp.logical_not(used))
    def _():
        o_ref[...] = jnp.zeros_like(o_ref)


def _moe_experts(xs, tab, wg, wu, wd):
    r_tot = xs.shape[0]
    last = lambda i, nu: jnp.minimum(i, nu[0] - 1)
    grid_spec = pltpu.PrefetchScalarGridSpec(
        num_scalar_prefetch=2,
        grid=(r_tot // MOE_TILE,),
        in_specs=[pl.BlockSpec((MOE_TILE, D_MODEL), lambda i, te, nu: (last(i, nu), 0)),
                  pl.BlockSpec((1, D_MODEL, D_EXPERT), lambda i, te, nu: (te[last(i, nu)], 0, 0)),
                  pl.BlockSpec((1, D_MODEL, D_EXPERT), lambda i, te, nu: (te[last(i, nu)], 0, 0)),
                  pl.BlockSpec((1, D_EXPERT, D_MODEL), lambda i, te, nu: (te[last(i, nu)], 0, 0))],
        out_specs=pl.BlockSpec((MOE_TILE, D_MODEL), lambda i, te, nu: (i, 0)))
    return pl.pallas_call(
        _expert_body,
        out_shape=jax.ShapeDtypeStruct((r_tot, D_MODEL), F32),
        grid_spec=grid_spec,
        compiler_params=_cparams(("arbitrary",)),
        name="moe_experts",
    )(tab["tile_expert"], tab["n_used"], xs, wg, wu, wd)


def _combine_body(lstart_ref, gstart_ref, nchunk_ref, ys_hbm, xa_ref, xb_ref, meta_ref, g_ref, b_ref,
                  oa_ref, ob_ref, yloc, moe_s, sem, *, n_t, n_ta):
    t = pl.program_id(0)
    first = t < n_ta
    slot = lax.rem(t, 2)
    tab = (lstart_ref, gstart_ref, nchunk_ref)
    rows = yloc.shape[1]

    def chunk_copy(s, lrow, grow):
        return pltpu.make_async_copy(ys_hbm.at[pl.ds(grow, MOE_CHUNK)], yloc.at[s, pl.ds(lrow, MOE_CHUNK)], sem.at[s])

    def fetch(s, tile):
        _seg_loop(tab, tile, lambda lrow, grow: chunk_copy(s, lrow, grow).start())

    @pl.when(t == 0)
    def _():
        yloc[...] = jnp.zeros_like(yloc)
        fetch(0, 0)

    @pl.when(t + 1 < n_t)
    def _():
        fetch(1 - slot, t + 1)

    def wbody(j, c):
        chunk_copy(slot, 0, 0).wait()
        return c
    lax.fori_loop(0, _n_chunks(nchunk_ref, t), wbody, 0)

    meta = meta_ref[...]
    tt = meta.shape[0]
    l_iota = lax.broadcasted_iota(jnp.int32, (tt, rows), 1).astype(F32)
    sel = (jnp.where(l_iota == meta[:, 0:1], meta[:, 2:3], 0.0) + jnp.where(l_iota == meta[:, 1:2], meta[:, 3:4], 0.0))
    sel_hi, sel_lo = _split_bf16(sel)
    yb = yloc[slot].astype(BF16)
    moe_s[...] = jnp.dot(sel_hi, yb, preferred_element_type=F32) + jnp.dot(sel_lo, yb, preferred_element_type=F32)

    @pl.when(first)
    def _():
        oa_ref[...] = _layer_norm(DN_ALPHA * xa_ref[...] + moe_s[...], g_ref[...], b_ref[...])

    @pl.when(jnp.logical_not(first))
    def _():
        ob_ref[...] = _layer_norm(DN_ALPHA * xb_ref[...] + moe_s[...], g_ref[...], b_ref[...])


def _moe_combine(ys, xa, xb, meta_n, tab, g, b, tt):
    n_ta, n_tb = xa.shape[0] // tt, xb.shape[0] // tt
    n_t = n_ta + n_tb
    rows = _local_rows(tt)
    vec = lambda: pl.BlockSpec((1, D_MODEL), lambda t, *_: (0, 0))
    in_a = pl.BlockSpec((tt, D_MODEL), lambda t, *_: (jnp.minimum(t, n_ta - 1), 0))
    in_b = pl.BlockSpec((tt, D_MODEL), lambda t, *_: (jnp.maximum(t - n_ta, 0), 0))
    grid_spec = pltpu.PrefetchScalarGridSpec(
        num_scalar_prefetch=3,
        grid=(n_t,),
        in_specs=[pl.BlockSpec(memory_space=pl.ANY), in_a, in_b,
                  pl.BlockSpec((tt, 8), lambda t, *_: (t, 0)), vec(), vec()],
        out_specs=(in_a, in_b),
        scratch_shapes=[pltpu.VMEM((2, rows, D_MODEL), F32), pltpu.VMEM((tt, D_MODEL), F32),
                        pltpu.SemaphoreType.DMA((2,))])
    return pl.pallas_call(
        functools.partial(_combine_body, n_t=n_t, n_ta=n_ta),
        out_shape=(jax.ShapeDtypeStruct(xa.shape, F32), jax.ShapeDtypeStruct(xb.shape, F32)),
        grid_spec=grid_spec,
        compiler_params=_cparams(("arbitrary",)),
        name="moe_combine_ln",
    )(tab["lstart"], tab["gstart"], tab["nchunk"], ys, xa, xb, meta_n, g, b)


def _moe(xa, xb, meta_t, p):
    n = xa.shape[0] + xb.shape[0]
    tt = 256
    while xa.shape[0] % tt or xb.shape[0] % tt:
        tt //= 2
    r_tot, lpos, tab = _dispatch(meta_t, n, tt)
    xs = _moe_sort(xa, xb, lpos, tab, tt, r_tot)
    ys = _moe_experts(xs, tab, p["w_gate"], p["w_up"], p["w_down"])
    meta_n = jnp.concatenate([lpos.astype(F32), meta_t[2:4], jnp.zeros((4, n), F32)], axis=0).T
    return _moe_combine(ys, xa, xb, meta_n, tab, p["ln2_g"], p["ln2_b"], tt)


def _largest_tile(n, cap, mult):
    best = None
    for d in range(mult, min(n, cap) + 1, mult):
        if n % d == 0:
            best = d
    assert best is not None, (n, cap, mult)
    return best


def _seq_tails(x, off, b, t, k, c0, c1):
    if b <= 8:
        return jnp.stack([lax.slice(x, (off + i * t + t - k, c0), (off + (i + 1) * t, c1)) for i in range(b)])
    return lax.slice(x, (off, c0), (off + b * t, c1)).reshape(b, t, c1 - c0)[:, t - k:, :]


def _kv_rows(qkv, g, off, b, t, k):
    def pick(which):
        h0 = (which * N_GROUPS + g) * HEADS
        if b <= 8:
            a = jnp.stack([lax.slice(qkv, (h0, off + i * t + t - k, 0), (h0 + HEADS, off + (i + 1) * t, HEAD_DIM))
                           for i in range(b)], axis=1)
        else:
            a = lax.slice(qkv, (h0, off, 0), (h0 + HEADS, off + b * t, HEAD_DIM)).reshape(HEADS, b, t, HEAD_DIM)[:, :, t - k:]
        return jnp.transpose(a, (1, 2, 0, 3))
    return jnp.stack([pick(1), pick(2)], axis=2)


def _block_diag_gates(w_a, w_x):
    per = RNN_CHUNK // RNN_BLOCK
    chunks = []
    for c in range(N_RNN_CHUNKS):
        halves = []
        for w in (w_a, w_x):
            m = jnp.zeros((RNN_CHUNK, RNN_CHUNK), F32)
            for i in range(per):
                m = lax.dynamic_update_slice(m, w[c * per + i], (i * RNN_BLOCK, i * RNN_BLOCK))
            halves.append(m)
        chunks.append(jnp.concatenate(halves, axis=1))
    return jnp.stack(chunks).astype(BF16)


def kernel(x_prompt, x_sample, cache_kv_w128, cache_kv_w512, cache_kv_w2048, cache_mem_kv, state_h, state_conv, mem_prompt, w_in, b_gates, conv_w, conv_b, w_a, b_a, w_x, b_x, lru_lambda, w_br_lru, w_br_att, w_br_mem, w_o, w_mem_kv, ln1_g, ln1_b, w_rg, b_rg, w_re, b_re, w_gate, w_up, w_down, ln2_g, ln2_b):
    row = lambda v: v.reshape(1, -1).astype(F32)
    w_router = jnp.zeros((ROUTER_ROWS, D_MODEL), F32)
    w_router = w_router.at[:N_EXPERT_GROUPS].set(w_rg.T).at[N_EXPERT_GROUPS:N_EXPERT_GROUPS + N_EXPERTS].set(w_re.T)
    b_router = jnp.zeros((ROUTER_ROWS, 1), F32)
    b_router = b_router.at[:N_EXPERT_GROUPS, 0].set(b_rg).at[N_EXPERT_GROUPS:N_EXPERT_GROUPS + N_EXPERTS, 0].set(b_re)
    p = dict(
        b_gates=row(b_gates), conv_w=conv_w, conv_b=row(conv_b), wax=_block_diag_gates(w_a, w_x),
        b_a=row(b_a), b_x=row(b_x), lam=row(lru_lambda),
        w_br_lru=w_br_lru.astype(BF16), w_br_att=w_br_att.astype(BF16), w_br_mem=w_br_mem.astype(BF16),
        w_o=w_o.astype(BF16), ln1_g=row(ln1_g), ln1_b=row(ln1_b), w_router=w_router, b_router=b_router,
        w_gate=w_gate.astype(BF16), w_up=w_up.astype(BF16), w_down=w_down.astype(BF16),
        ln2_g=row(ln2_g), ln2_b=row(ln2_b))

    bp, s, _ = x_prompt.shape
    bs, ts, _ = x_sample.shape
    n_p, n_s = bp * s, bs * ts
    xp2, xs2 = x_prompt.reshape(n_p, D_MODEL), x_sample.reshape(n_s, D_MODEL)

    tr = 512
    while n_p % tr or n_s % tr:
        tr //= 2
    xb = _cast_rows(xp2, xs2, BF16, tr)
    tm_a = _largest_tile(n_p + n_s, 2304, 16)
    wb = lambda c0, c1: w_in[:, c0:c1].astype(BF16)
    xrg = _matmul(xb, wb(0, COL_Q), 0, 2 * D_RNN, F32, tm_a, 768, "in_proj_rnn")
    qkv = _matmul(xb, wb(COL_Q, COL_QM), 0, N_QKV_HEADS * HEAD_DIM, F32, tm_a, 768, "in_proj_qkv", split=6)
    qm = _matmul(xb, wb(COL_QM, COL_GATES), 0, D_MEM, F32, tm_a, 512, "in_proj_qm", split=2)
    gates = _matmul(xb, wb(COL_GATES, COL_GATES + N_GATE_COLS), 0, N_GATE_COLS, BF16, tm_a, 1024, "gate_proj",
                    bias=p["b_gates"])
    lru = (p["conv_w"], p["conv_b"], p["wax"], p["b_a"], p["b_x"], p["lam"])
    branch_w = (p["w_br_lru"], p["w_br_att"], p["w_br_mem"])
    ln1 = (p["w_o"], p["ln1_g"], p["ln1_b"], p["w_router"], p["b_router"])

    mem_rows = mem_prompt.reshape(bp * N_MEM, D_MODEL)
    mem_kv_p = _matmul(mem_rows, w_mem_kv, 0, 2 * D_MEM, F32, _tile(bp * N_MEM, 512), 512, "mem_kv_proj")
    hs_p, h_p = _rglru(xrg, 0, bp, s, jnp.zeros((bp, CONV_W - 1, D_RNN), F32), jnp.zeros((bp, D_RNN), F32), *lru,
                       _tile(s, 256))
    att_p = _attn_prompt(qkv, bp, s)
    mem_p = _mem_attn(qm, mem_kv_p.reshape(bp, N_MEM, 2 * D_MEM), bp, s, _tile(s, 1024), BF16)
    merged_p = _branch_merge(hs_p, xrg, att_p, mem_p, gates, 0, *branch_w, _tile(n_p, 256))
    x1_p, meta_p = _proj_ln(merged_p, xp2, *ln1, _tile(n_p, 512))

    hs_s, h_s = _rglru(xrg, n_p, bs, ts, state_conv, state_h, *lru, ts)
    att_s = _attn_sample(qkv, (cache_kv_w128, cache_kv_w512, cache_kv_w2048), bs, ts, n_p)
    mem_s = _mem_attn_cache(qm, cache_mem_kv, bs, ts, n_p)
    tm_s = _tile(n_s, 256)
    assert n_p % tm_s == 0
    merged_s = _branch_merge(hs_s, xrg, att_s, mem_s, gates, n_p, *branch_w, tm_s)
    x1_s, meta_s = _proj_ln(merged_s, xs2, *ln1, _tile(n_s, 512))

    y_p, y_s = _moe(x1_p, x1_s, jnp.concatenate([meta_p, meta_s], axis=1), p)

    kc = CONV_W - 1
    conv_p = jnp.concatenate([jnp.zeros((bp, kc, D_RNN), F32), _seq_tails(xrg, 0, bp, s, min(kc, s), 0, D_RNN)], axis=1)[:, -kc:]
    conv_s = jnp.concatenate([state_conv, _seq_tails(xrg, n_p, bs, ts, min(kc, ts), 0, D_RNN)], axis=1)[:, -kc:]
    kv_p = [_kv_rows(qkv, g, 0, bp, s, min(KEYS * d, s)) for g, d in enumerate(DILATIONS)]
    kv_s = [_kv_rows(qkv, g, n_p, bs, ts, ts) for g in range(N_GROUPS)]
    return (y_p.reshape(bp, s, D_MODEL), y_s.reshape(bs, ts, D_MODEL), kv_p[0], kv_p[1], kv_p[2],
            mem_kv_p.reshape(bp, N_MEM, 2, MEM_HEADS, MEM_HEAD_DIM), h_p.reshape(bp, D_RNN), conv_p,
            kv_s[0], kv_s[1], kv_s[2], h_s.reshape(bs, D_RNN), conv_s)
```

```python
import functools

import jax
import jax.numpy as jnp
from jax import lax
from jax.experimental import pallas as pl
from jax.experimental.pallas import tpu as pltpu

F32 = jnp.float32
BF16 = jnp.bfloat16

D_MODEL = 2048
D_RNN = 1536
N_RNN_BLOCKS = 16
RNN_BLOCK = D_RNN // N_RNN_BLOCKS
RNN_CHUNK = 384
N_RNN_CHUNKS = D_RNN // RNN_CHUNK
CONV_W = 4
LRU_C = 8.0
HEAD_DIM = 128
HEADS = 4
DILATIONS = (1, 4, 16)
KEYS = 128
N_GROUPS = 3
D_ATT_OUT = HEADS * HEAD_DIM
N_QKV_HEADS = 3 * N_GROUPS * HEADS
ATT_SCALE = HEAD_DIM ** -0.5
ATT_TILE = KEYS * max(DILATIONS)
N_MEM = 256
MEM_HEADS = 4
MEM_HEAD_DIM = 256
D_MEM = MEM_HEADS * MEM_HEAD_DIM
MEM_SCALE = MEM_HEAD_DIM ** -0.5
N_EXPERT_GROUPS = 4
EXPERTS_PER_GROUP = 4
N_EXPERTS = 16
D_EXPERT = 512
MOE_TILE = 512
DN_ALPHA = 2.0 ** 0.25
LN_EPS = 1e-5
NEG_INF = -1e30

COL_Q = 2 * D_RNN
COL_QM = COL_Q + N_QKV_HEADS * HEAD_DIM
COL_GATES = COL_QM + D_MEM
N_GATE_COLS = 3 * D_MODEL
ROUTER_ROWS = 32

VMEM_LIMIT = 56 * 1024 * 1024


def _cparams(sem):
    return pltpu.CompilerParams(dimension_semantics=sem, vmem_limit_bytes=VMEM_LIMIT)


def _gelu(x):
    return 0.5 * x * (1.0 + jnp.tanh(0.7978845608028654 * (x + 0.044715 * (x * x * x))))


def _layer_norm(x, g, b):
    mu = jnp.mean(x, axis=-1, keepdims=True)
    xc = x - mu
    var = jnp.mean(xc * xc, axis=-1, keepdims=True)
    return xc * lax.rsqrt(var + LN_EPS) * g + b


def _dot_nt(a, b):
    return lax.dot_general(a, b, (((1,), (1,)), ((), ())), preferred_element_type=F32)


def _tile(m, cap):
    t = min(m, cap)
    assert m % t == 0, (m, t)
    return t


def _mm_body(x_ref, w_ref, *rest, split, gate):
    acc = jnp.dot(x_ref[...].astype(BF16), w_ref[...].astype(BF16), preferred_element_type=F32)
    if gate:
        b_ref, o_ref = rest
        acc = 0.5 * jnp.tanh(0.5 * (acc + b_ref[...])) + 0.5
    else:
        (o_ref,) = rest
    if split == 1:
        o_ref[...] = acc.astype(o_ref.dtype)
    else:
        w = acc.shape[1] // split
        for s in range(split):
            o_ref[s] = acc[:, s * w:(s + 1) * w].astype(o_ref.dtype)


def _matmul(x, w, col_off, n_cols, out_dtype, tm, tn, name, bias=None, split=1):
    m, k = x.shape
    cb = col_off // tn
    in_specs = [pl.BlockSpec((tm, k), lambda i, j: (i, 0)),
                pl.BlockSpec((k, tn), lambda i, j: (0, j + cb))]
    args = [x, w]
    if bias is not None:
        in_specs.append(pl.BlockSpec((1, tn), lambda i, j: (0, j)))
        args.append(bias)
    if split == 1:
        out_shape = jax.ShapeDtypeStruct((m, n_cols), out_dtype)
        out_spec = pl.BlockSpec((tm, tn), lambda i, j: (i, j))
    else:
        out_shape = jax.ShapeDtypeStruct((n_cols * split // tn, m, tn // split), out_dtype)
        out_spec = pl.BlockSpec((split, tm, tn // split), lambda i, j: (j, i, 0))
    return pl.pallas_call(
        functools.partial(_mm_body, split=split, gate=bias is not None),
        out_shape=out_shape,
        grid=(m // tm, n_cols // tn),
        in_specs=in_specs,
        out_specs=out_spec,
        compiler_params=_cparams(("parallel", "arbitrary")),
        name=name,
    )(*args)


def _cast_rows_body(a_ref, b_ref, o_ref, *, n_ta):
    o_ref[...] = jnp.where(pl.program_id(0) < n_ta, a_ref[...], b_ref[...]).astype(o_ref.dtype)


def _cast_rows(xa, xb, dtype, tr):
    n_ta, n_tb = xa.shape[0] // tr, xb.shape[0] // tr
    cols = xa.shape[1]
    return pl.pallas_call(
        functools.partial(_cast_rows_body, n_ta=n_ta),
        out_shape=jax.ShapeDtypeStruct((xa.shape[0] + xb.shape[0], cols), dtype),
        grid=(n_ta + n_tb,),
        in_specs=[pl.BlockSpec((tr, cols), lambda i: (jnp.minimum(i, n_ta - 1), 0)),
                  pl.BlockSpec((tr, cols), lambda i: (jnp.maximum(i - n_ta, 0), 0))],
        out_specs=pl.BlockSpec((tr, cols), lambda i: (i, 0)),
        compiler_params=_cparams(("parallel",)),
        name="cast_rows",
    )(xa, xb)


def _rglru_body(xr_ref, cbuf_ref, h0_ref, cw_ref, cb_ref, wax_ref, ba_ref, bx_ref, lam_ref,
                out_ref, hl_ref, ext_s, a_s, u_s, h_s, *, tt):
    t = pl.program_id(1)

    @pl.when(t == 0)
    def _():
        ext_s[0:8, :] = jnp.zeros((8, D_RNN), F32)
        ext_s[5:8, :] = cbuf_ref[0]
        h_s[...] = jnp.broadcast_to(h0_ref[0], (8, D_RNN))

    @pl.when(t > 0)
    def _():
        ext_s[0:8, :] = ext_s[tt:tt + 8, :]

    ext_s[8:8 + tt, :] = xr_ref[...]
    cw = cw_ref[...]
    xc = (cb_ref[...] + cw[3:4, :] * ext_s[8:8 + tt, :] + cw[2:3, :] * ext_s[7:7 + tt, :]
          + cw[1:2, :] * ext_s[6:6 + tt, :] + cw[0:1, :] * ext_s[5:5 + tt, :])
    xcb = xc.astype(BF16)
    r_parts, i_parts = [], []
    for c in range(N_RNN_CHUNKS):
        g = jnp.dot(xcb[:, c * RNN_CHUNK:(c + 1) * RNN_CHUNK], wax_ref[c], preferred_element_type=F32)
        r_parts.append(g[:, :RNN_CHUNK])
        i_parts.append(g[:, RNN_CHUNK:])
    sigmoid = lambda z: 0.5 * jnp.tanh(0.5 * z) + 0.5
    r = sigmoid(jnp.concatenate(r_parts, axis=1) + ba_ref[...])
    gi = sigmoid(jnp.concatenate(i_parts, axis=1) + bx_ref[...])
    nl = -lam_ref[...]
    softplus = jnp.maximum(nl, 0.0) + jnp.log1p(jnp.exp(-jnp.abs(nl)))
    log_a = (-LRU_C) * r * softplus
    th = jnp.tanh(log_a)
    a_s[...] = jnp.exp(log_a)
    u_s[...] = jnp.sqrt(-2.0 * th / (1.0 - th)) * (gi * xc)

    rows = lax.broadcasted_iota(jnp.int32, (8, D_RNN), 0)

    def blk(i, h):
        r0 = pl.multiple_of(i * 8, 8)
        ab = a_s[pl.ds(r0, 8), :]
        ub = u_s[pl.ds(r0, 8), :]
        for s in (1, 2, 4):
            keep = rows >= s
            ub = ab * jnp.where(keep, pltpu.roll(ub, s, 0), 0.0) + ub
            ab = ab * jnp.where(keep, pltpu.roll(ab, s, 0), 1.0)
        hb = ab * h + ub
        u_s[pl.ds(r0, 8), :] = hb
        return jnp.broadcast_to(hb[7:8, :], (8, D_RNN))

    h_fin = lax.fori_loop(0, tt // 8, blk, h_s[...])
    h_s[...] = h_fin
    hl_ref[0] = h_fin[0:1, :]
    out_ref[...] = u_s[...]


def _rglru(xrg, off, b, t, conv_buf, h0, conv_w, conv_b, wax, b_a, b_x, lam, tt):
    nt = t // tt
    ob = off // tt
    vec = lambda: pl.BlockSpec((1, D_RNN), lambda i, j: (0, 0))
    return pl.pallas_call(
        functools.partial(_rglru_body, tt=tt),
        out_shape=(jax.ShapeDtypeStruct((b * t, D_RNN), F32), jax.ShapeDtypeStruct((b, 1, D_RNN), F32)),
        grid=(b, nt),
        in_specs=[pl.BlockSpec((tt, D_RNN), lambda i, j: (ob + i * nt + j, 0)),
                  pl.BlockSpec((1, CONV_W - 1, D_RNN), lambda i, j: (i, 0, 0)),
                  pl.BlockSpec((1, 1, D_RNN), lambda i, j: (i, 0, 0)),
                  pl.BlockSpec((CONV_W, D_RNN), lambda i, j: (0, 0)),
                  vec(),
                  pl.BlockSpec((N_RNN_CHUNKS, RNN_CHUNK, 2 * RNN_CHUNK), lambda i, j: (0, 0, 0)),
                  vec(), vec(), vec()],
        out_specs=(pl.BlockSpec((tt, D_RNN), lambda i, j: (i * nt + j, 0)),
                   pl.BlockSpec((1, 1, D_RNN), lambda i, j: (i, 0, 0))),
        scratch_shapes=[pltpu.VMEM((tt + 8, D_RNN), F32), pltpu.VMEM((tt, D_RNN), F32),
                        pltpu.VMEM((tt, D_RNN), F32), pltpu.VMEM((8, D_RNN), F32)],
        compiler_params=_cparams(("parallel", "arbitrary")),
        name="rglru",
    )(xrg, conv_buf, h0.reshape(b, 1, D_RNN), conv_w, conv_b, wax, b_a, b_x, lam)


def _band_block(q, k, v, bias):
    s = _dot_nt(q, k) * ATT_SCALE + bias
    m = jnp.max(s, axis=-1, keepdims=True)
    p = jnp.exp(s - m)
    l = jnp.sum(p, axis=-1, keepdims=True)
    o = jnp.dot(p.astype(BF16), v, preferred_element_type=F32) / l
    return o, m + jnp.log(l)


def _attn_prompt_body(*refs):
    q_refs, k_refs, v_refs, kh_refs, vh_refs = (refs[3 * i:3 * i + 3] for i in range(5))
    o_ref, og_s, lg_s = refs[15:]
    row = lax.broadcasted_iota(jnp.int32, (KEYS, 2 * KEYS), 0)
    col = lax.broadcasted_iota(jnp.int32, (KEYS, 2 * KEYS), 1)
    band = jnp.logical_and(col >= row, col <= row + KEYS)
    bias = jnp.where(band, 0.0, NEG_INF)
    has_prev = pl.program_id(1) > 0
    bias_first = jnp.where(jnp.logical_and(band, jnp.logical_or(col >= KEYS, has_prev)), 0.0, NEG_INF)

    def put(g, rows, o, lse):
        og_s[g, rows, :] = o
        lg_s[g, rows, :] = jnp.broadcast_to(lse, (KEYS, HEAD_DIM))

    def rows_of(start, size, d):
        return pl.ds(start, size, stride=d) if d > 1 else pl.ds(start, size)

    for g, d in enumerate(DILATIONS):
        q_ref, k_ref, v_ref, kh_ref, vh_ref = q_refs[g], k_refs[g], v_refs[g], kh_refs[g], vh_refs[g]
        nqb = ATT_TILE // (d * KEYS)
        for r in range(d):
            own = rows_of(r, KEYS, d)
            kk = jnp.concatenate([kh_ref[0, own, :], k_ref[0, own, :]], axis=0).astype(BF16)
            vv = jnp.concatenate([vh_ref[0, own, :], v_ref[0, own, :]], axis=0).astype(BF16)
            o, lse = _band_block(q_ref[0, own, :].astype(BF16), kk, vv, bias_first)
            put(g, own, o, lse)
            for qb in range(1, nqb):
                rows = rows_of(qb * KEYS * d + r, KEYS, d)
                keys = rows_of((qb - 1) * KEYS * d + r, 2 * KEYS, d)
                o, lse = _band_block(q_ref[0, rows, :].astype(BF16), k_ref[0, keys, :].astype(BF16),
                                     v_ref[0, keys, :].astype(BF16), bias)
                put(g, rows, o, lse)

    l0, l1, l2 = lg_s[0], lg_s[1], lg_s[2]
    m = jnp.maximum(jnp.maximum(l0, l1), l2)
    e0, e1, e2 = jnp.exp(l0 - m), jnp.exp(l1 - m), jnp.exp(l2 - m)
    o_ref[...] = ((e0 * og_s[0] + e1 * og_s[1] + e2 * og_s[2]) / (e0 + e1 + e2)).astype(o_ref.dtype)


def _attn_prompt(qkv, b, s):
    assert s % ATT_TILE == 0
    nt = s // ATT_TILE
    n = b * s

    def cur(which, g):
        return pl.BlockSpec((1, ATT_TILE, HEAD_DIM),
                            lambda i, j, h: ((which * N_GROUPS + g) * HEADS + h, i * nt + j, 0))

    def halo(which, g):
        rows = KEYS * DILATIONS[g]
        per = ATT_TILE // rows
        return pl.BlockSpec((1, rows, HEAD_DIM),
                            lambda i, j, h: ((which * N_GROUPS + g) * HEADS + h,
                                             jnp.maximum((i * nt + j) * per - 1, 0), 0))

    in_specs = ([cur(0, g) for g in range(N_GROUPS)] + [cur(1, g) for g in range(N_GROUPS)]
                + [cur(2, g) for g in range(N_GROUPS)] + [halo(1, g) for g in range(N_GROUPS)]
                + [halo(2, g) for g in range(N_GROUPS)])
    return pl.pallas_call(
        _attn_prompt_body,
        out_shape=jax.ShapeDtypeStruct((n, D_ATT_OUT), BF16),
        grid=(b, nt, HEADS),
        in_specs=in_specs,
        out_specs=pl.BlockSpec((ATT_TILE, HEAD_DIM), lambda i, j, h: (i * nt + j, h)),
        scratch_shapes=[pltpu.VMEM((N_GROUPS, ATT_TILE, HEAD_DIM), F32),
                        pltpu.VMEM((N_GROUPS, ATT_TILE, HEAD_DIM), F32)],
        compiler_params=_cparams(("parallel", "parallel", "parallel")),
        name="attn_prompt",
    )(*([qkv] * 15))


def _attn_sample_group(q4, kn4, vn4, c_ref, d, t_new):
    half = len(c_ref.shape) == 4
    wb = c_ref.shape[1] * 16 if half else c_ref.shape[1] // (2 * HEADS)
    nk = wb // 2 if half else wb
    nr = HEADS * t_new
    zeros = jnp.zeros((t_new, HEAD_DIM), F32)
    qbd = jnp.concatenate(
        [jnp.concatenate([q4[h] if hh == h else zeros for hh in range(HEADS)], axis=1) for h in range(HEADS)],
        axis=0).astype(BF16)
    kn = jnp.concatenate([kn4[h] for h in range(HEADS)], axis=1).astype(BF16)
    vn = jnp.concatenate([vn4[h] for h in range(HEADS)], axis=1).astype(BF16)
    if half:
        cache_rows = lambda kv, h: c_ref[0, :, pl.ds(kv * HEADS + h, 8, stride=2 * HEADS), :].reshape(nk, HEAD_DIM)
    else:
        cache_rows = lambda kv, h: c_ref[0, pl.ds(kv * HEADS + h, wb, stride=2 * HEADS), :]
    kc = jnp.concatenate([cache_rows(0, h) for h in range(HEADS)], axis=1).astype(BF16)
    vc = jnp.concatenate([cache_rows(1, h) for h in range(HEADS)], axis=1).astype(BF16)
    tq_c = lax.broadcasted_iota(jnp.int32, (nr, nk), 0) & (t_new - 1)
    e_c = lax.broadcasted_iota(jnp.int32, (nr, nk), 1)
    if half:
        e_c = ((e_c >> 3) << 4) + (e_c & 7)
    dist_c = wb + tq_c - e_c
    ok_c = jnp.logical_and((dist_c & (d - 1)) == 0, dist_c <= KEYS * d)
    tq_n = lax.broadcasted_iota(jnp.int32, (nr, t_new), 0) & (t_new - 1)
    dist_n = tq_n - lax.broadcasted_iota(jnp.int32, (nr, t_new), 1)
    ok_n = jnp.logical_and(jnp.logical_and(dist_n >= 0, (dist_n & (d - 1)) == 0), dist_n <= KEYS * d)
    s_c = jnp.where(ok_c, _dot_nt(qbd, kc) * ATT_SCALE, NEG_INF)
    s_n = jnp.where(ok_n, _dot_nt(qbd, kn) * ATT_SCALE, NEG_INF)
    m = jnp.maximum(jnp.max(s_c, axis=-1, keepdims=True), jnp.max(s_n, axis=-1, keepdims=True))
    p_c = jnp.exp(s_c - m)
    p_n = jnp.exp(s_n - m)
    l = jnp.sum(p_c, axis=-1, keepdims=True) + jnp.sum(p_n, axis=-1, keepdims=True)
    o = (jnp.dot(p_c.astype(BF16), vc, preferred_element_type=F32)
         + jnp.dot(p_n.astype(BF16), vn, preferred_element_type=F32)) / l
    lse = m + jnp.log(l)
    o = jnp.concatenate(
        [o[h * t_new:(h + 1) * t_new, h * HEAD_DIM:(h + 1) * HEAD_DIM] for h in range(HEADS)], axis=1)
    lse = jnp.concatenate(
        [jnp.broadcast_to(lse[h * t_new:(h + 1) * t_new], (t_new, HEAD_DIM)) for h in range(HEADS)], axis=1)
    return o, lse


def _attn_sample_body(*refs, t_new):
    q_refs, k_refs, v_refs, c_refs = (refs[3 * i:3 * i + 3] for i in range(4))
    o_ref = refs[12]
    outs = [_attn_sample_group(q_refs[g][...], k_refs[g][...], v_refs[g][...], c_refs[g], d, t_new)
            for g, d in enumerate(DILATIONS)]
    (o0, l0), (o1, l1), (o2, l2) = outs
    m = jnp.maximum(jnp.maximum(l0, l1), l2)
    e0, e1, e2 = jnp.exp(l0 - m), jnp.exp(l1 - m), jnp.exp(l2 - m)
    o_ref[0] = (e0 * o0 + e1 * o1 + e2 * o2) / (e0 + e1 + e2)


def _attn_sample(qkv, caches, b, t_new, off):
    assert t_new & (t_new - 1) == 0 and off % t_new == 0
    ob = off // t_new
    new = lambda which, g: pl.BlockSpec((HEADS, t_new, HEAD_DIM), lambda i: (which * N_GROUPS + g, ob + i, 0))
    caches2, cache_specs = [], []
    for c, d in zip(caches, DILATIONS):
        wb = c.shape[1]
        if d == 16 and wb % 16 == 0 and t_new <= 8:
            caches2.append(c.reshape(b, wb // 16, 16 * 2 * HEADS, HEAD_DIM))
            cache_specs.append(pl.BlockSpec((1, wb // 16, 8 * 2 * HEADS, HEAD_DIM), lambda i: (i, 0, 0, 0)))
        else:
            caches2.append(c.reshape(b, wb * 2 * HEADS, HEAD_DIM))
            cache_specs.append(pl.BlockSpec((1, wb * 2 * HEADS, HEAD_DIM), lambda i: (i, 0, 0)))
    in_specs = ([new(0, g) for g in range(N_GROUPS)] + [new(1, g) for g in range(N_GROUPS)]
                + [new(2, g) for g in range(N_GROUPS)] + cache_specs)
    att = pl.pallas_call(
        functools.partial(_attn_sample_body, t_new=t_new),
        out_shape=jax.ShapeDtypeStruct((b, t_new, D_ATT_OUT), F32),
        grid=(b,),
        in_specs=in_specs,
        out_specs=pl.BlockSpec((1, t_new, D_ATT_OUT), lambda i: (i, 0, 0)),
        compiler_params=_cparams(("parallel",)),
        name="attn_sample",
    )(*([qkv] * 9), *caches2)
    return att.reshape(b * t_new, D_ATT_OUT)


def _mem_attn_body(q_ref, kv_ref, o_ref):
    for h in range(MEM_HEADS):
        k = kv_ref[0, :, h * MEM_HEAD_DIM:(h + 1) * MEM_HEAD_DIM].astype(BF16)
        v = kv_ref[0, :, D_MEM + h * MEM_HEAD_DIM:D_MEM + (h + 1) * MEM_HEAD_DIM].astype(BF16)
        s = _dot_nt(q_ref[h].astype(BF16), k) * MEM_SCALE
        m = jnp.max(s, axis=-1, keepdims=True)
        p = jnp.exp(s - m)
        l = jnp.sum(p, axis=-1, keepdims=True)
        o = jnp.dot(p.astype(BF16), v, preferred_element_type=F32) / l
        o_ref[:, h * MEM_HEAD_DIM:(h + 1) * MEM_HEAD_DIM] = o.astype(o_ref.dtype)


def _mem_attn(qm, mem_kv, b, t, tm, out_dtype):
    nt = t // tm
    return pl.pallas_call(
        _mem_attn_body,
        out_shape=jax.ShapeDtypeStruct((b * t, D_MEM), out_dtype),
        grid=(b, nt),
        in_specs=[pl.BlockSpec((MEM_HEADS, tm, MEM_HEAD_DIM), lambda i, j: (0, i * nt + j, 0)),
                  pl.BlockSpec((1, N_MEM, 2 * D_MEM), lambda i, j: (i, 0, 0))],
        out_specs=pl.BlockSpec((tm, D_MEM), lambda i, j: (i * nt + j, 0)),
        compiler_params=_cparams(("parallel", "parallel")),
        name="mem_attn",
    )(qm, mem_kv)


def _mem_attn_cache_body(q_ref, c_ref, o_ref):
    t = q_ref.shape[1]
    n_rows = N_MEM * 2 * MEM_HEADS
    flat = c_ref[0].reshape(n_rows, MEM_HEAD_DIM).astype(BF16)
    q = jnp.concatenate([q_ref[h] for h in range(MEM_HEADS)], axis=0).astype(BF16)
    assert t & (t - 1) == 0
    head = lax.broadcasted_iota(jnp.int32, (MEM_HEADS * t, n_rows), 0) >> (t.bit_length() - 1)
    col = lax.broadcasted_iota(jnp.int32, (MEM_HEADS * t, n_rows), 1)
    s = jnp.where((col & (2 * MEM_HEADS - 1)) == head, _dot_nt(q, flat) * MEM_SCALE, NEG_INF)
    m = jnp.max(s, axis=-1, keepdims=True)
    p = jnp.exp(s - m)
    l = jnp.sum(p, axis=-1, keepdims=True)
    o = jnp.dot(pltpu.roll(p, MEM_HEADS, 1).astype(BF16), flat, preferred_element_type=F32) / l
    o_ref[...] = jnp.concatenate([o[h * t:(h + 1) * t, :] for h in range(MEM_HEADS)], axis=1)


def _mem_attn_cache(qm, cache, b, t, off):
    ob = off // t
    return pl.pallas_call(
        _mem_attn_cache_body,
        out_shape=jax.ShapeDtypeStruct((b * t, D_MEM), F32),
        grid=(b,),
        in_specs=[pl.BlockSpec((MEM_HEADS, t, MEM_HEAD_DIM), lambda i: (0, ob + i, 0)),
                  pl.BlockSpec((1, N_MEM, 2, MEM_HEADS, MEM_HEAD_DIM), lambda i: (i, 0, 0, 0, 0))],
        out_specs=pl.BlockSpec((t, D_MEM), lambda i: (i, 0)),
        compiler_params=_cparams(("parallel",)),
        name="mem_attn_cache",
    )(qm, cache)


def _branch_body(hs_ref, xg_ref, att_ref, mem_ref, gt_ref, wl_ref, wa_ref, wm_ref, out_ref):
    a_lru = (hs_ref[...] * _gelu(xg_ref[...])).astype(BF16)
    acc = gt_ref[:, 0:D_MODEL].astype(F32) * jnp.dot(a_lru, wl_ref[...], preferred_element_type=F32)
    acc = acc + gt_ref[:, D_MODEL:2 * D_MODEL].astype(F32) * jnp.dot(att_ref[...].astype(BF16), wa_ref[...], preferred_element_type=F32)
    acc = acc + gt_ref[:, 2 * D_MODEL:3 * D_MODEL].astype(F32) * jnp.dot(mem_ref[...].astype(BF16), wm_ref[...], preferred_element_type=F32)
    out_ref[...] = acc.astype(out_ref.dtype)


def _branch_merge(hs, xrg, att, mem, gates, goff, wl, wa, wm, tm):
    n = hs.shape[0]
    gb = goff // tm
    row = lambda w: pl.BlockSpec((tm, w), lambda i: (i, 0))
    full = lambda a: pl.BlockSpec(a.shape, lambda i: (0, 0))
    return pl.pallas_call(
        _branch_body,
        out_shape=jax.ShapeDtypeStruct((n, D_MODEL), BF16),
        grid=(n // tm,),
        in_specs=[row(D_RNN), pl.BlockSpec((tm, D_RNN), lambda i: (gb + i, 1)), row(D_ATT_OUT), row(D_MEM),
                  pl.BlockSpec((tm, N_GATE_COLS), lambda i: (gb + i, 0)), full(wl), full(wa), full(wm)],
        out_specs=row(D_MODEL),
        compiler_params=_cparams(("parallel",)),
        name="branch_merge",
    )(hs, xrg, att, mem, gates, wl, wa, wm)


def _split_bf16(x):
    hi = x.astype(BF16)
    return hi, (x - hi.astype(F32)).astype(BF16)


def _route_rows(lg):
    g = [lg[i:i + 1, :] for i in range(N_EXPERT_GROUPS)]
    gmax = jnp.maximum(jnp.maximum(g[0], g[1]), jnp.maximum(g[2], g[3]))
    gidx = jnp.where(g[0] == gmax, 0.0, jnp.where(g[1] == gmax, 1.0, jnp.where(g[2] == gmax, 2.0, 3.0)))
    g_p = 1.0 / (jnp.exp(g[0] - gmax) + jnp.exp(g[1] - gmax) + jnp.exp(g[2] - gmax) + jnp.exp(g[3] - gmax))
    e = []
    for k in range(EXPERTS_PER_GROUP):
        rows = [lg[N_EXPERT_GROUPS + gg * EXPERTS_PER_GROUP + k:N_EXPERT_GROUPS + gg * EXPERTS_PER_GROUP + k + 1, :]
                for gg in range(N_EXPERT_GROUPS)]
        e.append(jnp.where(gidx == 0.0, rows[0], jnp.where(gidx == 1.0, rows[1], jnp.where(gidx == 2.0, rows[2], rows[3]))))

    def first_argmax(v):
        mx = jnp.maximum(jnp.maximum(v[0], v[1]), jnp.maximum(v[2], v[3]))
        ix = jnp.where(v[0] == mx, 0.0, jnp.where(v[1] == mx, 1.0, jnp.where(v[2] == mx, 2.0, 3.0)))
        return mx, ix

    v1, i1 = first_argmax(e)
    v2, i2 = first_argmax([jnp.where(i1 == float(k), -jnp.inf, e[k]) for k in range(EXPERTS_PER_GROUP)])
    ex = jnp.exp(v2 - v1)
    w1 = g_p / (1.0 + ex)
    w2 = g_p * ex / (1.0 + ex)
    base = gidx * float(EXPERTS_PER_GROUP)
    zero = jnp.zeros_like(w1)
    return jnp.concatenate([base + i1, base + i2, w1, w2, zero, zero, zero, zero], axis=0)


def _proj_ln_body(mg_ref, x_ref, wo_ref, g_ref, b_ref, wr_ref, br_ref, x1_ref, meta_ref, mix_s):
    @pl.when(pl.program_id(0) == 0)
    def _():
        mix_s[...] = jnp.zeros(mix_s.shape, F32)

    wh, wl = _split_bf16(wr_ref[...])
    tm = x_ref.shape[0]
    sub = min(256, tm)
    for r0 in range(0, tm, sub):
        rs = pl.ds(r0, sub)
        x1 = _layer_norm(DN_ALPHA * x_ref[rs, :] + mix_s[rs, :], g_ref[...], b_ref[...])
        x1_ref[rs, :] = x1
        xh, xl = _split_bf16(x1)
        lg = _dot_nt(wh, xh) + (_dot_nt(wh, xl) + _dot_nt(wl, xh)) + br_ref[...]
        meta_ref[:, rs] = _route_rows(lg)
    mix_s[...] = jnp.dot(mg_ref[...], wo_ref[...], preferred_element_type=F32)


def _proj_ln(merged, x, wo, g, b, wr, br, tm):
    n = merged.shape[0]
    nb = n // tm
    prev = lambda w: pl.BlockSpec((tm, w), lambda i: (jnp.maximum(i - 1, 0), 0))
    full = lambda a: pl.BlockSpec(a.shape, lambda i: (0, 0))
    return pl.pallas_call(
        _proj_ln_body,
        out_shape=(jax.ShapeDtypeStruct((n, D_MODEL), F32), jax.ShapeDtypeStruct((8, n), F32)),
        grid=(nb + 1,),
        in_specs=[pl.BlockSpec((tm, D_MODEL), lambda i: (jnp.minimum(i, nb - 1), 0)), prev(D_MODEL),
                  full(wo), full(g), full(b), full(wr), full(br)],
        out_specs=(prev(D_MODEL), pl.BlockSpec((8, tm), lambda i: (0, jnp.maximum(i - 1, 0)))),
        scratch_shapes=[pltpu.VMEM((tm, D_MODEL), F32)],
        compiler_params=_cparams(("arbitrary",)),
        name="proj_ln_router",
    )(merged, x, wo, g, b, wr, br)


MOE_CHUNK = 8


def _local_rows(tt):
    return -(-(2 * tt + N_EXPERTS * (MOE_CHUNK - 1)) // 128) * 128


def _dispatch(meta_t, n, tt):
    n_t = n // tt
    ids = meta_t[0:2].astype(jnp.int32)
    onehot = (ids[:, :, None] == jnp.arange(N_EXPERTS, dtype=jnp.int32)).astype(jnp.int32).reshape(2, n_t, tt, N_EXPERTS)
    cnt_slot = jnp.sum(onehot, axis=2)
    cnt = cnt_slot[0] + cnt_slot[1]
    pc = (cnt + MOE_CHUNK - 1) // MOE_CHUNK * MOE_CHUNK
    lstart = jnp.cumsum(pc, axis=1) - pc
    tri = (jnp.arange(tt)[:, None] >= jnp.arange(tt)[None, :]).astype(F32)
    csum = jnp.einsum("ut,snte->snue", tri, onehot.astype(F32)).astype(jnp.int32)
    rank = csum - onehot + jnp.stack([jnp.zeros_like(cnt), cnt_slot[0]])[:, :, None, :]
    lpos = jnp.sum(onehot * (lstart[None, :, None, :] + rank), axis=-1).reshape(2, n)
    seg = jnp.sum(pc, axis=0)
    pe = (seg + MOE_TILE - 1) // MOE_TILE * MOE_TILE
    ends = jnp.cumsum(pe)
    base = ends - pe
    gstart = base[None, :] + jnp.cumsum(pc, axis=0) - pc
    r_tot = -(-(2 * n + N_EXPERTS * (MOE_CHUNK - 1) * n_t) // MOE_TILE) * MOE_TILE + N_EXPERTS * MOE_TILE
    n_tiles = r_tot // MOE_TILE
    tile_start = jnp.arange(n_tiles, dtype=jnp.int32) * MOE_TILE
    tile_expert = jnp.minimum(jnp.sum((tile_start[:, None] >= ends[None, :]).astype(jnp.int32), axis=1), N_EXPERTS - 1)
    tables = dict(
        lstart=lstart.reshape(-1), gstart=gstart.reshape(-1), nchunk=(pc // MOE_CHUNK).reshape(-1),
        zstart=jnp.concatenate([base + seg, ends[-1:]]),
        zcount=jnp.concatenate([(pe - seg) // MOE_CHUNK, (r_tot - ends[-1:]) // MOE_TILE]),
        tile_expert=tile_expert, n_used=(ends[-1] // MOE_TILE).reshape(1))
    return r_tot, lpos, tables


def _seg_loop(tab, t, fn):
    lstart_ref, gstart_ref, nchunk_ref = tab
    for e in range(N_EXPERTS):
        ls = lstart_ref[t * N_EXPERTS + e]
        gs = gstart_ref[t * N_EXPERTS + e]

        def body(j, c, ls=ls, gs=gs):
            fn(pl.multiple_of(ls + j * MOE_CHUNK, MOE_CHUNK), pl.multiple_of(gs + j * MOE_CHUNK, MOE_CHUNK))
            return c
        lax.fori_loop(0, nchunk_ref[t * N_EXPERTS + e], body, 0)


def _n_chunks(nchunk_ref, t):
    tot = nchunk_ref[t * N_EXPERTS]
    for e in range(1, N_EXPERTS):
        tot = tot + nchunk_ref[t * N_EXPERTS + e]
    return tot


def _sort_body(lstart_ref, gstart_ref, nchunk_ref, zstart_ref, zcount_ref, lpos_ref, xa_ref, xb_ref, xs_hbm,
               xloc, zbuf, sem, zsem, *, n_t, n_ta):
    t = pl.program_id(0)
    x_tile = jnp.where(t < n_ta, xa_ref[...], xb_ref[...]).astype(BF16)
    slot = lax.rem(t, 2)
    tab = (lstart_ref, gstart_ref, nchunk_ref)
    rows = xloc.shape[1]

    def chunk_copy(s, lrow, grow):
        return pltpu.make_async_copy(xloc.at[s, pl.ds(lrow, MOE_CHUNK)], xs_hbm.at[pl.ds(grow, MOE_CHUNK)], sem.at[s])

    def wait_tile(s, tile):
        def body(j, c):
            chunk_copy(s, 0, 0).wait()
            return c
        lax.fori_loop(0, _n_chunks(nchunk_ref, tile), body, 0)

    @pl.when(t == 0)
    def _():
        zbuf[...] = jnp.zeros_like(zbuf)
        zero_copy = lambda grow: pltpu.make_async_copy(
            zbuf.at[pl.ds(0, MOE_CHUNK)], xs_hbm.at[pl.ds(grow, MOE_CHUNK)], zsem.at[0])
        zero_tile = lambda grow: pltpu.make_async_copy(zbuf, xs_hbm.at[pl.ds(grow, MOE_TILE)], zsem.at[0])
        tail_start = zstart_ref[N_EXPERTS]
        for e in range(N_EXPERTS):
            def zb(j, c, e=e):
                zero_copy(pl.multiple_of(zstart_ref[e] + j * MOE_CHUNK, MOE_CHUNK)).start()
                return c
            lax.fori_loop(0, zcount_ref[e], zb, 0)

        def tb(j, c):
            zero_tile(pl.multiple_of(tail_start + j * MOE_TILE, MOE_TILE)).start()
            return c
        lax.fori_loop(0, zcount_ref[N_EXPERTS], tb, 0)
        for e in range(N_EXPERTS):
            def zw(j, c):
                zero_copy(0).wait()
                return c
            lax.fori_loop(0, zcount_ref[e], zw, 0)

        def tw(j, c):
            zero_tile(0).wait()
            return c
        lax.fori_loop(0, zcount_ref[N_EXPERTS], tw, 0)

    @pl.when(t >= 2)
    def _():
        wait_tile(slot, t - 2)

    l_iota = lax.broadcasted_iota(jnp.int32, (rows, xa_ref.shape[0]), 0)
    perm = jnp.logical_or(l_iota == lpos_ref[0:1, :], l_iota == lpos_ref[1:2, :])
    perm = jnp.where(perm, 1.0, 0.0).astype(BF16)
    xloc[slot] = jnp.dot(perm, x_tile, preferred_element_type=F32)
    _seg_loop(tab, t, lambda lrow, grow: chunk_copy(slot, lrow, grow).start())

    @pl.when(t == n_t - 1)
    def _():
        wait_tile(slot, t)
        if n_t >= 2:
            wait_tile(1 - slot, t - 1)


def _moe_sort(xa, xb, lpos, tab, tt, r_tot):
    n_ta, n_tb = xa.shape[0] // tt, xb.shape[0] // tt
    n_t = n_ta + n_tb
    rows = _local_rows(tt)
    grid_spec = pltpu.PrefetchScalarGridSpec(
        num_scalar_prefetch=5,
        grid=(n_t,),
        in_specs=[pl.BlockSpec((2, tt), lambda t, *_: (0, t)),
                  pl.BlockSpec((tt, D_MODEL), lambda t, *_: (jnp.minimum(t, n_ta - 1), 0)),
                  pl.BlockSpec((tt, D_MODEL), lambda t, *_: (jnp.maximum(t - n_ta, 0), 0))],
        out_specs=pl.BlockSpec(memory_space=pl.ANY),
        scratch_shapes=[pltpu.VMEM((2, rows, D_MODEL), F32), pltpu.VMEM((MOE_TILE, D_MODEL), F32),
                        pltpu.SemaphoreType.DMA((2,)), pltpu.SemaphoreType.DMA((1,))])
    return pl.pallas_call(
        functools.partial(_sort_body, n_t=n_t, n_ta=n_ta),
        out_shape=jax.ShapeDtypeStruct((r_tot, D_MODEL), F32),
        grid_spec=grid_spec,
        compiler_params=_cparams(("arbitrary",)),
        name="moe_sort",
    )(tab["lstart"], tab["gstart"], tab["nchunk"], tab["zstart"], tab["zcount"], lpos, xa, xb)


def _expert_body(te_ref, nu_ref, x_ref, wg_ref, wu_ref, wd_ref, o_ref):
    del te_ref
    used = pl.program_id(0) < nu_ref[0]

    @pl.when(used)
    def _():
        xb = x_ref[...].astype(BF16)
        hid = _gelu(jnp.dot(xb, wg_ref[0], preferred_element_type=F32)) * jnp.dot(xb, wu_ref[0], preferred_element_type=F32)
        o_ref[...] = jnp.dot(hid.astype(BF16), wd_ref[0], preferred_element_type=F32)

    @pl.when(jnp.logical_not(used))
    def _():
        o_ref[...] = jnp.zeros_like(o_ref)


def _moe_experts(xs, tab, wg, wu, wd):
    r_tot = xs.shape[0]
    last = lambda i, nu: jnp.minimum(i, nu[0] - 1)
    grid_spec = pltpu.PrefetchScalarGridSpec(
        num_scalar_prefetch=2,
        grid=(r_tot // MOE_TILE,),
        in_specs=[pl.BlockSpec((MOE_TILE, D_MODEL), lambda i, te, nu: (last(i, nu), 0)),
                  pl.BlockSpec((1, D_MODEL, D_EXPERT), lambda i, te, nu: (te[last(i, nu)], 0, 0)),
                  pl.BlockSpec((1, D_MODEL, D_EXPERT), lambda i, te, nu: (te[last(i, nu)], 0, 0)),
                  pl.BlockSpec((1, D_EXPERT, D_MODEL), lambda i, te, nu: (te[last(i, nu)], 0, 0))],
        out_specs=pl.BlockSpec((MOE_TILE, D_MODEL), lambda i, te, nu: (i, 0)))
    return pl.pallas_call(
        _expert_body,
        out_shape=jax.ShapeDtypeStruct((r_tot, D_MODEL), F32),
        grid_spec=grid_spec,
        compiler_params=_cparams(("arbitrary",)),
        name="moe_experts",
    )(tab["tile_expert"], tab["n_used"], xs, wg, wu, wd)


def _combine_body(lstart_ref, gstart_ref, nchunk_ref, ys_hbm, xa_ref, xb_ref, meta_ref, g_ref, b_ref,
                  oa_ref, ob_ref, yloc, moe_s, sem, *, n_t, n_ta):
    t = pl.program_id(0)
    first = t < n_ta
    slot = lax.rem(t, 2)
    tab = (lstart_ref, gstart_ref, nchunk_ref)
    rows = yloc.shape[1]

    def chunk_copy(s, lrow, grow):
        return pltpu.make_async_copy(ys_hbm.at[pl.ds(grow, MOE_CHUNK)], yloc.at[s, pl.ds(lrow, MOE_CHUNK)], sem.at[s])

    def fetch(s, tile):
        _seg_loop(tab, tile, lambda lrow, grow: chunk_copy(s, lrow, grow).start())

    @pl.when(t == 0)
    def _():
        yloc[...] = jnp.zeros_like(yloc)
        fetch(0, 0)

    @pl.when(t + 1 < n_t)
    def _():
        fetch(1 - slot, t + 1)

    def wbody(j, c):
        chunk_copy(slot, 0, 0).wait()
        return c
    lax.fori_loop(0, _n_chunks(nchunk_ref, t), wbody, 0)

    meta = meta_ref[...]
    tt = meta.shape[0]
    l_iota = lax.broadcasted_iota(jnp.int32, (tt, rows), 1).astype(F32)
    sel = (jnp.where(l_iota == meta[:, 0:1], meta[:, 2:3], 0.0) + jnp.where(l_iota == meta[:, 1:2], meta[:, 3:4], 0.0))
    sel_hi, sel_lo = _split_bf16(sel)
    yb = yloc[slot].astype(BF16)
    moe_s[...] = jnp.dot(sel_hi, yb, preferred_element_type=F32) + jnp.dot(sel_lo, yb, preferred_element_type=F32)

    @pl.when(first)
    def _():
        oa_ref[...] = _layer_norm(DN_ALPHA * xa_ref[...] + moe_s[...], g_ref[...], b_ref[...])

    @pl.when(jnp.logical_not(first))
    def _():
        ob_ref[...] = _layer_norm(DN_ALPHA * xb_ref[...] + moe_s[...], g_ref[...], b_ref[...])


def _moe_combine(ys, xa, xb, meta_n, tab, g, b, tt):
    n_ta, n_tb = xa.shape[0] // tt, xb.shape[0] // tt
    n_t = n_ta + n_tb
    rows = _local_rows(tt)
    vec = lambda: pl.BlockSpec((1, D_MODEL), lambda t, *_: (0, 0))
    in_a = pl.BlockSpec((tt, D_MODEL), lambda t, *_: (jnp.minimum(t, n_ta - 1), 0))
    in_b = pl.BlockSpec((tt, D_MODEL), lambda t, *_: (jnp.maximum(t - n_ta, 0), 0))
    grid_spec = pltpu.PrefetchScalarGridSpec(
        num_scalar_prefetch=3,
        grid=(n_t,),
        in_specs=[pl.BlockSpec(memory_space=pl.ANY), in_a, in_b,
                  pl.BlockSpec((tt, 8), lambda t, *_: (t, 0)), vec(), vec()],
        out_specs=(in_a, in_b),
        scratch_shapes=[pltpu.VMEM((2, rows, D_MODEL), F32), pltpu.VMEM((tt, D_MODEL), F32),
                        pltpu.SemaphoreType.DMA((2,))])
    return pl.pallas_call(
        functools.partial(_combine_body, n_t=n_t, n_ta=n_ta),
        out_shape=(jax.ShapeDtypeStruct(xa.shape, F32), jax.ShapeDtypeStruct(xb.shape, F32)),
        grid_spec=grid_spec,
        compiler_params=_cparams(("arbitrary",)),
        name="moe_combine_ln",
    )(tab["lstart"], tab["gstart"], tab["nchunk"], ys, xa, xb, meta_n, g, b)


def _moe(xa, xb, meta_t, p):
    n = xa.shape[0] + xb.shape[0]
    tt = 256
    while xa.shape[0] % tt or xb.shape[0] % tt:
        tt //= 2
    r_tot, lpos, tab = _dispatch(meta_t, n, tt)
    xs = _moe_sort(xa, xb, lpos, tab, tt, r_tot)
    ys = _moe_experts(xs, tab, p["w_gate"], p["w_up"], p["w_down"])
    meta_n = jnp.concatenate([lpos.astype(F32), meta_t[2:4], jnp.zeros((4, n), F32)], axis=0).T
    return _moe_combine(ys, xa, xb, meta_n, tab, p["ln2_g"], p["ln2_b"], tt)


def _largest_tile(n, cap, mult):
    best = None
    for d in range(mult, min(n, cap) + 1, mult):
        if n % d == 0:
            best = d
    assert best is not None, (n, cap, mult)
    return best


def _seq_tails(x, off, b, t, k, c0, c1):
    if b <= 8:
        return jnp.stack([lax.slice(x, (off + i * t + t - k, c0), (off + (i + 1) * t, c1)) for i in range(b)])
    return lax.slice(x, (off, c0), (off + b * t, c1)).reshape(b, t, c1 - c0)[:, t - k:, :]


def _kv_rows(qkv, g, off, b, t, k):
    def pick(which):
        h0 = (which * N_GROUPS + g) * HEADS
        if b <= 8:
            a = jnp.stack([lax.slice(qkv, (h0, off + i * t + t - k, 0), (h0 + HEADS, off + (i + 1) * t, HEAD_DIM))
                           for i in range(b)], axis=1)
        else:
            a = lax.slice(qkv, (h0, off, 0), (h0 + HEADS, off + b * t, HEAD_DIM)).reshape(HEADS, b, t, HEAD_DIM)[:, :, t - k:]
        return jnp.transpose(a, (1, 2, 0, 3))
    return jnp.stack([pick(1), pick(2)], axis=2)


def _block_diag_gates(w_a, w_x):
    per = RNN_CHUNK // RNN_BLOCK
    chunks = []
    for c in range(N_RNN_CHUNKS):
        halves = []
        for w in (w_a, w_x):
            m = jnp.zeros((RNN_CHUNK, RNN_CHUNK), F32)
            for i in range(per):
                m = lax.dynamic_update_slice(m, w[c * per + i], (i * RNN_BLOCK, i * RNN_BLOCK))
            halves.append(m)
        chunks.append(jnp.concatenate(halves, axis=1))
    return jnp.stack(chunks).astype(BF16)


def kernel(x_prompt, x_sample, cache_kv_w128, cache_kv_w512, cache_kv_w2048, cache_mem_kv, state_h, state_conv, mem_prompt, w_in, b_gates, conv_w, conv_b, w_a, b_a, w_x, b_x, lru_lambda, w_br_lru, w_br_att, w_br_mem, w_o, w_mem_kv, ln1_g, ln1_b, w_rg, b_rg, w_re, b_re, w_gate, w_up, w_down, ln2_g, ln2_b):
    row = lambda v: v.reshape(1, -1).astype(F32)
    w_router = jnp.zeros((ROUTER_ROWS, D_MODEL), F32)
    w_router = w_router.at[:N_EXPERT_GROUPS].set(w_rg.T).at[N_EXPERT_GROUPS:N_EXPERT_GROUPS + N_EXPERTS].set(w_re.T)
    b_router = jnp.zeros((ROUTER_ROWS, 1), F32)
    b_router = b_router.at[:N_EXPERT_GROUPS, 0].set(b_rg).at[N_EXPERT_GROUPS:N_EXPERT_GROUPS + N_EXPERTS, 0].set(b_re)
    p = dict(
        b_gates=row(b_gates), conv_w=conv_w, conv_b=row(conv_b), wax=_block_diag_gates(w_a, w_x),
        b_a=row(b_a), b_x=row(b_x), lam=row(lru_lambda),
        w_br_lru=w_br_lru.astype(BF16), w_br_att=w_br_att.astype(BF16), w_br_mem=w_br_mem.astype(BF16),
        w_o=w_o.astype(BF16), ln1_g=row(ln1_g), ln1_b=row(ln1_b), w_router=w_router, b_router=b_router,
        w_gate=w_gate.astype(BF16), w_up=w_up.astype(BF16), w_down=w_down.astype(BF16),
        ln2_g=row(ln2_g), ln2_b=row(ln2_b))

    bp, s, _ = x_prompt.shape
    bs, ts, _ = x_sample.shape
    n_p, n_s = bp * s, bs * ts
    xp2, xs2 = x_prompt.reshape(n_p, D_MODEL), x_sample.reshape(n_s, D_MODEL)

    tr = 512
    while n_p % tr or n_s % tr:
        tr //= 2
    xb = _cast_rows(xp2, xs2, BF16, tr)
    tm_a = _largest_tile(n_p + n_s, 2304, 16)
    xrg = _matmul(xb, w_in, 0, 2 * D_RNN, F32, tm_a, 768, "in_proj_rnn")
    qkv = _matmul(xb, w_in, COL_Q, N_QKV_HEADS * HEAD_DIM, F32, tm_a, 768, "in_proj_qkv", split=6)
    qm = _matmul(xb, w_in, COL_QM, D_MEM, F32, tm_a, 512, "in_proj_qm", split=2)
    gates = _matmul(xb, w_in, COL_GATES, N_GATE_COLS, BF16, tm_a, 512, "gate_proj", bias=p["b_gates"])
    lru = (p["conv_w"], p["conv_b"], p["wax"], p["b_a"], p["b_x"], p["lam"])
    branch_w = (p["w_br_lru"], p["w_br_att"], p["w_br_mem"])
    ln1 = (p["w_o"], p["ln1_g"], p["ln1_b"], p["w_router"], p["b_router"])

    mem_rows = mem_prompt.reshape(bp * N_MEM, D_MODEL)
    mem_kv_p = _matmul(mem_rows, w_mem_kv, 0, 2 * D_MEM, F32, _tile(bp * N_MEM, 512), 512, "mem_kv_proj")
    hs_p, h_p = _rglru(xrg, 0, bp, s, jnp.zeros((bp, CONV_W - 1, D_RNN), F32), jnp.zeros((bp, D_RNN), F32), *lru,
                       _tile(s, 256))
    att_p = _attn_prompt(qkv, bp, s)
    mem_p = _mem_attn(qm, mem_kv_p.reshape(bp, N_MEM, 2 * D_MEM), bp, s, _tile(s, 1024), BF16)
    merged_p = _branch_merge(hs_p, xrg, att_p, mem_p, gates, 0, *branch_w, _tile(n_p, 256))
    x1_p, meta_p = _proj_ln(merged_p, xp2, *ln1, _tile(n_p, 512))

    hs_s, h_s = _rglru(xrg, n_p, bs, ts, state_conv, state_h, *lru, ts)
    att_s = _attn_sample(qkv, (cache_kv_w128, cache_kv_w512, cache_kv_w2048), bs, ts, n_p)
    mem_s = _mem_attn_cache(qm, cache_mem_kv, bs, ts, n_p)
    tm_s = _tile(n_s, 256)
    assert n_p % tm_s == 0
    merged_s = _branch_merge(hs_s, xrg, att_s, mem_s, gates, n_p, *branch_w, tm_s)
    x1_s, meta_s = _proj_ln(merged_s, xs2, *ln1, _tile(n_s, 512))

    y_p, y_s = _moe(x1_p, x1_s, jnp.concatenate([meta_p, meta_s], axis=1), p)

    kc = CONV_W - 1
    conv_p = jnp.concatenate([jnp.zeros((bp, kc, D_RNN), F32), _seq_tails(xrg, 0, bp, s, min(kc, s), 0, D_RNN)], axis=1)[:, -kc:]
    conv_s = jnp.concatenate([state_conv, _seq_tails(xrg, n_p, bs, ts, min(kc, ts), 0, D_RNN)], axis=1)[:, -kc:]
    kv_p = [_kv_rows(qkv, g, 0, bp, s, min(KEYS * d, s)) for g, d in enumerate(DILATIONS)]
    kv_s = [_kv_rows(qkv, g, n_p, bs, ts, ts) for g in range(N_GROUPS)]
    return (y_p.reshape(bp, s, D_MODEL), y_s.reshape(bs, ts, D_MODEL), kv_p[0], kv_p[1], kv_p[2],
            mem_kv_p.reshape(bp, N_MEM, 2, MEM_HEADS, MEM_HEAD_DIM), h_p.reshape(bp, D_RNN), conv_p,
            kv_s[0], kv_s[1], kv_s[2], h_s.reshape(bs, D_RNN), conv_s)
```

```python
import functools

import jax
import jax.numpy as jnp
from jax import lax
from jax.experimental import pallas as pl
from jax.experimental.pallas import tpu as pltpu

F32 = jnp.float32
BF16 = jnp.bfloat16

D_MODEL = 2048
D_RNN = 1536
N_RNN_BLOCKS = 16
RNN_BLOCK = D_RNN // N_RNN_BLOCKS
RNN_CHUNK = 384
N_RNN_CHUNKS = D_RNN // RNN_CHUNK
CONV_W = 4
LRU_C = 8.0
HEAD_DIM = 128
HEADS = 4
DILATIONS = (1, 4, 16)
KEYS = 128
N_GROUPS = 3
D_ATT_OUT = HEADS * HEAD_DIM
N_QKV_HEADS = 3 * N_GROUPS * HEADS
ATT_SCALE = HEAD_DIM ** -0.5
ATT_TILE = KEYS * max(DILATIONS)
N_MEM = 256
MEM_HEADS = 4
MEM_HEAD_DIM = 256
D_MEM = MEM_HEADS * MEM_HEAD_DIM
MEM_SCALE = MEM_HEAD_DIM ** -0.5
N_EXPERT_GROUPS = 4
EXPERTS_PER_GROUP = 4
N_EXPERTS = 16
D_EXPERT = 512
MOE_TILE = 512
DN_ALPHA = 2.0 ** 0.25
LN_EPS = 1e-5
NEG_INF = -1e30

COL_Q = 2 * D_RNN
COL_QM = COL_Q + N_QKV_HEADS * HEAD_DIM
COL_GATES = COL_QM + D_MEM
N_GATE_COLS = 3 * D_MODEL
ROUTER_ROWS = 32

VMEM_LIMIT = 56 * 1024 * 1024


def _cparams(sem):
    return pltpu.CompilerParams(dimension_semantics=sem, vmem_limit_bytes=VMEM_LIMIT)


def _gelu(x):
    return 0.5 * x * (1.0 + jnp.tanh(0.7978845608028654 * (x + 0.044715 * (x * x * x))))


def _layer_norm(x, g, b):
    mu = jnp.mean(x, axis=-1, keepdims=True)
    xc = x - mu
    var = jnp.mean(xc * xc, axis=-1, keepdims=True)
    return xc * lax.rsqrt(var + LN_EPS) * g + b


def _dot_nt(a, b):
    return lax.dot_general(a, b, (((1,), (1,)), ((), ())), preferred_element_type=F32)


def _tile(m, cap):
    t = min(m, cap)
    assert m % t == 0, (m, t)
    return t


def _mm_body(x_ref, w_ref, *rest, split, gate):
    if gate:
        b_ref, o_ref = rest
        w = w_ref[...].astype(BF16)
        rows = x_ref.shape[0]
        half = rows // 2 if rows % 32 == 0 else rows
        for r0 in range(0, rows, half):
            acc = jnp.dot(x_ref[r0:r0 + half, :].astype(BF16), w, preferred_element_type=F32)
            acc = 0.5 * jnp.tanh(0.5 * (acc + b_ref[...])) + 0.5
            o_ref[r0:r0 + half, :] = acc.astype(o_ref.dtype)
        return
    (o_ref,) = rest
    acc = jnp.dot(x_ref[...].astype(BF16), w_ref[...].astype(BF16), preferred_element_type=F32)
    if split == 1:
        o_ref[...] = acc.astype(o_ref.dtype)
    else:
        w = acc.shape[1] // split
        for s in range(split):
            o_ref[s] = acc[:, s * w:(s + 1) * w].astype(o_ref.dtype)


def _matmul(x, w, col_off, n_cols, out_dtype, tm, tn, name, bias=None, split=1):
    m, k = x.shape
    cb = col_off // tn
    in_specs = [pl.BlockSpec((tm, k), lambda i, j: (i, 0)),
                pl.BlockSpec((k, tn), lambda i, j: (0, j + cb))]
    args = [x, w]
    if bias is not None:
        in_specs.append(pl.BlockSpec((1, tn), lambda i, j: (0, j)))
        args.append(bias)
    if split == 1:
        out_shape = jax.ShapeDtypeStruct((m, n_cols), out_dtype)
        out_spec = pl.BlockSpec((tm, tn), lambda i, j: (i, j))
    else:
        out_shape = jax.ShapeDtypeStruct((n_cols * split // tn, m, tn // split), out_dtype)
        out_spec = pl.BlockSpec((split, tm, tn // split), lambda i, j: (j, i, 0))
    return pl.pallas_call(
        functools.partial(_mm_body, split=split, gate=bias is not None),
        out_shape=out_shape,
        grid=(m // tm, n_cols // tn),
        in_specs=in_specs,
        out_specs=out_spec,
        compiler_params=_cparams(("parallel", "arbitrary")),
        name=name,
    )(*args)


def _cast_rows_body(a_ref, b_ref, o_ref, *, n_ta):
    o_ref[...] = jnp.where(pl.program_id(0) < n_ta, a_ref[...], b_ref[...]).astype(o_ref.dtype)


def _cast_rows(xa, xb, dtype, tr):
    n_ta, n_tb = xa.shape[0] // tr, xb.shape[0] // tr
    cols = xa.shape[1]
    return pl.pallas_call(
        functools.partial(_cast_rows_body, n_ta=n_ta),
        out_shape=jax.ShapeDtypeStruct((xa.shape[0] + xb.shape[0], cols), dtype),
        grid=(n_ta + n_tb,),
        in_specs=[pl.BlockSpec((tr, cols), lambda i: (jnp.minimum(i, n_ta - 1), 0)),
                  pl.BlockSpec((tr, cols), lambda i: (jnp.maximum(i - n_ta, 0), 0))],
        out_specs=pl.BlockSpec((tr, cols), lambda i: (i, 0)),
        compiler_params=_cparams(("parallel",)),
        name="cast_rows",
    )(xa, xb)


def _rglru_body(xr_ref, cbuf_ref, h0_ref, cw_ref, cb_ref, wax_ref, ba_ref, bx_ref, lam_ref,
                out_ref, hl_ref, ext_s, a_s, u_s, h_s, *, tt):
    t = pl.program_id(1)

    @pl.when(t == 0)
    def _():
        ext_s[0:8, :] = jnp.zeros((8, D_RNN), F32)
        ext_s[5:8, :] = cbuf_ref[0]
        h_s[...] = jnp.broadcast_to(h0_ref[0], (8, D_RNN))

    @pl.when(t > 0)
    def _():
        ext_s[0:8, :] = ext_s[tt:tt + 8, :]

    ext_s[8:8 + tt, :] = xr_ref[...]
    cw = cw_ref[...]
    xc = (cb_ref[...] + cw[3:4, :] * ext_s[8:8 + tt, :] + cw[2:3, :] * ext_s[7:7 + tt, :]
          + cw[1:2, :] * ext_s[6:6 + tt, :] + cw[0:1, :] * ext_s[5:5 + tt, :])
    xcb = xc.astype(BF16)
    r_parts, i_parts = [], []
    for c in range(N_RNN_CHUNKS):
        g = jnp.dot(xcb[:, c * RNN_CHUNK:(c + 1) * RNN_CHUNK], wax_ref[c], preferred_element_type=F32)
        r_parts.append(g[:, :RNN_CHUNK])
        i_parts.append(g[:, RNN_CHUNK:])
    sigmoid = lambda z: 0.5 * jnp.tanh(0.5 * z) + 0.5
    r = sigmoid(jnp.concatenate(r_parts, axis=1) + ba_ref[...])
    gi = sigmoid(jnp.concatenate(i_parts, axis=1) + bx_ref[...])
    nl = -lam_ref[...]
    softplus = jnp.maximum(nl, 0.0) + jnp.log1p(jnp.exp(-jnp.abs(nl)))
    log_a = (-LRU_C) * r * softplus
    th = jnp.tanh(log_a)
    a_s[...] = jnp.exp(log_a)
    u_s[...] = jnp.sqrt(-2.0 * th / (1.0 - th)) * (gi * xc)

    rows = lax.broadcasted_iota(jnp.int32, (8, D_RNN), 0)

    def blk(i, h):
        r0 = pl.multiple_of(i * 8, 8)
        ab = a_s[pl.ds(r0, 8), :]
        ub = u_s[pl.ds(r0, 8), :]
        for s in (1, 2, 4):
            keep = rows >= s
            ub = ab * jnp.where(keep, pltpu.roll(ub, s, 0), 0.0) + ub
            ab = ab * jnp.where(keep, pltpu.roll(ab, s, 0), 1.0)
        hb = ab * h + ub
        u_s[pl.ds(r0, 8), :] = hb
        return jnp.broadcast_to(hb[7:8, :], (8, D_RNN))

    h_fin = lax.fori_loop(0, tt // 8, blk, h_s[...])
    h_s[...] = h_fin
    hl_ref[0] = h_fin[0:1, :]
    out_ref[...] = u_s[...]


def _rglru(xrg, off, b, t, conv_buf, h0, conv_w, conv_b, wax, b_a, b_x, lam, tt):
    nt = t // tt
    ob = off // tt
    vec = lambda: pl.BlockSpec((1, D_RNN), lambda i, j: (0, 0))
    return pl.pallas_call(
        functools.partial(_rglru_body, tt=tt),
        out_shape=(jax.ShapeDtypeStruct((b * t, D_RNN), F32), jax.ShapeDtypeStruct((b, 1, D_RNN), F32)),
        grid=(b, nt),
        in_specs=[pl.BlockSpec((tt, D_RNN), lambda i, j: (ob + i * nt + j, 0)),
                  pl.BlockSpec((1, CONV_W - 1, D_RNN), lambda i, j: (i, 0, 0)),
                  pl.BlockSpec((1, 1, D_RNN), lambda i, j: (i, 0, 0)),
                  pl.BlockSpec((CONV_W, D_RNN), lambda i, j: (0, 0)),
                  vec(),
                  pl.BlockSpec((N_RNN_CHUNKS, RNN_CHUNK, 2 * RNN_CHUNK), lambda i, j: (0, 0, 0)),
                  vec(), vec(), vec()],
        out_specs=(pl.BlockSpec((tt, D_RNN), lambda i, j: (i * nt + j, 0)),
                   pl.BlockSpec((1, 1, D_RNN), lambda i, j: (i, 0, 0))),
        scratch_shapes=[pltpu.VMEM((tt + 8, D_RNN), F32), pltpu.VMEM((tt, D_RNN), F32),
                        pltpu.VMEM((tt, D_RNN), F32), pltpu.VMEM((8, D_RNN), F32)],
        compiler_params=_cparams(("parallel", "arbitrary")),
        name="rglru",
    )(xrg, conv_buf, h0.reshape(b, 1, D_RNN), conv_w, conv_b, wax, b_a, b_x, lam)


def _band_block(q, k, v, bias):
    s = _dot_nt(q, k) * ATT_SCALE + bias
    m = jnp.max(s, axis=-1, keepdims=True)
    p = jnp.exp(s - m)
    l = jnp.sum(p, axis=-1, keepdims=True)
    o = jnp.dot(p.astype(BF16), v, preferred_element_type=F32) / l
    return o, m + jnp.log(l)


def _attn_prompt_body(*refs):
    q_refs, k_refs, v_refs, kh_refs, vh_refs = (refs[3 * i:3 * i + 3] for i in range(5))
    o_ref, og_s, lg_s = refs[15:]
    row = lax.broadcasted_iota(jnp.int32, (KEYS, 2 * KEYS), 0)
    col = lax.broadcasted_iota(jnp.int32, (KEYS, 2 * KEYS), 1)
    band = jnp.logical_and(col >= row, col <= row + KEYS)
    bias = jnp.where(band, 0.0, NEG_INF)
    has_prev = pl.program_id(1) > 0
    bias_first = jnp.where(jnp.logical_and(band, jnp.logical_or(col >= KEYS, has_prev)), 0.0, NEG_INF)

    def put(g, rows, o, lse):
        og_s[g, rows, :] = o
        lg_s[g, rows, :] = jnp.broadcast_to(lse, (KEYS, HEAD_DIM))

    def rows_of(start, size, d):
        return pl.ds(start, size, stride=d) if d > 1 else pl.ds(start, size)

    for g, d in enumerate(DILATIONS):
        q_ref, k_ref, v_ref, kh_ref, vh_ref = q_refs[g], k_refs[g], v_refs[g], kh_refs[g], vh_refs[g]
        nqb = ATT_TILE // (d * KEYS)
        for r in range(d):
            own = rows_of(r, KEYS, d)
            kk = jnp.concatenate([kh_ref[0, own, :], k_ref[0, own, :]], axis=0).astype(BF16)
            vv = jnp.concatenate([vh_ref[0, own, :], v_ref[0, own, :]], axis=0).astype(BF16)
            o, lse = _band_block(q_ref[0, own, :].astype(BF16), kk, vv, bias_first)
            put(g, own, o, lse)
            for qb in range(1, nqb):
                rows = rows_of(qb * KEYS * d + r, KEYS, d)
                keys = rows_of((qb - 1) * KEYS * d + r, 2 * KEYS, d)
                o, lse = _band_block(q_ref[0, rows, :].astype(BF16), k_ref[0, keys, :].astype(BF16),
                                     v_ref[0, keys, :].astype(BF16), bias)
                put(g, rows, o, lse)

    l0, l1, l2 = lg_s[0], lg_s[1], lg_s[2]
    m = jnp.maximum(jnp.maximum(l0, l1), l2)
    e0, e1, e2 = jnp.exp(l0 - m), jnp.exp(l1 - m), jnp.exp(l2 - m)
    o_ref[...] = ((e0 * og_s[0] + e1 * og_s[1] + e2 * og_s[2]) / (e0 + e1 + e2)).astype(o_ref.dtype)


def _attn_prompt(qkv, b, s):
    assert s % ATT_TILE == 0
    nt = s // ATT_TILE
    n = b * s

    def cur(which, g):
        return pl.BlockSpec((1, ATT_TILE, HEAD_DIM),
                            lambda i, j, h: ((which * N_GROUPS + g) * HEADS + h, i * nt + j, 0))

    def halo(which, g):
        rows = KEYS * DILATIONS[g]
        per = ATT_TILE // rows
        return pl.BlockSpec((1, rows, HEAD_DIM),
                            lambda i, j, h: ((which * N_GROUPS + g) * HEADS + h,
                                             jnp.maximum((i * nt + j) * per - 1, 0), 0))

    in_specs = ([cur(0, g) for g in range(N_GROUPS)] + [cur(1, g) for g in range(N_GROUPS)]
                + [cur(2, g) for g in range(N_GROUPS)] + [halo(1, g) for g in range(N_GROUPS)]
                + [halo(2, g) for g in range(N_GROUPS)])
    return pl.pallas_call(
        _attn_prompt_body,
        out_shape=jax.ShapeDtypeStruct((n, D_ATT_OUT), BF16),
        grid=(b, nt, HEADS),
        in_specs=in_specs,
        out_specs=pl.BlockSpec((ATT_TILE, HEAD_DIM), lambda i, j, h: (i * nt + j, h)),
        scratch_shapes=[pltpu.VMEM((N_GROUPS, ATT_TILE, HEAD_DIM), F32),
                        pltpu.VMEM((N_GROUPS, ATT_TILE, HEAD_DIM), F32)],
        compiler_params=_cparams(("parallel", "parallel", "parallel")),
        name="attn_prompt",
    )(*([qkv] * 15))


def _attn_sample_group(q4, kn4, vn4, c_ref, d, t_new):
    half = len(c_ref.shape) == 4
    wb = c_ref.shape[1] * 16 if half else c_ref.shape[1] // (2 * HEADS)
    nk = wb // 2 if half else wb
    nr = HEADS * t_new
    zeros = jnp.zeros((t_new, HEAD_DIM), F32)
    qbd = jnp.concatenate(
        [jnp.concatenate([q4[h] if hh == h else zeros for hh in range(HEADS)], axis=1) for h in range(HEADS)],
        axis=0).astype(BF16)
    kn = jnp.concatenate([kn4[h] for h in range(HEADS)], axis=1).astype(BF16)
    vn = jnp.concatenate([vn4[h] for h in range(HEADS)], axis=1).astype(BF16)
    if half:
        cache_rows = lambda kv, h: c_ref[0, :, pl.ds(kv * HEADS + h, 8, stride=2 * HEADS), :].reshape(nk, HEAD_DIM)
    else:
        cache_rows = lambda kv, h: c_ref[0, pl.ds(kv * HEADS + h, wb, stride=2 * HEADS), :]
    kc = jnp.concatenate([cache_rows(0, h) for h in range(HEADS)], axis=1).astype(BF16)
    vc = jnp.concatenate([cache_rows(1, h) for h in range(HEADS)], axis=1).astype(BF16)
    tq_c = lax.broadcasted_iota(jnp.int32, (nr, nk), 0) & (t_new - 1)
    e_c = lax.broadcasted_iota(jnp.int32, (nr, nk), 1)
    if half:
        e_c = ((e_c >> 3) << 4) + (e_c & 7)
    dist_c = wb + tq_c - e_c
    ok_c = jnp.logical_and((dist_c & (d - 1)) == 0, dist_c <= KEYS * d)
    tq_n = lax.broadcasted_iota(jnp.int32, (nr, t_new), 0) & (t_new - 1)
    dist_n = tq_n - lax.broadcasted_iota(jnp.int32, (nr, t_new), 1)
    ok_n = jnp.logical_and(jnp.logical_and(dist_n >= 0, (dist_n & (d - 1)) == 0), dist_n <= KEYS * d)
    s_c = jnp.where(ok_c, _dot_nt(qbd, kc) * ATT_SCALE, NEG_INF)
    s_n = jnp.where(ok_n, _dot_nt(qbd, kn) * ATT_SCALE, NEG_INF)
    m = jnp.maximum(jnp.max(s_c, axis=-1, keepdims=True), jnp.max(s_n, axis=-1, keepdims=True))
    p_c = jnp.exp(s_c - m)
    p_n = jnp.exp(s_n - m)
    l = jnp.sum(p_c, axis=-1, keepdims=True) + jnp.sum(p_n, axis=-1, keepdims=True)
    o = (jnp.dot(p_c.astype(BF16), vc, preferred_element_type=F32)
         + jnp.dot(p_n.astype(BF16), vn, preferred_element_type=F32)) / l
    lse = m + jnp.log(l)
    o = jnp.concatenate(
        [o[h * t_new:(h + 1) * t_new, h * HEAD_DIM:(h + 1) * HEAD_DIM] for h in range(HEADS)], axis=1)
    lse = jnp.concatenate(
        [jnp.broadcast_to(lse[h * t_new:(h + 1) * t_new], (t_new, HEAD_DIM)) for h in range(HEADS)], axis=1)
    return o, lse


def _attn_sample_body(*refs, t_new):
    q_refs, k_refs, v_refs, c_refs = (refs[3 * i:3 * i + 3] for i in range(4))
    o_ref = refs[12]
    outs = [_attn_sample_group(q_refs[g][...], k_refs[g][...], v_refs[g][...], c_refs[g], d, t_new)
            for g, d in enumerate(DILATIONS)]
    (o0, l0), (o1, l1), (o2, l2) = outs
    m = jnp.maximum(jnp.maximum(l0, l1), l2)
    e0, e1, e2 = jnp.exp(l0 - m), jnp.exp(l1 - m), jnp.exp(l2 - m)
    o_ref[0] = (e0 * o0 + e1 * o1 + e2 * o2) / (e0 + e1 + e2)


def _attn_sample(qkv, caches, b, t_new, off):
    assert t_new & (t_new - 1) == 0 and off % t_new == 0
    ob = off // t_new
    new = lambda which, g: pl.BlockSpec((HEADS, t_new, HEAD_DIM), lambda i: (which * N_GROUPS + g, ob + i, 0))
    caches2, cache_specs = [], []
    for c, d in zip(caches, DILATIONS):
        wb = c.shape[1]
        if d == 16 and wb % 16 == 0 and t_new <= 8:
            caches2.append(c.reshape(b, wb // 16, 16 * 2 * HEADS, HEAD_DIM))
            cache_specs.append(pl.BlockSpec((1, wb // 16, 8 * 2 * HEADS, HEAD_DIM), lambda i: (i, 0, 0, 0)))
        else:
            caches2.append(c.reshape(b, wb * 2 * HEADS, HEAD_DIM))
            cache_specs.append(pl.BlockSpec((1, wb * 2 * HEADS, HEAD_DIM), lambda i: (i, 0, 0)))
    in_specs = ([new(0, g) for g in range(N_GROUPS)] + [new(1, g) for g in range(N_GROUPS)]
                + [new(2, g) for g in range(N_GROUPS)] + cache_specs)
    att = pl.pallas_call(
        functools.partial(_attn_sample_body, t_new=t_new),
        out_shape=jax.ShapeDtypeStruct((b, t_new, D_ATT_OUT), F32),
        grid=(b,),
        in_specs=in_specs,
        out_specs=pl.BlockSpec((1, t_new, D_ATT_OUT), lambda i: (i, 0, 0)),
        compiler_params=_cparams(("parallel",)),
        name="attn_sample",
    )(*([qkv] * 9), *caches2)
    return att.reshape(b * t_new, D_ATT_OUT)


def _mem_q_specs(rows, index):
    first = N_QKV_HEADS // 2
    return [pl.BlockSpec((2, rows, HEAD_DIM), functools.partial(index, first + h)) for h in range(MEM_HEADS)]


def _mem_attn_body(q0_ref, q1_ref, q2_ref, q3_ref, kv_ref, o_ref):
    for h, q_ref in enumerate((q0_ref, q1_ref, q2_ref, q3_ref)):
        k = kv_ref[0, :, h * MEM_HEAD_DIM:(h + 1) * MEM_HEAD_DIM].astype(BF16)
        v = kv_ref[0, :, D_MEM + h * MEM_HEAD_DIM:D_MEM + (h + 1) * MEM_HEAD_DIM].astype(BF16)
        q = jnp.concatenate([q_ref[0], q_ref[1]], axis=1).astype(BF16)
        s = _dot_nt(q, k) * MEM_SCALE
        m = jnp.max(s, axis=-1, keepdims=True)
        p = jnp.exp(s - m)
        l = jnp.sum(p, axis=-1, keepdims=True)
        o = jnp.dot(p.astype(BF16), v, preferred_element_type=F32) / l
        o_ref[:, h * MEM_HEAD_DIM:(h + 1) * MEM_HEAD_DIM] = o.astype(o_ref.dtype)


def _mem_attn(qkvm, mem_kv, b, t, tm, out_dtype):
    nt = t // tm
    return pl.pallas_call(
        _mem_attn_body,
        out_shape=jax.ShapeDtypeStruct((b * t, D_MEM), out_dtype),
        grid=(b, nt),
        in_specs=_mem_q_specs(tm, lambda blk, i, j: (blk, i * nt + j, 0))
        + [pl.BlockSpec((1, N_MEM, 2 * D_MEM), lambda i, j: (i, 0, 0))],
        out_specs=pl.BlockSpec((tm, D_MEM), lambda i, j: (i * nt + j, 0)),
        compiler_params=_cparams(("parallel", "parallel")),
        name="mem_attn",
    )(qkvm, qkvm, qkvm, qkvm, mem_kv)


def _mem_attn_cache_body(q0_ref, q1_ref, q2_ref, q3_ref, c_ref, o_ref):
    t = q0_ref.shape[1]
    n_rows = N_MEM * 2 * MEM_HEADS
    flat = c_ref[0].reshape(n_rows, MEM_HEAD_DIM).astype(BF16)
    q = jnp.concatenate([jnp.concatenate([q_ref[0], q_ref[1]], axis=1) for q_ref in (q0_ref, q1_ref, q2_ref, q3_ref)],
                        axis=0).astype(BF16)
    assert t & (t - 1) == 0
    head = lax.broadcasted_iota(jnp.int32, (MEM_HEADS * t, n_rows), 0) >> (t.bit_length() - 1)
    col = lax.broadcasted_iota(jnp.int32, (MEM_HEADS * t, n_rows), 1)
    s = jnp.where((col & (2 * MEM_HEADS - 1)) == head, _dot_nt(q, flat) * MEM_SCALE, NEG_INF)
    m = jnp.max(s, axis=-1, keepdims=True)
    p = jnp.exp(s - m)
    l = jnp.sum(p, axis=-1, keepdims=True)
    o = jnp.dot(pltpu.roll(p, MEM_HEADS, 1).astype(BF16), flat, preferred_element_type=F32) / l
    o_ref[...] = jnp.concatenate([o[h * t:(h + 1) * t, :] for h in range(MEM_HEADS)], axis=1)


def _mem_attn_cache(qkvm, cache, b, t, off):
    ob = off // t
    return pl.pallas_call(
        _mem_attn_cache_body,
        out_shape=jax.ShapeDtypeStruct((b * t, D_MEM), F32),
        grid=(b,),
        in_specs=_mem_q_specs(t, lambda blk, i: (blk, ob + i, 0))
        + [pl.BlockSpec((1, N_MEM, 2, MEM_HEADS, MEM_HEAD_DIM), lambda i: (i, 0, 0, 0, 0))],
        out_specs=pl.BlockSpec((t, D_MEM), lambda i: (i, 0)),
        compiler_params=_cparams(("parallel",)),
        name="mem_attn_cache",
    )(qkvm, qkvm, qkvm, qkvm, cache)


def _branch_body(hs_ref, xg_ref, att_ref, mem_ref, gt_ref, wl_ref, wa_ref, wm_ref, out_ref):
    a_lru = (hs_ref[...] * _gelu(xg_ref[...])).astype(BF16)
    acc = gt_ref[:, 0:D_MODEL].astype(F32) * jnp.dot(a_lru, wl_ref[...], preferred_element_type=F32)
    acc = acc + gt_ref[:, D_MODEL:2 * D_MODEL].astype(F32) * jnp.dot(att_ref[...].astype(BF16), wa_ref[...], preferred_element_type=F32)
    acc = acc + gt_ref[:, 2 * D_MODEL:3 * D_MODEL].astype(F32) * jnp.dot(mem_ref[...].astype(BF16), wm_ref[...], preferred_element_type=F32)
    out_ref[...] = acc.astype(out_ref.dtype)


def _branch_merge(hs, xrg, att, mem, gates, goff, wl, wa, wm, tm):
    n = hs.shape[0]
    gb = goff // tm
    row = lambda w: pl.BlockSpec((tm, w), lambda i: (i, 0))
    full = lambda a: pl.BlockSpec(a.shape, lambda i: (0, 0))
    return pl.pallas_call(
        _branch_body,
        out_shape=jax.ShapeDtypeStruct((n, D_MODEL), BF16),
        grid=(n // tm,),
        in_specs=[row(D_RNN), pl.BlockSpec((tm, D_RNN), lambda i: (gb + i, 1)), row(D_ATT_OUT), row(D_MEM),
                  pl.BlockSpec((tm, N_GATE_COLS), lambda i: (gb + i, 0)), full(wl), full(wa), full(wm)],
        out_specs=row(D_MODEL),
        compiler_params=_cparams(("parallel",)),
        name="branch_merge",
    )(hs, xrg, att, mem, gates, wl, wa, wm)


def _split_bf16(x):
    hi = x.astype(BF16)
    return hi, (x - hi.astype(F32)).astype(BF16)


def _route_rows(lg):
    g = [lg[i:i + 1, :] for i in range(N_EXPERT_GROUPS)]
    gmax = jnp.maximum(jnp.maximum(g[0], g[1]), jnp.maximum(g[2], g[3]))
    gidx = jnp.where(g[0] == gmax, 0.0, jnp.where(g[1] == gmax, 1.0, jnp.where(g[2] == gmax, 2.0, 3.0)))
    g_p = 1.0 / (jnp.exp(g[0] - gmax) + jnp.exp(g[1] - gmax) + jnp.exp(g[2] - gmax) + jnp.exp(g[3] - gmax))
    e = []
    for k in range(EXPERTS_PER_GROUP):
        rows = [lg[N_EXPERT_GROUPS + gg * EXPERTS_PER_GROUP + k:N_EXPERT_GROUPS + gg * EXPERTS_PER_GROUP + k + 1, :]
                for gg in range(N_EXPERT_GROUPS)]
        e.append(jnp.where(gidx == 0.0, rows[0], jnp.where(gidx == 1.0, rows[1], jnp.where(gidx == 2.0, rows[2], rows[3]))))

    def first_argmax(v):
        mx = jnp.maximum(jnp.maximum(v[0], v[1]), jnp.maximum(v[2], v[3]))
        ix = jnp.where(v[0] == mx, 0.0, jnp.where(v[1] == mx, 1.0, jnp.where(v[2] == mx, 2.0, 3.0)))
        return mx, ix

    v1, i1 = first_argmax(e)
    v2, i2 = first_argmax([jnp.where(i1 == float(k), -jnp.inf, e[k]) for k in range(EXPERTS_PER_GROUP)])
    ex = jnp.exp(v2 - v1)
    w1 = g_p / (1.0 + ex)
    w2 = g_p * ex / (1.0 + ex)
    base = gidx * float(EXPERTS_PER_GROUP)
    zero = jnp.zeros_like(w1)
    return jnp.concatenate([base + i1, base + i2, w1, w2, zero, zero, zero, zero], axis=0)


def _proj_ln_body(mg_ref, x_ref, wo_ref, g_ref, b_ref, wr_ref, br_ref, x1_ref, meta_ref, mix_s):
    @pl.when(pl.program_id(0) == 0)
    def _():
        mix_s[...] = jnp.zeros(mix_s.shape, F32)

    wh, wl = _split_bf16(wr_ref[...])
    tm = x_ref.shape[0]
    sub = min(256, tm)
    for r0 in range(0, tm, sub):
        rs = pl.ds(r0, sub)
        x1 = _layer_norm(DN_ALPHA * x_ref[rs, :] + mix_s[rs, :], g_ref[...], b_ref[...])
        x1_ref[rs, :] = x1
        xh, xl = _split_bf16(x1)
        lg = _dot_nt(wh, xh) + (_dot_nt(wh, xl) + _dot_nt(wl, xh)) + br_ref[...]
        meta_ref[:, rs] = _route_rows(lg)
    mix_s[...] = jnp.dot(mg_ref[...], wo_ref[...], preferred_element_type=F32)


def _proj_ln(merged, x, wo, g, b, wr, br, tm):
    n = merged.shape[0]
    nb = n // tm
    prev = lambda w: pl.BlockSpec((tm, w), lambda i: (jnp.maximum(i - 1, 0), 0))
    full = lambda a: pl.BlockSpec(a.shape, lambda i: (0, 0))
    return pl.pallas_call(
        _proj_ln_body,
        out_shape=(jax.ShapeDtypeStruct((n, D_MODEL), F32), jax.ShapeDtypeStruct((8, n), F32)),
        grid=(nb + 1,),
        in_specs=[pl.BlockSpec((tm, D_MODEL), lambda i: (jnp.minimum(i, nb - 1), 0)), prev(D_MODEL),
                  full(wo), full(g), full(b), full(wr), full(br)],
        out_specs=(prev(D_MODEL), pl.BlockSpec((8, tm), lambda i: (0, jnp.maximum(i - 1, 0)))),
        scratch_shapes=[pltpu.VMEM((tm, D_MODEL), F32)],
        compiler_params=_cparams(("arbitrary",)),
        name="proj_ln_router",
    )(merged, x, wo, g, b, wr, br)


MOE_CHUNK = 8


def _local_rows(tt):
    return -(-(2 * tt + N_EXPERTS * (MOE_CHUNK - 1)) // 128) * 128


def _dispatch(meta_t, n, tt):
    n_t = n // tt
    ids = meta_t[0:2].astype(jnp.int32)
    onehot = (ids[:, :, None] == jnp.arange(N_EXPERTS, dtype=jnp.int32)).astype(jnp.int32).reshape(2, n_t, tt, N_EXPERTS)
    cnt_slot = jnp.sum(onehot, axis=2)
    cnt = cnt_slot[0] + cnt_slot[1]
    pc = (cnt + MOE_CHUNK - 1) // MOE_CHUNK * MOE_CHUNK
    lstart = jnp.cumsum(pc, axis=1) - pc
    tri = (jnp.arange(tt)[:, None] >= jnp.arange(tt)[None, :]).astype(F32)
    csum = jnp.einsum("ut,snte->snue", tri, onehot.astype(F32)).astype(jnp.int32)
    rank = csum - onehot + jnp.stack([jnp.zeros_like(cnt), cnt_slot[0]])[:, :, None, :]
    lpos = jnp.sum(onehot * (lstart[None, :, None, :] + rank), axis=-1).reshape(2, n)
    seg = jnp.sum(pc, axis=0)
    pe = (seg + MOE_TILE - 1) // MOE_TILE * MOE_TILE
    ends = jnp.cumsum(pe)
    base = ends - pe
    gstart = base[None, :] + jnp.cumsum(pc, axis=0) - pc
    r_tot = -(-(2 * n + N_EXPERTS * (MOE_CHUNK - 1) * n_t) // MOE_TILE) * MOE_TILE + N_EXPERTS * MOE_TILE
    n_tiles = r_tot // MOE_TILE
    tile_start = jnp.arange(n_tiles, dtype=jnp.int32) * MOE_TILE
    tile_expert = jnp.minimum(jnp.sum((tile_start[:, None] >= ends[None, :]).astype(jnp.int32), axis=1), N_EXPERTS - 1)
    tables = dict(
        lstart=lstart.reshape(-1), gstart=gstart.reshape(-1), nchunk=(pc // MOE_CHUNK).reshape(-1),
        zstart=jnp.concatenate([base + seg, ends[-1:]]),
        zcount=jnp.concatenate([(pe - seg) // MOE_CHUNK, (r_tot - ends[-1:]) // MOE_TILE]),
        tile_expert=tile_expert, n_used=(ends[-1] // MOE_TILE).reshape(1))
    return r_tot, lpos, tables


def _seg_loop(tab, t, fn):
    lstart_ref, gstart_ref, nchunk_ref = tab
    for e in range(N_EXPERTS):
        ls = lstart_ref[t * N_EXPERTS + e]
        gs = gstart_ref[t * N_EXPERTS + e]

        def body(j, c, ls=ls, gs=gs):
            fn(pl.multiple_of(ls + j * MOE_CHUNK, MOE_CHUNK), pl.multiple_of(gs + j * MOE_CHUNK, MOE_CHUNK))
            return c
        lax.fori_loop(0, nchunk_ref[t * N_EXPERTS + e], body, 0)


def _n_chunks(nchunk_ref, t):
    tot = nchunk_ref[t * N_EXPERTS]
    for e in range(1, N_EXPERTS):
        tot = tot + nchunk_ref[t * N_EXPERTS + e]
    return tot


def _sort_body(lstart_ref, gstart_ref, nchunk_ref, zstart_ref, zcount_ref, lpos_ref, xa_ref, xb_ref, xs_hbm,
               xloc, zbuf, sem, zsem, *, n_t, n_ta):
    t = pl.program_id(0)
    x_tile = jnp.where(t < n_ta, xa_ref[...], xb_ref[...]).astype(BF16)
    slot = lax.rem(t, 2)
    tab = (lstart_ref, gstart_ref, nchunk_ref)
    rows = xloc.shape[1]

    def chunk_copy(s, lrow, grow):
        return pltpu.make_async_copy(xloc.at[s, pl.ds(lrow, MOE_CHUNK)], xs_hbm.at[pl.ds(grow, MOE_CHUNK)], sem.at[s])

    def wait_tile(s, tile):
        def body(j, c):
            chunk_copy(s, 0, 0).wait()
            return c
        lax.fori_loop(0, _n_chunks(nchunk_ref, tile), body, 0)

    @pl.when(t == 0)
    def _():
        zbuf[...] = jnp.zeros_like(zbuf)
        zero_copy = lambda grow: pltpu.make_async_copy(
            zbuf.at[pl.ds(0, MOE_CHUNK)], xs_hbm.at[pl.ds(grow, MOE_CHUNK)], zsem.at[0])
        zero_tile = lambda grow: pltpu.make_async_copy(zbuf, xs_hbm.at[pl.ds(grow, MOE_TILE)], zsem.at[0])
        tail_start = zstart_ref[N_EXPERTS]
        for e in range(N_EXPERTS):
            def zb(j, c, e=e):
                zero_copy(pl.multiple_of(zstart_ref[e] + j * MOE_CHUNK, MOE_CHUNK)).start()
                return c
            lax.fori_loop(0, zcount_ref[e], zb, 0)

        def tb(j, c):
            zero_tile(pl.multiple_of(tail_start + j * MOE_TILE, MOE_TILE)).start()
            return c
        lax.fori_loop(0, zcount_ref[N_EXPERTS], tb, 0)
        for e in range(N_EXPERTS):
            def zw(j, c):
                zero_copy(0).wait()
                return c
            lax.fori_loop(0, zcount_ref[e], zw, 0)

        def tw(j, c):
            zero_tile(0).wait()
            return c
        lax.fori_loop(0, zcount_ref[N_EXPERTS], tw, 0)

    @pl.when(t >= 2)
    def _():
        wait_tile(slot, t - 2)

    l_iota = lax.broadcasted_iota(jnp.int32, (rows, xa_ref.shape[0]), 0)
    perm = jnp.logical_or(l_iota == lpos_ref[0:1, :], l_iota == lpos_ref[1:2, :])
    perm = jnp.where(perm, 1.0, 0.0).astype(BF16)
    xloc[slot] = jnp.dot(perm, x_tile, preferred_element_type=F32)
    _seg_loop(tab, t, lambda lrow, grow: chunk_copy(slot, lrow, grow).start())

    @pl.when(t == n_t - 1)
    def _():
        wait_tile(slot, t)
        if n_t >= 2:
            wait_tile(1 - slot, t - 1)


def _moe_sort(xa, xb, lpos, tab, tt, r_tot):
    n_ta, n_tb = xa.shape[0] // tt, xb.shape[0] // tt
    n_t = n_ta + n_tb
    rows = _local_rows(tt)
    grid_spec = pltpu.PrefetchScalarGridSpec(
        num_scalar_prefetch=5,
        grid=(n_t,),
        in_specs=[pl.BlockSpec((2, tt), lambda t, *_: (0, t)),
                  pl.BlockSpec((tt, D_MODEL), lambda t, *_: (jnp.minimum(t, n_ta - 1), 0)),
                  pl.BlockSpec((tt, D_MODEL), lambda t, *_: (jnp.maximum(t - n_ta, 0), 0))],
        out_specs=pl.BlockSpec(memory_space=pl.ANY),
        scratch_shapes=[pltpu.VMEM((2, rows, D_MODEL), F32), pltpu.VMEM((MOE_TILE, D_MODEL), F32),
                        pltpu.SemaphoreType.DMA((2,)), pltpu.SemaphoreType.DMA((1,))])
    return pl.pallas_call(
        functools.partial(_sort_body, n_t=n_t, n_ta=n_ta),
        out_shape=jax.ShapeDtypeStruct((r_tot, D_MODEL), F32),
        grid_spec=grid_spec,
        compiler_params=_cparams(("arbitrary",)),
        name="moe_sort",
    )(tab["lstart"], tab["gstart"], tab["nchunk"], tab["zstart"], tab["zcount"], lpos, xa, xb)


def _expert_body(te_ref, nu_ref, x_ref, wg_ref, wu_ref, wd_ref, o_ref, wg_s, wu_s, wd_s):
    i = pl.program_id(0)
    used = i < nu_ref[0]
    new_expert = jnp.logical_or(i == 0, te_ref[i] != te_ref[jnp.maximum(i - 1, 0)])

    @pl.when(jnp.logical_and(used, new_expert))
    def _():
        wg_s[...] = wg_ref[0].astype(BF16)
        wu_s[...] = wu_ref[0].astype(BF16)
        wd_s[...] = wd_ref[0].astype(BF16)

    @pl.when(used)
    def _():
        xb = x_ref[...].astype(BF16)
        hid = _gelu(jnp.dot(xb, wg_s[...], preferred_element_type=F32)) * jnp.dot(xb, wu_s[...], preferred_element_type=F32)
        o_ref[...] = jnp.dot(hid.astype(BF16), wd_s[...], preferred_element_type=F32)

    @pl.when(jnp.logical_not(used))
    def _():
        o_ref[...] = jnp.zeros_like(o_ref)


def _moe_experts(xs, tab, wg, wu, wd):
    r_tot = xs.shape[0]
    last = lambda i, nu: jnp.minimum(i, nu[0] - 1)
    grid_spec = pltpu.PrefetchScalarGridSpec(
        num_scalar_prefetch=2,
        grid=(r_tot // MOE_TILE,),
        in_specs=[pl.BlockSpec((MOE_TILE, D_MODEL), lambda i, te, nu: (last(i, nu), 0)),
                  pl.BlockSpec((1, D_MODEL, D_EXPERT), lambda i, te, nu: (te[last(i, nu)], 0, 0)),
                  pl.BlockSpec((1, D_MODEL, D_EXPERT), lambda i, te, nu: (te[last(i, nu)], 0, 0)),
                  pl.BlockSpec((1, D_EXPERT, D_MODEL), lambda i, te, nu: (te[last(i, nu)], 0, 0))],
        out_specs=pl.BlockSpec((MOE_TILE, D_MODEL), lambda i, te, nu: (i, 0)),
        scratch_shapes=[pltpu.VMEM((D_MODEL, D_EXPERT), BF16), pltpu.VMEM((D_MODEL, D_EXPERT), BF16),
                        pltpu.VMEM((D_EXPERT, D_MODEL), BF16)])
    return pl.pallas_call(
        _expert_body,
        out_shape=jax.ShapeDtypeStruct((r_tot, D_MODEL), F32),
        grid_spec=grid_spec,
        compiler_params=_cparams(("arbitrary",)),
        name="moe_experts",
    )(tab["tile_expert"], tab["n_used"], xs, wg, wu, wd)


def _combine_body(lstart_ref, gstart_ref, nchunk_ref, ys_hbm, xa_ref, xb_ref, meta_ref, g_ref, b_ref,
                  oa_ref, ob_ref, yloc, moe_s, sem, *, n_t, n_ta):
    t = pl.program_id(0)
    first = t < n_ta
    slot = lax.rem(t, 2)
    tab = (lstart_ref, gstart_ref, nchunk_ref)
    rows = yloc.shape[1]

    def chunk_copy(s, lrow, grow):
        return pltpu.make_async_copy(ys_hbm.at[pl.ds(grow, MOE_CHUNK)], yloc.at[s, pl.ds(lrow, MOE_CHUNK)], sem.at[s])

    def fetch(s, tile):
        _seg_loop(tab, tile, lambda lrow, grow: chunk_copy(s, lrow, grow).start())

    @pl.when(t == 0)
    def _():
        yloc[...] = jnp.zeros_like(yloc)
        fetch(0, 0)

    @pl.when(t + 1 < n_t)
    def _():
        fetch(1 - slot, t + 1)

    def wbody(j, c):
        chunk_copy(slot, 0, 0).wait()
        return c
    lax.fori_loop(0, _n_chunks(nchunk_ref, t), wbody, 0)

    meta = meta_ref[...]
    tt = meta.shape[0]
    l_iota = lax.broadcasted_iota(jnp.int32, (tt, rows), 1).astype(F32)
    sel = (jnp.where(l_iota == meta[:, 0:1], meta[:, 2:3], 0.0) + jnp.where(l_iota == meta[:, 1:2], meta[:, 3:4], 0.0))
    sel_hi, sel_lo = _split_bf16(sel)
    yb = yloc[slot].astype(BF16)
    moe_s[...] = jnp.dot(sel_hi, yb, preferred_element_type=F32) + jnp.dot(sel_lo, yb, preferred_element_type=F32)

    @pl.when(first)
    def _():
        oa_ref[...] = _layer_norm(DN_ALPHA * xa_ref[...] + moe_s[...], g_ref[...], b_ref[...])

    @pl.when(jnp.logical_not(first))
    def _():
        ob_ref[...] = _layer_norm(DN_ALPHA * xb_ref[...] + moe_s[...], g_ref[...], b_ref[...])


def _moe_combine(ys, xa, xb, meta_n, tab, g, b, tt):
    n_ta, n_tb = xa.shape[0] // tt, xb.shape[0] // tt
    n_t = n_ta + n_tb
    rows = _local_rows(tt)
    vec = lambda: pl.BlockSpec((1, D_MODEL), lambda t, *_: (0, 0))
    in_a = pl.BlockSpec((tt, D_MODEL), lambda t, *_: (jnp.minimum(t, n_ta - 1), 0))
    in_b = pl.BlockSpec((tt, D_MODEL), lambda t, *_: (jnp.maximum(t - n_ta, 0), 0))
    grid_spec = pltpu.PrefetchScalarGridSpec(
        num_scalar_prefetch=3,
        grid=(n_t,),
        in_specs=[pl.BlockSpec(memory_space=pl.ANY), in_a, in_b,
                  pl.BlockSpec((tt, 8), lambda t, *_: (t, 0)), vec(), vec()],
        out_specs=(in_a, in_b),
        scratch_shapes=[pltpu.VMEM((2, rows, D_MODEL), F32), pltpu.VMEM((tt, D_MODEL), F32),
                        pltpu.SemaphoreType.DMA((2,))])
    return pl.pallas_call(
        functools.partial(_combine_body, n_t=n_t, n_ta=n_ta),
        out_shape=(jax.ShapeDtypeStruct(xa.shape, F32), jax.ShapeDtypeStruct(xb.shape, F32)),
        grid_spec=grid_spec,
        compiler_params=_cparams(("arbitrary",)),
        name="moe_combine_ln",
    )(tab["lstart"], tab["gstart"], tab["nchunk"], ys, xa, xb, meta_n, g, b)


def _moe(xa, xb, meta_t, p):
    n = xa.shape[0] + xb.shape[0]
    tt = 256
    while xa.shape[0] % tt or xb.shape[0] % tt:
        tt //= 2
    r_tot, lpos, tab = _dispatch(meta_t, n, tt)
    xs = _moe_sort(xa, xb, lpos, tab, tt, r_tot)
    ys = _moe_experts(xs, tab, p["w_gate"], p["w_up"], p["w_down"])
    meta_n = jnp.concatenate([lpos.astype(F32), meta_t[2:4], jnp.zeros((4, n), F32)], axis=0).T
    return _moe_combine(ys, xa, xb, meta_n, tab, p["ln2_g"], p["ln2_b"], tt)


def _largest_tile(n, cap, mult):
    best = None
    for d in range(mult, min(n, cap) + 1, mult):
        if n % d == 0:
            best = d
    assert best is not None, (n, cap, mult)
    return best


def _seq_tails(x, off, b, t, k, c0, c1):
    if b <= 8:
        return jnp.stack([lax.slice(x, (off + i * t + t - k, c0), (off + (i + 1) * t, c1)) for i in range(b)])
    return lax.slice(x, (off, c0), (off + b * t, c1)).reshape(b, t, c1 - c0)[:, t - k:, :]


def _kv_rows(qkv, g, off, b, t, k):
    def pick(which):
        h0 = (which * N_GROUPS + g) * HEADS
        if b <= 8:
            a = jnp.stack([lax.slice(qkv, (h0, off + i * t + t - k, 0), (h0 + HEADS, off + (i + 1) * t, HEAD_DIM))
                           for i in range(b)], axis=1)
        else:
            a = lax.slice(qkv, (h0, off, 0), (h0 + HEADS, off + b * t, HEAD_DIM)).reshape(HEADS, b, t, HEAD_DIM)[:, :, t - k:]
        return jnp.transpose(a, (1, 2, 0, 3))
    return jnp.stack([pick(1), pick(2)], axis=2)


def _block_diag_gates(w_a, w_x):
    per = RNN_CHUNK // RNN_BLOCK
    chunks = []
    for c in range(N_RNN_CHUNKS):
        halves = []
        for w in (w_a, w_x):
            m = jnp.zeros((RNN_CHUNK, RNN_CHUNK), F32)
            for i in range(per):
                m = lax.dynamic_update_slice(m, w[c * per + i], (i * RNN_BLOCK, i * RNN_BLOCK))
            halves.append(m)
        chunks.append(jnp.concatenate(halves, axis=1))
    return jnp.stack(chunks).astype(BF16)


def kernel(x_prompt, x_sample, cache_kv_w128, cache_kv_w512, cache_kv_w2048, cache_mem_kv, state_h, state_conv, mem_prompt, w_in, b_gates, conv_w, conv_b, w_a, b_a, w_x, b_x, lru_lambda, w_br_lru, w_br_att, w_br_mem, w_o, w_mem_kv, ln1_g, ln1_b, w_rg, b_rg, w_re, b_re, w_gate, w_up, w_down, ln2_g, ln2_b):
    row = lambda v: v.reshape(1, -1).astype(F32)
    w_router = jnp.zeros((ROUTER_ROWS, D_MODEL), F32)
    w_router = w_router.at[:N_EXPERT_GROUPS].set(w_rg.T).at[N_EXPERT_GROUPS:N_EXPERT_GROUPS + N_EXPERTS].set(w_re.T)
    b_router = jnp.zeros((ROUTER_ROWS, 1), F32)
    b_router = b_router.at[:N_EXPERT_GROUPS, 0].set(b_rg).at[N_EXPERT_GROUPS:N_EXPERT_GROUPS + N_EXPERTS, 0].set(b_re)
    p = dict(
        b_gates=row(b_gates), conv_w=conv_w, conv_b=row(conv_b), wax=_block_diag_gates(w_a, w_x),
        b_a=row(b_a), b_x=row(b_x), lam=row(lru_lambda),
        w_br_lru=w_br_lru.astype(BF16), w_br_att=w_br_att.astype(BF16), w_br_mem=w_br_mem.astype(BF16),
        w_o=w_o.astype(BF16), ln1_g=row(ln1_g), ln1_b=row(ln1_b), w_router=w_router, b_router=b_router,
        w_gate=w_gate, w_up=w_up, w_down=w_down,
        ln2_g=row(ln2_g), ln2_b=row(ln2_b))

    bp, s, _ = x_prompt.shape
    bs, ts, _ = x_sample.shape
    n_p, n_s = bp * s, bs * ts
    xp2, xs2 = x_prompt.reshape(n_p, D_MODEL), x_sample.reshape(n_s, D_MODEL)

    tr = 512
    while n_p % tr or n_s % tr:
        tr //= 2
    xb = _cast_rows(xp2, xs2, BF16, tr)
    tm_a = _largest_tile(n_p + n_s, 2304, 16)
    xrg = _matmul(xb, w_in, 0, 2 * D_RNN, F32, tm_a, 768, "in_proj_rnn")
    qkv = _matmul(xb, w_in, COL_Q, COL_GATES - COL_Q, F32, tm_a, 512, "in_proj_qkv", split=4)
    gates = _matmul(xb, w_in, COL_GATES, N_GATE_COLS, BF16, tm_a, 512, "gate_proj", bias=p["b_gates"])
    lru = (p["conv_w"], p["conv_b"], p["wax"], p["b_a"], p["b_x"], p["lam"])
    branch_w = (p["w_br_lru"], p["w_br_att"], p["w_br_mem"])
    ln1 = (p["w_o"], p["ln1_g"], p["ln1_b"], p["w_router"], p["b_router"])

    mem_rows = mem_prompt.reshape(bp * N_MEM, D_MODEL)
    mem_kv_p = _matmul(mem_rows, w_mem_kv, 0, 2 * D_MEM, F32, _tile(bp * N_MEM, 512), 512, "mem_kv_proj")
    hs_p, h_p = _rglru(xrg, 0, bp, s, jnp.zeros((bp, CONV_W - 1, D_RNN), F32), jnp.zeros((bp, D_RNN), F32), *lru,
                       _tile(s, 256))
    att_p = _attn_prompt(qkv, bp, s)
    mem_p = _mem_attn(qkv, mem_kv_p.reshape(bp, N_MEM, 2 * D_MEM), bp, s, _tile(s, 1024), BF16)
    merged_p = _branch_merge(hs_p, xrg, att_p, mem_p, gates, 0, *branch_w, _tile(n_p, 256))
    x1_p, meta_p = _proj_ln(merged_p, xp2, *ln1, _tile(n_p, 512))

    hs_s, h_s = _rglru(xrg, n_p, bs, ts, state_conv, state_h, *lru, ts)
    att_s = _attn_sample(qkv, (cache_kv_w128, cache_kv_w512, cache_kv_w2048), bs, ts, n_p)
    mem_s = _mem_attn_cache(qkv, cache_mem_kv, bs, ts, n_p)
    tm_s = _tile(n_s, 256)
    assert n_p % tm_s == 0
    merged_s = _branch_merge(hs_s, xrg, att_s, mem_s, gates, n_p, *branch_w, tm_s)
    x1_s, meta_s = _proj_ln(merged_s, xs2, *ln1, _tile(n_s, 512))

    y_p, y_s = _moe(x1_p, x1_s, jnp.concatenate([meta_p, meta_s], axis=1), p)

    kc = CONV_W - 1
    conv_p = jnp.concatenate([jnp.zeros((bp, kc, D_RNN), F32), _seq_tails(xrg, 0, bp, s, min(kc, s), 0, D_RNN)], axis=1)[:, -kc:]
    conv_s = jnp.concatenate([state_conv, _seq_tails(xrg, n_p, bs, ts, min(kc, ts), 0, D_RNN)], axis=1)[:, -kc:]
    kv_p = [_kv_rows(qkv, g, 0, bp, s, min(KEYS * d, s)) for g, d in enumerate(DILATIONS)]
    kv_s = [_kv_rows(qkv, g, n_p, bs, ts, ts) for g in range(N_GROUPS)]
    return (y_p.reshape(bp, s, D_MODEL), y_s.reshape(bs, ts, D_MODEL), kv_p[0], kv_p[1], kv_p[2],
            mem_kv_p.reshape(bp, N_MEM, 2, MEM_HEADS, MEM_HEAD_DIM), h_p.reshape(bp, D_RNN), conv_p,
            kv_s[0], kv_s[1], kv_s[2], h_s.reshape(bs, D_RNN), conv_s)
```

```python
import functools

import jax
import jax.numpy as jnp
from jax import lax
from jax.experimental import pallas as pl
from jax.experimental.pallas import tpu as pltpu

F32 = jnp.float32
BF16 = jnp.bfloat16

D_MODEL = 2048
D_RNN = 1536
N_RNN_BLOCKS = 16
RNN_BLOCK = D_RNN // N_RNN_BLOCKS
RNN_CHUNK = 384
N_RNN_CHUNKS = D_RNN // RNN_CHUNK
CONV_W = 4
LRU_C = 8.0
HEAD_DIM = 128
HEADS = 4
DILATIONS = (1, 4, 16)
KEYS = 128
N_GROUPS = 3
D_ATT_OUT = HEADS * HEAD_DIM
N_QKV_HEADS = 3 * N_GROUPS * HEADS
ATT_SCALE = HEAD_DIM ** -0.5
ATT_TILE = KEYS * max(DILATIONS)
N_MEM = 256
MEM_HEADS = 4
MEM_HEAD_DIM = 256
D_MEM = MEM_HEADS * MEM_HEAD_DIM
MEM_SCALE = MEM_HEAD_DIM ** -0.5
N_EXPERT_GROUPS = 4
EXPERTS_PER_GROUP = 4
N_EXPERTS = 16
D_EXPERT = 512
MOE_TILE = 512
DN_ALPHA = 2.0 ** 0.25
LN_EPS = 1e-5
NEG_INF = -1e30

COL_Q = 2 * D_RNN
COL_QM = COL_Q + N_QKV_HEADS * HEAD_DIM
COL_GATES = COL_QM + D_MEM
N_GATE_COLS = 3 * D_MODEL
ROUTER_ROWS = 32

VMEM_LIMIT = 56 * 1024 * 1024


def _cparams(sem):
    return pltpu.CompilerParams(dimension_semantics=sem, vmem_limit_bytes=VMEM_LIMIT)


def _gelu(x):
    return 0.5 * x * (1.0 + jnp.tanh(0.7978845608028654 * (x + 0.044715 * (x * x * x))))


def _layer_norm(x, g, b):
    mu = jnp.mean(x, axis=-1, keepdims=True)
    xc = x - mu
    var = jnp.mean(xc * xc, axis=-1, keepdims=True)
    return xc * lax.rsqrt(var + LN_EPS) * g + b


def _dot_nt(a, b):
    return lax.dot_general(a, b, (((1,), (1,)), ((), ())), preferred_element_type=F32)


def _tile(m, cap):
    t = min(m, cap)
    assert m % t == 0, (m, t)
    return t


def _mm_body(x_ref, w_ref, *rest, split, gate):
    if gate:
        b_ref, o_ref = rest
        w = w_ref[...].astype(BF16)
        rows = x_ref.shape[0]
        half = rows // 2 if rows % 32 == 0 else rows
        for r0 in range(0, rows, half):
            acc = jnp.dot(x_ref[r0:r0 + half, :].astype(BF16), w, preferred_element_type=F32)
            acc = 0.5 * jnp.tanh(0.5 * (acc + b_ref[...])) + 0.5
            o_ref[r0:r0 + half, :] = acc.astype(o_ref.dtype)
        return
    (o_ref,) = rest
    acc = jnp.dot(x_ref[...].astype(BF16), w_ref[...].astype(BF16), preferred_element_type=F32)
    if split == 1:
        o_ref[...] = acc.astype(o_ref.dtype)
    else:
        w = acc.shape[1] // split
        for s in range(split):
            o_ref[s] = acc[:, s * w:(s + 1) * w].astype(o_ref.dtype)


def _matmul(x, w, col_off, n_cols, out_dtype, tm, tn, name, bias=None, split=1):
    m, k = x.shape
    cb = col_off // tn
    in_specs = [pl.BlockSpec((tm, k), lambda i, j: (i, 0)),
                pl.BlockSpec((k, tn), lambda i, j: (0, j + cb))]
    args = [x, w]
    if bias is not None:
        in_specs.append(pl.BlockSpec((1, tn), lambda i, j: (0, j)))
        args.append(bias)
    if split == 1:
        out_shape = jax.ShapeDtypeStruct((m, n_cols), out_dtype)
        out_spec = pl.BlockSpec((tm, tn), lambda i, j: (i, j))
    else:
        out_shape = jax.ShapeDtypeStruct((n_cols * split // tn, m, tn // split), out_dtype)
        out_spec = pl.BlockSpec((split, tm, tn // split), lambda i, j: (j, i, 0))
    return pl.pallas_call(
        functools.partial(_mm_body, split=split, gate=bias is not None),
        out_shape=out_shape,
        grid=(m // tm, n_cols // tn),
        in_specs=in_specs,
        out_specs=out_spec,
        compiler_params=_cparams(("parallel", "arbitrary")),
        name=name,
    )(*args)


def _cast_rows_body(a_ref, b_ref, o_ref, *, n_ta):
    o_ref[...] = jnp.where(pl.program_id(0) < n_ta, a_ref[...], b_ref[...]).astype(o_ref.dtype)


def _cast_rows(xa, xb, dtype, tr):
    n_ta, n_tb = xa.shape[0] // tr, xb.shape[0] // tr
    cols = xa.shape[1]
    return pl.pallas_call(
        functools.partial(_cast_rows_body, n_ta=n_ta),
        out_shape=jax.ShapeDtypeStruct((xa.shape[0] + xb.shape[0], cols), dtype),
        grid=(n_ta + n_tb,),
        in_specs=[pl.BlockSpec((tr, cols), lambda i: (jnp.minimum(i, n_ta - 1), 0)),
                  pl.BlockSpec((tr, cols), lambda i: (jnp.maximum(i - n_ta, 0), 0))],
        out_specs=pl.BlockSpec((tr, cols), lambda i: (i, 0)),
        compiler_params=_cparams(("parallel",)),
        name="cast_rows",
    )(xa, xb)


def _rglru_body(xr_ref, cbuf_ref, h0_ref, cw_ref, cb_ref, wax_ref, ba_ref, bx_ref, lam_ref,
                out_ref, hl_ref, ext_s, a_s, u_s, h_s, *, tt):
    t = pl.program_id(1)

    @pl.when(t == 0)
    def _():
        ext_s[0:8, :] = jnp.zeros((8, D_RNN), F32)
        ext_s[5:8, :] = cbuf_ref[0]
        h_s[...] = jnp.broadcast_to(h0_ref[0], (8, D_RNN))

    @pl.when(t > 0)
    def _():
        ext_s[0:8, :] = ext_s[tt:tt + 8, :]

    ext_s[8:8 + tt, :] = xr_ref[...]
    cw = cw_ref[...]
    xc = (cb_ref[...] + cw[3:4, :] * ext_s[8:8 + tt, :] + cw[2:3, :] * ext_s[7:7 + tt, :]
          + cw[1:2, :] * ext_s[6:6 + tt, :] + cw[0:1, :] * ext_s[5:5 + tt, :])
    xcb = xc.astype(BF16)
    r_parts, i_parts = [], []
    for c in range(N_RNN_CHUNKS):
        g = jnp.dot(xcb[:, c * RNN_CHUNK:(c + 1) * RNN_CHUNK], wax_ref[c], preferred_element_type=F32)
        r_parts.append(g[:, :RNN_CHUNK])
        i_parts.append(g[:, RNN_CHUNK:])
    sigmoid = lambda z: 0.5 * jnp.tanh(0.5 * z) + 0.5
    r = sigmoid(jnp.concatenate(r_parts, axis=1) + ba_ref[...])
    gi = sigmoid(jnp.concatenate(i_parts, axis=1) + bx_ref[...])
    nl = -lam_ref[...]
    softplus = jnp.maximum(nl, 0.0) + jnp.log1p(jnp.exp(-jnp.abs(nl)))
    log_a = (-LRU_C) * r * softplus
    th = jnp.tanh(log_a)
    a_s[...] = jnp.exp(log_a)
    u_s[...] = jnp.sqrt(-2.0 * th / (1.0 - th)) * (gi * xc)

    rows = lax.broadcasted_iota(jnp.int32, (8, D_RNN), 0)

    def blk(i, h):
        r0 = pl.multiple_of(i * 8, 8)
        ab = a_s[pl.ds(r0, 8), :]
        ub = u_s[pl.ds(r0, 8), :]
        for s in (1, 2, 4):
            keep = rows >= s
            ub = ab * jnp.where(keep, pltpu.roll(ub, s, 0), 0.0) + ub
            ab = ab * jnp.where(keep, pltpu.roll(ab, s, 0), 1.0)
        hb = ab * h + ub
        u_s[pl.ds(r0, 8), :] = hb
        return jnp.broadcast_to(hb[7:8, :], (8, D_RNN))

    h_fin = lax.fori_loop(0, tt // 8, blk, h_s[...])
    h_s[...] = h_fin
    hl_ref[0] = h_fin[0:1, :]
    out_ref[...] = u_s[...]


def _rglru(xrg, off, b, t, conv_buf, h0, conv_w, conv_b, wax, b_a, b_x, lam, tt):
    nt = t // tt
    ob = off // tt
    vec = lambda: pl.BlockSpec((1, D_RNN), lambda i, j: (0, 0))
    return pl.pallas_call(
        functools.partial(_rglru_body, tt=tt),
        out_shape=(jax.ShapeDtypeStruct((b * t, D_RNN), F32), jax.ShapeDtypeStruct((b, 1, D_RNN), F32)),
        grid=(b, nt),
        in_specs=[pl.BlockSpec((tt, D_RNN), lambda i, j: (ob + i * nt + j, 0)),
                  pl.BlockSpec((1, CONV_W - 1, D_RNN), lambda i, j: (i, 0, 0)),
                  pl.BlockSpec((1, 1, D_RNN), lambda i, j: (i, 0, 0)),
                  pl.BlockSpec((CONV_W, D_RNN), lambda i, j: (0, 0)),
                  vec(),
                  pl.BlockSpec((N_RNN_CHUNKS, RNN_CHUNK, 2 * RNN_CHUNK), lambda i, j: (0, 0, 0)),
                  vec(), vec(), vec()],
        out_specs=(pl.BlockSpec((tt, D_RNN), lambda i, j: (i * nt + j, 0)),
                   pl.BlockSpec((1, 1, D_RNN), lambda i, j: (i, 0, 0))),
        scratch_shapes=[pltpu.VMEM((tt + 8, D_RNN), F32), pltpu.VMEM((tt, D_RNN), F32),
                        pltpu.VMEM((tt, D_RNN), F32), pltpu.VMEM((8, D_RNN), F32)],
        compiler_params=_cparams(("parallel", "arbitrary")),
        name="rglru",
    )(xrg, conv_buf, h0.reshape(b, 1, D_RNN), conv_w, conv_b, wax, b_a, b_x, lam)


def _band_block(q, k, v, bias):
    s = _dot_nt(q, k) * ATT_SCALE + bias
    m = jnp.max(s, axis=-1, keepdims=True)
    p = jnp.exp(s - m)
    l = jnp.sum(p, axis=-1, keepdims=True)
    o = jnp.dot(p.astype(BF16), v, preferred_element_type=F32) / l
    return o, m + jnp.log(l)


def _attn_prompt_body(*refs):
    q_refs, k_refs, v_refs, kh_refs, vh_refs = (refs[3 * i:3 * i + 3] for i in range(5))
    o_ref, og_s, lg_s = refs[15:]
    row = lax.broadcasted_iota(jnp.int32, (KEYS, 2 * KEYS), 0)
    col = lax.broadcasted_iota(jnp.int32, (KEYS, 2 * KEYS), 1)
    band = jnp.logical_and(col >= row, col <= row + KEYS)
    bias = jnp.where(band, 0.0, NEG_INF)
    has_prev = pl.program_id(1) > 0
    bias_first = jnp.where(jnp.logical_and(band, jnp.logical_or(col >= KEYS, has_prev)), 0.0, NEG_INF)

    def put(g, rows, o, lse):
        og_s[g, rows, :] = o
        lg_s[g, rows, :] = jnp.broadcast_to(lse, (KEYS, HEAD_DIM))

    def rows_of(start, size, d):
        return pl.ds(start, size, stride=d) if d > 1 else pl.ds(start, size)

    for g, d in enumerate(DILATIONS):
        q_ref, k_ref, v_ref, kh_ref, vh_ref = q_refs[g], k_refs[g], v_refs[g], kh_refs[g], vh_refs[g]
        nqb = ATT_TILE // (d * KEYS)
        for r in range(d):
            own = rows_of(r, KEYS, d)
            kk = jnp.concatenate([kh_ref[0, own, :], k_ref[0, own, :]], axis=0).astype(BF16)
            vv = jnp.concatenate([vh_ref[0, own, :], v_ref[0, own, :]], axis=0).astype(BF16)
            o, lse = _band_block(q_ref[0, own, :].astype(BF16), kk, vv, bias_first)
            put(g, own, o, lse)
            for qb in range(1, nqb):
                rows = rows_of(qb * KEYS * d + r, KEYS, d)
                keys = rows_of((qb - 1) * KEYS * d + r, 2 * KEYS, d)
                o, lse = _band_block(q_ref[0, rows, :].astype(BF16), k_ref[0, keys, :].astype(BF16),
                                     v_ref[0, keys, :].astype(BF16), bias)
                put(g, rows, o, lse)

    l0, l1, l2 = lg_s[0], lg_s[1], lg_s[2]
    m = jnp.maximum(jnp.maximum(l0, l1), l2)
    e0, e1, e2 = jnp.exp(l0 - m), jnp.exp(l1 - m), jnp.exp(l2 - m)
    o_ref[...] = ((e0 * og_s[0] + e1 * og_s[1] + e2 * og_s[2]) / (e0 + e1 + e2)).astype(o_ref.dtype)


def _attn_prompt(qkv, b, s):
    assert s % ATT_TILE == 0
    nt = s // ATT_TILE
    n = b * s

    def cur(which, g):
        return pl.BlockSpec((1, ATT_TILE, HEAD_DIM),
                            lambda i, j, h: ((which * N_GROUPS + g) * HEADS + h, i * nt + j, 0))

    def halo(which, g):
        rows = KEYS * DILATIONS[g]
        per = ATT_TILE // rows
        return pl.BlockSpec((1, rows, HEAD_DIM),
                            lambda i, j, h: ((which * N_GROUPS + g) * HEADS + h,
                                             jnp.maximum((i * nt + j) * per - 1, 0), 0))

    in_specs = ([cur(0, g) for g in range(N_GROUPS)] + [cur(1, g) for g in range(N_GROUPS)]
                + [cur(2, g) for g in range(N_GROUPS)] + [halo(1, g) for g in range(N_GROUPS)]
                + [halo(2, g) for g in range(N_GROUPS)])
    return pl.pallas_call(
        _attn_prompt_body,
        out_shape=jax.ShapeDtypeStruct((n, D_ATT_OUT), BF16),
        grid=(b, nt, HEADS),
        in_specs=in_specs,
        out_specs=pl.BlockSpec((ATT_TILE, HEAD_DIM), lambda i, j, h: (i * nt + j, h)),
        scratch_shapes=[pltpu.VMEM((N_GROUPS, ATT_TILE, HEAD_DIM), F32),
                        pltpu.VMEM((N_GROUPS, ATT_TILE, HEAD_DIM), F32)],
        compiler_params=_cparams(("parallel", "parallel", "parallel")),
        name="attn_prompt",
    )(*([qkv] * 15))


def _attn_sample_group(q4, kn4, vn4, c_ref, d, t_new):
    half = len(c_ref.shape) == 4
    wb = c_ref.shape[1] * 16 if half else c_ref.shape[1] // (2 * HEADS)
    nk = wb // 2 if half else wb
    nr = HEADS * t_new
    zeros = jnp.zeros((t_new, HEAD_DIM), F32)
    qbd = jnp.concatenate(
        [jnp.concatenate([q4[h] if hh == h else zeros for hh in range(HEADS)], axis=1) for h in range(HEADS)],
        axis=0).astype(BF16)
    kn = jnp.concatenate([kn4[h] for h in range(HEADS)], axis=1).astype(BF16)
    vn = jnp.concatenate([vn4[h] for h in range(HEADS)], axis=1).astype(BF16)
    if half:
        cache_rows = lambda kv, h: c_ref[0, :, pl.ds(kv * HEADS + h, 8, stride=2 * HEADS), :].reshape(nk, HEAD_DIM)
    else:
        cache_rows = lambda kv, h: c_ref[0, pl.ds(kv * HEADS + h, wb, stride=2 * HEADS), :]
    kc = jnp.concatenate([cache_rows(0, h) for h in range(HEADS)], axis=1).astype(BF16)
    vc = jnp.concatenate([cache_rows(1, h) for h in range(HEADS)], axis=1).astype(BF16)
    tq_c = lax.broadcasted_iota(jnp.int32, (nr, nk), 0) & (t_new - 1)
    e_c = lax.broadcasted_iota(jnp.int32, (nr, nk), 1)
    if half:
        e_c = ((e_c >> 3) << 4) + (e_c & 7)
    dist_c = wb + tq_c - e_c
    ok_c = jnp.logical_and((dist_c & (d - 1)) == 0, dist_c <= KEYS * d)
    tq_n = lax.broadcasted_iota(jnp.int32, (nr, t_new), 0) & (t_new - 1)
    dist_n = tq_n - lax.broadcasted_iota(jnp.int32, (nr, t_new), 1)
    ok_n = jnp.logical_and(jnp.logical_and(dist_n >= 0, (dist_n & (d - 1)) == 0), dist_n <= KEYS * d)
    s_c = jnp.where(ok_c, _dot_nt(qbd, kc) * ATT_SCALE, NEG_INF)
    s_n = jnp.where(ok_n, _dot_nt(qbd, kn) * ATT_SCALE, NEG_INF)
    m = jnp.maximum(jnp.max(s_c, axis=-1, keepdims=True), jnp.max(s_n, axis=-1, keepdims=True))
    p_c = jnp.exp(s_c - m)
    p_n = jnp.exp(s_n - m)
    l = jnp.sum(p_c, axis=-1, keepdims=True) + jnp.sum(p_n, axis=-1, keepdims=True)
    o = (jnp.dot(p_c.astype(BF16), vc, preferred_element_type=F32)
         + jnp.dot(p_n.astype(BF16), vn, preferred_element_type=F32)) / l
    lse = m + jnp.log(l)
    o = jnp.concatenate(
        [o[h * t_new:(h + 1) * t_new, h * HEAD_DIM:(h + 1) * HEAD_DIM] for h in range(HEADS)], axis=1)
    lse = jnp.concatenate(
        [jnp.broadcast_to(lse[h * t_new:(h + 1) * t_new], (t_new, HEAD_DIM)) for h in range(HEADS)], axis=1)
    return o, lse


def _attn_sample_body(*refs, t_new):
    q_refs, k_refs, v_refs, c_refs = (refs[3 * i:3 * i + 3] for i in range(4))
    o_ref = refs[12]
    outs = [_attn_sample_group(q_refs[g][...], k_refs[g][...], v_refs[g][...], c_refs[g], d, t_new)
            for g, d in enumerate(DILATIONS)]
    (o0, l0), (o1, l1), (o2, l2) = outs
    m = jnp.maximum(jnp.maximum(l0, l1), l2)
    e0, e1, e2 = jnp.exp(l0 - m), jnp.exp(l1 - m), jnp.exp(l2 - m)
    o_ref[0] = (e0 * o0 + e1 * o1 + e2 * o2) / (e0 + e1 + e2)


def _attn_sample(qkv, caches, b, t_new, off):
    assert t_new & (t_new - 1) == 0 and off % t_new == 0
    ob = off // t_new
    new = lambda which, g: pl.BlockSpec((HEADS, t_new, HEAD_DIM), lambda i: (which * N_GROUPS + g, ob + i, 0))
    caches2, cache_specs = [], []
    for c, d in zip(caches, DILATIONS):
        wb = c.shape[1]
        if d == 16 and wb % 16 == 0 and t_new <= 8:
            caches2.append(c.reshape(b, wb // 16, 16 * 2 * HEADS, HEAD_DIM))
            cache_specs.append(pl.BlockSpec((1, wb // 16, 8 * 2 * HEADS, HEAD_DIM), lambda i: (i, 0, 0, 0)))
        else:
            caches2.append(c.reshape(b, wb * 2 * HEADS, HEAD_DIM))
            cache_specs.append(pl.BlockSpec((1, wb * 2 * HEADS, HEAD_DIM), lambda i: (i, 0, 0)))
    in_specs = ([new(0, g) for g in range(N_GROUPS)] + [new(1, g) for g in range(N_GROUPS)]
                + [new(2, g) for g in range(N_GROUPS)] + cache_specs)
    att = pl.pallas_call(
        functools.partial(_attn_sample_body, t_new=t_new),
        out_shape=jax.ShapeDtypeStruct((b, t_new, D_ATT_OUT), F32),
        grid=(b,),
        in_specs=in_specs,
        out_specs=pl.BlockSpec((1, t_new, D_ATT_OUT), lambda i: (i, 0, 0)),
        compiler_params=_cparams(("parallel",)),
        name="attn_sample",
    )(*([qkv] * 9), *caches2)
    return att.reshape(b * t_new, D_ATT_OUT)


def _mem_q_specs(rows, index):
    first = N_QKV_HEADS // 2
    return [pl.BlockSpec((2, rows, HEAD_DIM), functools.partial(index, first + h)) for h in range(MEM_HEADS)]


def _mem_attn_body(q0_ref, q1_ref, q2_ref, q3_ref, kv_ref, o_ref):
    for h, q_ref in enumerate((q0_ref, q1_ref, q2_ref, q3_ref)):
        k = kv_ref[0, :, h * MEM_HEAD_DIM:(h + 1) * MEM_HEAD_DIM].astype(BF16)
        v = kv_ref[0, :, D_MEM + h * MEM_HEAD_DIM:D_MEM + (h + 1) * MEM_HEAD_DIM].astype(BF16)
        q = jnp.concatenate([q_ref[0], q_ref[1]], axis=1).astype(BF16)
        s = _dot_nt(q, k) * MEM_SCALE
        m = jnp.max(s, axis=-1, keepdims=True)
        p = jnp.exp(s - m)
        l = jnp.sum(p, axis=-1, keepdims=True)
        o = jnp.dot(p.astype(BF16), v, preferred_element_type=F32) / l
        o_ref[:, h * MEM_HEAD_DIM:(h + 1) * MEM_HEAD_DIM] = o.astype(o_ref.dtype)


def _mem_attn(qkvm, mem_kv, b, t, tm, out_dtype):
    nt = t // tm
    return pl.pallas_call(
        _mem_attn_body,
        out_shape=jax.ShapeDtypeStruct((b * t, D_MEM), out_dtype),
        grid=(b, nt),
        in_specs=_mem_q_specs(tm, lambda blk, i, j: (blk, i * nt + j, 0))
        + [pl.BlockSpec((1, N_MEM, 2 * D_MEM), lambda i, j: (i, 0, 0))],
        out_specs=pl.BlockSpec((tm, D_MEM), lambda i, j: (i * nt + j, 0)),
        compiler_params=_cparams(("parallel", "parallel")),
        name="mem_attn",
    )(qkvm, qkvm, qkvm, qkvm, mem_kv)


def _mem_attn_cache_body(q0_ref, q1_ref, q2_ref, q3_ref, c_ref, o_ref):
    t = q0_ref.shape[1]
    n_rows = N_MEM * 2 * MEM_HEADS
    flat = c_ref[0].reshape(n_rows, MEM_HEAD_DIM).astype(BF16)
    q = jnp.concatenate([jnp.concatenate([q_ref[0], q_ref[1]], axis=1) for q_ref in (q0_ref, q1_ref, q2_ref, q3_ref)],
                        axis=0).astype(BF16)
    assert t & (t - 1) == 0
    head = lax.broadcasted_iota(jnp.int32, (MEM_HEADS * t, n_rows), 0) >> (t.bit_length() - 1)
    col = lax.broadcasted_iota(jnp.int32, (MEM_HEADS * t, n_rows), 1)
    s = jnp.where((col & (2 * MEM_HEADS - 1)) == head, _dot_nt(q, flat) * MEM_SCALE, NEG_INF)
    m = jnp.max(s, axis=-1, keepdims=True)
    p = jnp.exp(s - m)
    l = jnp.sum(p, axis=-1, keepdims=True)
    o = jnp.dot(pltpu.roll(p, MEM_HEADS, 1).astype(BF16), flat, preferred_element_type=F32) / l
    o_ref[...] = jnp.concatenate([o[h * t:(h + 1) * t, :] for h in range(MEM_HEADS)], axis=1)


def _mem_attn_cache(qkvm, cache, b, t, off):
    ob = off // t
    return pl.pallas_call(
        _mem_attn_cache_body,
        out_shape=jax.ShapeDtypeStruct((b * t, D_MEM), F32),
        grid=(b,),
        in_specs=_mem_q_specs(t, lambda blk, i: (blk, ob + i, 0))
        + [pl.BlockSpec((1, N_MEM, 2, MEM_HEADS, MEM_HEAD_DIM), lambda i: (i, 0, 0, 0, 0))],
        out_specs=pl.BlockSpec((t, D_MEM), lambda i: (i, 0)),
        compiler_params=_cparams(("parallel",)),
        name="mem_attn_cache",
    )(qkvm, qkvm, qkvm, qkvm, cache)


def _branch_body(hs_ref, xg_ref, att_ref, mem_ref, gt_ref, wl_ref, wa_ref, wm_ref, out_ref):
    a_lru = (hs_ref[...] * _gelu(xg_ref[...])).astype(BF16)
    acc = gt_ref[:, 0:D_MODEL].astype(F32) * jnp.dot(a_lru, wl_ref[...], preferred_element_type=F32)
    acc = acc + gt_ref[:, D_MODEL:2 * D_MODEL].astype(F32) * jnp.dot(att_ref[...].astype(BF16), wa_ref[...], preferred_element_type=F32)
    acc = acc + gt_ref[:, 2 * D_MODEL:3 * D_MODEL].astype(F32) * jnp.dot(mem_ref[...].astype(BF16), wm_ref[...], preferred_element_type=F32)
    out_ref[...] = acc.astype(out_ref.dtype)


def _branch_merge(hs, xrg, att, mem, gates, goff, wl, wa, wm, tm):
    n = hs.shape[0]
    gb = goff // tm
    row = lambda w: pl.BlockSpec((tm, w), lambda i: (i, 0))
    full = lambda a: pl.BlockSpec(a.shape, lambda i: (0, 0))
    return pl.pallas_call(
        _branch_body,
        out_shape=jax.ShapeDtypeStruct((n, D_MODEL), BF16),
        grid=(n // tm,),
        in_specs=[row(D_RNN), pl.BlockSpec((tm, D_RNN), lambda i: (gb + i, 1)), row(D_ATT_OUT), row(D_MEM),
                  pl.BlockSpec((tm, N_GATE_COLS), lambda i: (gb + i, 0)), full(wl), full(wa), full(wm)],
        out_specs=row(D_MODEL),
        compiler_params=_cparams(("parallel",)),
        name="branch_merge",
    )(hs, xrg, att, mem, gates, wl, wa, wm)


def _split_bf16(x):
    hi = x.astype(BF16)
    return hi, (x - hi.astype(F32)).astype(BF16)


def _route_rows(lg):
    g = [lg[i:i + 1, :] for i in range(N_EXPERT_GROUPS)]
    gmax = jnp.maximum(jnp.maximum(g[0], g[1]), jnp.maximum(g[2], g[3]))
    gidx = jnp.where(g[0] == gmax, 0.0, jnp.where(g[1] == gmax, 1.0, jnp.where(g[2] == gmax, 2.0, 3.0)))
    g_p = 1.0 / (jnp.exp(g[0] - gmax) + jnp.exp(g[1] - gmax) + jnp.exp(g[2] - gmax) + jnp.exp(g[3] - gmax))
    e = []
    for k in range(EXPERTS_PER_GROUP):
        rows = [lg[N_EXPERT_GROUPS + gg * EXPERTS_PER_GROUP + k:N_EXPERT_GROUPS + gg * EXPERTS_PER_GROUP + k + 1, :]
                for gg in range(N_EXPERT_GROUPS)]
        e.append(jnp.where(gidx == 0.0, rows[0], jnp.where(gidx == 1.0, rows[1], jnp.where(gidx == 2.0, rows[2], rows[3]))))

    def first_argmax(v):
        mx = jnp.maximum(jnp.maximum(v[0], v[1]), jnp.maximum(v[2], v[3]))
        ix = jnp.where(v[0] == mx, 0.0, jnp.where(v[1] == mx, 1.0, jnp.where(v[2] == mx, 2.0, 3.0)))
        return mx, ix

    v1, i1 = first_argmax(e)
    v2, i2 = first_argmax([jnp.where(i1 == float(k), -jnp.inf, e[k]) for k in range(EXPERTS_PER_GROUP)])
    ex = jnp.exp(v2 - v1)
    w1 = g_p / (1.0 + ex)
    w2 = g_p * ex / (1.0 + ex)
    base = gidx * float(EXPERTS_PER_GROUP)
    zero = jnp.zeros_like(w1)
    return jnp.concatenate([base + i1, base + i2, w1, w2, zero, zero, zero, zero], axis=0)


def _proj_ln_body(mg_ref, x_ref, wo_ref, g_ref, b_ref, wr_ref, br_ref, x1_ref, meta_ref, mix_s):
    @pl.when(pl.program_id(0) == 0)
    def _():
        mix_s[...] = jnp.zeros(mix_s.shape, F32)

    mix_next = jnp.dot(mg_ref[...], wo_ref[...], preferred_element_type=F32)
    wh, wl = _split_bf16(wr_ref[...])
    tm = x_ref.shape[0]
    sub = min(256, tm)
    for r0 in range(0, tm, sub):
        rs = pl.ds(r0, sub)
        x1 = _layer_norm(DN_ALPHA * x_ref[rs, :] + mix_s[rs, :], g_ref[...], b_ref[...])
        x1_ref[rs, :] = x1
        xh, xl = _split_bf16(x1)
        lg = _dot_nt(wh, xh) + (_dot_nt(wh, xl) + _dot_nt(wl, xh)) + br_ref[...]
        meta_ref[:, rs] = _route_rows(lg)
    mix_s[...] = mix_next


def _proj_ln(merged, x, wo, g, b, wr, br, tm):
    n = merged.shape[0]
    nb = n // tm
    prev = lambda w: pl.BlockSpec((tm, w), lambda i: (jnp.maximum(i - 1, 0), 0))
    full = lambda a: pl.BlockSpec(a.shape, lambda i: (0, 0))
    return pl.pallas_call(
        _proj_ln_body,
        out_shape=(jax.ShapeDtypeStruct((n, D_MODEL), F32), jax.ShapeDtypeStruct((8, n), F32)),
        grid=(nb + 1,),
        in_specs=[pl.BlockSpec((tm, D_MODEL), lambda i: (jnp.minimum(i, nb - 1), 0)), prev(D_MODEL),
                  full(wo), full(g), full(b), full(wr), full(br)],
        out_specs=(prev(D_MODEL), pl.BlockSpec((8, tm), lambda i: (0, jnp.maximum(i - 1, 0)))),
        scratch_shapes=[pltpu.VMEM((tm, D_MODEL), F32)],
        compiler_params=_cparams(("arbitrary",)),
        name="proj_ln_router",
    )(merged, x, wo, g, b, wr, br)


MOE_CHUNK = 16


def _local_rows(tt):
    return -(-(2 * tt + N_EXPERTS * (MOE_CHUNK - 1)) // 128) * 128


def _dispatch(meta_t, n, tt):
    n_t = n // tt
    ids = meta_t[0:2].astype(jnp.int32)
    onehot = (ids[:, :, None] == jnp.arange(N_EXPERTS, dtype=jnp.int32)).astype(jnp.int32).reshape(2, n_t, tt, N_EXPERTS)
    cnt_slot = jnp.sum(onehot, axis=2)
    cnt = cnt_slot[0] + cnt_slot[1]
    pc = (cnt + MOE_CHUNK - 1) // MOE_CHUNK * MOE_CHUNK
    lstart = jnp.cumsum(pc, axis=1) - pc
    tri = (jnp.arange(tt)[:, None] >= jnp.arange(tt)[None, :]).astype(F32)
    csum = jnp.einsum("ut,snte->snue", tri, onehot.astype(F32)).astype(jnp.int32)
    rank = csum - onehot + jnp.stack([jnp.zeros_like(cnt), cnt_slot[0]])[:, :, None, :]
    lpos = jnp.sum(onehot * (lstart[None, :, None, :] + rank), axis=-1).reshape(2, n)
    seg = jnp.sum(pc, axis=0)
    pe = (seg + MOE_TILE - 1) // MOE_TILE * MOE_TILE
    ends = jnp.cumsum(pe)
    base = ends - pe
    gstart = base[None, :] + jnp.cumsum(pc, axis=0) - pc
    r_tot = -(-(2 * n + N_EXPERTS * (MOE_CHUNK - 1) * n_t) // MOE_TILE) * MOE_TILE + N_EXPERTS * MOE_TILE
    n_tiles = r_tot // MOE_TILE
    tile_start = jnp.arange(n_tiles, dtype=jnp.int32) * MOE_TILE
    tile_expert = jnp.minimum(jnp.sum((tile_start[:, None] >= ends[None, :]).astype(jnp.int32), axis=1), N_EXPERTS - 1)
    tables = dict(
        lstart=lstart.reshape(-1), gstart=gstart.reshape(-1), nchunk=(pc // MOE_CHUNK).reshape(-1),
        zstart=jnp.concatenate([base + seg, ends[-1:]]),
        zcount=jnp.concatenate([(pe - seg) // MOE_CHUNK, (r_tot - ends[-1:]) // MOE_TILE]),
        tile_expert=tile_expert, n_used=(ends[-1] // MOE_TILE).reshape(1))
    return r_tot, lpos, tables


def _seg_loop(tab, t, fn):
    lstart_ref, gstart_ref, nchunk_ref = tab
    for e in range(N_EXPERTS):
        ls = lstart_ref[t * N_EXPERTS + e]
        gs = gstart_ref[t * N_EXPERTS + e]

        def body(j, c, ls=ls, gs=gs):
            fn(pl.multiple_of(ls + j * MOE_CHUNK, MOE_CHUNK), pl.multiple_of(gs + j * MOE_CHUNK, MOE_CHUNK))
            return c
        lax.fori_loop(0, nchunk_ref[t * N_EXPERTS + e], body, 0)


def _n_chunks(nchunk_ref, t):
    tot = nchunk_ref[t * N_EXPERTS]
    for e in range(1, N_EXPERTS):
        tot = tot + nchunk_ref[t * N_EXPERTS + e]
    return tot


def _sort_body(lstart_ref, gstart_ref, nchunk_ref, zstart_ref, zcount_ref, lpos_ref, xa_ref, xb_ref, xs_hbm,
               xloc, zbuf, sem, zsem, *, n_t, n_ta):
    t = pl.program_id(0)
    x_tile = jnp.where(t < n_ta, xa_ref[...], xb_ref[...]).astype(BF16)
    slot = lax.rem(t, 2)
    tab = (lstart_ref, gstart_ref, nchunk_ref)
    rows = xloc.shape[1]

    def chunk_copy(s, lrow, grow):
        return pltpu.make_async_copy(xloc.at[s, pl.ds(lrow, MOE_CHUNK)], xs_hbm.at[pl.ds(grow, MOE_CHUNK)], sem.at[s])

    def wait_tile(s, tile):
        def body(j, c):
            chunk_copy(s, 0, 0).wait()
            return c
        lax.fori_loop(0, _n_chunks(nchunk_ref, tile), body, 0)

    @pl.when(t == 0)
    def _():
        zbuf[...] = jnp.zeros_like(zbuf)
        zero_copy = lambda grow: pltpu.make_async_copy(
            zbuf.at[pl.ds(0, MOE_CHUNK)], xs_hbm.at[pl.ds(grow, MOE_CHUNK)], zsem.at[0])
        zero_tile = lambda grow: pltpu.make_async_copy(zbuf, xs_hbm.at[pl.ds(grow, MOE_TILE)], zsem.at[0])
        tail_start = zstart_ref[N_EXPERTS]
        for e in range(N_EXPERTS):
            def zb(j, c, e=e):
                zero_copy(pl.multiple_of(zstart_ref[e] + j * MOE_CHUNK, MOE_CHUNK)).start()
                return c
            lax.fori_loop(0, zcount_ref[e], zb, 0)

        def tb(j, c):
            zero_tile(pl.multiple_of(tail_start + j * MOE_TILE, MOE_TILE)).start()
            return c
        lax.fori_loop(0, zcount_ref[N_EXPERTS], tb, 0)
        for e in range(N_EXPERTS):
            def zw(j, c):
                zero_copy(0).wait()
                return c
            lax.fori_loop(0, zcount_ref[e], zw, 0)

        def tw(j, c):
            zero_tile(0).wait()
            return c
        lax.fori_loop(0, zcount_ref[N_EXPERTS], tw, 0)

    @pl.when(t >= 2)
    def _():
        wait_tile(slot, t - 2)

    l_iota = lax.broadcasted_iota(jnp.int32, (rows, xa_ref.shape[0]), 0)
    perm = jnp.logical_or(l_iota == lpos_ref[0:1, :], l_iota == lpos_ref[1:2, :])
    perm = jnp.where(perm, 1.0, 0.0).astype(BF16)
    xloc[slot] = jnp.dot(perm, x_tile, preferred_element_type=F32).astype(BF16)
    _seg_loop(tab, t, lambda lrow, grow: chunk_copy(slot, lrow, grow).start())

    @pl.when(t == n_t - 1)
    def _():
        wait_tile(slot, t)
        if n_t >= 2:
            wait_tile(1 - slot, t - 1)


def _moe_sort(xa, xb, lpos, tab, tt, r_tot):
    n_ta, n_tb = xa.shape[0] // tt, xb.shape[0] // tt
    n_t = n_ta + n_tb
    rows = _local_rows(tt)
    grid_spec = pltpu.PrefetchScalarGridSpec(
        num_scalar_prefetch=5,
        grid=(n_t,),
        in_specs=[pl.BlockSpec((2, tt), lambda t, *_: (0, t)),
                  pl.BlockSpec((tt, D_MODEL), lambda t, *_: (jnp.minimum(t, n_ta - 1), 0)),
                  pl.BlockSpec((tt, D_MODEL), lambda t, *_: (jnp.maximum(t - n_ta, 0), 0))],
        out_specs=pl.BlockSpec(memory_space=pl.ANY),
        scratch_shapes=[pltpu.VMEM((2, rows, D_MODEL), BF16), pltpu.VMEM((MOE_TILE, D_MODEL), BF16),
                        pltpu.SemaphoreType.DMA((2,)), pltpu.SemaphoreType.DMA((1,))])
    return pl.pallas_call(
        functools.partial(_sort_body, n_t=n_t, n_ta=n_ta),
        out_shape=jax.ShapeDtypeStruct((r_tot, D_MODEL), BF16),
        grid_spec=grid_spec,
        compiler_params=_cparams(("arbitrary",)),
        name="moe_sort",
    )(tab["lstart"], tab["gstart"], tab["nchunk"], tab["zstart"], tab["zcount"], lpos, xa, xb)


def _expert_body(te_ref, nu_ref, x_ref, wg_ref, wu_ref, wd_ref, o_ref, wg_s, wu_s, wd_s):
    i = pl.program_id(0)
    used = i < nu_ref[0]
    new_expert = jnp.logical_or(i == 0, te_ref[i] != te_ref[jnp.maximum(i - 1, 0)])

    @pl.when(jnp.logical_and(used, new_expert))
    def _():
        wg_s[...] = wg_ref[0].astype(BF16)
        wu_s[...] = wu_ref[0].astype(BF16)
        wd_s[...] = wd_ref[0].astype(BF16)

    @pl.when(used)
    def _():
        xb = x_ref[...]
        hid = _gelu(jnp.dot(xb, wg_s[...], preferred_element_type=F32)) * jnp.dot(xb, wu_s[...], preferred_element_type=F32)
        o_ref[...] = jnp.dot(hid.astype(BF16), wd_s[...], preferred_element_type=F32).astype(o_ref.dtype)

    @pl.when(jnp.logical_not(used))
    def _():
        o_ref[...] = jnp.zeros_like(o_ref)


def _moe_experts(xs, tab, wg, wu, wd):
    r_tot = xs.shape[0]
    last = lambda i, nu: jnp.minimum(i, nu[0] - 1)
    grid_spec = pltpu.PrefetchScalarGridSpec(
        num_scalar_prefetch=2,
        grid=(r_tot // MOE_TILE,),
        in_specs=[pl.BlockSpec((MOE_TILE, D_MODEL), lambda i, te, nu: (last(i, nu), 0)),
                  pl.BlockSpec((1, D_MODEL, D_EXPERT), lambda i, te, nu: (te[last(i, nu)], 0, 0)),
                  pl.BlockSpec((1, D_MODEL, D_EXPERT), lambda i, te, nu: (te[last(i, nu)], 0, 0)),
                  pl.BlockSpec((1, D_EXPERT, D_MODEL), lambda i, te, nu: (te[last(i, nu)], 0, 0))],
        out_specs=pl.BlockSpec((MOE_TILE, D_MODEL), lambda i, te, nu: (i, 0)),
        scratch_shapes=[pltpu.VMEM((D_MODEL, D_EXPERT), BF16), pltpu.VMEM((D_MODEL, D_EXPERT), BF16),
                        pltpu.VMEM((D_EXPERT, D_MODEL), BF16)])
    return pl.pallas_call(
        _expert_body,
        out_shape=jax.ShapeDtypeStruct((r_tot, D_MODEL), BF16),
        grid_spec=grid_spec,
        compiler_params=_cparams(("arbitrary",)),
        name="moe_experts",
    )(tab["tile_expert"], tab["n_used"], xs, wg, wu, wd)


def _combine_body(lstart_ref, gstart_ref, nchunk_ref, ys_hbm, xa_ref, xb_ref, meta_ref, g_ref, b_ref,
                  oa_ref, ob_ref, yloc, moe_s, sem, *, n_t, n_ta):
    t = pl.program_id(0)
    first = t < n_ta
    slot = lax.rem(t, 2)
    tab = (lstart_ref, gstart_ref, nchunk_ref)
    rows = yloc.shape[1]

    def chunk_copy(s, lrow, grow):
        return pltpu.make_async_copy(ys_hbm.at[pl.ds(grow, MOE_CHUNK)], yloc.at[s, pl.ds(lrow, MOE_CHUNK)], sem.at[s])

    def fetch(s, tile):
        _seg_loop(tab, tile, lambda lrow, grow: chunk_copy(s, lrow, grow).start())

    @pl.when(t == 0)
    def _():
        yloc[...] = jnp.zeros_like(yloc)
        fetch(0, 0)

    @pl.when(t + 1 < n_t)
    def _():
        fetch(1 - slot, t + 1)

    def wbody(j, c):
        chunk_copy(slot, 0, 0).wait()
        return c
    lax.fori_loop(0, _n_chunks(nchunk_ref, t), wbody, 0)

    meta = meta_ref[...]
    tt = meta.shape[0]
    l_iota = lax.broadcasted_iota(jnp.int32, (tt, rows), 1).astype(F32)
    sel = (jnp.where(l_iota == meta[:, 0:1], meta[:, 2:3], 0.0) + jnp.where(l_iota == meta[:, 1:2], meta[:, 3:4], 0.0))
    sel_hi, sel_lo = _split_bf16(sel)
    yb = yloc[slot]
    moe_s[...] = jnp.dot(sel_hi, yb, preferred_element_type=F32) + jnp.dot(sel_lo, yb, preferred_element_type=F32)

    @pl.when(first)
    def _():
        oa_ref[...] = _layer_norm(DN_ALPHA * xa_ref[...] + moe_s[...], g_ref[...], b_ref[...])

    @pl.when(jnp.logical_not(first))
    def _():
        ob_ref[...] = _layer_norm(DN_ALPHA * xb_ref[...] + moe_s[...], g_ref[...], b_ref[...])


def _moe_combine(ys, xa, xb, meta_n, tab, g, b, tt):
    n_ta, n_tb = xa.shape[0] // tt, xb.shape[0] // tt
    n_t = n_ta + n_tb
    rows = _local_rows(tt)
    vec = lambda: pl.BlockSpec((1, D_MODEL), lambda t, *_: (0, 0))
    in_a = pl.BlockSpec((tt, D_MODEL), lambda t, *_: (jnp.minimum(t, n_ta - 1), 0))
    in_b = pl.BlockSpec((tt, D_MODEL), lambda t, *_: (jnp.maximum(t - n_ta, 0), 0))
    grid_spec = pltpu.PrefetchScalarGridSpec(
        num_scalar_prefetch=3,
        grid=(n_t,),
        in_specs=[pl.BlockSpec(memory_space=pl.ANY), in_a, in_b,
                  pl.BlockSpec((tt, 8), lambda t, *_: (t, 0)), vec(), vec()],
        out_specs=(in_a, in_b),
        scratch_shapes=[pltpu.VMEM((2, rows, D_MODEL), BF16), pltpu.VMEM((tt, D_MODEL), F32),
                        pltpu.SemaphoreType.DMA((2,))])
    return pl.pallas_call(
        functools.partial(_combine_body, n_t=n_t, n_ta=n_ta),
        out_shape=(jax.ShapeDtypeStruct(xa.shape, F32), jax.ShapeDtypeStruct(xb.shape, F32)),
        grid_spec=grid_spec,
        compiler_params=_cparams(("arbitrary",)),
        name="moe_combine_ln",
    )(tab["lstart"], tab["gstart"], tab["nchunk"], ys, xa, xb, meta_n, g, b)


def _moe(xa, xb, meta_t, p):
    n = xa.shape[0] + xb.shape[0]
    tt = 256
    while xa.shape[0] % tt or xb.shape[0] % tt:
        tt //= 2
    r_tot, lpos, tab = _dispatch(meta_t, n, tt)
    xs = _moe_sort(xa, xb, lpos, tab, tt, r_tot)
    ys = _moe_experts(xs, tab, p["w_gate"], p["w_up"], p["w_down"])
    meta_n = jnp.concatenate([lpos.astype(F32), meta_t[2:4], jnp.zeros((4, n), F32)], axis=0).T
    return _moe_combine(ys, xa, xb, meta_n, tab, p["ln2_g"], p["ln2_b"], tt)


def _largest_tile(n, cap, mult):
    best = None
    for d in range(mult, min(n, cap) + 1, mult):
        if n % d == 0:
            best = d
    assert best is not None, (n, cap, mult)
    return best


def _seq_tails(x, off, b, t, k, c0, c1):
    if b <= 8:
        return jnp.stack([lax.slice(x, (off + i * t + t - k, c0), (off + (i + 1) * t, c1)) for i in range(b)])
    return lax.slice(x, (off, c0), (off + b * t, c1)).reshape(b, t, c1 - c0)[:, t - k:, :]


def _kv_rows(qkv, g, off, b, t, k):
    def pick(which):
        h0 = (which * N_GROUPS + g) * HEADS
        if b <= 8:
            a = jnp.stack([lax.slice(qkv, (h0, off + i * t + t - k, 0), (h0 + HEADS, off + (i + 1) * t, HEAD_DIM))
                           for i in range(b)], axis=1)
        else:
            a = lax.slice(qkv, (h0, off, 0), (h0 + HEADS, off + b * t, HEAD_DIM)).reshape(HEADS, b, t, HEAD_DIM)[:, :, t - k:]
        return jnp.transpose(a, (1, 2, 0, 3))
    return jnp.stack([pick(1), pick(2)], axis=2)


def _block_diag_gates(w_a, w_x):
    per = RNN_CHUNK // RNN_BLOCK
    eye = jnp.eye(per, dtype=F32)

    def chunks(w):
        w4 = w.reshape(N_RNN_CHUNKS, per, RNN_BLOCK, RNN_BLOCK)
        return (w4[:, :, :, None, :] * eye[None, :, None, :, None]).reshape(N_RNN_CHUNKS, RNN_CHUNK, RNN_CHUNK)
    return jnp.concatenate([chunks(w_a), chunks(w_x)], axis=2).astype(BF16)


def kernel(x_prompt, x_sample, cache_kv_w128, cache_kv_w512, cache_kv_w2048, cache_mem_kv, state_h, state_conv, mem_prompt, w_in, b_gates, conv_w, conv_b, w_a, b_a, w_x, b_x, lru_lambda, w_br_lru, w_br_att, w_br_mem, w_o, w_mem_kv, ln1_g, ln1_b, w_rg, b_rg, w_re, b_re, w_gate, w_up, w_down, ln2_g, ln2_b):
    row = lambda v: v.reshape(1, -1).astype(F32)
    w_router = jnp.zeros((ROUTER_ROWS, D_MODEL), F32)
    w_router = w_router.at[:N_EXPERT_GROUPS].set(w_rg.T).at[N_EXPERT_GROUPS:N_EXPERT_GROUPS + N_EXPERTS].set(w_re.T)
    b_router = jnp.zeros((ROUTER_ROWS, 1), F32)
    b_router = b_router.at[:N_EXPERT_GROUPS, 0].set(b_rg).at[N_EXPERT_GROUPS:N_EXPERT_GROUPS + N_EXPERTS, 0].set(b_re)
    p = dict(
        b_gates=row(b_gates), conv_w=conv_w, conv_b=row(conv_b), wax=_block_diag_gates(w_a, w_x),
        b_a=row(b_a), b_x=row(b_x), lam=row(lru_lambda),
        w_br_lru=w_br_lru.astype(BF16), w_br_att=w_br_att.astype(BF16), w_br_mem=w_br_mem.astype(BF16),
        w_o=w_o.astype(BF16), ln1_g=row(ln1_g), ln1_b=row(ln1_b), w_router=w_router, b_router=b_router,
        w_gate=w_gate, w_up=w_up, w_down=w_down,
        ln2_g=row(ln2_g), ln2_b=row(ln2_b))

    bp, s, _ = x_prompt.shape
    bs, ts, _ = x_sample.shape
    n_p, n_s = bp * s, bs * ts
    xp2, xs2 = x_prompt.reshape(n_p, D_MODEL), x_sample.reshape(n_s, D_MODEL)

    tr = 512
    while n_p % tr or n_s % tr:
        tr //= 2
    xb = _cast_rows(xp2, xs2, BF16, tr)
    tm_a = _largest_tile(n_p + n_s, 2304, 16)
    xrg = _matmul(xb, w_in, 0, 2 * D_RNN, F32, tm_a, 768, "in_proj_rnn")
    qkv = _matmul(xb, w_in, COL_Q, COL_GATES - COL_Q, F32, tm_a, 512, "in_proj_qkv", split=4)
    gates = _matmul(xb, w_in, COL_GATES, N_GATE_COLS, BF16, tm_a, 512, "gate_proj", bias=p["b_gates"])
    lru = (p["conv_w"], p["conv_b"], p["wax"], p["b_a"], p["b_x"], p["lam"])
    branch_w = (p["w_br_lru"], p["w_br_att"], p["w_br_mem"])
    ln1 = (p["w_o"], p["ln1_g"], p["ln1_b"], p["w_router"], p["b_router"])

    mem_rows = mem_prompt.reshape(bp * N_MEM, D_MODEL)
    mem_kv_p = _matmul(mem_rows, w_mem_kv, 0, 2 * D_MEM, F32, _tile(bp * N_MEM, 512), 512, "mem_kv_proj")
    hs_p, h_p = _rglru(xrg, 0, bp, s, jnp.zeros((bp, CONV_W - 1, D_RNN), F32), jnp.zeros((bp, D_RNN), F32), *lru,
                       _tile(s, 256))
    att_p = _attn_prompt(qkv, bp, s)
    mem_p = _mem_attn(qkv, mem_kv_p.reshape(bp, N_MEM, 2 * D_MEM), bp, s, _tile(s, 1024), BF16)
    merged_p = _branch_merge(hs_p, xrg, att_p, mem_p, gates, 0, *branch_w, _tile(n_p, 256))
    x1_p, meta_p = _proj_ln(merged_p, xp2, *ln1, _tile(n_p, 512))

    hs_s, h_s = _rglru(xrg, n_p, bs, ts, state_conv, state_h, *lru, ts)
    att_s = _attn_sample(qkv, (cache_kv_w128, cache_kv_w512, cache_kv_w2048), bs, ts, n_p)
    mem_s = _mem_attn_cache(qkv, cache_mem_kv, bs, ts, n_p)
    tm_s = _tile(n_s, 256)
    assert n_p % tm_s == 0
    merged_s = _branch_merge(hs_s, xrg, att_s, mem_s, gates, n_p, *branch_w, tm_s)
    x1_s, meta_s = _proj_ln(merged_s, xs2, *ln1, _tile(n_s, 512))

    y_p, y_s = _moe(x1_p, x1_s, jnp.concatenate([meta_p, meta_s], axis=1), p)

    kc = CONV_W - 1
    conv_p = jnp.concatenate([jnp.zeros((bp, kc, D_RNN), F32), _seq_tails(xrg, 0, bp, s, min(kc, s), 0, D_RNN)], axis=1)[:, -kc:]
    conv_s = jnp.concatenate([state_conv, _seq_tails(xrg, n_p, bs, ts, min(kc, ts), 0, D_RNN)], axis=1)[:, -kc:]
    kv_p = [_kv_rows(qkv, g, 0, bp, s, min(KEYS * d, s)) for g, d in enumerate(DILATIONS)]
    kv_s = [_kv_rows(qkv, g, n_p, bs, ts, ts) for g in range(N_GROUPS)]
    return (y_p.reshape(bp, s, D_MODEL), y_s.reshape(bs, ts, D_MODEL), kv_p[0], kv_p[1], kv_p[2],
            mem_kv_p.reshape(bp, N_MEM, 2, MEM_HEADS, MEM_HEAD_DIM), h_p.reshape(bp, D_RNN), conv_p,
            kv_s[0], kv_s[1], kv_s[2], h_s.reshape(bs, D_RNN), conv_s)
```

```python
import functools

import jax
import jax.numpy as jnp
from jax import lax
from jax.experimental import pallas as pl
from jax.experimental.pallas import tpu as pltpu

F32 = jnp.float32
BF16 = jnp.bfloat16

D_MODEL = 2048
D_RNN = 1536
N_RNN_BLOCKS = 16
RNN_BLOCK = D_RNN // N_RNN_BLOCKS
RNN_CHUNK = 384
N_RNN_CHUNKS = D_RNN // RNN_CHUNK
CONV_W = 4
LRU_C = 8.0
HEAD_DIM = 128
HEADS = 4
DILATIONS = (1, 4, 16)
KEYS = 128
N_GROUPS = 3
D_ATT_OUT = HEADS * HEAD_DIM
N_QKV_HEADS = 3 * N_GROUPS * HEADS
ATT_SCALE = HEAD_DIM ** -0.5
ATT_TILE = KEYS * max(DILATIONS)
N_MEM = 256
MEM_HEADS = 4
MEM_HEAD_DIM = 256
D_MEM = MEM_HEADS * MEM_HEAD_DIM
MEM_SCALE = MEM_HEAD_DIM ** -0.5
N_EXPERT_GROUPS = 4
EXPERTS_PER_GROUP = 4
N_EXPERTS = 16
D_EXPERT = 512
MOE_TILE = 512
DN_ALPHA = 2.0 ** 0.25
LN_EPS = 1e-5
NEG_INF = -1e30

COL_Q = 2 * D_RNN
COL_QM = COL_Q + N_QKV_HEADS * HEAD_DIM
COL_GATES = COL_QM + D_MEM
N_GATE_COLS = 3 * D_MODEL
ROUTER_ROWS = 32

VMEM_LIMIT = 56 * 1024 * 1024


def _cparams(sem):
    return pltpu.CompilerParams(dimension_semantics=sem, vmem_limit_bytes=VMEM_LIMIT)


def _gelu(x):
    return 0.5 * x * (1.0 + jnp.tanh(0.7978845608028654 * (x + 0.044715 * (x * x * x))))


def _layer_norm(x, g, b):
    mu = jnp.mean(x, axis=-1, keepdims=True)
    xc = x - mu
    var = jnp.mean(xc * xc, axis=-1, keepdims=True)
    return xc * lax.rsqrt(var + LN_EPS) * g + b


def _dot_nt(a, b):
    return lax.dot_general(a, b, (((1,), (1,)), ((), ())), preferred_element_type=F32)


def _tile(m, cap):
    t = min(m, cap)
    assert m % t == 0, (m, t)
    return t


def _mm_body(x_ref, w_ref, *rest, split, gate):
    acc = jnp.dot(x_ref[...].astype(BF16), w_ref[...].astype(BF16), preferred_element_type=F32)
    if gate:
        b_ref, o_ref = rest
        acc = 0.5 * jnp.tanh(0.5 * (acc + b_ref[...])) + 0.5
    else:
        (o_ref,) = rest
    if split == 1:
        o_ref[...] = acc.astype(o_ref.dtype)
    else:
        w = acc.shape[1] // split
        for s in range(split):
            o_ref[s] = acc[:, s * w:(s + 1) * w].astype(o_ref.dtype)


def _matmul(x, w, col_off, n_cols, out_dtype, tm, tn, name, bias=None, split=1):
    m, k = x.shape
    cb = col_off // tn
    in_specs = [pl.BlockSpec((tm, k), lambda i, j: (i, 0)),
                pl.BlockSpec((k, tn), lambda i, j: (0, j + cb))]
    args = [x, w]
    if bias is not None:
        in_specs.append(pl.BlockSpec((1, tn), lambda i, j: (0, j)))
        args.append(bias)
    if split == 1:
        out_shape = jax.ShapeDtypeStruct((m, n_cols), out_dtype)
        out_spec = pl.BlockSpec((tm, tn), lambda i, j: (i, j))
    else:
        out_shape = jax.ShapeDtypeStruct((n_cols * split // tn, m, tn // split), out_dtype)
        out_spec = pl.BlockSpec((split, tm, tn // split), lambda i, j: (j, i, 0))
    return pl.pallas_call(
        functools.partial(_mm_body, split=split, gate=bias is not None),
        out_shape=out_shape,
        grid=(m // tm, n_cols // tn),
        in_specs=in_specs,
        out_specs=out_spec,
        compiler_params=_cparams(("parallel", "arbitrary")),
        name=name,
    )(*args)


def _cast_rows_body(a_ref, b_ref, o_ref, *, n_ta):
    o_ref[...] = jnp.where(pl.program_id(0) < n_ta, a_ref[...], b_ref[...]).astype(o_ref.dtype)


def _cast_rows(xa, xb, dtype, tr):
    n_ta, n_tb = xa.shape[0] // tr, xb.shape[0] // tr
    cols = xa.shape[1]
    return pl.pallas_call(
        functools.partial(_cast_rows_body, n_ta=n_ta),
        out_shape=jax.ShapeDtypeStruct((xa.shape[0] + xb.shape[0], cols), dtype),
        grid=(n_ta + n_tb,),
        in_specs=[pl.BlockSpec((tr, cols), lambda i: (jnp.minimum(i, n_ta - 1), 0)),
                  pl.BlockSpec((tr, cols), lambda i: (jnp.maximum(i - n_ta, 0), 0))],
        out_specs=pl.BlockSpec((tr, cols), lambda i: (i, 0)),
        compiler_params=_cparams(("parallel",)),
        name="cast_rows",
    )(xa, xb)


def _rglru_body(xr_ref, cbuf_ref, h0_ref, cw_ref, cb_ref, wax_ref, ba_ref, bx_ref, lam_ref,
                out_ref, hl_ref, ext_s, a_s, u_s, h_s, *, tt):
    t = pl.program_id(1)

    @pl.when(t == 0)
    def _():
        ext_s[0:8, :] = jnp.zeros((8, D_RNN), F32)
        ext_s[5:8, :] = cbuf_ref[0]
        h_s[...] = jnp.broadcast_to(h0_ref[0], (8, D_RNN))

    @pl.when(t > 0)
    def _():
        ext_s[0:8, :] = ext_s[tt:tt + 8, :]

    ext_s[8:8 + tt, :] = xr_ref[...]
    cw = cw_ref[...]
    xc = (cb_ref[...] + cw[3:4, :] * ext_s[8:8 + tt, :] + cw[2:3, :] * ext_s[7:7 + tt, :]
          + cw[1:2, :] * ext_s[6:6 + tt, :] + cw[0:1, :] * ext_s[5:5 + tt, :])
    xcb = xc.astype(BF16)
    r_parts, i_parts = [], []
    for c in range(N_RNN_CHUNKS):
        g = jnp.dot(xcb[:, c * RNN_CHUNK:(c + 1) * RNN_CHUNK], wax_ref[c], preferred_element_type=F32)
        r_parts.append(g[:, :RNN_CHUNK])
        i_parts.append(g[:, RNN_CHUNK:])
    sigmoid = lambda z: 0.5 * jnp.tanh(0.5 * z) + 0.5
    r = sigmoid(jnp.concatenate(r_parts, axis=1) + ba_ref[...])
    gi = sigmoid(jnp.concatenate(i_parts, axis=1) + bx_ref[...])
    nl = -lam_ref[...]
    softplus = jnp.maximum(nl, 0.0) + jnp.log1p(jnp.exp(-jnp.abs(nl)))
    log_a = (-LRU_C) * r * softplus
    th = jnp.tanh(log_a)
    a_s[...] = jnp.exp(log_a)
    u_s[...] = jnp.sqrt(-2.0 * th / (1.0 - th)) * (gi * xc)

    rows = lax.broadcasted_iota(jnp.int32, (8, D_RNN), 0)

    def blk(i, h):
        r0 = pl.multiple_of(i * 8, 8)
        ab = a_s[pl.ds(r0, 8), :]
        ub = u_s[pl.ds(r0, 8), :]
        for s in (1, 2, 4):
            keep = rows >= s
            ub = ab * jnp.where(keep, pltpu.roll(ub, s, 0), 0.0) + ub
            ab = ab * jnp.where(keep, pltpu.roll(ab, s, 0), 1.0)
        hb = ab * h + ub
        u_s[pl.ds(r0, 8), :] = hb
        return jnp.broadcast_to(hb[7:8, :], (8, D_RNN))

    h_fin = lax.fori_loop(0, tt // 8, blk, h_s[...])
    h_s[...] = h_fin
    hl_ref[0] = h_fin[0:1, :]
    out_ref[...] = u_s[...]


def _rglru(xrg, off, b, t, conv_buf, h0, conv_w, conv_b, wax, b_a, b_x, lam, tt):
    nt = t // tt
    ob = off // tt
    vec = lambda: pl.BlockSpec((1, D_RNN), lambda i, j: (0, 0))
    return pl.pallas_call(
        functools.partial(_rglru_body, tt=tt),
        out_shape=(jax.ShapeDtypeStruct((b * t, D_RNN), F32), jax.ShapeDtypeStruct((b, 1, D_RNN), F32)),
        grid=(b, nt),
        in_specs=[pl.BlockSpec((tt, D_RNN), lambda i, j: (ob + i * nt + j, 0)),
                  pl.BlockSpec((1, CONV_W - 1, D_RNN), lambda i, j: (i, 0, 0)),
                  pl.BlockSpec((1, 1, D_RNN), lambda i, j: (i, 0, 0)),
                  pl.BlockSpec((CONV_W, D_RNN), lambda i, j: (0, 0)),
                  vec(),
                  pl.BlockSpec((N_RNN_CHUNKS, RNN_CHUNK, 2 * RNN_CHUNK), lambda i, j: (0, 0, 0)),
                  vec(), vec(), vec()],
        out_specs=(pl.BlockSpec((tt, D_RNN), lambda i, j: (i * nt + j, 0)),
                   pl.BlockSpec((1, 1, D_RNN), lambda i, j: (i, 0, 0))),
        scratch_shapes=[pltpu.VMEM((tt + 8, D_RNN), F32), pltpu.VMEM((tt, D_RNN), F32),
                        pltpu.VMEM((tt, D_RNN), F32), pltpu.VMEM((8, D_RNN), F32)],
        compiler_params=_cparams(("parallel", "arbitrary")),
        name="rglru",
    )(xrg, conv_buf, h0.reshape(b, 1, D_RNN), conv_w, conv_b, wax, b_a, b_x, lam)


def _band_block(q, k, v, bias):
    s = _dot_nt(q, k) * ATT_SCALE + bias
    m = jnp.max(s, axis=-1, keepdims=True)
    p = jnp.exp(s - m)
    l = jnp.sum(p, axis=-1, keepdims=True)
    o = jnp.dot(p.astype(BF16), v, preferred_element_type=F32) / l
    return o, m + jnp.log(l)


def _attn_prompt_body(*refs):
    q_refs, k_refs, v_refs, kh_refs, vh_refs = (refs[3 * i:3 * i + 3] for i in range(5))
    o_ref, og_s, lg_s = refs[15:]
    row = lax.broadcasted_iota(jnp.int32, (KEYS, 2 * KEYS), 0)
    col = lax.broadcasted_iota(jnp.int32, (KEYS, 2 * KEYS), 1)
    band = jnp.logical_and(col >= row, col <= row + KEYS)
    bias = jnp.where(band, 0.0, NEG_INF)
    has_prev = pl.program_id(1) > 0
    bias_first = jnp.where(jnp.logical_and(band, jnp.logical_or(col >= KEYS, has_prev)), 0.0, NEG_INF)

    def put(g, rows, o, lse):
        og_s[g, rows, :] = o
        lg_s[g, rows, :] = jnp.broadcast_to(lse, (KEYS, HEAD_DIM))

    def rows_of(start, size, d):
        return pl.ds(start, size, stride=d) if d > 1 else pl.ds(start, size)

    for g, d in enumerate(DILATIONS):
        q_ref, k_ref, v_ref, kh_ref, vh_ref = q_refs[g], k_refs[g], v_refs[g], kh_refs[g], vh_refs[g]
        nqb = ATT_TILE // (d * KEYS)
        for r in range(d):
            own = rows_of(r, KEYS, d)
            kk = jnp.concatenate([kh_ref[0, own, :], k_ref[0, own, :]], axis=0).astype(BF16)
            vv = jnp.concatenate([vh_ref[0, own, :], v_ref[0, own, :]], axis=0).astype(BF16)
            o, lse = _band_block(q_ref[0, own, :].astype(BF16), kk, vv, bias_first)
            put(g, own, o, lse)
            for qb in range(1, nqb):
                rows = rows_of(qb * KEYS * d + r, KEYS, d)
                keys = rows_of((qb - 1) * KEYS * d + r, 2 * KEYS, d)
                o, lse = _band_block(q_ref[0, rows, :].astype(BF16), k_ref[0, keys, :].astype(BF16),
                                     v_ref[0, keys, :].astype(BF16), bias)
                put(g, rows, o, lse)

    l0, l1, l2 = lg_s[0], lg_s[1], lg_s[2]
    m = jnp.maximum(jnp.maximum(l0, l1), l2)
    e0, e1, e2 = jnp.exp(l0 - m), jnp.exp(l1 - m), jnp.exp(l2 - m)
    o_ref[...] = ((e0 * og_s[0] + e1 * og_s[1] + e2 * og_s[2]) / (e0 + e1 + e2)).astype(o_ref.dtype)


def _attn_prompt(qkv, b, s):
    assert s % ATT_TILE == 0
    nt = s // ATT_TILE
    n = b * s

    def cur(which, g):
        return pl.BlockSpec((1, ATT_TILE, HEAD_DIM),
                            lambda i, j, h: ((which * N_GROUPS + g) * HEADS + h, i * nt + j, 0))

    def halo(which, g):
        rows = KEYS * DILATIONS[g]
        per = ATT_TILE // rows
        return pl.BlockSpec((1, rows, HEAD_DIM),
                            lambda i, j, h: ((which * N_GROUPS + g) * HEADS + h,
                                             jnp.maximum((i * nt + j) * per - 1, 0), 0))

    in_specs = ([cur(0, g) for g in range(N_GROUPS)] + [cur(1, g) for g in range(N_GROUPS)]
                + [cur(2, g) for g in range(N_GROUPS)] + [halo(1, g) for g in range(N_GROUPS)]
                + [halo(2, g) for g in range(N_GROUPS)])
    return pl.pallas_call(
        _attn_prompt_body,
        out_shape=jax.ShapeDtypeStruct((n, D_ATT_OUT), BF16),
        grid=(b, nt, HEADS),
        in_specs=in_specs,
        out_specs=pl.BlockSpec((ATT_TILE, HEAD_DIM), lambda i, j, h: (i * nt + j, h)),
        scratch_shapes=[pltpu.VMEM((N_GROUPS, ATT_TILE, HEAD_DIM), F32),
                        pltpu.VMEM((N_GROUPS, ATT_TILE, HEAD_DIM), F32)],
        compiler_params=_cparams(("parallel", "parallel", "parallel")),
        name="attn_prompt",
    )(*([qkv] * 15))


def _attn_sample_group(q4, kn4, vn4, c_ref, d, t_new):
    half = len(c_ref.shape) == 4
    wb = c_ref.shape[1] * 16 if half else c_ref.shape[1] // (2 * HEADS)
    nk = wb // 2 if half else wb
    nr = HEADS * t_new
    zeros = jnp.zeros((t_new, HEAD_DIM), F32)
    qbd = jnp.concatenate(
        [jnp.concatenate([q4[h] if hh == h else zeros for hh in range(HEADS)], axis=1) for h in range(HEADS)],
        axis=0).astype(BF16)
    kn = jnp.concatenate([kn4[h] for h in range(HEADS)], axis=1).astype(BF16)
    vn = jnp.concatenate([vn4[h] for h in range(HEADS)], axis=1).astype(BF16)
    if half:
        cache_rows = lambda kv, h: c_ref[0, :, pl.ds(kv * HEADS + h, 8, stride=2 * HEADS), :].reshape(nk, HEAD_DIM)
    else:
        cache_rows = lambda kv, h: c_ref[0, pl.ds(kv * HEADS + h, wb, stride=2 * HEADS), :]
    kc = jnp.concatenate([cache_rows(0, h) for h in range(HEADS)], axis=1).astype(BF16)
    vc = jnp.concatenate([cache_rows(1, h) for h in range(HEADS)], axis=1).astype(BF16)
    tq_c = lax.broadcasted_iota(jnp.int32, (nr, nk), 0) & (t_new - 1)
    e_c = lax.broadcasted_iota(jnp.int32, (nr, nk), 1)
    if half:
        e_c = ((e_c >> 3) << 4) + (e_c & 7)
    dist_c = wb + tq_c - e_c
    ok_c = jnp.logical_and((dist_c & (d - 1)) == 0, dist_c <= KEYS * d)
    tq_n = lax.broadcasted_iota(jnp.int32, (nr, t_new), 0) & (t_new - 1)
    dist_n = tq_n - lax.broadcasted_iota(jnp.int32, (nr, t_new), 1)
    ok_n = jnp.logical_and(jnp.logical_and(dist_n >= 0, (dist_n & (d - 1)) == 0), dist_n <= KEYS * d)
    s_c = jnp.where(ok_c, _dot_nt(qbd, kc) * ATT_SCALE, NEG_INF)
    s_n = jnp.where(ok_n, _dot_nt(qbd, kn) * ATT_SCALE, NEG_INF)
    m = jnp.maximum(jnp.max(s_c, axis=-1, keepdims=True), jnp.max(s_n, axis=-1, keepdims=True))
    p_c = jnp.exp(s_c - m)
    p_n = jnp.exp(s_n - m)
    l = jnp.sum(p_c, axis=-1, keepdims=True) + jnp.sum(p_n, axis=-1, keepdims=True)
    o = (jnp.dot(p_c.astype(BF16), vc, preferred_element_type=F32)
         + jnp.dot(p_n.astype(BF16), vn, preferred_element_type=F32)) / l
    lse = m + jnp.log(l)
    o = jnp.concatenate(
        [o[h * t_new:(h + 1) * t_new, h * HEAD_DIM:(h + 1) * HEAD_DIM] for h in range(HEADS)], axis=1)
    lse = jnp.concatenate(
        [jnp.broadcast_to(lse[h * t_new:(h + 1) * t_new], (t_new, HEAD_DIM)) for h in range(HEADS)], axis=1)
    return o, lse


def _attn_sample_body(*refs, t_new):
    q_refs, k_refs, v_refs, c_refs = (refs[3 * i:3 * i + 3] for i in range(4))
    o_ref = refs[12]
    outs = [_attn_sample_group(q_refs[g][...], k_refs[g][...], v_refs[g][...], c_refs[g], d, t_new)
            for g, d in enumerate(DILATIONS)]
    (o0, l0), (o1, l1), (o2, l2) = outs
    m = jnp.maximum(jnp.maximum(l0, l1), l2)
    e0, e1, e2 = jnp.exp(l0 - m), jnp.exp(l1 - m), jnp.exp(l2 - m)
    o_ref[0] = (e0 * o0 + e1 * o1 + e2 * o2) / (e0 + e1 + e2)


def _attn_sample(qkv, caches, b, t_new, off):
    assert t_new & (t_new - 1) == 0 and off % t_new == 0
    ob = off // t_new
    new = lambda which, g: pl.BlockSpec((HEADS, t_new, HEAD_DIM), lambda i: (which * N_GROUPS + g, ob + i, 0))
    caches2, cache_specs = [], []
    for c, d in zip(caches, DILATIONS):
        wb = c.shape[1]
        if d == 16 and wb % 16 == 0 and t_new <= 8:
            caches2.append(c.reshape(b, wb // 16, 16 * 2 * HEADS, HEAD_DIM))
            cache_specs.append(pl.BlockSpec((1, wb // 16, 8 * 2 * HEADS, HEAD_DIM), lambda i: (i, 0, 0, 0)))
        else:
            caches2.append(c.reshape(b, wb * 2 * HEADS, HEAD_DIM))
            cache_specs.append(pl.BlockSpec((1, wb * 2 * HEADS, HEAD_DIM), lambda i: (i, 0, 0)))
    in_specs = ([new(0, g) for g in range(N_GROUPS)] + [new(1, g) for g in range(N_GROUPS)]
                + [new(2, g) for g in range(N_GROUPS)] + cache_specs)
    att = pl.pallas_call(
        functools.partial(_attn_sample_body, t_new=t_new),
        out_shape=jax.ShapeDtypeStruct((b, t_new, D_ATT_OUT), F32),
        grid=(b,),
        in_specs=in_specs,
        out_specs=pl.BlockSpec((1, t_new, D_ATT_OUT), lambda i: (i, 0, 0)),
        compiler_params=_cparams(("parallel",)),
        name="attn_sample",
    )(*([qkv] * 9), *caches2)
    return att.reshape(b * t_new, D_ATT_OUT)


def _mem_q_specs(rows, index):
    first = N_QKV_HEADS // 2
    return [pl.BlockSpec((2, rows, HEAD_DIM), functools.partial(index, first + h)) for h in range(MEM_HEADS)]


def _mem_attn_body(q0_ref, q1_ref, q2_ref, q3_ref, kv_ref, o_ref):
    for h, q_ref in enumerate((q0_ref, q1_ref, q2_ref, q3_ref)):
        k = kv_ref[0, :, h * MEM_HEAD_DIM:(h + 1) * MEM_HEAD_DIM].astype(BF16)
        v = kv_ref[0, :, D_MEM + h * MEM_HEAD_DIM:D_MEM + (h + 1) * MEM_HEAD_DIM].astype(BF16)
        q = jnp.concatenate([q_ref[0], q_ref[1]], axis=1).astype(BF16)
        s = _dot_nt(q, k) * MEM_SCALE
        m = jnp.max(s, axis=-1, keepdims=True)
        p = jnp.exp(s - m)
        l = jnp.sum(p, axis=-1, keepdims=True)
        o = jnp.dot(p.astype(BF16), v, preferred_element_type=F32) / l
        o_ref[:, h * MEM_HEAD_DIM:(h + 1) * MEM_HEAD_DIM] = o.astype(o_ref.dtype)


def _mem_attn(qkvm, mem_kv, b, t, tm, out_dtype):
    nt = t // tm
    return pl.pallas_call(
        _mem_attn_body,
        out_shape=jax.ShapeDtypeStruct((b * t, D_MEM), out_dtype),
        grid=(b, nt),
        in_specs=_mem_q_specs(tm, lambda blk, i, j: (blk, i * nt + j, 0))
        + [pl.BlockSpec((1, N_MEM, 2 * D_MEM), lambda i, j: (i, 0, 0))],
        out_specs=pl.BlockSpec((tm, D_MEM), lambda i, j: (i * nt + j, 0)),
        compiler_params=_cparams(("parallel", "parallel")),
        name="mem_attn",
    )(qkvm, qkvm, qkvm, qkvm, mem_kv)


def _mem_attn_cache_body(q0_ref, q1_ref, q2_ref, q3_ref, c_ref, o_ref):
    t = q0_ref.shape[1]
    n_rows = N_MEM * 2 * MEM_HEADS
    flat = c_ref[0].reshape(n_rows, MEM_HEAD_DIM).astype(BF16)
    q = jnp.concatenate([jnp.concatenate([q_ref[0], q_ref[1]], axis=1) for q_ref in (q0_ref, q1_ref, q2_ref, q3_ref)],
                        axis=0).astype(BF16)
    assert t & (t - 1) == 0
    head = lax.broadcasted_iota(jnp.int32, (MEM_HEADS * t, n_rows), 0) >> (t.bit_length() - 1)
    col = lax.broadcasted_iota(jnp.int32, (MEM_HEADS * t, n_rows), 1)
    s = jnp.where((col & (2 * MEM_HEADS - 1)) == head, _dot_nt(q, flat) * MEM_SCALE, NEG_INF)
    m = jnp.max(s, axis=-1, keepdims=True)
    p = jnp.exp(s - m)
    l = jnp.sum(p, axis=-1, keepdims=True)
    o = jnp.dot(pltpu.roll(p, MEM_HEADS, 1).astype(BF16), flat, preferred_element_type=F32) / l
    o_ref[...] = jnp.concatenate([o[h * t:(h + 1) * t, :] for h in range(MEM_HEADS)], axis=1)


def _mem_attn_cache(qkvm, cache, b, t, off):
    ob = off // t
    return pl.pallas_call(
        _mem_attn_cache_body,
        out_shape=jax.ShapeDtypeStruct((b * t, D_MEM), F32),
        grid=(b,),
        in_specs=_mem_q_specs(t, lambda blk, i: (blk, ob + i, 0))
        + [pl.BlockSpec((1, N_MEM, 2, MEM_HEADS, MEM_HEAD_DIM), lambda i: (i, 0, 0, 0, 0))],
        out_specs=pl.BlockSpec((t, D_MEM), lambda i: (i, 0)),
        compiler_params=_cparams(("parallel",)),
        name="mem_attn_cache",
    )(qkvm, qkvm, qkvm, qkvm, cache)


def _branch_body(hs_ref, xg_ref, att_ref, mem_ref, gt_ref, wl_ref, wa_ref, wm_ref, out_ref):
    a_lru = (hs_ref[...] * _gelu(xg_ref[...])).astype(BF16)
    acc = gt_ref[:, 0:D_MODEL].astype(F32) * jnp.dot(a_lru, wl_ref[...], preferred_element_type=F32)
    acc = acc + gt_ref[:, D_MODEL:2 * D_MODEL].astype(F32) * jnp.dot(att_ref[...].astype(BF16), wa_ref[...], preferred_element_type=F32)
    acc = acc + gt_ref[:, 2 * D_MODEL:3 * D_MODEL].astype(F32) * jnp.dot(mem_ref[...].astype(BF16), wm_ref[...], preferred_element_type=F32)
    out_ref[...] = acc.astype(out_ref.dtype)


def _branch_merge(hs, xrg, att, mem, gates, goff, wl, wa, wm, tm):
    n = hs.shape[0]
    gb = goff // tm
    row = lambda w: pl.BlockSpec((tm, w), lambda i: (i, 0))
    full = lambda a: pl.BlockSpec(a.shape, lambda i: (0, 0))
    return pl.pallas_call(
        _branch_body,
        out_shape=jax.ShapeDtypeStruct((n, D_MODEL), BF16),
        grid=(n // tm,),
        in_specs=[row(D_RNN), pl.BlockSpec((tm, D_RNN), lambda i: (gb + i, 1)), row(D_ATT_OUT), row(D_MEM),
                  pl.BlockSpec((tm, N_GATE_COLS), lambda i: (gb + i, 0)), full(wl), full(wa), full(wm)],
        out_specs=row(D_MODEL),
        compiler_params=_cparams(("parallel",)),
        name="branch_merge",
    )(hs, xrg, att, mem, gates, wl, wa, wm)


def _split_bf16(x):
    hi = x.astype(BF16)
    return hi, (x - hi.astype(F32)).astype(BF16)


def _route_rows(lg):
    g = [lg[i:i + 1, :] for i in range(N_EXPERT_GROUPS)]
    gmax = jnp.maximum(jnp.maximum(g[0], g[1]), jnp.maximum(g[2], g[3]))
    gidx = jnp.where(g[0] == gmax, 0.0, jnp.where(g[1] == gmax, 1.0, jnp.where(g[2] == gmax, 2.0, 3.0)))
    g_p = 1.0 / (jnp.exp(g[0] - gmax) + jnp.exp(g[1] - gmax) + jnp.exp(g[2] - gmax) + jnp.exp(g[3] - gmax))
    e = []
    for k in range(EXPERTS_PER_GROUP):
        rows = [lg[N_EXPERT_GROUPS + gg * EXPERTS_PER_GROUP + k:N_EXPERT_GROUPS + gg * EXPERTS_PER_GROUP + k + 1, :]
                for gg in range(N_EXPERT_GROUPS)]
        e.append(jnp.where(gidx == 0.0, rows[0], jnp.where(gidx == 1.0, rows[1], jnp.where(gidx == 2.0, rows[2], rows[3]))))

    def first_argmax(v):
        mx = jnp.maximum(jnp.maximum(v[0], v[1]), jnp.maximum(v[2], v[3]))
        ix = jnp.where(v[0] == mx, 0.0, jnp.where(v[1] == mx, 1.0, jnp.where(v[2] == mx, 2.0, 3.0)))
        return mx, ix

    v1, i1 = first_argmax(e)
    v2, i2 = first_argmax([jnp.where(i1 == float(k), -jnp.inf, e[k]) for k in range(EXPERTS_PER_GROUP)])
    ex = jnp.exp(v2 - v1)
    w1 = g_p / (1.0 + ex)
    w2 = g_p * ex / (1.0 + ex)
    base = gidx * float(EXPERTS_PER_GROUP)
    zero = jnp.zeros_like(w1)
    return jnp.concatenate([base + i1, base + i2, w1, w2, zero, zero, zero, zero], axis=0)


def _proj_ln_body(mg_ref, x_ref, wo_ref, g_ref, b_ref, wr_ref, br_ref, x1_ref, meta_ref, mix_s):
    @pl.when(pl.program_id(0) == 0)
    def _():
        mix_s[...] = jnp.zeros(mix_s.shape, F32)

    mix_next = jnp.dot(mg_ref[...], wo_ref[...], preferred_element_type=F32)
    wh, wl = _split_bf16(wr_ref[...])
    tm = x_ref.shape[0]
    sub = min(256, tm)
    for r0 in range(0, tm, sub):
        rs = pl.ds(r0, sub)
        x1 = _layer_norm(DN_ALPHA * x_ref[rs, :] + mix_s[rs, :], g_ref[...], b_ref[...])
        x1_ref[rs, :] = x1
        xh, xl = _split_bf16(x1)
        lg = _dot_nt(wh, xh) + (_dot_nt(wh, xl) + _dot_nt(wl, xh)) + br_ref[...]
        meta_ref[:, rs] = _route_rows(lg)
    mix_s[...] = mix_next


def _proj_ln(merged, x, wo, g, b, wr, br, tm):
    n = merged.shape[0]
    nb = n // tm
    prev = lambda w: pl.BlockSpec((tm, w), lambda i: (jnp.maximum(i - 1, 0), 0))
    full = lambda a: pl.BlockSpec(a.shape, lambda i: (0, 0))
    return pl.pallas_call(
        _proj_ln_body,
        out_shape=(jax.ShapeDtypeStruct((n, D_MODEL), F32), jax.ShapeDtypeStruct((8, n), F32)),
        grid=(nb + 1,),
        in_specs=[pl.BlockSpec((tm, D_MODEL), lambda i: (jnp.minimum(i, nb - 1), 0)), prev(D_MODEL),
                  full(wo), full(g), full(b), full(wr), full(br)],
        out_specs=(prev(D_MODEL), pl.BlockSpec((8, tm), lambda i: (0, jnp.maximum(i - 1, 0)))),
        scratch_shapes=[pltpu.VMEM((tm, D_MODEL), F32)],
        compiler_params=_cparams(("arbitrary",)),
        name="proj_ln_router",
    )(merged, x, wo, g, b, wr, br)


MOE_CHUNK = 16


def _local_rows(tt):
    return -(-(2 * tt + N_EXPERTS * (MOE_CHUNK - 1)) // 128) * 128


def _dispatch(meta_t, n, tt):
    n_t = n // tt
    ids = meta_t[0:2].astype(jnp.int32)
    onehot = (ids[:, :, None] == jnp.arange(N_EXPERTS, dtype=jnp.int32)).astype(jnp.int32).reshape(2, n_t, tt, N_EXPERTS)
    cnt_slot = jnp.sum(onehot, axis=2)
    cnt = cnt_slot[0] + cnt_slot[1]
    pc = (cnt + MOE_CHUNK - 1) // MOE_CHUNK * MOE_CHUNK
    lstart = jnp.cumsum(pc, axis=1) - pc
    tri = (jnp.arange(tt)[:, None] >= jnp.arange(tt)[None, :]).astype(F32)
    csum = jnp.einsum("ut,snte->snue", tri, onehot.astype(F32)).astype(jnp.int32)
    rank = csum - onehot + jnp.stack([jnp.zeros_like(cnt), cnt_slot[0]])[:, :, None, :]
    lpos = jnp.sum(onehot * (lstart[None, :, None, :] + rank), axis=-1).reshape(2, n)
    seg = jnp.sum(pc, axis=0)
    pe = (seg + MOE_TILE - 1) // MOE_TILE * MOE_TILE
    ends = jnp.cumsum(pe)
    base = ends - pe
    gstart = base[None, :] + jnp.cumsum(pc, axis=0) - pc
    r_tot = -(-(2 * n + N_EXPERTS * (MOE_CHUNK - 1) * n_t) // MOE_TILE) * MOE_TILE + N_EXPERTS * MOE_TILE
    n_tiles = r_tot // MOE_TILE
    tile_start = jnp.arange(n_tiles, dtype=jnp.int32) * MOE_TILE
    tile_expert = jnp.minimum(jnp.sum((tile_start[:, None] >= ends[None, :]).astype(jnp.int32), axis=1), N_EXPERTS - 1)
    tables = dict(
        lstart=lstart.reshape(-1), gstart=gstart.reshape(-1), nchunk=(pc // MOE_CHUNK).reshape(-1),
        zstart=jnp.concatenate([base + seg, ends[-1:]]),
        zcount=jnp.concatenate([(pe - seg) // MOE_CHUNK, (r_tot - ends[-1:]) // MOE_TILE]),
        tile_expert=tile_expert, n_used=(ends[-1] // MOE_TILE).reshape(1))
    return r_tot, lpos, tables


def _seg_loop(tab, t, fn):
    lstart_ref, gstart_ref, nchunk_ref = tab
    for e in range(N_EXPERTS):
        ls = lstart_ref[t * N_EXPERTS + e]
        gs = gstart_ref[t * N_EXPERTS + e]

        def body(j, c, ls=ls, gs=gs):
            fn(pl.multiple_of(ls + j * MOE_CHUNK, MOE_CHUNK), pl.multiple_of(gs + j * MOE_CHUNK, MOE_CHUNK))
            return c
        lax.fori_loop(0, nchunk_ref[t * N_EXPERTS + e], body, 0)


def _n_chunks(nchunk_ref, t):
    tot = nchunk_ref[t * N_EXPERTS]
    for e in range(1, N_EXPERTS):
        tot = tot + nchunk_ref[t * N_EXPERTS + e]
    return tot


def _sort_body(lstart_ref, gstart_ref, nchunk_ref, zstart_ref, zcount_ref, lpos_ref, xa_ref, xb_ref, xs_hbm,
               xloc, zbuf, sem, zsem, *, n_t, n_ta):
    t = pl.program_id(0)
    x_tile = jnp.where(t < n_ta, xa_ref[...], xb_ref[...]).astype(BF16)
    slot = lax.rem(t, 2)
    tab = (lstart_ref, gstart_ref, nchunk_ref)
    rows = xloc.shape[1]

    def chunk_copy(s, lrow, grow):
        return pltpu.make_async_copy(xloc.at[s, pl.ds(lrow, MOE_CHUNK)], xs_hbm.at[pl.ds(grow, MOE_CHUNK)], sem.at[s])

    def wait_tile(s, tile):
        def body(j, c):
            chunk_copy(s, 0, 0).wait()
            return c
        lax.fori_loop(0, _n_chunks(nchunk_ref, tile), body, 0)

    @pl.when(t == 0)
    def _():
        zbuf[...] = jnp.zeros_like(zbuf)
        zero_copy = lambda grow: pltpu.make_async_copy(
            zbuf.at[pl.ds(0, MOE_CHUNK)], xs_hbm.at[pl.ds(grow, MOE_CHUNK)], zsem.at[0])
        zero_tile = lambda grow: pltpu.make_async_copy(zbuf, xs_hbm.at[pl.ds(grow, MOE_TILE)], zsem.at[0])
        tail_start = zstart_ref[N_EXPERTS]
        for e in range(N_EXPERTS):
            def zb(j, c, e=e):
                zero_copy(pl.multiple_of(zstart_ref[e] + j * MOE_CHUNK, MOE_CHUNK)).start()
                return c
            lax.fori_loop(0, zcount_ref[e], zb, 0)

        def tb(j, c):
            zero_tile(pl.multiple_of(tail_start + j * MOE_TILE, MOE_TILE)).start()
            return c
        lax.fori_loop(0, zcount_ref[N_EXPERTS], tb, 0)
        for e in range(N_EXPERTS):
            def zw(j, c):
                zero_copy(0).wait()
                return c
            lax.fori_loop(0, zcount_ref[e], zw, 0)

        def tw(j, c):
            zero_tile(0).wait()
            return c
        lax.fori_loop(0, zcount_ref[N_EXPERTS], tw, 0)

    @pl.when(t >= 2)
    def _():
        wait_tile(slot, t - 2)

    l_iota = lax.broadcasted_iota(jnp.int32, (rows, xa_ref.shape[0]), 0)
    perm = jnp.logical_or(l_iota == lpos_ref[0:1, :], l_iota == lpos_ref[1:2, :])
    perm = jnp.where(perm, 1.0, 0.0).astype(BF16)
    xloc[slot] = jnp.dot(perm, x_tile, preferred_element_type=F32).astype(BF16)
    _seg_loop(tab, t, lambda lrow, grow: chunk_copy(slot, lrow, grow).start())

    @pl.when(t == n_t - 1)
    def _():
        wait_tile(slot, t)
        if n_t >= 2:
            wait_tile(1 - slot, t - 1)


def _moe_sort(xa, xb, lpos, tab, tt, r_tot):
    n_ta, n_tb = xa.shape[0] // tt, xb.shape[0] // tt
    n_t = n_ta + n_tb
    rows = _local_rows(tt)
    grid_spec = pltpu.PrefetchScalarGridSpec(
        num_scalar_prefetch=5,
        grid=(n_t,),
        in_specs=[pl.BlockSpec((2, tt), lambda t, *_: (0, t)),
                  pl.BlockSpec((tt, D_MODEL), lambda t, *_: (jnp.minimum(t, n_ta - 1), 0)),
                  pl.BlockSpec((tt, D_MODEL), lambda t, *_: (jnp.maximum(t - n_ta, 0), 0))],
        out_specs=pl.BlockSpec(memory_space=pl.ANY),
        scratch_shapes=[pltpu.VMEM((2, rows, D_MODEL), BF16), pltpu.VMEM((MOE_TILE, D_MODEL), BF16),
                        pltpu.SemaphoreType.DMA((2,)), pltpu.SemaphoreType.DMA((1,))])
    return pl.pallas_call(
        functools.partial(_sort_body, n_t=n_t, n_ta=n_ta),
        out_shape=jax.ShapeDtypeStruct((r_tot, D_MODEL), BF16),
        grid_spec=grid_spec,
        compiler_params=_cparams(("arbitrary",)),
        name="moe_sort",
    )(tab["lstart"], tab["gstart"], tab["nchunk"], tab["zstart"], tab["zcount"], lpos, xa, xb)


def _expert_body(te_ref, nu_ref, x_ref, wg_ref, wu_ref, wd_ref, o_ref, wg_s, wu_s, wd_s):
    i = pl.program_id(0)
    used = i < nu_ref[0]
    new_expert = jnp.logical_or(i == 0, te_ref[i] != te_ref[jnp.maximum(i - 1, 0)])

    @pl.when(jnp.logical_and(used, new_expert))
    def _():
        wg_s[...] = wg_ref[0].astype(BF16)
        wu_s[...] = wu_ref[0].astype(BF16)
        wd_s[...] = wd_ref[0].astype(BF16)

    @pl.when(used)
    def _():
        xb = x_ref[...]
        hid = _gelu(jnp.dot(xb, wg_s[...], preferred_element_type=F32)) * jnp.dot(xb, wu_s[...], preferred_element_type=F32)
        o_ref[...] = jnp.dot(hid.astype(BF16), wd_s[...], preferred_element_type=F32).astype(o_ref.dtype)

    @pl.when(jnp.logical_not(used))
    def _():
        o_ref[...] = jnp.zeros_like(o_ref)


def _moe_experts(xs, tab, wg, wu, wd):
    r_tot = xs.shape[0]
    last = lambda i, nu: jnp.minimum(i, nu[0] - 1)
    grid_spec = pltpu.PrefetchScalarGridSpec(
        num_scalar_prefetch=2,
        grid=(r_tot // MOE_TILE,),
        in_specs=[pl.BlockSpec((MOE_TILE, D_MODEL), lambda i, te, nu: (last(i, nu), 0)),
                  pl.BlockSpec((1, D_MODEL, D_EXPERT), lambda i, te, nu: (te[last(i, nu)], 0, 0)),
                  pl.BlockSpec((1, D_MODEL, D_EXPERT), lambda i, te, nu: (te[last(i, nu)], 0, 0)),
                  pl.BlockSpec((1, D_EXPERT, D_MODEL), lambda i, te, nu: (te[last(i, nu)], 0, 0))],
        out_specs=pl.BlockSpec((MOE_TILE, D_MODEL), lambda i, te, nu: (i, 0)),
        scratch_shapes=[pltpu.VMEM((D_MODEL, D_EXPERT), BF16), pltpu.VMEM((D_MODEL, D_EXPERT), BF16),
                        pltpu.VMEM((D_EXPERT, D_MODEL), BF16)])
    return pl.pallas_call(
        _expert_body,
        out_shape=jax.ShapeDtypeStruct((r_tot, D_MODEL), BF16),
        grid_spec=grid_spec,
        compiler_params=_cparams(("arbitrary",)),
        name="moe_experts",
    )(tab["tile_expert"], tab["n_used"], xs, wg, wu, wd)


def _combine_body(lstart_ref, gstart_ref, nchunk_ref, ys_hbm, xa_ref, xb_ref, meta_ref, g_ref, b_ref,
                  oa_ref, ob_ref, yloc, moe_s, sem, *, n_t, n_ta):
    t = pl.program_id(0)
    first = t < n_ta
    slot = lax.rem(t, 2)
    tab = (lstart_ref, gstart_ref, nchunk_ref)
    rows = yloc.shape[1]

    def chunk_copy(s, lrow, grow):
        return pltpu.make_async_copy(ys_hbm.at[pl.ds(grow, MOE_CHUNK)], yloc.at[s, pl.ds(lrow, MOE_CHUNK)], sem.at[s])

    def fetch(s, tile):
        _seg_loop(tab, tile, lambda lrow, grow: chunk_copy(s, lrow, grow).start())

    @pl.when(t == 0)
    def _():
        yloc[...] = jnp.zeros_like(yloc)
        fetch(0, 0)

    @pl.when(t + 1 < n_t)
    def _():
        fetch(1 - slot, t + 1)

    def wbody(j, c):
        chunk_copy(slot, 0, 0).wait()
        return c
    lax.fori_loop(0, _n_chunks(nchunk_ref, t), wbody, 0)

    meta = meta_ref[...]
    tt = meta.shape[0]
    l_iota = lax.broadcasted_iota(jnp.int32, (tt, rows), 1).astype(F32)
    sel = (jnp.where(l_iota == meta[:, 0:1], meta[:, 2:3], 0.0) + jnp.where(l_iota == meta[:, 1:2], meta[:, 3:4], 0.0))
    sel_hi, sel_lo = _split_bf16(sel)
    yb = yloc[slot]
    moe_s[...] = jnp.dot(sel_hi, yb, preferred_element_type=F32) + jnp.dot(sel_lo, yb, preferred_element_type=F32)

    @pl.when(first)
    def _():
        oa_ref[...] = _layer_norm(DN_ALPHA * xa_ref[...] + moe_s[...], g_ref[...], b_ref[...])

    @pl.when(jnp.logical_not(first))
    def _():
        ob_ref[...] = _layer_norm(DN_ALPHA * xb_ref[...] + moe_s[...], g_ref[...], b_ref[...])


def _moe_combine(ys, xa, xb, meta_n, tab, g, b, tt):
    n_ta, n_tb = xa.shape[0] // tt, xb.shape[0] // tt
    n_t = n_ta + n_tb
    rows = _local_rows(tt)
    vec = lambda: pl.BlockSpec((1, D_MODEL), lambda t, *_: (0, 0))
    in_a = pl.BlockSpec((tt, D_MODEL), lambda t, *_: (jnp.minimum(t, n_ta - 1), 0))
    in_b = pl.BlockSpec((tt, D_MODEL), lambda t, *_: (jnp.maximum(t - n_ta, 0), 0))
    grid_spec = pltpu.PrefetchScalarGridSpec(
        num_scalar_prefetch=3,
        grid=(n_t,),
        in_specs=[pl.BlockSpec(memory_space=pl.ANY), in_a, in_b,
                  pl.BlockSpec((tt, 8), lambda t, *_: (t, 0)), vec(), vec()],
        out_specs=(in_a, in_b),
        scratch_shapes=[pltpu.VMEM((2, rows, D_MODEL), BF16), pltpu.VMEM((tt, D_MODEL), F32),
                        pltpu.SemaphoreType.DMA((2,))])
    return pl.pallas_call(
        functools.partial(_combine_body, n_t=n_t, n_ta=n_ta),
        out_shape=(jax.ShapeDtypeStruct(xa.shape, F32), jax.ShapeDtypeStruct(xb.shape, F32)),
        grid_spec=grid_spec,
        compiler_params=_cparams(("arbitrary",)),
        name="moe_combine_ln",
    )(tab["lstart"], tab["gstart"], tab["nchunk"], ys, xa, xb, meta_n, g, b)


def _moe(xa, xb, meta_t, p):
    n = xa.shape[0] + xb.shape[0]
    tt = 256
    while xa.shape[0] % tt or xb.shape[0] % tt:
        tt //= 2
    r_tot, lpos, tab = _dispatch(meta_t, n, tt)
    xs = _moe_sort(xa, xb, lpos, tab, tt, r_tot)
    ys = _moe_experts(xs, tab, p["w_gate"], p["w_up"], p["w_down"])
    meta_n = jnp.concatenate([lpos.astype(F32), meta_t[2:4], jnp.zeros((4, n), F32)], axis=0).T
    return _moe_combine(ys, xa, xb, meta_n, tab, p["ln2_g"], p["ln2_b"], tt)


def _largest_tile(n, cap, mult):
    best = None
    for d in range(mult, min(n, cap) + 1, mult):
        if n % d == 0:
            best = d
    assert best is not None, (n, cap, mult)
    return best


def _seq_tails(x, off, b, t, k, c0, c1):
    if b <= 8:
        return jnp.stack([lax.slice(x, (off + i * t + t - k, c0), (off + (i + 1) * t, c1)) for i in range(b)])
    return lax.slice(x, (off, c0), (off + b * t, c1)).reshape(b, t, c1 - c0)[:, t - k:, :]


def _kv_rows(qkv, g, off, b, t, k):
    def pick(which):
        h0 = (which * N_GROUPS + g) * HEADS
        if b <= 8:
            a = jnp.stack([lax.slice(qkv, (h0, off + i * t + t - k, 0), (h0 + HEADS, off + (i + 1) * t, HEAD_DIM))
                           for i in range(b)], axis=1)
        else:
            a = lax.slice(qkv, (h0, off, 0), (h0 + HEADS, off + b * t, HEAD_DIM)).reshape(HEADS, b, t, HEAD_DIM)[:, :, t - k:]
        return jnp.transpose(a, (1, 2, 0, 3))
    return jnp.stack([pick(1), pick(2)], axis=2)


def _block_diag_gates(w_a, w_x):
    per = RNN_CHUNK // RNN_BLOCK
    eye = jnp.eye(per, dtype=F32)

    def chunks(w):
        w4 = w.reshape(N_RNN_CHUNKS, per, RNN_BLOCK, RNN_BLOCK)
        return (w4[:, :, :, None, :] * eye[None, :, None, :, None]).reshape(N_RNN_CHUNKS, RNN_CHUNK, RNN_CHUNK)
    return jnp.concatenate([chunks(w_a), chunks(w_x)], axis=2).astype(BF16)


def kernel(x_prompt, x_sample, cache_kv_w128, cache_kv_w512, cache_kv_w2048, cache_mem_kv, state_h, state_conv, mem_prompt, w_in, b_gates, conv_w, conv_b, w_a, b_a, w_x, b_x, lru_lambda, w_br_lru, w_br_att, w_br_mem, w_o, w_mem_kv, ln1_g, ln1_b, w_rg, b_rg, w_re, b_re, w_gate, w_up, w_down, ln2_g, ln2_b):
    row = lambda v: v.reshape(1, -1).astype(F32)
    w_router = jnp.zeros((ROUTER_ROWS, D_MODEL), F32)
    w_router = w_router.at[:N_EXPERT_GROUPS].set(w_rg.T).at[N_EXPERT_GROUPS:N_EXPERT_GROUPS + N_EXPERTS].set(w_re.T)
    b_router = jnp.zeros((ROUTER_ROWS, 1), F32)
    b_router = b_router.at[:N_EXPERT_GROUPS, 0].set(b_rg).at[N_EXPERT_GROUPS:N_EXPERT_GROUPS + N_EXPERTS, 0].set(b_re)
    p = dict(
        b_gates=row(b_gates), conv_w=conv_w, conv_b=row(conv_b), wax=_block_diag_gates(w_a, w_x),
        b_a=row(b_a), b_x=row(b_x), lam=row(lru_lambda),
        w_br_lru=w_br_lru.astype(BF16), w_br_att=w_br_att.astype(BF16), w_br_mem=w_br_mem.astype(BF16),
        w_o=w_o.astype(BF16), ln1_g=row(ln1_g), ln1_b=row(ln1_b), w_router=w_router, b_router=b_router,
        w_gate=w_gate, w_up=w_up, w_down=w_down,
        ln2_g=row(ln2_g), ln2_b=row(ln2_b))

    bp, s, _ = x_prompt.shape
    bs, ts, _ = x_sample.shape
    n_p, n_s = bp * s, bs * ts
    xp2, xs2 = x_prompt.reshape(n_p, D_MODEL), x_sample.reshape(n_s, D_MODEL)

    tr = 512
    while n_p % tr or n_s % tr:
        tr //= 2
    xb = _cast_rows(xp2, xs2, BF16, tr)
    tm_a = _largest_tile(n_p + n_s, 2304, 16)
    xrg = _matmul(xb, w_in, 0, 2 * D_RNN, F32, tm_a, 768, "in_proj_rnn")
    qkv = _matmul(xb, w_in, COL_Q, COL_GATES - COL_Q, F32, tm_a, 512, "in_proj_qkv", split=4)
    gates = _matmul(xb, w_in, COL_GATES, N_GATE_COLS, BF16, tm_a, 512, "gate_proj", bias=p["b_gates"])
    lru = (p["conv_w"], p["conv_b"], p["wax"], p["b_a"], p["b_x"], p["lam"])
    branch_w = (p["w_br_lru"], p["w_br_att"], p["w_br_mem"])
    ln1 = (p["w_o"], p["ln1_g"], p["ln1_b"], p["w_router"], p["b_router"])

    mem_rows = mem_prompt.reshape(bp * N_MEM, D_MODEL)
    mem_kv_p = _matmul(mem_rows, w_mem_kv, 0, 2 * D_MEM, F32, _tile(bp * N_MEM, 512), 512, "mem_kv_proj")
    hs_p, h_p = _rglru(xrg, 0, bp, s, jnp.zeros((bp, CONV_W - 1, D_RNN), F32), jnp.zeros((bp, D_RNN), F32), *lru,
                       _tile(s, 512))
    att_p = _attn_prompt(qkv, bp, s)
    mem_p = _mem_attn(qkv, mem_kv_p.reshape(bp, N_MEM, 2 * D_MEM), bp, s, _tile(s, 1024), BF16)
    merged_p = _branch_merge(hs_p, xrg, att_p, mem_p, gates, 0, *branch_w, _tile(n_p, 256))
    x1_p, meta_p = _proj_ln(merged_p, xp2, *ln1, _tile(n_p, 512))

    hs_s, h_s = _rglru(xrg, n_p, bs, ts, state_conv, state_h, *lru, ts)
    att_s = _attn_sample(qkv, (cache_kv_w128, cache_kv_w512, cache_kv_w2048), bs, ts, n_p)
    mem_s = _mem_attn_cache(qkv, cache_mem_kv, bs, ts, n_p)
    tm_s = _tile(n_s, 256)
    assert n_p % tm_s == 0
    merged_s = _branch_merge(hs_s, xrg, att_s, mem_s, gates, n_p, *branch_w, tm_s)
    x1_s, meta_s = _proj_ln(merged_s, xs2, *ln1, _tile(n_s, 512))

    y_p, y_s = _moe(x1_p, x1_s, jnp.concatenate([meta_p, meta_s], axis=1), p)

    kc = CONV_W - 1
    conv_p = jnp.concatenate([jnp.zeros((bp, kc, D_RNN), F32), _seq_tails(xrg, 0, bp, s, min(kc, s), 0, D_RNN)], axis=1)[:, -kc:]
    conv_s = jnp.concatenate([state_conv, _seq_tails(xrg, n_p, bs, ts, min(kc, ts), 0, D_RNN)], axis=1)[:, -kc:]
    kv_p = [_kv_rows(qkv, g, 0, bp, s, min(KEYS * d, s)) for g, d in enumerate(DILATIONS)]
    kv_s = [_kv_rows(qkv, g, n_p, bs, ts, ts) for g in range(N_GROUPS)]
    return (y_p.reshape(bp, s, D_MODEL), y_s.reshape(bs, ts, D_MODEL), kv_p[0], kv_p[1], kv_p[2],
            mem_kv_p.reshape(bp, N_MEM, 2, MEM_HEADS, MEM_HEAD_DIM), h_p.reshape(bp, D_RNN), conv_p,
            kv_s[0], kv_s[1], kv_s[2], h_s.reshape(bs, D_RNN), conv_s)
```

```python
import functools

import jax
import jax.numpy as jnp
from jax import lax
from jax.experimental import pallas as pl
from jax.experimental.pallas import tpu as pltpu

F32 = jnp.float32
BF16 = jnp.bfloat16

D_MODEL = 2048
D_RNN = 1536
N_RNN_BLOCKS = 16
RNN_BLOCK = D_RNN // N_RNN_BLOCKS
RNN_CHUNK = 384
N_RNN_CHUNKS = D_RNN // RNN_CHUNK
CONV_W = 4
LRU_C = 8.0
HEAD_DIM = 128
HEADS = 4
DILATIONS = (1, 4, 16)
KEYS = 128
N_GROUPS = 3
D_ATT_OUT = HEADS * HEAD_DIM
N_QKV_HEADS = 3 * N_GROUPS * HEADS
ATT_SCALE = HEAD_DIM ** -0.5
ATT_TILE = KEYS * max(DILATIONS)
N_MEM = 256
MEM_HEADS = 4
MEM_HEAD_DIM = 256
D_MEM = MEM_HEADS * MEM_HEAD_DIM
MEM_SCALE = MEM_HEAD_DIM ** -0.5
N_EXPERT_GROUPS = 4
EXPERTS_PER_GROUP = 4
N_EXPERTS = 16
D_EXPERT = 512
MOE_TILE = 512
DN_ALPHA = 2.0 ** 0.25
LN_EPS = 1e-5
NEG_INF = -1e30

COL_Q = 2 * D_RNN
COL_QM = COL_Q + N_QKV_HEADS * HEAD_DIM
COL_GATES = COL_QM + D_MEM
N_GATE_COLS = 3 * D_MODEL
ROUTER_ROWS = 32

VMEM_LIMIT = 56 * 1024 * 1024


def _cparams(sem):
    return pltpu.CompilerParams(dimension_semantics=sem, vmem_limit_bytes=VMEM_LIMIT)


def _gelu(x):
    return 0.5 * x * (1.0 + jnp.tanh(0.7978845608028654 * (x + 0.044715 * (x * x * x))))


def _layer_norm(x, g, b):
    mu = jnp.mean(x, axis=-1, keepdims=True)
    xc = x - mu
    var = jnp.mean(xc * xc, axis=-1, keepdims=True)
    return xc * lax.rsqrt(var + LN_EPS) * g + b


def _dot_nt(a, b):
    return lax.dot_general(a, b, (((1,), (1,)), ((), ())), preferred_element_type=F32)


def _tile(m, cap):
    t = min(m, cap)
    assert m % t == 0, (m, t)
    return t


def _mm_body(x_ref, w_ref, *rest, split, gate):
    acc = jnp.dot(x_ref[...].astype(BF16), w_ref[...].astype(BF16), preferred_element_type=F32)
    if gate:
        b_ref, o_ref = rest
        acc = 0.5 * jnp.tanh(0.5 * (acc + b_ref[...])) + 0.5
    else:
        (o_ref,) = rest
    if split == 1:
        o_ref[...] = acc.astype(o_ref.dtype)
    else:
        w = acc.shape[1] // split
        for s in range(split):
            o_ref[s] = acc[:, s * w:(s + 1) * w].astype(o_ref.dtype)


def _matmul(x, w, col_off, n_cols, out_dtype, tm, tn, name, bias=None, split=1):
    m, k = x.shape
    cb = col_off // tn
    in_specs = [pl.BlockSpec((tm, k), lambda i, j: (i, 0)),
                pl.BlockSpec((k, tn), lambda i, j: (0, j + cb))]
    args = [x, w]
    if bias is not None:
        in_specs.append(pl.BlockSpec((1, tn), lambda i, j: (0, j)))
        args.append(bias)
    if split == 1:
        out_shape = jax.ShapeDtypeStruct((m, n_cols), out_dtype)
        out_spec = pl.BlockSpec((tm, tn), lambda i, j: (i, j))
    else:
        out_shape = jax.ShapeDtypeStruct((n_cols * split // tn, m, tn // split), out_dtype)
        out_spec = pl.BlockSpec((split, tm, tn // split), lambda i, j: (j, i, 0))
    return pl.pallas_call(
        functools.partial(_mm_body, split=split, gate=bias is not None),
        out_shape=out_shape,
        grid=(m // tm, n_cols // tn),
        in_specs=in_specs,
        out_specs=out_spec,
        compiler_params=_cparams(("parallel", "arbitrary")),
        name=name,
    )(*args)


def _cast_rows_body(a_ref, b_ref, o_ref, *, n_ta):
    o_ref[...] = jnp.where(pl.program_id(0) < n_ta, a_ref[...], b_ref[...]).astype(o_ref.dtype)


def _cast_rows(xa, xb, dtype, tr):
    n_ta, n_tb = xa.shape[0] // tr, xb.shape[0] // tr
    cols = xa.shape[1]
    return pl.pallas_call(
        functools.partial(_cast_rows_body, n_ta=n_ta),
        out_shape=jax.ShapeDtypeStruct((xa.shape[0] + xb.shape[0], cols), dtype),
        grid=(n_ta + n_tb,),
        in_specs=[pl.BlockSpec((tr, cols), lambda i: (jnp.minimum(i, n_ta - 1), 0)),
                  pl.BlockSpec((tr, cols), lambda i: (jnp.maximum(i - n_ta, 0), 0))],
        out_specs=pl.BlockSpec((tr, cols), lambda i: (i, 0)),
        compiler_params=_cparams(("parallel",)),
        name="cast_rows",
    )(xa, xb)


def _rglru_body(xr_ref, cbuf_ref, h0_ref, cw_ref, cb_ref, wax_ref, ba_ref, bx_ref, lam_ref,
                out_ref, hl_ref, ext_s, a_s, u_s, h_s, *, tt):
    t = pl.program_id(1)

    @pl.when(t == 0)
    def _():
        ext_s[0:8, :] = jnp.zeros((8, D_RNN), F32)
        ext_s[5:8, :] = cbuf_ref[0]
        h_s[...] = jnp.broadcast_to(h0_ref[0], (8, D_RNN))

    @pl.when(t > 0)
    def _():
        ext_s[0:8, :] = ext_s[tt:tt + 8, :]

    ext_s[8:8 + tt, :] = xr_ref[...]
    cw = cw_ref[...]
    xc = (cb_ref[...] + cw[3:4, :] * ext_s[8:8 + tt, :] + cw[2:3, :] * ext_s[7:7 + tt, :]
          + cw[1:2, :] * ext_s[6:6 + tt, :] + cw[0:1, :] * ext_s[5:5 + tt, :])
    xcb = xc.astype(BF16)
    r_parts, i_parts = [], []
    for c in range(N_RNN_CHUNKS):
        g = jnp.dot(xcb[:, c * RNN_CHUNK:(c + 1) * RNN_CHUNK], wax_ref[c], preferred_element_type=F32)
        r_parts.append(g[:, :RNN_CHUNK])
        i_parts.append(g[:, RNN_CHUNK:])
    sigmoid = lambda z: 0.5 * jnp.tanh(0.5 * z) + 0.5
    r = sigmoid(jnp.concatenate(r_parts, axis=1) + ba_ref[...])
    gi = sigmoid(jnp.concatenate(i_parts, axis=1) + bx_ref[...])
    nl = -lam_ref[...]
    softplus = jnp.maximum(nl, 0.0) + jnp.log1p(jnp.exp(-jnp.abs(nl)))
    log_a = (-LRU_C) * r * softplus
    th = jnp.tanh(log_a)
    a_s[...] = jnp.exp(log_a)
    u_s[...] = jnp.sqrt(-2.0 * th / (1.0 - th)) * (gi * xc)

    rows = lax.broadcasted_iota(jnp.int32, (8, D_RNN), 0)

    def blk(i, h):
        r0 = pl.multiple_of(i * 8, 8)
        ab = a_s[pl.ds(r0, 8), :]
        ub = u_s[pl.ds(r0, 8), :]
        for s in (1, 2, 4):
            keep = rows >= s
            ub = ab * jnp.where(keep, pltpu.roll(ub, s, 0), 0.0) + ub
            ab = ab * jnp.where(keep, pltpu.roll(ab, s, 0), 1.0)
        hb = ab * h + ub
        u_s[pl.ds(r0, 8), :] = hb
        return jnp.broadcast_to(hb[7:8, :], (8, D_RNN))

    h_fin = lax.fori_loop(0, tt // 8, blk, h_s[...])
    h_s[...] = h_fin
    hl_ref[0] = h_fin[0:1, :]
    out_ref[...] = u_s[...]


def _rglru(xrg, off, b, t, conv_buf, h0, conv_w, conv_b, wax, b_a, b_x, lam, tt):
    nt = t // tt
    ob = off // tt
    vec = lambda: pl.BlockSpec((1, D_RNN), lambda i, j: (0, 0))
    return pl.pallas_call(
        functools.partial(_rglru_body, tt=tt),
        out_shape=(jax.ShapeDtypeStruct((b * t, D_RNN), F32), jax.ShapeDtypeStruct((b, 1, D_RNN), F32)),
        grid=(b, nt),
        in_specs=[pl.BlockSpec((tt, D_RNN), lambda i, j: (ob + i * nt + j, 0)),
                  pl.BlockSpec((1, CONV_W - 1, D_RNN), lambda i, j: (i, 0, 0)),
                  pl.BlockSpec((1, 1, D_RNN), lambda i, j: (i, 0, 0)),
                  pl.BlockSpec((CONV_W, D_RNN), lambda i, j: (0, 0)),
                  vec(),
                  pl.BlockSpec((N_RNN_CHUNKS, RNN_CHUNK, 2 * RNN_CHUNK), lambda i, j: (0, 0, 0)),
                  vec(), vec(), vec()],
        out_specs=(pl.BlockSpec((tt, D_RNN), lambda i, j: (i * nt + j, 0)),
                   pl.BlockSpec((1, 1, D_RNN), lambda i, j: (i, 0, 0))),
        scratch_shapes=[pltpu.VMEM((tt + 8, D_RNN), F32), pltpu.VMEM((tt, D_RNN), F32),
                        pltpu.VMEM((tt, D_RNN), F32), pltpu.VMEM((8, D_RNN), F32)],
        compiler_params=_cparams(("parallel", "arbitrary")),
        name="rglru",
    )(xrg, conv_buf, h0.reshape(b, 1, D_RNN), conv_w, conv_b, wax, b_a, b_x, lam)


def _band_block(q, k, v, bias):
    s = _dot_nt(q, k) * ATT_SCALE + bias
    m = jnp.max(s, axis=-1, keepdims=True)
    p = jnp.exp(s - m)
    l = jnp.sum(p, axis=-1, keepdims=True)
    o = jnp.dot(p.astype(BF16), v, preferred_element_type=F32) / l
    return o, m + jnp.log(l)


def _attn_prompt_body(*refs):
    q_refs, k_refs, v_refs, kh_refs, vh_refs = (refs[3 * i:3 * i + 3] for i in range(5))
    o_ref, og_s, lg_s = refs[15:]
    row = lax.broadcasted_iota(jnp.int32, (KEYS, 2 * KEYS), 0)
    col = lax.broadcasted_iota(jnp.int32, (KEYS, 2 * KEYS), 1)
    band = jnp.logical_and(col >= row, col <= row + KEYS)
    bias = jnp.where(band, 0.0, NEG_INF)
    has_prev = pl.program_id(1) > 0
    bias_first = jnp.where(jnp.logical_and(band, jnp.logical_or(col >= KEYS, has_prev)), 0.0, NEG_INF)

    def put(g, rows, o, lse):
        og_s[g, rows, :] = o
        lg_s[g, rows, :] = jnp.broadcast_to(lse, (KEYS, HEAD_DIM))

    def rows_of(start, size, d):
        return pl.ds(start, size, stride=d) if d > 1 else pl.ds(start, size)

    for g, d in enumerate(DILATIONS):
        q_ref, k_ref, v_ref, kh_ref, vh_ref = q_refs[g], k_refs[g], v_refs[g], kh_refs[g], vh_refs[g]
        nqb = ATT_TILE // (d * KEYS)
        for r in range(d):
            own = rows_of(r, KEYS, d)
            kk = jnp.concatenate([kh_ref[0, own, :], k_ref[0, own, :]], axis=0).astype(BF16)
            vv = jnp.concatenate([vh_ref[0, own, :], v_ref[0, own, :]], axis=0).astype(BF16)
            o, lse = _band_block(q_ref[0, own, :].astype(BF16), kk, vv, bias_first)
            put(g, own, o, lse)
            for qb in range(1, nqb):
                rows = rows_of(qb * KEYS * d + r, KEYS, d)
                keys = rows_of((qb - 1) * KEYS * d + r, 2 * KEYS, d)
                o, lse = _band_block(q_ref[0, rows, :].astype(BF16), k_ref[0, keys, :].astype(BF16),
                                     v_ref[0, keys, :].astype(BF16), bias)
                put(g, rows, o, lse)

    l0, l1, l2 = lg_s[0], lg_s[1], lg_s[2]
    m = jnp.maximum(jnp.maximum(l0, l1), l2)
    e0, e1, e2 = jnp.exp(l0 - m), jnp.exp(l1 - m), jnp.exp(l2 - m)
    o_ref[...] = ((e0 * og_s[0] + e1 * og_s[1] + e2 * og_s[2]) / (e0 + e1 + e2)).astype(o_ref.dtype)


def _attn_prompt(qkv, b, s):
    assert s % ATT_TILE == 0
    nt = s // ATT_TILE
    n = b * s

    def cur(which, g):
        return pl.BlockSpec((1, ATT_TILE, HEAD_DIM),
                            lambda i, j, h: ((which * N_GROUPS + g) * HEADS + h, i * nt + j, 0))

    def halo(which, g):
        rows = KEYS * DILATIONS[g]
        per = ATT_TILE // rows
        return pl.BlockSpec((1, rows, HEAD_DIM),
                            lambda i, j, h: ((which * N_GROUPS + g) * HEADS + h,
                                             jnp.maximum((i * nt + j) * per - 1, 0), 0))

    in_specs = ([cur(0, g) for g in range(N_GROUPS)] + [cur(1, g) for g in range(N_GROUPS)]
                + [cur(2, g) for g in range(N_GROUPS)] + [halo(1, g) for g in range(N_GROUPS)]
                + [halo(2, g) for g in range(N_GROUPS)])
    return pl.pallas_call(
        _attn_prompt_body,
        out_shape=jax.ShapeDtypeStruct((n, D_ATT_OUT), BF16),
        grid=(b, nt, HEADS),
        in_specs=in_specs,
        out_specs=pl.BlockSpec((ATT_TILE, HEAD_DIM), lambda i, j, h: (i * nt + j, h)),
        scratch_shapes=[pltpu.VMEM((N_GROUPS, ATT_TILE, HEAD_DIM), F32),
                        pltpu.VMEM((N_GROUPS, ATT_TILE, HEAD_DIM), F32)],
        compiler_params=_cparams(("parallel", "parallel", "parallel")),
        name="attn_prompt",
    )(*([qkv] * 15))


def _attn_sample_group(q4, kn4, vn4, c_ref, d, t_new):
    half = len(c_ref.shape) == 4
    wb = c_ref.shape[1] * 16 if half else c_ref.shape[1] // (2 * HEADS)
    nk = wb // 2 if half else wb
    nr = HEADS * t_new
    zeros = jnp.zeros((t_new, HEAD_DIM), F32)
    qbd = jnp.concatenate(
        [jnp.concatenate([q4[h] if hh == h else zeros for hh in range(HEADS)], axis=1) for h in range(HEADS)],
        axis=0).astype(BF16)
    kn = jnp.concatenate([kn4[h] for h in range(HEADS)], axis=1).astype(BF16)
    vn = jnp.concatenate([vn4[h] for h in range(HEADS)], axis=1).astype(BF16)
    if half:
        cache_rows = lambda kv, h: c_ref[0, :, pl.ds(kv * HEADS + h, 8, stride=2 * HEADS), :].reshape(nk, HEAD_DIM)
    else:
        cache_rows = lambda kv, h: c_ref[0, pl.ds(kv * HEADS + h, wb, stride=2 * HEADS), :]
    kc = jnp.concatenate([cache_rows(0, h) for h in range(HEADS)], axis=1).astype(BF16)
    vc = jnp.concatenate([cache_rows(1, h) for h in range(HEADS)], axis=1).astype(BF16)
    tq_c = lax.broadcasted_iota(jnp.int32, (nr, nk), 0) & (t_new - 1)
    e_c = lax.broadcasted_iota(jnp.int32, (nr, nk), 1)
    if half:
        e_c = ((e_c >> 3) << 4) + (e_c & 7)
    dist_c = wb + tq_c - e_c
    ok_c = jnp.logical_and((dist_c & (d - 1)) == 0, dist_c <= KEYS * d)
    tq_n = lax.broadcasted_iota(jnp.int32, (nr, t_new), 0) & (t_new - 1)
    dist_n = tq_n - lax.broadcasted_iota(jnp.int32, (nr, t_new), 1)
    ok_n = jnp.logical_and(jnp.logical_and(dist_n >= 0, (dist_n & (d - 1)) == 0), dist_n <= KEYS * d)
    s_c = jnp.where(ok_c, _dot_nt(qbd, kc) * ATT_SCALE, NEG_INF)
    s_n = jnp.where(ok_n, _dot_nt(qbd, kn) * ATT_SCALE, NEG_INF)
    m = jnp.maximum(jnp.max(s_c, axis=-1, keepdims=True), jnp.max(s_n, axis=-1, keepdims=True))
    p_c = jnp.exp(s_c - m)
    p_n = jnp.exp(s_n - m)
    l = jnp.sum(p_c, axis=-1, keepdims=True) + jnp.sum(p_n, axis=-1, keepdims=True)
    o = (jnp.dot(p_c.astype(BF16), vc, preferred_element_type=F32)
         + jnp.dot(p_n.astype(BF16), vn, preferred_element_type=F32)) / l
    lse = m + jnp.log(l)
    o = jnp.concatenate(
        [o[h * t_new:(h + 1) * t_new, h * HEAD_DIM:(h + 1) * HEAD_DIM] for h in range(HEADS)], axis=1)
    lse = jnp.concatenate(
        [jnp.broadcast_to(lse[h * t_new:(h + 1) * t_new], (t_new, HEAD_DIM)) for h in range(HEADS)], axis=1)
    return o, lse


def _attn_sample_body(*refs, t_new):
    q_refs, k_refs, v_refs, c_refs = (refs[3 * i:3 * i + 3] for i in range(4))
    o_ref = refs[12]
    outs = [_attn_sample_group(q_refs[g][...], k_refs[g][...], v_refs[g][...], c_refs[g], d, t_new)
            for g, d in enumerate(DILATIONS)]
    (o0, l0), (o1, l1), (o2, l2) = outs
    m = jnp.maximum(jnp.maximum(l0, l1), l2)
    e0, e1, e2 = jnp.exp(l0 - m), jnp.exp(l1 - m), jnp.exp(l2 - m)
    o_ref[0] = (e0 * o0 + e1 * o1 + e2 * o2) / (e0 + e1 + e2)


def _attn_sample(qkv, caches, b, t_new, off):
    assert t_new & (t_new - 1) == 0 and off % t_new == 0
    ob = off // t_new
    new = lambda which, g: pl.BlockSpec((HEADS, t_new, HEAD_DIM), lambda i: (which * N_GROUPS + g, ob + i, 0))
    caches2, cache_specs = [], []
    for c, d in zip(caches, DILATIONS):
        wb = c.shape[1]
        if d == 16 and wb % 16 == 0 and t_new <= 8:
            caches2.append(c.reshape(b, wb // 16, 16 * 2 * HEADS, HEAD_DIM))
            cache_specs.append(pl.BlockSpec((1, wb // 16, 8 * 2 * HEADS, HEAD_DIM), lambda i: (i, 0, 0, 0)))
        else:
            caches2.append(c.reshape(b, wb * 2 * HEADS, HEAD_DIM))
            cache_specs.append(pl.BlockSpec((1, wb * 2 * HEADS, HEAD_DIM), lambda i: (i, 0, 0)))
    in_specs = ([new(0, g) for g in range(N_GROUPS)] + [new(1, g) for g in range(N_GROUPS)]
                + [new(2, g) for g in range(N_GROUPS)] + cache_specs)
    att = pl.pallas_call(
        functools.partial(_attn_sample_body, t_new=t_new),
        out_shape=jax.ShapeDtypeStruct((b, t_new, D_ATT_OUT), F32),
        grid=(b,),
        in_specs=in_specs,
        out_specs=pl.BlockSpec((1, t_new, D_ATT_OUT), lambda i: (i, 0, 0)),
        compiler_params=_cparams(("parallel",)),
        name="attn_sample",
    )(*([qkv] * 9), *caches2)
    return att.reshape(b * t_new, D_ATT_OUT)


def _mem_q_specs(rows, index):
    first = N_QKV_HEADS // 2
    return [pl.BlockSpec((2, rows, HEAD_DIM), functools.partial(index, first + h)) for h in range(MEM_HEADS)]


def _mem_attn_body(q0_ref, q1_ref, q2_ref, q3_ref, kv_ref, o_ref):
    for h, q_ref in enumerate((q0_ref, q1_ref, q2_ref, q3_ref)):
        k = kv_ref[0, :, h * MEM_HEAD_DIM:(h + 1) * MEM_HEAD_DIM].astype(BF16)
        v = kv_ref[0, :, D_MEM + h * MEM_HEAD_DIM:D_MEM + (h + 1) * MEM_HEAD_DIM].astype(BF16)
        q = jnp.concatenate([q_ref[0], q_ref[1]], axis=1).astype(BF16)
        s = _dot_nt(q, k) * MEM_SCALE
        m = jnp.max(s, axis=-1, keepdims=True)
        p = jnp.exp(s - m)
        l = jnp.sum(p, axis=-1, keepdims=True)
        o = jnp.dot(p.astype(BF16), v, preferred_element_type=F32) / l
        o_ref[:, h * MEM_HEAD_DIM:(h + 1) * MEM_HEAD_DIM] = o.astype(o_ref.dtype)


def _mem_attn(qkvm, mem_kv, b, t, tm, out_dtype):
    nt = t // tm
    return pl.pallas_call(
        _mem_attn_body,
        out_shape=jax.ShapeDtypeStruct((b * t, D_MEM), out_dtype),
        grid=(b, nt),
        in_specs=_mem_q_specs(tm, lambda blk, i, j: (blk, i * nt + j, 0))
        + [pl.BlockSpec((1, N_MEM, 2 * D_MEM), lambda i, j: (i, 0, 0))],
        out_specs=pl.BlockSpec((tm, D_MEM), lambda i, j: (i * nt + j, 0)),
        compiler_params=_cparams(("parallel", "parallel")),
        name="mem_attn",
    )(qkvm, qkvm, qkvm, qkvm, mem_kv)


def _mem_attn_cache_body(q0_ref, q1_ref, q2_ref, q3_ref, c_ref, o_ref):
    t = q0_ref.shape[1]
    n_rows = N_MEM * 2 * MEM_HEADS
    flat = c_ref[0].reshape(n_rows, MEM_HEAD_DIM).astype(BF16)
    q = jnp.concatenate([jnp.concatenate([q_ref[0], q_ref[1]], axis=1) for q_ref in (q0_ref, q1_ref, q2_ref, q3_ref)],
                        axis=0).astype(BF16)
    assert t & (t - 1) == 0
    head = lax.broadcasted_iota(jnp.int32, (MEM_HEADS * t, n_rows), 0) >> (t.bit_length() - 1)
    col = lax.broadcasted_iota(jnp.int32, (MEM_HEADS * t, n_rows), 1)
    s = jnp.where((col & (2 * MEM_HEADS - 1)) == head, _dot_nt(q, flat) * MEM_SCALE, NEG_INF)
    m = jnp.max(s, axis=-1, keepdims=True)
    p = jnp.exp(s - m)
    l = jnp.sum(p, axis=-1, keepdims=True)
    o = jnp.dot(pltpu.roll(p, MEM_HEADS, 1).astype(BF16), flat, preferred_element_type=F32) / l
    o_ref[...] = jnp.concatenate([o[h * t:(h + 1) * t, :] for h in range(MEM_HEADS)], axis=1)


def _mem_attn_cache(qkvm, cache, b, t, off):
    ob = off // t
    return pl.pallas_call(
        _mem_attn_cache_body,
        out_shape=jax.ShapeDtypeStruct((b * t, D_MEM), F32),
        grid=(b,),
        in_specs=_mem_q_specs(t, lambda blk, i: (blk, ob + i, 0))
        + [pl.BlockSpec((1, N_MEM, 2, MEM_HEADS, MEM_HEAD_DIM), lambda i: (i, 0, 0, 0, 0))],
        out_specs=pl.BlockSpec((t, D_MEM), lambda i: (i, 0)),
        compiler_params=_cparams(("parallel",)),
        name="mem_attn_cache",
    )(qkvm, qkvm, qkvm, qkvm, cache)


def _branch_body(hs_ref, xg_ref, att_ref, mem_ref, gt_ref, wl_ref, wa_ref, wm_ref, out_ref):
    a_lru = (hs_ref[...] * _gelu(xg_ref[...])).astype(BF16)
    acc = gt_ref[:, 0:D_MODEL].astype(F32) * jnp.dot(a_lru, wl_ref[...], preferred_element_type=F32)
    acc = acc + gt_ref[:, D_MODEL:2 * D_MODEL].astype(F32) * jnp.dot(att_ref[...].astype(BF16), wa_ref[...], preferred_element_type=F32)
    acc = acc + gt_ref[:, 2 * D_MODEL:3 * D_MODEL].astype(F32) * jnp.dot(mem_ref[...].astype(BF16), wm_ref[...], preferred_element_type=F32)
    out_ref[...] = acc.astype(out_ref.dtype)


def _branch_merge(hs, xrg, att, mem, gates, goff, wl, wa, wm, tm):
    n = hs.shape[0]
    gb = goff // tm
    row = lambda w: pl.BlockSpec((tm, w), lambda i: (i, 0))
    full = lambda a: pl.BlockSpec(a.shape, lambda i: (0, 0), pipeline_mode=pl.Buffered(1))
    return pl.pallas_call(
        _branch_body,
        out_shape=jax.ShapeDtypeStruct((n, D_MODEL), BF16),
        grid=(n // tm,),
        in_specs=[row(D_RNN), pl.BlockSpec((tm, D_RNN), lambda i: (gb + i, 1)), row(D_ATT_OUT), row(D_MEM),
                  pl.BlockSpec((tm, N_GATE_COLS), lambda i: (gb + i, 0)), full(wl), full(wa), full(wm)],
        out_specs=row(D_MODEL),
        compiler_params=_cparams(("parallel",)),
        name="branch_merge",
    )(hs, xrg, att, mem, gates, wl, wa, wm)


def _split_bf16(x):
    hi = x.astype(BF16)
    return hi, (x - hi.astype(F32)).astype(BF16)


def _route_rows(lg):
    g = [lg[i:i + 1, :] for i in range(N_EXPERT_GROUPS)]
    gmax = jnp.maximum(jnp.maximum(g[0], g[1]), jnp.maximum(g[2], g[3]))
    gidx = jnp.where(g[0] == gmax, 0.0, jnp.where(g[1] == gmax, 1.0, jnp.where(g[2] == gmax, 2.0, 3.0)))
    g_p = 1.0 / (jnp.exp(g[0] - gmax) + jnp.exp(g[1] - gmax) + jnp.exp(g[2] - gmax) + jnp.exp(g[3] - gmax))
    e = []
    for k in range(EXPERTS_PER_GROUP):
        rows = [lg[N_EXPERT_GROUPS + gg * EXPERTS_PER_GROUP + k:N_EXPERT_GROUPS + gg * EXPERTS_PER_GROUP + k + 1, :]
                for gg in range(N_EXPERT_GROUPS)]
        e.append(jnp.where(gidx == 0.0, rows[0], jnp.where(gidx == 1.0, rows[1], jnp.where(gidx == 2.0, rows[2], rows[3]))))

    def first_argmax(v):
        mx = jnp.maximum(jnp.maximum(v[0], v[1]), jnp.maximum(v[2], v[3]))
        ix = jnp.where(v[0] == mx, 0.0, jnp.where(v[1] == mx, 1.0, jnp.where(v[2] == mx, 2.0, 3.0)))
        return mx, ix

    v1, i1 = first_argmax(e)
    v2, i2 = first_argmax([jnp.where(i1 == float(k), -jnp.inf, e[k]) for k in range(EXPERTS_PER_GROUP)])
    ex = jnp.exp(v2 - v1)
    w1 = g_p / (1.0 + ex)
    w2 = g_p * ex / (1.0 + ex)
    base = gidx * float(EXPERTS_PER_GROUP)
    zero = jnp.zeros_like(w1)
    return jnp.concatenate([base + i1, base + i2, w1, w2, zero, zero, zero, zero], axis=0)


def _proj_ln_body(mg_ref, x_ref, wo_ref, g_ref, b_ref, wr_ref, br_ref, x1_ref, meta_ref, mix_s):
    @pl.when(pl.program_id(0) == 0)
    def _():
        mix_s[...] = jnp.zeros(mix_s.shape, F32)

    mix_next = jnp.dot(mg_ref[...], wo_ref[...], preferred_element_type=F32)
    wh, wl = _split_bf16(wr_ref[...])
    tm = x_ref.shape[0]
    sub = min(256, tm)
    for r0 in range(0, tm, sub):
        rs = pl.ds(r0, sub)
        x1 = _layer_norm(DN_ALPHA * x_ref[rs, :] + mix_s[rs, :], g_ref[...], b_ref[...])
        x1_ref[rs, :] = x1
        xh, xl = _split_bf16(x1)
        lg = _dot_nt(wh, xh) + (_dot_nt(wh, xl) + _dot_nt(wl, xh)) + br_ref[...]
        meta_ref[:, rs] = _route_rows(lg)
    mix_s[...] = mix_next


def _proj_ln(merged, x, wo, g, b, wr, br, tm):
    n = merged.shape[0]
    nb = n // tm
    prev = lambda w: pl.BlockSpec((tm, w), lambda i: (jnp.maximum(i - 1, 0), 0))
    full = lambda a: pl.BlockSpec(a.shape, lambda i: (0, 0))
    return pl.pallas_call(
        _proj_ln_body,
        out_shape=(jax.ShapeDtypeStruct((n, D_MODEL), F32), jax.ShapeDtypeStruct((8, n), F32)),
        grid=(nb + 1,),
        in_specs=[pl.BlockSpec((tm, D_MODEL), lambda i: (jnp.minimum(i, nb - 1), 0)), prev(D_MODEL),
                  full(wo), full(g), full(b), full(wr), full(br)],
        out_specs=(prev(D_MODEL), pl.BlockSpec((8, tm), lambda i: (0, jnp.maximum(i - 1, 0)))),
        scratch_shapes=[pltpu.VMEM((tm, D_MODEL), F32)],
        compiler_params=_cparams(("arbitrary",)),
        name="proj_ln_router",
    )(merged, x, wo, g, b, wr, br)


MOE_CHUNK = 16


def _local_rows(tt):
    return -(-(2 * tt + N_EXPERTS * (MOE_CHUNK - 1)) // 128) * 128


def _dispatch(meta_t, n, tt):
    n_t = n // tt
    ids = meta_t[0:2].astype(jnp.int32)
    onehot = (ids[:, :, None] == jnp.arange(N_EXPERTS, dtype=jnp.int32)).astype(jnp.int32).reshape(2, n_t, tt, N_EXPERTS)
    cnt_slot = jnp.sum(onehot, axis=2)
    cnt = cnt_slot[0] + cnt_slot[1]
    pc = (cnt + MOE_CHUNK - 1) // MOE_CHUNK * MOE_CHUNK
    lstart = jnp.cumsum(pc, axis=1) - pc
    tri = (jnp.arange(tt)[:, None] >= jnp.arange(tt)[None, :]).astype(F32)
    csum = jnp.einsum("ut,snte->snue", tri, onehot.astype(F32)).astype(jnp.int32)
    rank = csum - onehot + jnp.stack([jnp.zeros_like(cnt), cnt_slot[0]])[:, :, None, :]
    lpos = jnp.sum(onehot * (lstart[None, :, None, :] + rank), axis=-1).reshape(2, n)
    seg = jnp.sum(pc, axis=0)
    pe = (seg + MOE_TILE - 1) // MOE_TILE * MOE_TILE
    ends = jnp.cumsum(pe)
    base = ends - pe
    gstart = base[None, :] + jnp.cumsum(pc, axis=0) - pc
    r_tot = -(-(2 * n + N_EXPERTS * (MOE_CHUNK - 1) * n_t) // MOE_TILE) * MOE_TILE + N_EXPERTS * MOE_TILE
    n_tiles = r_tot // MOE_TILE
    tile_start = jnp.arange(n_tiles, dtype=jnp.int32) * MOE_TILE
    tile_expert = jnp.minimum(jnp.sum((tile_start[:, None] >= ends[None, :]).astype(jnp.int32), axis=1), N_EXPERTS - 1)
    tables = dict(
        lstart=lstart.reshape(-1), gstart=gstart.reshape(-1), nchunk=(pc // MOE_CHUNK).reshape(-1),
        zstart=jnp.concatenate([base + seg, ends[-1:]]),
        zcount=jnp.concatenate([(pe - seg) // MOE_CHUNK, (r_tot - ends[-1:]) // MOE_TILE]),
        tile_expert=tile_expert, n_used=(ends[-1] // MOE_TILE).reshape(1))
    return r_tot, lpos, tables


def _seg_loop(tab, t, fn):
    lstart_ref, gstart_ref, nchunk_ref = tab
    for e in range(N_EXPERTS):
        ls = lstart_ref[t * N_EXPERTS + e]
        gs = gstart_ref[t * N_EXPERTS + e]

        def body(j, c, ls=ls, gs=gs):
            fn(pl.multiple_of(ls + j * MOE_CHUNK, MOE_CHUNK), pl.multiple_of(gs + j * MOE_CHUNK, MOE_CHUNK))
            return c
        lax.fori_loop(0, nchunk_ref[t * N_EXPERTS + e], body, 0)


def _n_chunks(nchunk_ref, t):
    tot = nchunk_ref[t * N_EXPERTS]
    for e in range(1, N_EXPERTS):
        tot = tot + nchunk_ref[t * N_EXPERTS + e]
    return tot


def _sort_body(lstart_ref, gstart_ref, nchunk_ref, zstart_ref, zcount_ref, lpos_ref, xa_ref, xb_ref, xs_hbm,
               xloc, zbuf, sem, zsem, *, n_t, n_ta):
    t = pl.program_id(0)
    x_tile = jnp.where(t < n_ta, xa_ref[...], xb_ref[...]).astype(BF16)
    slot = lax.rem(t, 2)
    tab = (lstart_ref, gstart_ref, nchunk_ref)
    rows = xloc.shape[1]

    def chunk_copy(s, lrow, grow):
        return pltpu.make_async_copy(xloc.at[s, pl.ds(lrow, MOE_CHUNK)], xs_hbm.at[pl.ds(grow, MOE_CHUNK)], sem.at[s])

    def wait_tile(s, tile):
        def body(j, c):
            chunk_copy(s, 0, 0).wait()
            return c
        lax.fori_loop(0, _n_chunks(nchunk_ref, tile), body, 0)

    @pl.when(t == 0)
    def _():
        zbuf[...] = jnp.zeros_like(zbuf)
        zero_copy = lambda grow: pltpu.make_async_copy(
            zbuf.at[pl.ds(0, MOE_CHUNK)], xs_hbm.at[pl.ds(grow, MOE_CHUNK)], zsem.at[0])
        zero_tile = lambda grow: pltpu.make_async_copy(zbuf, xs_hbm.at[pl.ds(grow, MOE_TILE)], zsem.at[0])
        tail_start = zstart_ref[N_EXPERTS]
        for e in range(N_EXPERTS):
            def zb(j, c, e=e):
                zero_copy(pl.multiple_of(zstart_ref[e] + j * MOE_CHUNK, MOE_CHUNK)).start()
                return c
            lax.fori_loop(0, zcount_ref[e], zb, 0)

        def tb(j, c):
            zero_tile(pl.multiple_of(tail_start + j * MOE_TILE, MOE_TILE)).start()
            return c
        lax.fori_loop(0, zcount_ref[N_EXPERTS], tb, 0)
        for e in range(N_EXPERTS):
            def zw(j, c):
                zero_copy(0).wait()
                return c
            lax.fori_loop(0, zcount_ref[e], zw, 0)

        def tw(j, c):
            zero_tile(0).wait()
            return c
        lax.fori_loop(0, zcount_ref[N_EXPERTS], tw, 0)

    @pl.when(t >= 2)
    def _():
        wait_tile(slot, t - 2)

    l_iota = lax.broadcasted_iota(jnp.int32, (rows, xa_ref.shape[0]), 0)
    perm = jnp.logical_or(l_iota == lpos_ref[0:1, :], l_iota == lpos_ref[1:2, :])
    perm = jnp.where(perm, 1.0, 0.0).astype(BF16)
    xloc[slot] = jnp.dot(perm, x_tile, preferred_element_type=F32).astype(BF16)
    _seg_loop(tab, t, lambda lrow, grow: chunk_copy(slot, lrow, grow).start())

    @pl.when(t == n_t - 1)
    def _():
        wait_tile(slot, t)
        if n_t >= 2:
            wait_tile(1 - slot, t - 1)


def _moe_sort(xa, xb, lpos, tab, tt, r_tot):
    n_ta, n_tb = xa.shape[0] // tt, xb.shape[0] // tt
    n_t = n_ta + n_tb
    rows = _local_rows(tt)
    grid_spec = pltpu.PrefetchScalarGridSpec(
        num_scalar_prefetch=5,
        grid=(n_t,),
        in_specs=[pl.BlockSpec((2, tt), lambda t, *_: (0, t)),
                  pl.BlockSpec((tt, D_MODEL), lambda t, *_: (jnp.minimum(t, n_ta - 1), 0)),
                  pl.BlockSpec((tt, D_MODEL), lambda t, *_: (jnp.maximum(t - n_ta, 0), 0))],
        out_specs=pl.BlockSpec(memory_space=pl.ANY),
        scratch_shapes=[pltpu.VMEM((2, rows, D_MODEL), BF16), pltpu.VMEM((MOE_TILE, D_MODEL), BF16),
                        pltpu.SemaphoreType.DMA((2,)), pltpu.SemaphoreType.DMA((1,))])
    return pl.pallas_call(
        functools.partial(_sort_body, n_t=n_t, n_ta=n_ta),
        out_shape=jax.ShapeDtypeStruct((r_tot, D_MODEL), BF16),
        grid_spec=grid_spec,
        compiler_params=_cparams(("arbitrary",)),
        name="moe_sort",
    )(tab["lstart"], tab["gstart"], tab["nchunk"], tab["zstart"], tab["zcount"], lpos, xa, xb)


def _expert_body(te_ref, nu_ref, x_ref, wg_ref, wu_ref, wd_ref, o_ref, wg_s, wu_s, wd_s):
    i = pl.program_id(0)
    used = i < nu_ref[0]
    new_expert = jnp.logical_or(i == 0, te_ref[i] != te_ref[jnp.maximum(i - 1, 0)])

    @pl.when(jnp.logical_and(used, new_expert))
    def _():
        wg_s[...] = wg_ref[0].astype(BF16)
        wu_s[...] = wu_ref[0].astype(BF16)
        wd_s[...] = wd_ref[0].astype(BF16)

    @pl.when(used)
    def _():
        xb = x_ref[...]
        hid = _gelu(jnp.dot(xb, wg_s[...], preferred_element_type=F32)) * jnp.dot(xb, wu_s[...], preferred_element_type=F32)
        o_ref[...] = jnp.dot(hid.astype(BF16), wd_s[...], preferred_element_type=F32).astype(o_ref.dtype)

    @pl.when(jnp.logical_not(used))
    def _():
        o_ref[...] = jnp.zeros_like(o_ref)


def _moe_experts(xs, tab, wg, wu, wd):
    r_tot = xs.shape[0]
    last = lambda i, nu: jnp.minimum(i, nu[0] - 1)
    grid_spec = pltpu.PrefetchScalarGridSpec(
        num_scalar_prefetch=2,
        grid=(r_tot // MOE_TILE,),
        in_specs=[pl.BlockSpec((MOE_TILE, D_MODEL), lambda i, te, nu: (last(i, nu), 0)),
                  pl.BlockSpec((1, D_MODEL, D_EXPERT), lambda i, te, nu: (te[last(i, nu)], 0, 0)),
                  pl.BlockSpec((1, D_MODEL, D_EXPERT), lambda i, te, nu: (te[last(i, nu)], 0, 0)),
                  pl.BlockSpec((1, D_EXPERT, D_MODEL), lambda i, te, nu: (te[last(i, nu)], 0, 0))],
        out_specs=pl.BlockSpec((MOE_TILE, D_MODEL), lambda i, te, nu: (i, 0)),
        scratch_shapes=[pltpu.VMEM((D_MODEL, D_EXPERT), BF16), pltpu.VMEM((D_MODEL, D_EXPERT), BF16),
                        pltpu.VMEM((D_EXPERT, D_MODEL), BF16)])
    return pl.pallas_call(
        _expert_body,
        out_shape=jax.ShapeDtypeStruct((r_tot, D_MODEL), BF16),
        grid_spec=grid_spec,
        compiler_params=_cparams(("arbitrary",)),
        name="moe_experts",
    )(tab["tile_expert"], tab["n_used"], xs, wg, wu, wd)


def _combine_body(lstart_ref, gstart_ref, nchunk_ref, ys_hbm, xa_ref, xb_ref, meta_ref, g_ref, b_ref,
                  oa_ref, ob_ref, yloc, moe_s, sem, *, n_t, n_ta):
    t = pl.program_id(0)
    first = t < n_ta
    slot = lax.rem(t, 2)
    tab = (lstart_ref, gstart_ref, nchunk_ref)
    rows = yloc.shape[1]

    def chunk_copy(s, lrow, grow):
        return pltpu.make_async_copy(ys_hbm.at[pl.ds(grow, MOE_CHUNK)], yloc.at[s, pl.ds(lrow, MOE_CHUNK)], sem.at[s])

    def fetch(s, tile):
        _seg_loop(tab, tile, lambda lrow, grow: chunk_copy(s, lrow, grow).start())

    @pl.when(t == 0)
    def _():
        yloc[...] = jnp.zeros_like(yloc)
        fetch(0, 0)

    @pl.when(t + 1 < n_t)
    def _():
        fetch(1 - slot, t + 1)

    def wbody(j, c):
        chunk_copy(slot, 0, 0).wait()
        return c
    lax.fori_loop(0, _n_chunks(nchunk_ref, t), wbody, 0)

    meta = meta_ref[...]
    tt = meta.shape[0]
    l_iota = lax.broadcasted_iota(jnp.int32, (tt, rows), 1).astype(F32)
    sel = (jnp.where(l_iota == meta[:, 0:1], meta[:, 2:3], 0.0) + jnp.where(l_iota == meta[:, 1:2], meta[:, 3:4], 0.0))
    sel_hi, sel_lo = _split_bf16(sel)
    yb = yloc[slot]
    moe_s[...] = jnp.dot(sel_hi, yb, preferred_element_type=F32) + jnp.dot(sel_lo, yb, preferred_element_type=F32)

    @pl.when(first)
    def _():
        oa_ref[...] = _layer_norm(DN_ALPHA * xa_ref[...] + moe_s[...], g_ref[...], b_ref[...])

    @pl.when(jnp.logical_not(first))
    def _():
        ob_ref[...] = _layer_norm(DN_ALPHA * xb_ref[...] + moe_s[...], g_ref[...], b_ref[...])


def _moe_combine(ys, xa, xb, meta_n, tab, g, b, tt):
    n_ta, n_tb = xa.shape[0] // tt, xb.shape[0] // tt
    n_t = n_ta + n_tb
    rows = _local_rows(tt)
    vec = lambda: pl.BlockSpec((1, D_MODEL), lambda t, *_: (0, 0))
    in_a = pl.BlockSpec((tt, D_MODEL), lambda t, *_: (jnp.minimum(t, n_ta - 1), 0))
    in_b = pl.BlockSpec((tt, D_MODEL), lambda t, *_: (jnp.maximum(t - n_ta, 0), 0))
    grid_spec = pltpu.PrefetchScalarGridSpec(
        num_scalar_prefetch=3,
        grid=(n_t,),
        in_specs=[pl.BlockSpec(memory_space=pl.ANY), in_a, in_b,
                  pl.BlockSpec((tt, 8), lambda t, *_: (t, 0)), vec(), vec()],
        out_specs=(in_a, in_b),
        scratch_shapes=[pltpu.VMEM((2, rows, D_MODEL), BF16), pltpu.VMEM((tt, D_MODEL), F32),
                        pltpu.SemaphoreType.DMA((2,))])
    return pl.pallas_call(
        functools.partial(_combine_body, n_t=n_t, n_ta=n_ta),
        out_shape=(jax.ShapeDtypeStruct(xa.shape, F32), jax.ShapeDtypeStruct(xb.shape, F32)),
        grid_spec=grid_spec,
        compiler_params=_cparams(("arbitrary",)),
        name="moe_combine_ln",
    )(tab["lstart"], tab["gstart"], tab["nchunk"], ys, xa, xb, meta_n, g, b)


def _moe(xa, xb, meta_t, p):
    n = xa.shape[0] + xb.shape[0]
    tt = 256
    while xa.shape[0] % tt or xb.shape[0] % tt:
        tt //= 2
    r_tot, lpos, tab = _dispatch(meta_t, n, tt)
    xs = _moe_sort(xa, xb, lpos, tab, tt, r_tot)
    ys = _moe_experts(xs, tab, p["w_gate"], p["w_up"], p["w_down"])
    meta_n = jnp.concatenate([lpos.astype(F32), meta_t[2:4], jnp.zeros((4, n), F32)], axis=0).T
    return _moe_combine(ys, xa, xb, meta_n, tab, p["ln2_g"], p["ln2_b"], tt)


def _largest_tile(n, cap, mult):
    best = None
    for d in range(mult, min(n, cap) + 1, mult):
        if n % d == 0:
            best = d
    assert best is not None, (n, cap, mult)
    return best


def _seq_tails(x, off, b, t, k, c0, c1):
    if b <= 8:
        return jnp.stack([lax.slice(x, (off + i * t + t - k, c0), (off + (i + 1) * t, c1)) for i in range(b)])
    return lax.slice(x, (off, c0), (off + b * t, c1)).reshape(b, t, c1 - c0)[:, t - k:, :]


def _kv_rows(qkv, g, off, b, t, k):
    def pick(which):
        h0 = (which * N_GROUPS + g) * HEADS
        if b <= 8:
            a = jnp.stack([lax.slice(qkv, (h0, off + i * t + t - k, 0), (h0 + HEADS, off + (i + 1) * t, HEAD_DIM))
                           for i in range(b)], axis=1)
        else:
            a = lax.slice(qkv, (h0, off, 0), (h0 + HEADS, off + b * t, HEAD_DIM)).reshape(HEADS, b, t, HEAD_DIM)[:, :, t - k:]
        return jnp.transpose(a, (1, 2, 0, 3))
    return jnp.stack([pick(1), pick(2)], axis=2)


def _block_diag_gates(w_a, w_x):
    per = RNN_CHUNK // RNN_BLOCK
    eye = jnp.eye(per, dtype=F32)

    def chunks(w):
        w4 = w.reshape(N_RNN_CHUNKS, per, RNN_BLOCK, RNN_BLOCK)
        return (w4[:, :, :, None, :] * eye[None, :, None, :, None]).reshape(N_RNN_CHUNKS, RNN_CHUNK, RNN_CHUNK)
    return jnp.concatenate([chunks(w_a), chunks(w_x)], axis=2).astype(BF16)


def kernel(x_prompt, x_sample, cache_kv_w128, cache_kv_w512, cache_kv_w2048, cache_mem_kv, state_h, state_conv, mem_prompt, w_in, b_gates, conv_w, conv_b, w_a, b_a, w_x, b_x, lru_lambda, w_br_lru, w_br_att, w_br_mem, w_o, w_mem_kv, ln1_g, ln1_b, w_rg, b_rg, w_re, b_re, w_gate, w_up, w_down, ln2_g, ln2_b):
    row = lambda v: v.reshape(1, -1).astype(F32)
    w_router = jnp.zeros((ROUTER_ROWS, D_MODEL), F32)
    w_router = w_router.at[:N_EXPERT_GROUPS].set(w_rg.T).at[N_EXPERT_GROUPS:N_EXPERT_GROUPS + N_EXPERTS].set(w_re.T)
    b_router = jnp.zeros((ROUTER_ROWS, 1), F32)
    b_router = b_router.at[:N_EXPERT_GROUPS, 0].set(b_rg).at[N_EXPERT_GROUPS:N_EXPERT_GROUPS + N_EXPERTS, 0].set(b_re)
    p = dict(
        b_gates=row(b_gates), conv_w=conv_w, conv_b=row(conv_b), wax=_block_diag_gates(w_a, w_x),
        b_a=row(b_a), b_x=row(b_x), lam=row(lru_lambda),
        w_br_lru=w_br_lru.astype(BF16), w_br_att=w_br_att.astype(BF16), w_br_mem=w_br_mem.astype(BF16),
        w_o=w_o.astype(BF16), ln1_g=row(ln1_g), ln1_b=row(ln1_b), w_router=w_router, b_router=b_router,
        w_gate=w_gate, w_up=w_up, w_down=w_down,
        ln2_g=row(ln2_g), ln2_b=row(ln2_b))

    bp, s, _ = x_prompt.shape
    bs, ts, _ = x_sample.shape
    n_p, n_s = bp * s, bs * ts
    xp2, xs2 = x_prompt.reshape(n_p, D_MODEL), x_sample.reshape(n_s, D_MODEL)

    tr = 512
    while n_p % tr or n_s % tr:
        tr //= 2
    xb = _cast_rows(xp2, xs2, BF16, tr)
    tm_a = _largest_tile(n_p + n_s, 2304, 16)
    xrg = _matmul(xb, w_in, 0, 2 * D_RNN, F32, tm_a, 768, "in_proj_rnn")
    qkv = _matmul(xb, w_in, COL_Q, COL_GATES - COL_Q, F32, tm_a, 512, "in_proj_qkv", split=4)
    gates = _matmul(xb, w_in, COL_GATES, N_GATE_COLS, BF16, tm_a, 512, "gate_proj", bias=p["b_gates"])
    lru = (p["conv_w"], p["conv_b"], p["wax"], p["b_a"], p["b_x"], p["lam"])
    branch_w = (p["w_br_lru"], p["w_br_att"], p["w_br_mem"])
    ln1 = (p["w_o"], p["ln1_g"], p["ln1_b"], p["w_router"], p["b_router"])

    mem_rows = mem_prompt.reshape(bp * N_MEM, D_MODEL)
    mem_kv_p = _matmul(mem_rows, w_mem_kv, 0, 2 * D_MEM, F32, _tile(bp * N_MEM, 512), 512, "mem_kv_proj")
    hs_p, h_p = _rglru(xrg, 0, bp, s, jnp.zeros((bp, CONV_W - 1, D_RNN), F32), jnp.zeros((bp, D_RNN), F32), *lru,
                       _tile(s, 512))
    att_p = _attn_prompt(qkv, bp, s)
    mem_p = _mem_attn(qkv, mem_kv_p.reshape(bp, N_MEM, 2 * D_MEM), bp, s, _tile(s, 1024), BF16)
    merged_p = _branch_merge(hs_p, xrg, att_p, mem_p, gates, 0, *branch_w, _tile(n_p, 512))
    x1_p, meta_p = _proj_ln(merged_p, xp2, *ln1, _tile(n_p, 512))

    hs_s, h_s = _rglru(xrg, n_p, bs, ts, state_conv, state_h, *lru, ts)
    att_s = _attn_sample(qkv, (cache_kv_w128, cache_kv_w512, cache_kv_w2048), bs, ts, n_p)
    mem_s = _mem_attn_cache(qkv, cache_mem_kv, bs, ts, n_p)
    tm_s = _tile(n_s, 256)
    assert n_p % tm_s == 0
    merged_s = _branch_merge(hs_s, xrg, att_s, mem_s, gates, n_p, *branch_w, tm_s)
    x1_s, meta_s = _proj_ln(merged_s, xs2, *ln1, _tile(n_s, 512))

    y_p, y_s = _moe(x1_p, x1_s, jnp.concatenate([meta_p, meta_s], axis=1), p)

    kc = CONV_W - 1
    conv_p = jnp.concatenate([jnp.zeros((bp, kc, D_RNN), F32), _seq_tails(xrg, 0, bp, s, min(kc, s), 0, D_RNN)], axis=1)[:, -kc:]
    conv_s = jnp.concatenate([state_conv, _seq_tails(xrg, n_p, bs, ts, min(kc, ts), 0, D_RNN)], axis=1)[:, -kc:]
    kv_p = [_kv_rows(qkv, g, 0, bp, s, min(KEYS * d, s)) for g, d in enumerate(DILATIONS)]
    kv_s = [_kv_rows(qkv, g, n_p, bs, ts, ts) for g in range(N_GROUPS)]
    return (y_p.reshape(bp, s, D_MODEL), y_s.reshape(bs, ts, D_MODEL), kv_p[0], kv_p[1], kv_p[2],
            mem_kv_p.reshape(bp, N_MEM, 2, MEM_HEADS, MEM_HEAD_DIM), h_p.reshape(bp, D_RNN), conv_p,
            kv_s[0], kv_s[1], kv_s[2], h_s.reshape(bs, D_RNN), conv_s)
```

```python
import functools

import jax
import jax.numpy as jnp
from jax import lax
from jax.experimental import pallas as pl
from jax.experimental.pallas import tpu as pltpu

F32 = jnp.float32
BF16 = jnp.bfloat16

D_MODEL = 2048
D_RNN = 1536
N_RNN_BLOCKS = 16
RNN_BLOCK = D_RNN // N_RNN_BLOCKS
RNN_CHUNK = 384
N_RNN_CHUNKS = D_RNN // RNN_CHUNK
CONV_W = 4
LRU_C = 8.0
HEAD_DIM = 128
HEADS = 4
DILATIONS = (1, 4, 16)
KEYS = 128
N_GROUPS = 3
D_ATT_OUT = HEADS * HEAD_DIM
N_QKV_HEADS = 3 * N_GROUPS * HEADS
ATT_SCALE = HEAD_DIM ** -0.5
ATT_TILE = KEYS * max(DILATIONS)
N_MEM = 256
MEM_HEADS = 4
MEM_HEAD_DIM = 256
D_MEM = MEM_HEADS * MEM_HEAD_DIM
MEM_SCALE = MEM_HEAD_DIM ** -0.5
N_EXPERT_GROUPS = 4
EXPERTS_PER_GROUP = 4
N_EXPERTS = 16
D_EXPERT = 512
MOE_TILE = 512
DN_ALPHA = 2.0 ** 0.25
LN_EPS = 1e-5
NEG_INF = -1e30

COL_Q = 2 * D_RNN
COL_QM = COL_Q + N_QKV_HEADS * HEAD_DIM
COL_GATES = COL_QM + D_MEM
N_GATE_COLS = 3 * D_MODEL
ROUTER_ROWS = 32

VMEM_LIMIT = 56 * 1024 * 1024


def _cparams(sem):
    return pltpu.CompilerParams(dimension_semantics=sem, vmem_limit_bytes=VMEM_LIMIT)


def _gelu(x):
    return 0.5 * x * (1.0 + jnp.tanh(0.7978845608028654 * (x + 0.044715 * (x * x * x))))


def _layer_norm(x, g, b):
    mu = jnp.mean(x, axis=-1, keepdims=True)
    xc = x - mu
    var = jnp.mean(xc * xc, axis=-1, keepdims=True)
    return xc * lax.rsqrt(var + LN_EPS) * g + b


def _dot_nt(a, b):
    return lax.dot_general(a, b, (((1,), (1,)), ((), ())), preferred_element_type=F32)


def _tile(m, cap):
    t = min(m, cap)
    assert m % t == 0, (m, t)
    return t


def _mm_body(x_ref, w_ref, *rest, split, gate):
    acc = jnp.dot(x_ref[...].astype(BF16), w_ref[...].astype(BF16), preferred_element_type=F32)
    if gate:
        b_ref, o_ref = rest
        acc = 0.5 * jnp.tanh(0.5 * (acc + b_ref[...])) + 0.5
    else:
        (o_ref,) = rest
    if split == 1:
        o_ref[...] = acc.astype(o_ref.dtype)
    else:
        w = acc.shape[1] // split
        for s in range(split):
            o_ref[s] = acc[:, s * w:(s + 1) * w].astype(o_ref.dtype)


def _matmul(x, w, col_off, n_cols, out_dtype, tm, tn, name, bias=None, split=1):
    m, k = x.shape
    cb = col_off // tn
    in_specs = [pl.BlockSpec((tm, k), lambda i, j: (i, 0)),
                pl.BlockSpec((k, tn), lambda i, j: (0, j + cb))]
    args = [x, w]
    if bias is not None:
        in_specs.append(pl.BlockSpec((1, tn), lambda i, j: (0, j)))
        args.append(bias)
    if split == 1:
        out_shape = jax.ShapeDtypeStruct((m, n_cols), out_dtype)
        out_spec = pl.BlockSpec((tm, tn), lambda i, j: (i, j))
    else:
        out_shape = jax.ShapeDtypeStruct((n_cols * split // tn, m, tn // split), out_dtype)
        out_spec = pl.BlockSpec((split, tm, tn // split), lambda i, j: (j, i, 0))
    return pl.pallas_call(
        functools.partial(_mm_body, split=split, gate=bias is not None),
        out_shape=out_shape,
        grid=(m // tm, n_cols // tn),
        in_specs=in_specs,
        out_specs=out_spec,
        compiler_params=_cparams(("parallel", "arbitrary")),
        name=name,
    )(*args)


def _cast_rows_body(a_ref, b_ref, o_ref, *, n_ta):
    o_ref[...] = jnp.where(pl.program_id(0) < n_ta, a_ref[...], b_ref[...]).astype(o_ref.dtype)


def _cast_rows(xa, xb, dtype, tr):
    n_ta, n_tb = xa.shape[0] // tr, xb.shape[0] // tr
    cols = xa.shape[1]
    return pl.pallas_call(
        functools.partial(_cast_rows_body, n_ta=n_ta),
        out_shape=jax.ShapeDtypeStruct((xa.shape[0] + xb.shape[0], cols), dtype),
        grid=(n_ta + n_tb,),
        in_specs=[pl.BlockSpec((tr, cols), lambda i: (jnp.minimum(i, n_ta - 1), 0)),
                  pl.BlockSpec((tr, cols), lambda i: (jnp.maximum(i - n_ta, 0), 0))],
        out_specs=pl.BlockSpec((tr, cols), lambda i: (i, 0)),
        compiler_params=_cparams(("parallel",)),
        name="cast_rows",
    )(xa, xb)


def _rglru_body(xr_ref, cbuf_ref, h0_ref, cw_ref, cb_ref, wax_ref, ba_ref, bx_ref, lam_ref,
                out_ref, hl_ref, ext_s, a_s, u_s, h_s, *, tt):
    t = pl.program_id(1)

    @pl.when(t == 0)
    def _():
        ext_s[0:8, :] = jnp.zeros((8, D_RNN), F32)
        ext_s[5:8, :] = cbuf_ref[0]
        h_s[...] = jnp.broadcast_to(h0_ref[0], (8, D_RNN))

    @pl.when(t > 0)
    def _():
        ext_s[0:8, :] = ext_s[tt:tt + 8, :]

    ext_s[8:8 + tt, :] = xr_ref[...]
    cw = cw_ref[...]
    xc = (cb_ref[...] + cw[3:4, :] * ext_s[8:8 + tt, :] + cw[2:3, :] * ext_s[7:7 + tt, :]
          + cw[1:2, :] * ext_s[6:6 + tt, :] + cw[0:1, :] * ext_s[5:5 + tt, :])
    xcb = xc.astype(BF16)
    r_parts, i_parts = [], []
    for c in range(N_RNN_CHUNKS):
        g = jnp.dot(xcb[:, c * RNN_CHUNK:(c + 1) * RNN_CHUNK], wax_ref[c], preferred_element_type=F32)
        r_parts.append(g[:, :RNN_CHUNK])
        i_parts.append(g[:, RNN_CHUNK:])
    sigmoid = lambda z: 0.5 * jnp.tanh(0.5 * z) + 0.5
    r = sigmoid(jnp.concatenate(r_parts, axis=1) + ba_ref[...])
    gi = sigmoid(jnp.concatenate(i_parts, axis=1) + bx_ref[...])
    nl = -lam_ref[...]
    softplus = jnp.maximum(nl, 0.0) + jnp.log1p(jnp.exp(-jnp.abs(nl)))
    log_a = (-LRU_C) * r * softplus
    th = jnp.tanh(log_a)
    a_s[...] = jnp.exp(log_a)
    u_s[...] = jnp.sqrt(-2.0 * th / (1.0 - th)) * (gi * xc)

    rows = lax.broadcasted_iota(jnp.int32, (8, D_RNN), 0)

    def blk(i, h):
        r0 = pl.multiple_of(i * 8, 8)
        ab = a_s[pl.ds(r0, 8), :]
        ub = u_s[pl.ds(r0, 8), :]
        for s in (1, 2, 4):
            keep = rows >= s
            ub = ab * jnp.where(keep, pltpu.roll(ub, s, 0), 0.0) + ub
            ab = ab * jnp.where(keep, pltpu.roll(ab, s, 0), 1.0)
        hb = ab * h + ub
        u_s[pl.ds(r0, 8), :] = hb
        return jnp.broadcast_to(hb[7:8, :], (8, D_RNN))

    h_fin = lax.fori_loop(0, tt // 8, blk, h_s[...])
    h_s[...] = h_fin
    hl_ref[0] = h_fin[0:1, :]
    out_ref[...] = u_s[...]


def _rglru(xrg, off, b, t, conv_buf, h0, conv_w, conv_b, wax, b_a, b_x, lam, tt):
    nt = t // tt
    ob = off // tt
    vec = lambda: pl.BlockSpec((1, D_RNN), lambda i, j: (0, 0))
    return pl.pallas_call(
        functools.partial(_rglru_body, tt=tt),
        out_shape=(jax.ShapeDtypeStruct((b * t, D_RNN), F32), jax.ShapeDtypeStruct((b, 1, D_RNN), F32)),
        grid=(b, nt),
        in_specs=[pl.BlockSpec((tt, D_RNN), lambda i, j: (ob + i * nt + j, 0)),
                  pl.BlockSpec((1, CONV_W - 1, D_RNN), lambda i, j: (i, 0, 0)),
                  pl.BlockSpec((1, 1, D_RNN), lambda i, j: (i, 0, 0)),
                  pl.BlockSpec((CONV_W, D_RNN), lambda i, j: (0, 0)),
                  vec(),
                  pl.BlockSpec((N_RNN_CHUNKS, RNN_CHUNK, 2 * RNN_CHUNK), lambda i, j: (0, 0, 0)),
                  vec(), vec(), vec()],
        out_specs=(pl.BlockSpec((tt, D_RNN), lambda i, j: (i * nt + j, 0)),
                   pl.BlockSpec((1, 1, D_RNN), lambda i, j: (i, 0, 0))),
        scratch_shapes=[pltpu.VMEM((tt + 8, D_RNN), F32), pltpu.VMEM((tt, D_RNN), F32),
                        pltpu.VMEM((tt, D_RNN), F32), pltpu.VMEM((8, D_RNN), F32)],
        compiler_params=_cparams(("parallel", "arbitrary")),
        name="rglru",
    )(xrg, conv_buf, h0.reshape(b, 1, D_RNN), conv_w, conv_b, wax, b_a, b_x, lam)


def _band_block(q, k, v, bias):
    s = _dot_nt(q, k) * ATT_SCALE + bias
    m = jnp.max(s, axis=-1, keepdims=True)
    p = jnp.exp(s - m)
    l = jnp.sum(p, axis=-1, keepdims=True)
    o = jnp.dot(p.astype(BF16), v, preferred_element_type=F32) / l
    return o, m + jnp.log(l)


def _attn_prompt_body(*refs):
    q_refs, k_refs, v_refs, kh_refs, vh_refs = (refs[3 * i:3 * i + 3] for i in range(5))
    o_ref, og_s, lg_s = refs[15:]
    row = lax.broadcasted_iota(jnp.int32, (KEYS, 2 * KEYS), 0)
    col = lax.broadcasted_iota(jnp.int32, (KEYS, 2 * KEYS), 1)
    band = jnp.logical_and(col >= row, col <= row + KEYS)
    bias = jnp.where(band, 0.0, NEG_INF)
    has_prev = pl.program_id(1) > 0
    bias_first = jnp.where(jnp.logical_and(band, jnp.logical_or(col >= KEYS, has_prev)), 0.0, NEG_INF)

    def put(g, rows, o, lse):
        og_s[g, rows, :] = o
        lg_s[g, rows, :] = jnp.broadcast_to(lse, (KEYS, HEAD_DIM))

    def rows_of(start, size, d):
        return pl.ds(start, size, stride=d) if d > 1 else pl.ds(start, size)

    for g, d in enumerate(DILATIONS):
        q_ref, k_ref, v_ref, kh_ref, vh_ref = q_refs[g], k_refs[g], v_refs[g], kh_refs[g], vh_refs[g]
        nqb = ATT_TILE // (d * KEYS)
        for r in range(d):
            own = rows_of(r, KEYS, d)
            kk = jnp.concatenate([kh_ref[0, own, :], k_ref[0, own, :]], axis=0).astype(BF16)
            vv = jnp.concatenate([vh_ref[0, own, :], v_ref[0, own, :]], axis=0).astype(BF16)
            o, lse = _band_block(q_ref[0, own, :].astype(BF16), kk, vv, bias_first)
            put(g, own, o, lse)
            for qb in range(1, nqb):
                rows = rows_of(qb * KEYS * d + r, KEYS, d)
                keys = rows_of((qb - 1) * KEYS * d + r, 2 * KEYS, d)
                o, lse = _band_block(q_ref[0, rows, :].astype(BF16), k_ref[0, keys, :].astype(BF16),
                                     v_ref[0, keys, :].astype(BF16), bias)
                put(g, rows, o, lse)

    l0, l1, l2 = lg_s[0], lg_s[1], lg_s[2]
    m = jnp.maximum(jnp.maximum(l0, l1), l2)
    e0, e1, e2 = jnp.exp(l0 - m), jnp.exp(l1 - m), jnp.exp(l2 - m)
    o_ref[...] = ((e0 * og_s[0] + e1 * og_s[1] + e2 * og_s[2]) / (e0 + e1 + e2)).astype(o_ref.dtype)


def _attn_prompt(qkv, b, s):
    assert s % ATT_TILE == 0
    nt = s // ATT_TILE
    n = b * s

    def cur(which, g):
        return pl.BlockSpec((1, ATT_TILE, HEAD_DIM),
                            lambda i, j, h: ((which * N_GROUPS + g) * HEADS + h, i * nt + j, 0))

    def halo(which, g):
        rows = KEYS * DILATIONS[g]
        per = ATT_TILE // rows
        return pl.BlockSpec((1, rows, HEAD_DIM),
                            lambda i, j, h: ((which * N_GROUPS + g) * HEADS + h,
                                             jnp.maximum((i * nt + j) * per - 1, 0), 0))

    in_specs = ([cur(0, g) for g in range(N_GROUPS)] + [cur(1, g) for g in range(N_GROUPS)]
                + [cur(2, g) for g in range(N_GROUPS)] + [halo(1, g) for g in range(N_GROUPS)]
                + [halo(2, g) for g in range(N_GROUPS)])
    return pl.pallas_call(
        _attn_prompt_body,
        out_shape=jax.ShapeDtypeStruct((n, D_ATT_OUT), BF16),
        grid=(b, nt, HEADS),
        in_specs=in_specs,
        out_specs=pl.BlockSpec((ATT_TILE, HEAD_DIM), lambda i, j, h: (i * nt + j, h)),
        scratch_shapes=[pltpu.VMEM((N_GROUPS, ATT_TILE, HEAD_DIM), F32),
                        pltpu.VMEM((N_GROUPS, ATT_TILE, HEAD_DIM), F32)],
        compiler_params=_cparams(("parallel", "parallel", "parallel")),
        name="attn_prompt",
    )(*([qkv] * 15))


def _attn_sample_group(q4, kn4, vn4, c_ref, d, t_new):
    half = len(c_ref.shape) == 4
    wb = c_ref.shape[1] * 16 if half else c_ref.shape[1] // (2 * HEADS)
    nk = wb // 2 if half else wb
    nr = HEADS * t_new
    zeros = jnp.zeros((t_new, HEAD_DIM), F32)
    qbd = jnp.concatenate(
        [jnp.concatenate([q4[h] if hh == h else zeros for hh in range(HEADS)], axis=1) for h in range(HEADS)],
        axis=0).astype(BF16)
    kn = jnp.concatenate([kn4[h] for h in range(HEADS)], axis=1).astype(BF16)
    vn = jnp.concatenate([vn4[h] for h in range(HEADS)], axis=1).astype(BF16)
    if half:
        cache_rows = lambda kv, h: c_ref[0, :, pl.ds(kv * HEADS + h, 8, stride=2 * HEADS), :].reshape(nk, HEAD_DIM)
    else:
        cache_rows = lambda kv, h: c_ref[0, pl.ds(kv * HEADS + h, wb, stride=2 * HEADS), :]
    kc = jnp.concatenate([cache_rows(0, h) for h in range(HEADS)], axis=1).astype(BF16)
    vc = jnp.concatenate([cache_rows(1, h) for h in range(HEADS)], axis=1).astype(BF16)
    tq_c = lax.broadcasted_iota(jnp.int32, (nr, nk), 0) & (t_new - 1)
    e_c = lax.broadcasted_iota(jnp.int32, (nr, nk), 1)
    if half:
        e_c = ((e_c >> 3) << 4) + (e_c & 7)
    dist_c = wb + tq_c - e_c
    ok_c = jnp.logical_and((dist_c & (d - 1)) == 0, dist_c <= KEYS * d)
    tq_n = lax.broadcasted_iota(jnp.int32, (nr, t_new), 0) & (t_new - 1)
    dist_n = tq_n - lax.broadcasted_iota(jnp.int32, (nr, t_new), 1)
    ok_n = jnp.logical_and(jnp.logical_and(dist_n >= 0, (dist_n & (d - 1)) == 0), dist_n <= KEYS * d)
    s_c = jnp.where(ok_c, _dot_nt(qbd, kc) * ATT_SCALE, NEG_INF)
    s_n = jnp.where(ok_n, _dot_nt(qbd, kn) * ATT_SCALE, NEG_INF)
    m = jnp.maximum(jnp.max(s_c, axis=-1, keepdims=True), jnp.max(s_n, axis=-1, keepdims=True))
    p_c = jnp.exp(s_c - m)
    p_n = jnp.exp(s_n - m)
    l = jnp.sum(p_c, axis=-1, keepdims=True) + jnp.sum(p_n, axis=-1, keepdims=True)
    o = (jnp.dot(p_c.astype(BF16), vc, preferred_element_type=F32)
         + jnp.dot(p_n.astype(BF16), vn, preferred_element_type=F32)) / l
    lse = m + jnp.log(l)
    o = jnp.concatenate(
        [o[h * t_new:(h + 1) * t_new, h * HEAD_DIM:(h + 1) * HEAD_DIM] for h in range(HEADS)], axis=1)
    lse = jnp.concatenate(
        [jnp.broadcast_to(lse[h * t_new:(h + 1) * t_new], (t_new, HEAD_DIM)) for h in range(HEADS)], axis=1)
    return o, lse


def _attn_sample_body(*refs, t_new):
    q_refs, k_refs, v_refs, c_refs = (refs[3 * i:3 * i + 3] for i in range(4))
    o_ref = refs[12]
    outs = [_attn_sample_group(q_refs[g][...], k_refs[g][...], v_refs[g][...], c_refs[g], d, t_new)
            for g, d in enumerate(DILATIONS)]
    (o0, l0), (o1, l1), (o2, l2) = outs
    m = jnp.maximum(jnp.maximum(l0, l1), l2)
    e0, e1, e2 = jnp.exp(l0 - m), jnp.exp(l1 - m), jnp.exp(l2 - m)
    o_ref[0] = (e0 * o0 + e1 * o1 + e2 * o2) / (e0 + e1 + e2)


def _attn_sample(qkv, caches, b, t_new, off):
    assert t_new & (t_new - 1) == 0 and off % t_new == 0
    ob = off // t_new
    new = lambda which, g: pl.BlockSpec((HEADS, t_new, HEAD_DIM), lambda i: (which * N_GROUPS + g, ob + i, 0))
    caches2, cache_specs = [], []
    for c, d in zip(caches, DILATIONS):
        wb = c.shape[1]
        if d == 16 and wb % 16 == 0 and t_new <= 8:
            caches2.append(c.reshape(b, wb // 16, 16 * 2 * HEADS, HEAD_DIM))
            cache_specs.append(pl.BlockSpec((1, wb // 16, 8 * 2 * HEADS, HEAD_DIM), lambda i: (i, 0, 0, 0)))
        else:
            caches2.append(c.reshape(b, wb * 2 * HEADS, HEAD_DIM))
            cache_specs.append(pl.BlockSpec((1, wb * 2 * HEADS, HEAD_DIM), lambda i: (i, 0, 0)))
    in_specs = ([new(0, g) for g in range(N_GROUPS)] + [new(1, g) for g in range(N_GROUPS)]
                + [new(2, g) for g in range(N_GROUPS)] + cache_specs)
    att = pl.pallas_call(
        functools.partial(_attn_sample_body, t_new=t_new),
        out_shape=jax.ShapeDtypeStruct((b, t_new, D_ATT_OUT), F32),
        grid=(b,),
        in_specs=in_specs,
        out_specs=pl.BlockSpec((1, t_new, D_ATT_OUT), lambda i: (i, 0, 0)),
        compiler_params=_cparams(("parallel",)),
        name="attn_sample",
    )(*([qkv] * 9), *caches2)
    return att.reshape(b * t_new, D_ATT_OUT)


def _mem_q_specs(rows, index):
    first = N_QKV_HEADS // 2
    return [pl.BlockSpec((2, rows, HEAD_DIM), functools.partial(index, first + h)) for h in range(MEM_HEADS)]


def _mem_attn_body(q0_ref, q1_ref, q2_ref, q3_ref, kv_ref, o_ref):
    for h, q_ref in enumerate((q0_ref, q1_ref, q2_ref, q3_ref)):
        k = kv_ref[0, :, h * MEM_HEAD_DIM:(h + 1) * MEM_HEAD_DIM].astype(BF16)
        v = kv_ref[0, :, D_MEM + h * MEM_HEAD_DIM:D_MEM + (h + 1) * MEM_HEAD_DIM].astype(BF16)
        q = jnp.concatenate([q_ref[0], q_ref[1]], axis=1).astype(BF16)
        s = _dot_nt(q, k) * MEM_SCALE
        m = jnp.max(s, axis=-1, keepdims=True)
        p = jnp.exp(s - m)
        l = jnp.sum(p, axis=-1, keepdims=True)
        o = jnp.dot(p.astype(BF16), v, preferred_element_type=F32) / l
        o_ref[:, h * MEM_HEAD_DIM:(h + 1) * MEM_HEAD_DIM] = o.astype(o_ref.dtype)


def _mem_attn(qkvm, mem_kv, b, t, tm, out_dtype):
    nt = t // tm
    return pl.pallas_call(
        _mem_attn_body,
        out_shape=jax.ShapeDtypeStruct((b * t, D_MEM), out_dtype),
        grid=(b, nt),
        in_specs=_mem_q_specs(tm, lambda blk, i, j: (blk, i * nt + j, 0))
        + [pl.BlockSpec((1, N_MEM, 2 * D_MEM), lambda i, j: (i, 0, 0))],
        out_specs=pl.BlockSpec((tm, D_MEM), lambda i, j: (i * nt + j, 0)),
        compiler_params=_cparams(("parallel", "parallel")),
        name="mem_attn",
    )(qkvm, qkvm, qkvm, qkvm, mem_kv)


def _mem_attn_cache_body(q0_ref, q1_ref, q2_ref, q3_ref, c_ref, o_ref):
    t = q0_ref.shape[1]
    n_rows = N_MEM * 2 * MEM_HEADS
    flat = c_ref[0].reshape(n_rows, MEM_HEAD_DIM).astype(BF16)
    q = jnp.concatenate([jnp.concatenate([q_ref[0], q_ref[1]], axis=1) for q_ref in (q0_ref, q1_ref, q2_ref, q3_ref)],
                        axis=0).astype(BF16)
    assert t & (t - 1) == 0
    head = lax.broadcasted_iota(jnp.int32, (MEM_HEADS * t, n_rows), 0) >> (t.bit_length() - 1)
    col = lax.broadcasted_iota(jnp.int32, (MEM_HEADS * t, n_rows), 1)
    s = jnp.where((col & (2 * MEM_HEADS - 1)) == head, _dot_nt(q, flat) * MEM_SCALE, NEG_INF)
    m = jnp.max(s, axis=-1, keepdims=True)
    p = jnp.exp(s - m)
    l = jnp.sum(p, axis=-1, keepdims=True)
    o = jnp.dot(pltpu.roll(p, MEM_HEADS, 1).astype(BF16), flat, preferred_element_type=F32) / l
    o_ref[...] = jnp.concatenate([o[h * t:(h + 1) * t, :] for h in range(MEM_HEADS)], axis=1)


def _mem_attn_cache(qkvm, cache, b, t, off):
    ob = off // t
    return pl.pallas_call(
        _mem_attn_cache_body,
        out_shape=jax.ShapeDtypeStruct((b * t, D_MEM), F32),
        grid=(b,),
        in_specs=_mem_q_specs(t, lambda blk, i: (blk, ob + i, 0))
        + [pl.BlockSpec((1, N_MEM, 2, MEM_HEADS, MEM_HEAD_DIM), lambda i: (i, 0, 0, 0, 0))],
        out_specs=pl.BlockSpec((t, D_MEM), lambda i: (i, 0)),
        compiler_params=_cparams(("parallel",)),
        name="mem_attn_cache",
    )(qkvm, qkvm, qkvm, qkvm, cache)


def _branch_body(hs_ref, xg_ref, att_ref, mem_ref, gt_ref, wl_ref, wa_ref, wm_ref, out_ref):
    a_lru = (hs_ref[...] * _gelu(xg_ref[...])).astype(BF16)
    acc = gt_ref[:, 0:D_MODEL].astype(F32) * jnp.dot(a_lru, wl_ref[...], preferred_element_type=F32)
    acc = acc + gt_ref[:, D_MODEL:2 * D_MODEL].astype(F32) * jnp.dot(att_ref[...].astype(BF16), wa_ref[...], preferred_element_type=F32)
    acc = acc + gt_ref[:, 2 * D_MODEL:3 * D_MODEL].astype(F32) * jnp.dot(mem_ref[...].astype(BF16), wm_ref[...], preferred_element_type=F32)
    out_ref[...] = acc.astype(out_ref.dtype)


def _branch_merge(hs, xrg, att, mem, gates, goff, wl, wa, wm, tm):
    n = hs.shape[0]
    gb = goff // tm
    row = lambda w: pl.BlockSpec((tm, w), lambda i: (i, 0))
    full = lambda a: pl.BlockSpec(a.shape, lambda i: (0, 0), pipeline_mode=pl.Buffered(1))
    return pl.pallas_call(
        _branch_body,
        out_shape=jax.ShapeDtypeStruct((n, D_MODEL), BF16),
        grid=(n // tm,),
        in_specs=[row(D_RNN), pl.BlockSpec((tm, D_RNN), lambda i: (gb + i, 1)), row(D_ATT_OUT), row(D_MEM),
                  pl.BlockSpec((tm, N_GATE_COLS), lambda i: (gb + i, 0)), full(wl), full(wa), full(wm)],
        out_specs=row(D_MODEL),
        compiler_params=_cparams(("parallel",)),
        name="branch_merge",
    )(hs, xrg, att, mem, gates, wl, wa, wm)


def _split_bf16(x):
    hi = x.astype(BF16)
    return hi, (x - hi.astype(F32)).astype(BF16)


def _route_rows(lg):
    g = [lg[i:i + 1, :] for i in range(N_EXPERT_GROUPS)]
    gmax = jnp.maximum(jnp.maximum(g[0], g[1]), jnp.maximum(g[2], g[3]))
    gidx = jnp.where(g[0] == gmax, 0.0, jnp.where(g[1] == gmax, 1.0, jnp.where(g[2] == gmax, 2.0, 3.0)))
    g_p = 1.0 / (jnp.exp(g[0] - gmax) + jnp.exp(g[1] - gmax) + jnp.exp(g[2] - gmax) + jnp.exp(g[3] - gmax))
    e = []
    for k in range(EXPERTS_PER_GROUP):
        rows = [lg[N_EXPERT_GROUPS + gg * EXPERTS_PER_GROUP + k:N_EXPERT_GROUPS + gg * EXPERTS_PER_GROUP + k + 1, :]
                for gg in range(N_EXPERT_GROUPS)]
        e.append(jnp.where(gidx == 0.0, rows[0], jnp.where(gidx == 1.0, rows[1], jnp.where(gidx == 2.0, rows[2], rows[3]))))

    def first_argmax(v):
        mx = jnp.maximum(jnp.maximum(v[0], v[1]), jnp.maximum(v[2], v[3]))
        ix = jnp.where(v[0] == mx, 0.0, jnp.where(v[1] == mx, 1.0, jnp.where(v[2] == mx, 2.0, 3.0)))
        return mx, ix

    v1, i1 = first_argmax(e)
    v2, i2 = first_argmax([jnp.where(i1 == float(k), -jnp.inf, e[k]) for k in range(EXPERTS_PER_GROUP)])
    ex = jnp.exp(v2 - v1)
    w1 = g_p / (1.0 + ex)
    w2 = g_p * ex / (1.0 + ex)
    base = gidx * float(EXPERTS_PER_GROUP)
    zero = jnp.zeros_like(w1)
    return jnp.concatenate([base + i1, base + i2, w1, w2, zero, zero, zero, zero], axis=0)


def _proj_ln_body(mg_ref, x_ref, wo_ref, g_ref, b_ref, wr_ref, br_ref, x1_ref, meta_ref, mix_s):
    @pl.when(pl.program_id(0) == 0)
    def _():
        mix_s[...] = jnp.zeros(mix_s.shape, F32)

    mix_next = jnp.dot(mg_ref[...], wo_ref[...], preferred_element_type=F32)
    wh, wl = _split_bf16(wr_ref[...])
    tm = x_ref.shape[0]
    sub = min(256, tm)
    for r0 in range(0, tm, sub):
        rs = pl.ds(r0, sub)
        x1 = _layer_norm(DN_ALPHA * x_ref[rs, :] + mix_s[rs, :], g_ref[...], b_ref[...])
        x1_ref[rs, :] = x1
        xh, xl = _split_bf16(x1)
        lg = _dot_nt(wh, xh) + (_dot_nt(wh, xl) + _dot_nt(wl, xh)) + br_ref[...]
        meta_ref[:, rs] = _route_rows(lg)
    mix_s[...] = mix_next


def _proj_ln(merged, x, wo, g, b, wr, br, tm):
    n = merged.shape[0]
    nb = n // tm
    prev = lambda w: pl.BlockSpec((tm, w), lambda i: (jnp.maximum(i - 1, 0), 0))
    full = lambda a: pl.BlockSpec(a.shape, lambda i: (0, 0))
    return pl.pallas_call(
        _proj_ln_body,
        out_shape=(jax.ShapeDtypeStruct((n, D_MODEL), F32), jax.ShapeDtypeStruct((8, n), F32)),
        grid=(nb + 1,),
        in_specs=[pl.BlockSpec((tm, D_MODEL), lambda i: (jnp.minimum(i, nb - 1), 0)), prev(D_MODEL),
                  full(wo), full(g), full(b), full(wr), full(br)],
        out_specs=(prev(D_MODEL), pl.BlockSpec((8, tm), lambda i: (0, jnp.maximum(i - 1, 0)))),
        scratch_shapes=[pltpu.VMEM((tm, D_MODEL), F32)],
        compiler_params=_cparams(("arbitrary",)),
        name="proj_ln_router",
    )(merged, x, wo, g, b, wr, br)


MOE_CHUNK = 16


def _local_rows(tt):
    return -(-(2 * tt + N_EXPERTS * (MOE_CHUNK - 1)) // 128) * 128


def _dispatch(meta_t, n, tt):
    n_t = n // tt
    ids = meta_t[0:2].astype(jnp.int32)
    onehot = (ids[:, :, None] == jnp.arange(N_EXPERTS, dtype=jnp.int32)).astype(jnp.int32).reshape(2, n_t, tt, N_EXPERTS)
    cnt_slot = jnp.sum(onehot, axis=2)
    cnt = cnt_slot[0] + cnt_slot[1]
    pc = (cnt + MOE_CHUNK - 1) // MOE_CHUNK * MOE_CHUNK
    lstart = jnp.cumsum(pc, axis=1) - pc
    tri = (jnp.arange(tt)[:, None] >= jnp.arange(tt)[None, :]).astype(F32)
    csum = jnp.einsum("ut,snte->snue", tri, onehot.astype(F32)).astype(jnp.int32)
    rank = csum - onehot + jnp.stack([jnp.zeros_like(cnt), cnt_slot[0]])[:, :, None, :]
    lpos = jnp.sum(onehot * (lstart[None, :, None, :] + rank), axis=-1).reshape(2, n)
    seg = jnp.sum(pc, axis=0)
    pe = (seg + MOE_TILE - 1) // MOE_TILE * MOE_TILE
    ends = jnp.cumsum(pe)
    base = ends - pe
    gstart = base[None, :] + jnp.cumsum(pc, axis=0) - pc
    r_tot = -(-(2 * n + N_EXPERTS * (MOE_CHUNK - 1) * n_t) // MOE_TILE) * MOE_TILE + N_EXPERTS * MOE_TILE
    n_tiles = r_tot // MOE_TILE
    tile_start = jnp.arange(n_tiles, dtype=jnp.int32) * MOE_TILE
    tile_expert = jnp.minimum(jnp.sum((tile_start[:, None] >= ends[None, :]).astype(jnp.int32), axis=1), N_EXPERTS - 1)
    tables = dict(
        lstart=lstart.reshape(-1), gstart=gstart.reshape(-1), nchunk=(pc // MOE_CHUNK).reshape(-1),
        zstart=jnp.concatenate([base + seg, ends[-1:]]),
        zcount=jnp.concatenate([(pe - seg) // MOE_CHUNK, (r_tot - ends[-1:]) // MOE_TILE]),
        tile_expert=tile_expert, n_used=(ends[-1] // MOE_TILE).reshape(1))
    return r_tot, lpos, tables


def _seg_loop(tab, t, fn):
    lstart_ref, gstart_ref, nchunk_ref = tab
    for e in range(N_EXPERTS):
        ls = lstart_ref[t * N_EXPERTS + e]
        gs = gstart_ref[t * N_EXPERTS + e]

        def body(j, c, ls=ls, gs=gs, e=e):
            fn(pl.multiple_of(ls + j * MOE_CHUNK, MOE_CHUNK), pl.multiple_of(gs + j * MOE_CHUNK, MOE_CHUNK), e % 2)
            return c
        lax.fori_loop(0, nchunk_ref[t * N_EXPERTS + e], body, 0)


def _n_chunks(nchunk_ref, t):
    tot = nchunk_ref[t * N_EXPERTS]
    for e in range(1, N_EXPERTS):
        tot = tot + nchunk_ref[t * N_EXPERTS + e]
    return tot


def _sort_body(lstart_ref, gstart_ref, nchunk_ref, zstart_ref, zcount_ref, lpos_ref, xa_ref, xb_ref, xs_hbm,
               xloc, zbuf, sem, zsem, *, n_t, n_ta):
    t = pl.program_id(0)
    x_tile = jnp.where(t < n_ta, xa_ref[...], xb_ref[...]).astype(BF16)
    slot = lax.rem(t, 2)
    tab = (lstart_ref, gstart_ref, nchunk_ref)
    rows = xloc.shape[1]

    def chunk_copy(s, lrow, grow):
        return pltpu.make_async_copy(xloc.at[s, pl.ds(lrow, MOE_CHUNK)], xs_hbm.at[pl.ds(grow, MOE_CHUNK)], sem.at[s])

    def wait_tile(s, tile):
        def body(j, c):
            chunk_copy(s, 0, 0).wait()
            return c
        lax.fori_loop(0, _n_chunks(nchunk_ref, tile), body, 0)

    @pl.when(t == 0)
    def _():
        zbuf[...] = jnp.zeros_like(zbuf)
        zero_copy = lambda grow: pltpu.make_async_copy(
            zbuf.at[pl.ds(0, MOE_CHUNK)], xs_hbm.at[pl.ds(grow, MOE_CHUNK)], zsem.at[0])
        zero_tile = lambda grow: pltpu.make_async_copy(zbuf, xs_hbm.at[pl.ds(grow, MOE_TILE)], zsem.at[0])
        tail_start = zstart_ref[N_EXPERTS]
        for e in range(N_EXPERTS):
            def zb(j, c, e=e):
                zero_copy(pl.multiple_of(zstart_ref[e] + j * MOE_CHUNK, MOE_CHUNK)).start()
                return c
            lax.fori_loop(0, zcount_ref[e], zb, 0)

        def tb(j, c):
            zero_tile(pl.multiple_of(tail_start + j * MOE_TILE, MOE_TILE)).start()
            return c
        lax.fori_loop(0, zcount_ref[N_EXPERTS], tb, 0)
        for e in range(N_EXPERTS):
            def zw(j, c):
                zero_copy(0).wait()
                return c
            lax.fori_loop(0, zcount_ref[e], zw, 0)

        def tw(j, c):
            zero_tile(0).wait()
            return c
        lax.fori_loop(0, zcount_ref[N_EXPERTS], tw, 0)

    @pl.when(t >= 2)
    def _():
        wait_tile(slot, t - 2)

    l_iota = lax.broadcasted_iota(jnp.int32, (rows, xa_ref.shape[0]), 0)
    perm = jnp.logical_or(l_iota == lpos_ref[0:1, :], l_iota == lpos_ref[1:2, :])
    perm = jnp.where(perm, 1.0, 0.0).astype(BF16)
    xloc[slot] = jnp.dot(perm, x_tile, preferred_element_type=F32).astype(BF16)
    _seg_loop(tab, t, lambda lrow, grow, queue: chunk_copy(slot, lrow, grow).start(priority=queue))

    @pl.when(t == n_t - 1)
    def _():
        wait_tile(slot, t)
        if n_t >= 2:
            wait_tile(1 - slot, t - 1)


def _moe_sort(xa, xb, lpos, tab, tt, r_tot):
    n_ta, n_tb = xa.shape[0] // tt, xb.shape[0] // tt
    n_t = n_ta + n_tb
    rows = _local_rows(tt)
    grid_spec = pltpu.PrefetchScalarGridSpec(
        num_scalar_prefetch=5,
        grid=(n_t,),
        in_specs=[pl.BlockSpec((2, tt), lambda t, *_: (0, t)),
                  pl.BlockSpec((tt, D_MODEL), lambda t, *_: (jnp.minimum(t, n_ta - 1), 0)),
                  pl.BlockSpec((tt, D_MODEL), lambda t, *_: (jnp.maximum(t - n_ta, 0), 0))],
        out_specs=pl.BlockSpec(memory_space=pl.ANY),
        scratch_shapes=[pltpu.VMEM((2, rows, D_MODEL), BF16), pltpu.VMEM((MOE_TILE, D_MODEL), BF16),
                        pltpu.SemaphoreType.DMA((2,)), pltpu.SemaphoreType.DMA((1,))])
    return pl.pallas_call(
        functools.partial(_sort_body, n_t=n_t, n_ta=n_ta),
        out_shape=jax.ShapeDtypeStruct((r_tot, D_MODEL), BF16),
        grid_spec=grid_spec,
        compiler_params=_cparams(("arbitrary",)),
        name="moe_sort",
    )(tab["lstart"], tab["gstart"], tab["nchunk"], tab["zstart"], tab["zcount"], lpos, xa, xb)


def _expert_body(te_ref, nu_ref, x_ref, wg_ref, wu_ref, wd_ref, o_ref, wg_s, wu_s, wd_s):
    i = pl.program_id(0)
    used = i < nu_ref[0]
    new_expert = jnp.logical_or(i == 0, te_ref[i] != te_ref[jnp.maximum(i - 1, 0)])

    @pl.when(jnp.logical_and(used, new_expert))
    def _():
        wg_s[...] = wg_ref[0].astype(BF16)
        wu_s[...] = wu_ref[0].astype(BF16)
        wd_s[...] = wd_ref[0].astype(BF16)

    @pl.when(used)
    def _():
        xb = x_ref[...]
        hid = _gelu(jnp.dot(xb, wg_s[...], preferred_element_type=F32)) * jnp.dot(xb, wu_s[...], preferred_element_type=F32)
        o_ref[...] = jnp.dot(hid.astype(BF16), wd_s[...], preferred_element_type=F32).astype(o_ref.dtype)

    @pl.when(jnp.logical_not(used))
    def _():
        o_ref[...] = jnp.zeros_like(o_ref)


def _moe_experts(xs, tab, wg, wu, wd):
    r_tot = xs.shape[0]
    last = lambda i, nu: jnp.minimum(i, nu[0] - 1)
    grid_spec = pltpu.PrefetchScalarGridSpec(
        num_scalar_prefetch=2,
        grid=(r_tot // MOE_TILE,),
        in_specs=[pl.BlockSpec((MOE_TILE, D_MODEL), lambda i, te, nu: (last(i, nu), 0)),
                  pl.BlockSpec((1, D_MODEL, D_EXPERT), lambda i, te, nu: (te[last(i, nu)], 0, 0)),
                  pl.BlockSpec((1, D_MODEL, D_EXPERT), lambda i, te, nu: (te[last(i, nu)], 0, 0)),
                  pl.BlockSpec((1, D_EXPERT, D_MODEL), lambda i, te, nu: (te[last(i, nu)], 0, 0))],
        out_specs=pl.BlockSpec((MOE_TILE, D_MODEL), lambda i, te, nu: (i, 0)),
        scratch_shapes=[pltpu.VMEM((D_MODEL, D_EXPERT), BF16), pltpu.VMEM((D_MODEL, D_EXPERT), BF16),
                        pltpu.VMEM((D_EXPERT, D_MODEL), BF16)])
    return pl.pallas_call(
        _expert_body,
        out_shape=jax.ShapeDtypeStruct((r_tot, D_MODEL), BF16),
        grid_spec=grid_spec,
        compiler_params=_cparams(("arbitrary",)),
        name="moe_experts",
    )(tab["tile_expert"], tab["n_used"], xs, wg, wu, wd)


def _combine_body(lstart_ref, gstart_ref, nchunk_ref, ys_hbm, xa_ref, xb_ref, meta_ref, g_ref, b_ref,
                  oa_ref, ob_ref, yloc, moe_s, sem, *, n_t, n_ta):
    t = pl.program_id(0)
    first = t < n_ta
    slot = lax.rem(t, 2)
    tab = (lstart_ref, gstart_ref, nchunk_ref)
    rows = yloc.shape[1]

    def chunk_copy(s, lrow, grow):
        return pltpu.make_async_copy(ys_hbm.at[pl.ds(grow, MOE_CHUNK)], yloc.at[s, pl.ds(lrow, MOE_CHUNK)], sem.at[s])

    def fetch(s, tile):
        _seg_loop(tab, tile, lambda lrow, grow, queue: chunk_copy(s, lrow, grow).start(priority=queue))

    @pl.when(t == 0)
    def _():
        yloc[...] = jnp.zeros_like(yloc)
        fetch(0, 0)

    @pl.when(t + 1 < n_t)
    def _():
        fetch(1 - slot, t + 1)

    def wbody(j, c):
        chunk_copy(slot, 0, 0).wait()
        return c
    lax.fori_loop(0, _n_chunks(nchunk_ref, t), wbody, 0)

    meta = meta_ref[...]
    tt = meta.shape[0]
    l_iota = lax.broadcasted_iota(jnp.int32, (tt, rows), 1).astype(F32)
    sel = (jnp.where(l_iota == meta[:, 0:1], meta[:, 2:3], 0.0) + jnp.where(l_iota == meta[:, 1:2], meta[:, 3:4], 0.0))
    sel_hi, sel_lo = _split_bf16(sel)
    yb = yloc[slot]
    moe_s[...] = jnp.dot(sel_hi, yb, preferred_element_type=F32) + jnp.dot(sel_lo, yb, preferred_element_type=F32)

    @pl.when(first)
    def _():
        oa_ref[...] = _layer_norm(DN_ALPHA * xa_ref[...] + moe_s[...], g_ref[...], b_ref[...])

    @pl.when(jnp.logical_not(first))
    def _():
        ob_ref[...] = _layer_norm(DN_ALPHA * xb_ref[...] + moe_s[...], g_ref[...], b_ref[...])


def _moe_combine(ys, xa, xb, meta_n, tab, g, b, tt):
    n_ta, n_tb = xa.shape[0] // tt, xb.shape[0] // tt
    n_t = n_ta + n_tb
    rows = _local_rows(tt)
    vec = lambda: pl.BlockSpec((1, D_MODEL), lambda t, *_: (0, 0))
    in_a = pl.BlockSpec((tt, D_MODEL), lambda t, *_: (jnp.minimum(t, n_ta - 1), 0))
    in_b = pl.BlockSpec((tt, D_MODEL), lambda t, *_: (jnp.maximum(t - n_ta, 0), 0))
    grid_spec = pltpu.PrefetchScalarGridSpec(
        num_scalar_prefetch=3,
        grid=(n_t,),
        in_specs=[pl.BlockSpec(memory_space=pl.ANY), in_a, in_b,
                  pl.BlockSpec((tt, 8), lambda t, *_: (t, 0)), vec(), vec()],
        out_specs=(in_a, in_b),
        scratch_shapes=[pltpu.VMEM((2, rows, D_MODEL), BF16), pltpu.VMEM((tt, D_MODEL), F32),
                        pltpu.SemaphoreType.DMA((2,))])
    return pl.pallas_call(
        functools.partial(_combine_body, n_t=n_t, n_ta=n_ta),
        out_shape=(jax.ShapeDtypeStruct(xa.shape, F32), jax.ShapeDtypeStruct(xb.shape, F32)),
        grid_spec=grid_spec,
        compiler_params=_cparams(("arbitrary",)),
        name="moe_combine_ln",
    )(tab["lstart"], tab["gstart"], tab["nchunk"], ys, xa, xb, meta_n, g, b)


def _moe(xa, xb, meta_t, p):
    n = xa.shape[0] + xb.shape[0]
    tt = 256
    while xa.shape[0] % tt or xb.shape[0] % tt:
        tt //= 2
    r_tot, lpos, tab = _dispatch(meta_t, n, tt)
    xs = _moe_sort(xa, xb, lpos, tab, tt, r_tot)
    ys = _moe_experts(xs, tab, p["w_gate"], p["w_up"], p["w_down"])
    meta_n = jnp.concatenate([lpos.astype(F32), meta_t[2:4], jnp.zeros((4, n), F32)], axis=0).T
    return _moe_combine(ys, xa, xb, meta_n, tab, p["ln2_g"], p["ln2_b"], tt)


def _largest_tile(n, cap, mult):
    best = None
    for d in range(mult, min(n, cap) + 1, mult):
        if n % d == 0:
            best = d
    assert best is not None, (n, cap, mult)
    return best


def _seq_tails(x, off, b, t, k, c0, c1):
    if b <= 8:
        return jnp.stack([lax.slice(x, (off + i * t + t - k, c0), (off + (i + 1) * t, c1)) for i in range(b)])
    return lax.slice(x, (off, c0), (off + b * t, c1)).reshape(b, t, c1 - c0)[:, t - k:, :]


def _kv_rows(qkv, g, off, b, t, k):
    def pick(which):
        h0 = (which * N_GROUPS + g) * HEADS
        if b <= 8:
            a = jnp.stack([lax.slice(qkv, (h0, off + i * t + t - k, 0), (h0 + HEADS, off + (i + 1) * t, HEAD_DIM))
                           for i in range(b)], axis=1)
        else:
            a = lax.slice(qkv, (h0, off, 0), (h0 + HEADS, off + b * t, HEAD_DIM)).reshape(HEADS, b, t, HEAD_DIM)[:, :, t - k:]
        return jnp.transpose(a, (1, 2, 0, 3))
    return jnp.stack([pick(1), pick(2)], axis=2)


def _block_diag_gates(w_a, w_x):
    per = RNN_CHUNK // RNN_BLOCK
    eye = jnp.eye(per, dtype=F32)

    def chunks(w):
        w4 = w.reshape(N_RNN_CHUNKS, per, RNN_BLOCK, RNN_BLOCK)
        return (w4[:, :, :, None, :] * eye[None, :, None, :, None]).reshape(N_RNN_CHUNKS, RNN_CHUNK, RNN_CHUNK)
    return jnp.concatenate([chunks(w_a), chunks(w_x)], axis=2).astype(BF16)


def kernel(x_prompt, x_sample, cache_kv_w128, cache_kv_w512, cache_kv_w2048, cache_mem_kv, state_h, state_conv, mem_prompt, w_in, b_gates, conv_w, conv_b, w_a, b_a, w_x, b_x, lru_lambda, w_br_lru, w_br_att, w_br_mem, w_o, w_mem_kv, ln1_g, ln1_b, w_rg, b_rg, w_re, b_re, w_gate, w_up, w_down, ln2_g, ln2_b):
    row = lambda v: v.reshape(1, -1).astype(F32)
    w_router = jnp.zeros((ROUTER_ROWS, D_MODEL), F32)
    w_router = w_router.at[:N_EXPERT_GROUPS].set(w_rg.T).at[N_EXPERT_GROUPS:N_EXPERT_GROUPS + N_EXPERTS].set(w_re.T)
    b_router = jnp.zeros((ROUTER_ROWS, 1), F32)
    b_router = b_router.at[:N_EXPERT_GROUPS, 0].set(b_rg).at[N_EXPERT_GROUPS:N_EXPERT_GROUPS + N_EXPERTS, 0].set(b_re)
    p = dict(
        b_gates=row(b_gates), conv_w=conv_w, conv_b=row(conv_b), wax=_block_diag_gates(w_a, w_x),
        b_a=row(b_a), b_x=row(b_x), lam=row(lru_lambda),
        w_br_lru=w_br_lru.astype(BF16), w_br_att=w_br_att.astype(BF16), w_br_mem=w_br_mem.astype(BF16),
        w_o=w_o.astype(BF16), ln1_g=row(ln1_g), ln1_b=row(ln1_b), w_router=w_router, b_router=b_router,
        w_gate=w_gate, w_up=w_up, w_down=w_down,
        ln2_g=row(ln2_g), ln2_b=row(ln2_b))

    bp, s, _ = x_prompt.shape
    bs, ts, _ = x_sample.shape
    n_p, n_s = bp * s, bs * ts
    xp2, xs2 = x_prompt.reshape(n_p, D_MODEL), x_sample.reshape(n_s, D_MODEL)

    tr = 512
    while n_p % tr or n_s % tr:
        tr //= 2
    xb = _cast_rows(xp2, xs2, BF16, tr)
    tm_a = _largest_tile(n_p + n_s, 2304, 16)
    xrg = _matmul(xb, w_in, 0, 2 * D_RNN, F32, tm_a, 768, "in_proj_rnn")
    qkv = _matmul(xb, w_in, COL_Q, COL_GATES - COL_Q, F32, tm_a, 512, "in_proj_qkv", split=4)
    gates = _matmul(xb, w_in, COL_GATES, N_GATE_COLS, BF16, tm_a, 512, "gate_proj", bias=p["b_gates"])
    lru = (p["conv_w"], p["conv_b"], p["wax"], p["b_a"], p["b_x"], p["lam"])
    branch_w = (p["w_br_lru"], p["w_br_att"], p["w_br_mem"])
    ln1 = (p["w_o"], p["ln1_g"], p["ln1_b"], p["w_router"], p["b_router"])

    mem_rows = mem_prompt.reshape(bp * N_MEM, D_MODEL)
    mem_kv_p = _matmul(mem_rows, w_mem_kv, 0, 2 * D_MEM, F32, _tile(bp * N_MEM, 512), 512, "mem_kv_proj")
    hs_p, h_p = _rglru(xrg, 0, bp, s, jnp.zeros((bp, CONV_W - 1, D_RNN), F32), jnp.zeros((bp, D_RNN), F32), *lru,
                       _tile(s, 512))
    att_p = _attn_prompt(qkv, bp, s)
    mem_p = _mem_attn(qkv, mem_kv_p.reshape(bp, N_MEM, 2 * D_MEM), bp, s, _tile(s, 1024), BF16)
    merged_p = _branch_merge(hs_p, xrg, att_p, mem_p, gates, 0, *branch_w, _tile(n_p, 512))
    x1_p, meta_p = _proj_ln(merged_p, xp2, *ln1, _tile(n_p, 512))

    hs_s, h_s = _rglru(xrg, n_p, bs, ts, state_conv, state_h, *lru, ts)
    att_s = _attn_sample(qkv, (cache_kv_w128, cache_kv_w512, cache_kv_w2048), bs, ts, n_p)
    mem_s = _mem_attn_cache(qkv, cache_mem_kv, bs, ts, n_p)
    tm_s = _tile(n_s, 256)
    assert n_p % tm_s == 0
    merged_s = _branch_merge(hs_s, xrg, att_s, mem_s, gates, n_p, *branch_w, tm_s)
    x1_s, meta_s = _proj_ln(merged_s, xs2, *ln1, _tile(n_s, 512))

    y_p, y_s = _moe(x1_p, x1_s, jnp.concatenate([meta_p, meta_s], axis=1), p)

    kc = CONV_W - 1
    conv_p = jnp.concatenate([jnp.zeros((bp, kc, D_RNN), F32), _seq_tails(xrg, 0, bp, s, min(kc, s), 0, D_RNN)], axis=1)[:, -kc:]
    conv_s = jnp.concatenate([state_conv, _seq_tails(xrg, n_p, bs, ts, min(kc, ts), 0, D_RNN)], axis=1)[:, -kc:]
    kv_p = [_kv_rows(qkv, g, 0, bp, s, min(KEYS * d, s)) for g, d in enumerate(DILATIONS)]
    kv_s = [_kv_rows(qkv, g, n_p, bs, ts, ts) for g in range(N_GROUPS)]
    return (y_p.reshape(bp, s, D_MODEL), y_s.reshape(bs, ts, D_MODEL), kv_p[0], kv_p[1], kv_p[2],
            mem_kv_p.reshape(bp, N_MEM, 2, MEM_HEADS, MEM_HEAD_DIM), h_p.reshape(bp, D_RNN), conv_p,
            kv_s[0], kv_s[1], kv_s[2], h_s.reshape(bs, D_RNN), conv_s)
```
